```python
import jax, jax.numpy as jnp
from jax import lax
import numpy as np

D_MODEL = 4096
BATCH = 16
SEQ = 2048
DEPTH = 1

MIX_W = D_MODEL
RWKV_W = MIX_W // 2
RWKV_HEAD = 64
N_RWKV_HEADS = RWKV_W // RWKV_HEAD
DECAY_LORA = 96
AAA_LORA = 96
GATE_LORA = 256
LRU_W = MIX_W - RWKV_W
LRU_BLOCK_W = 128
LRU_BLOCKS = LRU_W // LRU_BLOCK_W
CONV_WIDTH = 4
LRU_C = 8.0
D_FF = ((8 * D_MODEL + 3 * 256 - 1) // (3 * 256)) * 256
RWKV_COLS = 3 * RWKV_W + DECAY_LORA + AAA_LORA + GATE_LORA
IN_COLS = RWKV_COLS + 2 * LRU_W
NORM_EPS = 1e-6
GN_EPS = 64e-5

kernel_name = "hymba_rwkv7_rglru_swiglu"


def _rmsnorm(x, g):
    xf = x.astype(jnp.float32)
    return xf * lax.rsqrt(jnp.mean(xf * xf, axis=-1, keepdims=True) + NORM_EPS) * g.astype(jnp.float32)


def _token_shift(p):
    return jnp.pad(p, ((0, 0), (1, 0), (0, 0)))[:, :-1]


def _rwkv7_scan(r, w, k, v, a, b):
    Bsz, T, H, N = r.shape

    def step(S, inp):
        r_t, w_t, k_t, v_t, a_t, b_t = inp
        sa = jnp.einsum('bhvk,bhk->bhv', S, a_t)
        S = (S * w_t[:, :, None, :] + sa[..., None] * b_t[:, :, None, :]
             + v_t[..., None] * k_t[:, :, None, :])
        y_t = jnp.einsum('bhvk,bhk->bhv', S, r_t)
        return S, y_t

    seq = tuple(jnp.swapaxes(t, 0, 1) for t in (r, w, k, v, a, b))
    S0 = jnp.zeros((Bsz, H, N, N), jnp.float32)
    _, ys = lax.scan(step, S0, seq)
    return jnp.swapaxes(ys, 0, 1)


def _rwkv7_mixer(p, mu, w0, w2, a0, a2, g2, k_k, k_a, r_k, ln_g, ln_b):
    Bsz, T, _ = p.shape
    p = p + (_token_shift(p) - p) * mu
    o1, o2, o3 = RWKV_W, 2 * RWKV_W, 3 * RWKV_W
    r, k, v, wd, ad, gd = jnp.split(p, [o1, o2, o3, o3 + DECAY_LORA, o3 + DECAY_LORA + AAA_LORA], axis=-1)
    w = -jax.nn.softplus(-(w0 + jnp.tanh(wd) @ w2)) - 0.5
    decay = jnp.exp(-jnp.exp(w))
    a = jax.nn.sigmoid(a0 + ad @ a2)
    g = jax.nn.sigmoid(gd) @ g2

    def heads(t):
        return t.reshape(Bsz, T, N_RWKV_HEADS, RWKV_HEAD)

    kk = heads(k * k_k)
    kk = kk * lax.rsqrt(jnp.maximum(jnp.sum(kk * kk, axis=-1, keepdims=True), 1e-24))
    k = k * (1.0 + (a - 1.0) * k_a)
    r, k, v, decay, a = heads(r), heads(k), heads(v), heads(decay), heads(a)
    y = _rwkv7_scan(r, decay, k, v, -kk, kk * a)
    mean = jnp.mean(y, axis=-1, keepdims=True)
    var = jnp.mean(jnp.square(y - mean), axis=-1, keepdims=True)
    y = (y - mean) * lax.rsqrt(var + GN_EPS)
    y = y * ln_g.reshape(N_RWKV_HEADS, RWKV_HEAD) + ln_b.reshape(N_RWKV_HEADS, RWKV_HEAD)
    y = y + jnp.sum(r * k * r_k, axis=-1, keepdims=True) * v
    return y.reshape(Bsz, T, RWKV_W) * g


def _rglru_mixer(p, conv_w, conv_b, wr, br, wi, bi, lam, norm_g):
    Bsz, T, _ = p.shape
    xb, gate = jnp.split(p, [LRU_W], axis=-1)
    xpad = jnp.pad(xb, ((0, 0), (CONV_WIDTH - 1, 0), (0, 0)))
    xc = conv_b + xpad[:, 0:T] * conv_w[0]
    for j in range(1, CONV_WIDTH):
        xc = xc + xpad[:, j:j + T] * conv_w[j]
    xh = xc.reshape(Bsz, T, LRU_BLOCKS, LRU_BLOCK_W)
    rg = jax.nn.sigmoid(jnp.einsum('bthi,hij->bthj', xh, wr).reshape(Bsz, T, LRU_W) + br)
    ig = jax.nn.sigmoid(jnp.einsum('bthi,hij->bthj', xh, wi).reshape(Bsz, T, LRU_W) + bi)
    log_a = -LRU_C * rg * jax.nn.softplus(-lam)
    a = jnp.exp(log_a)
    mult = jnp.sqrt(-jnp.expm1(2.0 * log_a))
    first = (jnp.arange(T) == 0)[None, :, None]
    mult = jnp.where(first, 1.0, mult)
    bx = mult * ig * xc

    def combine(left, right):
        a1, b1 = left
        a2, b2 = right
        return a1 * a2, a2 * b1 + b2

    _, h = lax.associative_scan(combine, (a, bx), axis=1)
    y = h * jax.nn.gelu(gate)
    return _rmsnorm(y, norm_g)


def _fwd_setup_inputs(seed: int = 0) -> dict:
    key = jax.random.key(seed)
    ks = jax.random.split(key, 32)
    f32 = jnp.float32
    L = DEPTH

    def nrm(k, shape, scale):
        return jax.random.normal(k, shape, f32) * scale

    u = jax.random.uniform(ks[20], (L, LRU_W), f32, 0.9, 0.999)
    a_base = u ** (1.0 / LRU_C)
    lru_lambda = jnp.log(a_base) - jnp.log1p(-a_base)
    return {
        "x": nrm(ks[0], (BATCH, SEQ, D_MODEL), 1.0),
        "norm_mix_g": 1.0 + nrm(ks[1], (L, D_MODEL), 0.02),
        "w_in": nrm(ks[2], (L, D_MODEL, IN_COLS), D_MODEL ** -0.5),
        "mu_shift": jax.random.uniform(ks[3], (L, RWKV_COLS), f32),
        "rwkv_w0": jax.random.uniform(ks[4], (L, RWKV_W), f32, -6.0, 0.0),
        "rwkv_w2": nrm(ks[5], (L, DECAY_LORA, RWKV_W), 0.1 * DECAY_LORA ** -0.5),
        "rwkv_a0": nrm(ks[6], (L, RWKV_W), 0.1),
        "rwkv_a2": nrm(ks[7], (L, AAA_LORA, RWKV_W), AAA_LORA ** -0.5),
        "rwkv_g2": nrm(ks[8], (L, GATE_LORA, RWKV_W), GATE_LORA ** -0.5),
        "rwkv_k_k": 0.85 + nrm(ks[9], (L, RWKV_W), 0.02),
        "rwkv_k_a": 1.0 + nrm(ks[10], (L, RWKV_W), 0.02),
        "rwkv_r_k": nrm(ks[11], (L, N_RWKV_HEADS, RWKV_HEAD), 0.1),
        "rwkv_ln_g": 1.0 + nrm(ks[12], (L, RWKV_W), 0.02),
        "rwkv_ln_b": nrm(ks[13], (L, RWKV_W), 0.01),
        "conv_w": nrm(ks[14], (L, CONV_WIDTH, LRU_W), CONV_WIDTH ** -0.5),
        "conv_b": nrm(ks[15], (L, LRU_W), 0.01),
        "lru_wr": nrm(ks[16], (L, LRU_BLOCKS, LRU_BLOCK_W, LRU_BLOCK_W), LRU_BLOCK_W ** -0.5),
        "lru_br": nrm(ks[17], (L, LRU_W), 0.01),
        "lru_wi": nrm(ks[18], (L, LRU_BLOCKS, LRU_BLOCK_W, LRU_BLOCK_W), LRU_BLOCK_W ** -0.5),
        "lru_bi": nrm(ks[19], (L, LRU_W), 0.01),
        "lru_lambda": lru_lambda,
        "lru_norm_g": 1.0 + nrm(ks[21], (L, LRU_W), 0.02),
        "w_out": nrm(ks[22], (L, MIX_W, D_MODEL), MIX_W ** -0.5),
        "norm_ffn_g": 1.0 + nrm(ks[23], (L, D_MODEL), 0.02),
        "ffn_w_gate": nrm(ks[24], (L, D_MODEL, D_FF), D_MODEL ** -0.5),
        "ffn_w_up": nrm(ks[25], (L, D_MODEL, D_FF), D_MODEL ** -0.5),
        "ffn_w_down": nrm(ks[26], (L, D_FF, D_MODEL), D_FF ** -0.5),
        "norm_final_g": 1.0 + nrm(ks[27], (D_MODEL,), 0.02),
    }


def _fwd_reference(x, norm_mix_g, w_in, mu_shift, rwkv_w0, rwkv_w2, rwkv_a0, rwkv_a2, rwkv_g2,
              rwkv_k_k, rwkv_k_a, rwkv_r_k, rwkv_ln_g, rwkv_ln_b, conv_w, conv_b,
              lru_wr, lru_br, lru_wi, lru_bi, lru_lambda, lru_norm_g, w_out,
              norm_ffn_g, ffn_w_gate, ffn_w_up, ffn_w_down, norm_final_g):
    h = x.astype(jnp.float32)
    for l in range(DEPTH):
        u = _rmsnorm(h, norm_mix_g[l])
        p = u @ w_in[l]
        y_a = _rwkv7_mixer(p[..., :RWKV_COLS], mu_shift[l], rwkv_w0[l], rwkv_w2[l], rwkv_a0[l],
                           rwkv_a2[l], rwkv_g2[l], rwkv_k_k[l], rwkv_k_a[l], rwkv_r_k[l],
                           rwkv_ln_g[l], rwkv_ln_b[l])
        y_b = _rglru_mixer(p[..., RWKV_COLS:], conv_w[l], conv_b[l], lru_wr[l], lru_br[l],
                           lru_wi[l], lru_bi[l], lru_lambda[l], lru_norm_g[l])
        h = h + jnp.concatenate([y_a, y_b], axis=-1) @ w_out[l]
        u = _rmsnorm(h, norm_ffn_g[l])
        h = h + (jax.nn.silu(u @ ffn_w_gate[l]) * (u @ ffn_w_up[l])) @ ffn_w_down[l]
    return _rmsnorm(h, norm_final_g).astype(x.dtype)


import jax as _jax
import jax.numpy as _jnp

TWIN_FORMAT = 'train_step'
FWD_PARAMS = ['x', 'norm_mix_g', 'w_in', 'mu_shift', 'rwkv_w0', 'rwkv_w2', 'rwkv_a0', 'rwkv_a2', 'rwkv_g2', 'rwkv_k_k', 'rwkv_k_a', 'rwkv_r_k', 'rwkv_ln_g', 'rwkv_ln_b', 'conv_w', 'conv_b', 'lru_wr', 'lru_br', 'lru_wi', 'lru_bi', 'lru_lambda', 'lru_norm_g', 'w_out', 'norm_ffn_g', 'ffn_w_gate', 'ffn_w_up', 'ffn_w_down', 'norm_final_g']
TWIN_WEIGHTS = ['norm_mix_g', 'w_in', 'mu_shift', 'rwkv_w0', 'rwkv_w2', 'rwkv_a0', 'rwkv_a2', 'rwkv_g2', 'rwkv_k_k', 'rwkv_k_a', 'rwkv_r_k', 'rwkv_ln_g', 'rwkv_ln_b', 'conv_w', 'conv_b', 'lru_wr', 'lru_br', 'lru_wi', 'lru_bi', 'lru_lambda', 'lru_norm_g', 'w_out', 'norm_ffn_g', 'ffn_w_gate', 'ffn_w_up', 'ffn_w_down', 'norm_final_g']
TWIN_DIFF_INPUT = 'x'
TWIN_INPUTS = ['x', 'norm_mix_g', 'w_in', 'mu_shift', 'rwkv_w0', 'rwkv_w2', 'rwkv_a0', 'rwkv_a2', 'rwkv_g2', 'rwkv_k_k', 'rwkv_k_a', 'rwkv_r_k', 'rwkv_ln_g', 'rwkv_ln_b', 'conv_w', 'conv_b', 'lru_wr', 'lru_br', 'lru_wi', 'lru_bi', 'lru_lambda', 'lru_norm_g', 'w_out', 'norm_ffn_g', 'ffn_w_gate', 'ffn_w_up', 'ffn_w_down', 'norm_final_g', 'loss_target', 'm_norm_mix_g', 'm_w_in', 'm_mu_shift', 'm_rwkv_w0', 'm_rwkv_w2', 'm_rwkv_a0', 'm_rwkv_a2', 'm_rwkv_g2', 'm_rwkv_k_k', 'm_rwkv_k_a', 'm_rwkv_r_k', 'm_rwkv_ln_g', 'm_rwkv_ln_b', 'm_conv_w', 'm_conv_b', 'm_lru_wr', 'm_lru_br', 'm_lru_wi', 'm_lru_bi', 'm_lru_lambda', 'm_lru_norm_g', 'm_w_out', 'm_norm_ffn_g', 'm_ffn_w_gate', 'm_ffn_w_up', 'm_ffn_w_down', 'm_norm_final_g', 'v_norm_mix_g', 'v_w_in', 'v_mu_shift', 'v_rwkv_w0', 'v_rwkv_w2', 'v_rwkv_a0', 'v_rwkv_a2', 'v_rwkv_g2', 'v_rwkv_k_k', 'v_rwkv_k_a', 'v_rwkv_r_k', 'v_rwkv_ln_g', 'v_rwkv_ln_b', 'v_conv_w', 'v_conv_b', 'v_lru_wr', 'v_lru_br', 'v_lru_wi', 'v_lru_bi', 'v_lru_lambda', 'v_lru_norm_g', 'v_w_out', 'v_norm_ffn_g', 'v_ffn_w_gate', 'v_ffn_w_up', 'v_ffn_w_down', 'v_norm_final_g']
TWIN_OUTPUTS = ['loss', 'grad_x', 'grad_norm_mix_g', 'grad_w_in', 'grad_mu_shift', 'grad_rwkv_w0', 'grad_rwkv_w2', 'grad_rwkv_a0', 'grad_rwkv_a2', 'grad_rwkv_g2', 'grad_rwkv_k_k', 'grad_rwkv_k_a', 'grad_rwkv_r_k', 'grad_rwkv_ln_g', 'grad_rwkv_ln_b', 'grad_conv_w', 'grad_conv_b', 'grad_lru_wr', 'grad_lru_br', 'grad_lru_wi', 'grad_lru_bi', 'grad_lru_lambda', 'grad_lru_norm_g', 'grad_w_out', 'grad_norm_ffn_g', 'grad_ffn_w_gate', 'grad_ffn_w_up', 'grad_ffn_w_down', 'grad_norm_final_g', 'delta_norm_mix_g', 'delta_w_in', 'delta_mu_shift', 'delta_rwkv_w0', 'delta_rwkv_w2', 'delta_rwkv_a0', 'delta_rwkv_a2', 'delta_rwkv_g2', 'delta_rwkv_k_k', 'delta_rwkv_k_a', 'delta_rwkv_r_k', 'delta_rwkv_ln_g', 'delta_rwkv_ln_b', 'delta_conv_w', 'delta_conv_b', 'delta_lru_wr', 'delta_lru_br', 'delta_lru_wi', 'delta_lru_bi', 'delta_lru_lambda', 'delta_lru_norm_g', 'delta_w_out', 'delta_norm_ffn_g', 'delta_ffn_w_gate', 'delta_ffn_w_up', 'delta_ffn_w_down', 'delta_norm_final_g', 'new_m_norm_mix_g', 'new_m_w_in', 'new_m_mu_shift', 'new_m_rwkv_w0', 'new_m_rwkv_w2', 'new_m_rwkv_a0', 'new_m_rwkv_a2', 'new_m_rwkv_g2', 'new_m_rwkv_k_k', 'new_m_rwkv_k_a', 'new_m_rwkv_r_k', 'new_m_rwkv_ln_g', 'new_m_rwkv_ln_b', 'new_m_conv_w', 'new_m_conv_b', 'new_m_lru_wr', 'new_m_lru_br', 'new_m_lru_wi', 'new_m_lru_bi', 'new_m_lru_lambda', 'new_m_lru_norm_g', 'new_m_w_out', 'new_m_norm_ffn_g', 'new_m_ffn_w_gate', 'new_m_ffn_w_up', 'new_m_ffn_w_down', 'new_m_norm_final_g', 'new_v_norm_mix_g', 'new_v_w_in', 'new_v_mu_shift', 'new_v_rwkv_w0', 'new_v_rwkv_w2', 'new_v_rwkv_a0', 'new_v_rwkv_a2', 'new_v_rwkv_g2', 'new_v_rwkv_k_k', 'new_v_rwkv_k_a', 'new_v_rwkv_r_k', 'new_v_rwkv_ln_g', 'new_v_rwkv_ln_b', 'new_v_conv_w', 'new_v_conv_b', 'new_v_lru_wr', 'new_v_lru_br', 'new_v_lru_wi', 'new_v_lru_bi', 'new_v_lru_lambda', 'new_v_lru_norm_g', 'new_v_w_out', 'new_v_norm_ffn_g', 'new_v_ffn_w_gate', 'new_v_ffn_w_up', 'new_v_ffn_w_down', 'new_v_norm_final_g']
TWIN_LEAF_KINDS = {'loss': 'loss', 'grad_x': 'grad_x', 'grad_norm_mix_g': 'grad_w', 'grad_w_in': 'grad_w', 'grad_mu_shift': 'grad_w', 'grad_rwkv_w0': 'grad_w', 'grad_rwkv_w2': 'grad_w', 'grad_rwkv_a0': 'grad_w', 'grad_rwkv_a2': 'grad_w', 'grad_rwkv_g2': 'grad_w', 'grad_rwkv_k_k': 'grad_w', 'grad_rwkv_k_a': 'grad_w', 'grad_rwkv_r_k': 'grad_w', 'grad_rwkv_ln_g': 'grad_w', 'grad_rwkv_ln_b': 'grad_w', 'grad_conv_w': 'grad_w', 'grad_conv_b': 'grad_w', 'grad_lru_wr': 'grad_w', 'grad_lru_br': 'grad_w', 'grad_lru_wi': 'grad_w', 'grad_lru_bi': 'grad_w', 'grad_lru_lambda': 'grad_w', 'grad_lru_norm_g': 'grad_w', 'grad_w_out': 'grad_w', 'grad_norm_ffn_g': 'grad_w', 'grad_ffn_w_gate': 'grad_w', 'grad_ffn_w_up': 'grad_w', 'grad_ffn_w_down': 'grad_w', 'grad_norm_final_g': 'grad_w', 'delta_norm_mix_g': 'delta_w', 'delta_w_in': 'delta_w', 'delta_mu_shift': 'delta_w', 'delta_rwkv_w0': 'delta_w', 'delta_rwkv_w2': 'delta_w', 'delta_rwkv_a0': 'delta_w', 'delta_rwkv_a2': 'delta_w', 'delta_rwkv_g2': 'delta_w', 'delta_rwkv_k_k': 'delta_w', 'delta_rwkv_k_a': 'delta_w', 'delta_rwkv_r_k': 'delta_w', 'delta_rwkv_ln_g': 'delta_w', 'delta_rwkv_ln_b': 'delta_w', 'delta_conv_w': 'delta_w', 'delta_conv_b': 'delta_w', 'delta_lru_wr': 'delta_w', 'delta_lru_br': 'delta_w', 'delta_lru_wi': 'delta_w', 'delta_lru_bi': 'delta_w', 'delta_lru_lambda': 'delta_w', 'delta_lru_norm_g': 'delta_w', 'delta_w_out': 'delta_w', 'delta_norm_ffn_g': 'delta_w', 'delta_ffn_w_gate': 'delta_w', 'delta_ffn_w_up': 'delta_w', 'delta_ffn_w_down': 'delta_w', 'delta_norm_final_g': 'delta_w', 'new_m_norm_mix_g': 'new_m', 'new_m_w_in': 'new_m', 'new_m_mu_shift': 'new_m', 'new_m_rwkv_w0': 'new_m', 'new_m_rwkv_w2': 'new_m', 'new_m_rwkv_a0': 'new_m', 'new_m_rwkv_a2': 'new_m', 'new_m_rwkv_g2': 'new_m', 'new_m_rwkv_k_k': 'new_m', 'new_m_rwkv_k_a': 'new_m', 'new_m_rwkv_r_k': 'new_m', 'new_m_rwkv_ln_g': 'new_m', 'new_m_rwkv_ln_b': 'new_m', 'new_m_conv_w': 'new_m', 'new_m_conv_b': 'new_m', 'new_m_lru_wr': 'new_m', 'new_m_lru_br': 'new_m', 'new_m_lru_wi': 'new_m', 'new_m_lru_bi': 'new_m', 'new_m_lru_lambda': 'new_m', 'new_m_lru_norm_g': 'new_m', 'new_m_w_out': 'new_m', 'new_m_norm_ffn_g': 'new_m', 'new_m_ffn_w_gate': 'new_m', 'new_m_ffn_w_up': 'new_m', 'new_m_ffn_w_down': 'new_m', 'new_m_norm_final_g': 'new_m', 'new_v_norm_mix_g': 'new_v', 'new_v_w_in': 'new_v', 'new_v_mu_shift': 'new_v', 'new_v_rwkv_w0': 'new_v', 'new_v_rwkv_w2': 'new_v', 'new_v_rwkv_a0': 'new_v', 'new_v_rwkv_a2': 'new_v', 'new_v_rwkv_g2': 'new_v', 'new_v_rwkv_k_k': 'new_v', 'new_v_rwkv_k_a': 'new_v', 'new_v_rwkv_r_k': 'new_v', 'new_v_rwkv_ln_g': 'new_v', 'new_v_rwkv_ln_b': 'new_v', 'new_v_conv_w': 'new_v', 'new_v_conv_b': 'new_v', 'new_v_lru_wr': 'new_v', 'new_v_lru_br': 'new_v', 'new_v_lru_wi': 'new_v', 'new_v_lru_bi': 'new_v', 'new_v_lru_lambda': 'new_v', 'new_v_lru_norm_g': 'new_v', 'new_v_w_out': 'new_v', 'new_v_norm_ffn_g': 'new_v', 'new_v_ffn_w_gate': 'new_v', 'new_v_ffn_w_up': 'new_v', 'new_v_ffn_w_down': 'new_v', 'new_v_norm_final_g': 'new_v'}


def _forward(args):
    return _fwd_reference(*[args[k] for k in FWD_PARAMS])


def _output_shape():
    def fwd():
        inp = _fwd_setup_inputs(0)
        return _fwd_reference(*[inp[k] for k in FWD_PARAMS])
    out = _jax.eval_shape(fwd)
    return out.shape, out.dtype

N_MICROBATCH = 1
ADAM_LR = 0.001
ADAM_B1 = 0.9
ADAM_B2 = 0.999
ADAM_EPS = 1e-08
ADAM_WD = 0.01
ADAM_STEP = 10
PER_EXAMPLE_BATCH_AXIS = {'x': 0, 'loss_target': 0}
SHARED_INPUTS = []
_WEIGHT_DTYPES = {'norm_mix_g': _jnp.float32, 'w_in': _jnp.float32, 'mu_shift': _jnp.float32, 'rwkv_w0': _jnp.float32, 'rwkv_w2': _jnp.float32, 'rwkv_a0': _jnp.float32, 'rwkv_a2': _jnp.float32, 'rwkv_g2': _jnp.float32, 'rwkv_k_k': _jnp.float32, 'rwkv_k_a': _jnp.float32, 'rwkv_r_k': _jnp.float32, 'rwkv_ln_g': _jnp.float32, 'rwkv_ln_b': _jnp.float32, 'conv_w': _jnp.float32, 'conv_b': _jnp.float32, 'lru_wr': _jnp.float32, 'lru_br': _jnp.float32, 'lru_wi': _jnp.float32, 'lru_bi': _jnp.float32, 'lru_lambda': _jnp.float32, 'lru_norm_g': _jnp.float32, 'w_out': _jnp.float32, 'norm_ffn_g': _jnp.float32, 'ffn_w_gate': _jnp.float32, 'ffn_w_up': _jnp.float32, 'ffn_w_down': _jnp.float32, 'norm_final_g': _jnp.float32}
MOMENT_SCALE = {'norm_mix_g': 4.773614e-02, 'w_in': 2.945898e-02, 'mu_shift': 3.508751e-02, 'rwkv_w0': 7.774518e-03, 'rwkv_w2': 1.007480e-03, 'rwkv_a0': 7.863483e-03, 'rwkv_a2': 7.225716e-03, 'rwkv_g2': 2.085998e-02, 'rwkv_k_k': 2.428332e-02, 'rwkv_k_a': 2.214094e-02, 'rwkv_r_k': 4.874012e-02, 'rwkv_ln_g': 2.043859e-02, 'rwkv_ln_b': 2.213506e-02, 'conv_w': 3.883914e-02, 'conv_b': 4.228705e-01, 'lru_wr': 1.254613e-02, 'lru_br': 1.111871e-02, 'lru_wi': 2.240233e-02, 'lru_bi': 1.362196e-02, 'lru_lambda': 2.088930e-02, 'lru_norm_g': 3.674612e-02, 'w_out': 3.045655e-02, 'norm_ffn_g': 2.696520e-02, 'ffn_w_gate': 1.175447e-02, 'ffn_w_up': 1.138323e-02, 'ffn_w_down': 1.866648e-02, 'norm_final_g': 8.000119e+00}


def _to_microbatches(a, axis):
    t = _jnp.moveaxis(a, axis, 0)
    t = t.reshape((N_MICROBATCH, t.shape[0] // N_MICROBATCH) + t.shape[1:])
    return _jnp.moveaxis(t, 1, axis + 1)


def setup_inputs(seed: int = 0) -> dict:
    inp = _fwd_setup_inputs(seed)
    key = _jax.random.fold_in(_jax.random.key(seed), 7919)
    shape, _ = _output_shape()
    out = dict(inp)
    out["loss_target"] = _jax.random.normal(_jax.random.fold_in(key, 0), shape, _jnp.float32)
    for i, name in enumerate(TWIN_WEIGHTS):
        w = inp[name].astype(_jnp.float32)
        if MOMENT_SCALE is None:
            s = _jnp.sqrt(_jnp.mean(_jnp.square(w)) + 1e-30)
        else:
            s = MOMENT_SCALE[name]
        km, kv = _jax.random.split(_jax.random.fold_in(key, i + 1))
        out[name] = w
        out["m_" + name] = s * _jax.random.normal(km, w.shape, _jnp.float32)
        out["v_" + name] = (s * s) * _jax.random.uniform(kv, w.shape, _jnp.float32, 0.5, 1.5)
    if N_MICROBATCH > 1:
        for name, axis in PER_EXAMPLE_BATCH_AXIS.items():
            out[name] = _to_microbatches(out[name], axis)
    return {'x': out['x'], 'norm_mix_g': out['norm_mix_g'], 'w_in': out['w_in'], 'mu_shift': out['mu_shift'], 'rwkv_w0': out['rwkv_w0'], 'rwkv_w2': out['rwkv_w2'], 'rwkv_a0': out['rwkv_a0'], 'rwkv_a2': out['rwkv_a2'], 'rwkv_g2': out['rwkv_g2'], 'rwkv_k_k': out['rwkv_k_k'], 'rwkv_k_a': out['rwkv_k_a'], 'rwkv_r_k': out['rwkv_r_k'], 'rwkv_ln_g': out['rwkv_ln_g'], 'rwkv_ln_b': out['rwkv_ln_b'], 'conv_w': out['conv_w'], 'conv_b': out['conv_b'], 'lru_wr': out['lru_wr'], 'lru_br': out['lru_br'], 'lru_wi': out['lru_wi'], 'lru_bi': out['lru_bi'], 'lru_lambda': out['lru_lambda'], 'lru_norm_g': out['lru_norm_g'], 'w_out': out['w_out'], 'norm_ffn_g': out['norm_ffn_g'], 'ffn_w_gate': out['ffn_w_gate'], 'ffn_w_up': out['ffn_w_up'], 'ffn_w_down': out['ffn_w_down'], 'norm_final_g': out['norm_final_g'], 'loss_target': out['loss_target'], 'm_norm_mix_g': out['m_norm_mix_g'], 'm_w_in': out['m_w_in'], 'm_mu_shift': out['m_mu_shift'], 'm_rwkv_w0': out['m_rwkv_w0'], 'm_rwkv_w2': out['m_rwkv_w2'], 'm_rwkv_a0': out['m_rwkv_a0'], 'm_rwkv_a2': out['m_rwkv_a2'], 'm_rwkv_g2': out['m_rwkv_g2'], 'm_rwkv_k_k': out['m_rwkv_k_k'], 'm_rwkv_k_a': out['m_rwkv_k_a'], 'm_rwkv_r_k': out['m_rwkv_r_k'], 'm_rwkv_ln_g': out['m_rwkv_ln_g'], 'm_rwkv_ln_b': out['m_rwkv_ln_b'], 'm_conv_w': out['m_conv_w'], 'm_conv_b': out['m_conv_b'], 'm_lru_wr': out['m_lru_wr'], 'm_lru_br': out['m_lru_br'], 'm_lru_wi': out['m_lru_wi'], 'm_lru_bi': out['m_lru_bi'], 'm_lru_lambda': out['m_lru_lambda'], 'm_lru_norm_g': out['m_lru_norm_g'], 'm_w_out': out['m_w_out'], 'm_norm_ffn_g': out['m_norm_ffn_g'], 'm_ffn_w_gate': out['m_ffn_w_gate'], 'm_ffn_w_up': out['m_ffn_w_up'], 'm_ffn_w_down': out['m_ffn_w_down'], 'm_norm_final_g': out['m_norm_final_g'], 'v_norm_mix_g': out['v_norm_mix_g'], 'v_w_in': out['v_w_in'], 'v_mu_shift': out['v_mu_shift'], 'v_rwkv_w0': out['v_rwkv_w0'], 'v_rwkv_w2': out['v_rwkv_w2'], 'v_rwkv_a0': out['v_rwkv_a0'], 'v_rwkv_a2': out['v_rwkv_a2'], 'v_rwkv_g2': out['v_rwkv_g2'], 'v_rwkv_k_k': out['v_rwkv_k_k'], 'v_rwkv_k_a': out['v_rwkv_k_a'], 'v_rwkv_r_k': out['v_rwkv_r_k'], 'v_rwkv_ln_g': out['v_rwkv_ln_g'], 'v_rwkv_ln_b': out['v_rwkv_ln_b'], 'v_conv_w': out['v_conv_w'], 'v_conv_b': out['v_conv_b'], 'v_lru_wr': out['v_lru_wr'], 'v_lru_br': out['v_lru_br'], 'v_lru_wi': out['v_lru_wi'], 'v_lru_bi': out['v_lru_bi'], 'v_lru_lambda': out['v_lru_lambda'], 'v_lru_norm_g': out['v_lru_norm_g'], 'v_w_out': out['v_w_out'], 'v_norm_ffn_g': out['v_norm_ffn_g'], 'v_ffn_w_gate': out['v_ffn_w_gate'], 'v_ffn_w_up': out['v_ffn_w_up'], 'v_ffn_w_down': out['v_ffn_w_down'], 'v_norm_final_g': out['v_norm_final_g']}


def _loss(weights, diff, rest, loss_target):
    with _jax.named_scope("forward"):
        args = {**rest, TWIN_DIFF_INPUT: diff, **{k: w.astype(_WEIGHT_DTYPES[k]) for k, w in weights.items()}}
        y = _forward(args)
    with _jax.named_scope("loss_head"):
        err = _jnp.square(y.astype(_jnp.float32) - loss_target)
        return 0.5 * _jnp.sum(_jnp.mean(err, axis=-1)) if err.ndim else 0.5 * err


def _adamw(w, g, m, v):
    m = ADAM_B1 * m + (1.0 - ADAM_B1) * g
    v = ADAM_B2 * v + (1.0 - ADAM_B2) * _jnp.square(g)
    m_hat = m / (1.0 - ADAM_B1 ** ADAM_STEP)
    v_hat = v / (1.0 - ADAM_B2 ** ADAM_STEP)
    delta = -ADAM_LR * (m_hat / (_jnp.sqrt(v_hat) + ADAM_EPS) + ADAM_WD * w)
    return delta, m, v


def reference(x, norm_mix_g, w_in, mu_shift, rwkv_w0, rwkv_w2, rwkv_a0, rwkv_a2, rwkv_g2, rwkv_k_k, rwkv_k_a, rwkv_r_k, rwkv_ln_g, rwkv_ln_b, conv_w, conv_b, lru_wr, lru_br, lru_wi, lru_bi, lru_lambda, lru_norm_g, w_out, norm_ffn_g, ffn_w_gate, ffn_w_up, ffn_w_down, norm_final_g, loss_target, m_norm_mix_g, m_w_in, m_mu_shift, m_rwkv_w0, m_rwkv_w2, m_rwkv_a0, m_rwkv_a2, m_rwkv_g2, m_rwkv_k_k, m_rwkv_k_a, m_rwkv_r_k, m_rwkv_ln_g, m_rwkv_ln_b, m_conv_w, m_conv_b, m_lru_wr, m_lru_br, m_lru_wi, m_lru_bi, m_lru_lambda, m_lru_norm_g, m_w_out, m_norm_ffn_g, m_ffn_w_gate, m_ffn_w_up, m_ffn_w_down, m_norm_final_g, v_norm_mix_g, v_w_in, v_mu_shift, v_rwkv_w0, v_rwkv_w2, v_rwkv_a0, v_rwkv_a2, v_rwkv_g2, v_rwkv_k_k, v_rwkv_k_a, v_rwkv_r_k, v_rwkv_ln_g, v_rwkv_ln_b, v_conv_w, v_conv_b, v_lru_wr, v_lru_br, v_lru_wi, v_lru_bi, v_lru_lambda, v_lru_norm_g, v_w_out, v_norm_ffn_g, v_ffn_w_gate, v_ffn_w_up, v_ffn_w_down, v_norm_final_g):
    given = dict(x=x, norm_mix_g=norm_mix_g, w_in=w_in, mu_shift=mu_shift, rwkv_w0=rwkv_w0, rwkv_w2=rwkv_w2, rwkv_a0=rwkv_a0, rwkv_a2=rwkv_a2, rwkv_g2=rwkv_g2, rwkv_k_k=rwkv_k_k, rwkv_k_a=rwkv_k_a, rwkv_r_k=rwkv_r_k, rwkv_ln_g=rwkv_ln_g, rwkv_ln_b=rwkv_ln_b, conv_w=conv_w, conv_b=conv_b, lru_wr=lru_wr, lru_br=lru_br, lru_wi=lru_wi, lru_bi=lru_bi, lru_lambda=lru_lambda, lru_norm_g=lru_norm_g, w_out=w_out, norm_ffn_g=norm_ffn_g, ffn_w_gate=ffn_w_gate, ffn_w_up=ffn_w_up, ffn_w_down=ffn_w_down, norm_final_g=norm_final_g, loss_target=loss_target, m_norm_mix_g=m_norm_mix_g, m_w_in=m_w_in, m_mu_shift=m_mu_shift, m_rwkv_w0=m_rwkv_w0, m_rwkv_w2=m_rwkv_w2, m_rwkv_a0=m_rwkv_a0, m_rwkv_a2=m_rwkv_a2, m_rwkv_g2=m_rwkv_g2, m_rwkv_k_k=m_rwkv_k_k, m_rwkv_k_a=m_rwkv_k_a, m_rwkv_r_k=m_rwkv_r_k, m_rwkv_ln_g=m_rwkv_ln_g, m_rwkv_ln_b=m_rwkv_ln_b, m_conv_w=m_conv_w, m_conv_b=m_conv_b, m_lru_wr=m_lru_wr, m_lru_br=m_lru_br, m_lru_wi=m_lru_wi, m_lru_bi=m_lru_bi, m_lru_lambda=m_lru_lambda, m_lru_norm_g=m_lru_norm_g, m_w_out=m_w_out, m_norm_ffn_g=m_norm_ffn_g, m_ffn_w_gate=m_ffn_w_gate, m_ffn_w_up=m_ffn_w_up, m_ffn_w_down=m_ffn_w_down, m_norm_final_g=m_norm_final_g, v_norm_mix_g=v_norm_mix_g, v_w_in=v_w_in, v_mu_shift=v_mu_shift, v_rwkv_w0=v_rwkv_w0, v_rwkv_w2=v_rwkv_w2, v_rwkv_a0=v_rwkv_a0, v_rwkv_a2=v_rwkv_a2, v_rwkv_g2=v_rwkv_g2, v_rwkv_k_k=v_rwkv_k_k, v_rwkv_k_a=v_rwkv_k_a, v_rwkv_r_k=v_rwkv_r_k, v_rwkv_ln_g=v_rwkv_ln_g, v_rwkv_ln_b=v_rwkv_ln_b, v_conv_w=v_conv_w, v_conv_b=v_conv_b, v_lru_wr=v_lru_wr, v_lru_br=v_lru_br, v_lru_wi=v_lru_wi, v_lru_bi=v_lru_bi, v_lru_lambda=v_lru_lambda, v_lru_norm_g=v_lru_norm_g, v_w_out=v_w_out, v_norm_ffn_g=v_norm_ffn_g, v_ffn_w_gate=v_ffn_w_gate, v_ffn_w_up=v_ffn_w_up, v_ffn_w_down=v_ffn_w_down, v_norm_final_g=v_norm_final_g)
    weights = {n: given[n] for n in TWIN_WEIGHTS}
    shared = {n: given[n] for n in SHARED_INPUTS}
    per_example = {n: given[n] for n in ['x']}
    grad_fn = _jax.value_and_grad(_loss, argnums=(0, 1))

    def one_microbatch(ex, loss_target):
        ex = dict(ex)
        diff = ex.pop(TWIN_DIFF_INPUT)
        return grad_fn(weights, diff, {**shared, **ex}, loss_target)

    if N_MICROBATCH == 1:
        loss, (grad_w, grad_x) = one_microbatch(per_example, given["loss_target"])
    else:
        def body(carry, xs):
            loss_sum, grad_sum = carry
            l_k, (gw_k, gx_k) = one_microbatch(xs[0], xs[1])
            with _jax.named_scope("update"):
                return (loss_sum + l_k, _jax.tree.map(_jnp.add, grad_sum, gw_k)), gx_k

        init = (_jnp.zeros((), _jnp.float32), _jax.tree.map(_jnp.zeros_like, weights))
        (loss, grad_w), grad_x = _jax.lax.scan(body, init, (per_example, given["loss_target"]))
    with _jax.named_scope("update"):
        delta_w, new_m, new_v = {}, {}, {}
        for n in TWIN_WEIGHTS:
            delta_w[n], new_m[n], new_v[n] = _adamw(weights[n], grad_w[n], given["m_" + n], given["v_" + n])
    return (loss, grad_x, *[grad_w[n] for n in TWIN_WEIGHTS], *[delta_w[n] for n in TWIN_WEIGHTS],
            *[new_m[n] for n in TWIN_WEIGHTS], *[new_v[n] for n in TWIN_WEIGHTS])
```

```python
import functools
import math

import jax
import jax.numpy as jnp
from jax import lax
from jax.experimental import pallas as pl
from jax.experimental.pallas import tpu as pltpu

F32 = jnp.float32
BF16 = jnp.bfloat16
HI = lax.Precision.HIGHEST
MESH = pl.DeviceIdType.MESH

N_DEV = 8
LANE = 128
HEAD = 64
SCAN_CHUNK = 64
VMEM_LIMIT = 56 * 1024 * 1024

NORM_EPS = 1e-6
GN_EPS = 64e-5
LRU_C = 8.0
ADAM_LR, ADAM_B1, ADAM_B2, ADAM_EPS, ADAM_WD, ADAM_STEP = 0.001, 0.9, 0.999, 1e-08, 0.01, 10


def _pick(n, cands):
    for c in cands:
        if n % c == 0:
            return c
    return n


def _rup(n, m):
    return (n + m - 1) // m * m


def _cparams(dims):
    return pltpu.CompilerParams(dimension_semantics=dims, vmem_limit_bytes=VMEM_LIMIT)


def _sigmoid(x):
    return 1.0 / (1.0 + jnp.exp(-x))


def _softplus(z):
    return jnp.maximum(z, 0.0) + jnp.log(1.0 + jnp.exp(-jnp.abs(z)))


def _neg_expm1(x):
    series = -(x * (1.0 + 0.5 * x * (1.0 + (x / 3.0) * (1.0 + 0.25 * x))))
    return jnp.where(jnp.abs(x) < 0.03, series, 1.0 - jnp.exp(x))


def _gelu(x):
    return 0.5 * x * (1.0 + jnp.tanh(0.7978845608028654 * (x + 0.044715 * (x * x * x))))


def _dot(a, b, dims, precision=None):
    return lax.dot_general(a, b, (dims, ((), ())), precision=precision, preferred_element_type=F32)


def _nn(a, b, precision=None):
    return _dot(a, b, ((1,), (0,)), precision)


def _nt(a, b, precision=None):
    return _dot(a, b, ((1,), (1,)), precision)


def _tn(a, b, precision=None):
    return _dot(a, b, ((0,), (0,)), precision)


def _mm(a, b, *, name, ta=False, tb=False, out_dtype=F32, add=None):
    M, K = (a.shape[1], a.shape[0]) if ta else a.shape
    N = b.shape[0] if tb else b.shape[1]
    assert (b.shape[1] if tb else b.shape[0]) == K, (a.shape, b.shape, ta, tb)
    tm = _pick(M, (512, 256, 128))
    tn = _pick(N, (1024, 1536, 768, 512, 256, 128))
    tk = _pick(K, (512, 256, 128))
    nk = K // tk
    dims = ((0 if ta else 1,), (1 if tb else 0,))

    def kern(*refs):
        if add is None:
            a_ref, b_ref, o_ref, acc = refs
        else:
            a_ref, b_ref, add_ref, o_ref, acc = refs
        k = pl.program_id(2)

        @pl.when(k == 0)
        def _():
            acc[...] = jnp.zeros_like(acc)

        acc[...] += _dot(a_ref[...], b_ref[...], dims)

        @pl.when(k == nk - 1)
        def _():
            r = acc[...]
            if add is not None:
                r = r + add_ref[...].astype(F32)
            o_ref[...] = r.astype(o_ref.dtype)

    a_spec = pl.BlockSpec((tk, tm), lambda i, j, k: (k, i)) if ta else pl.BlockSpec((tm, tk), lambda i, j, k: (i, k))
    b_spec = pl.BlockSpec((tn, tk), lambda i, j, k: (j, k)) if tb else pl.BlockSpec((tk, tn), lambda i, j, k: (k, j))
    o_spec = pl.BlockSpec((tm, tn), lambda i, j, k: (i, j))
    in_specs = [a_spec, b_spec] + ([o_spec] if add is not None else [])
    args = (a, b) + ((add,) if add is not None else ())
    return pl.pallas_call(
        kern, name=name, grid=(M // tm, N // tn, nk), in_specs=in_specs, out_specs=o_spec,
        out_shape=jax.ShapeDtypeStruct((M, N), out_dtype), scratch_shapes=[pltpu.VMEM((tm, tn), F32)],
        compiler_params=_cparams(("parallel", "parallel", "arbitrary")),
    )(*args)


def _stage_specs(acts, params, consts, tile, ct):
    act_specs = [pl.BlockSpec((tile, ct), functools.partial(lambda j, i, o: (i, o + j), o=off // ct)) for _, off in acts]
    par_specs = [pl.BlockSpec(bs, functools.partial(lambda j, i, im: im(j), im=im)) for _, bs, im in params]
    con_specs = [pl.BlockSpec(bs, functools.partial(lambda j, i, im: im(j), im=im)) for _, bs, im in consts]
    return act_specs, par_specs, con_specs


def _stage_fwd(f, name, n_rows, width, tile, ct, acts, params, consts, out_dtypes):
    for _, off in acts:
        assert off % ct == 0
    na, npar, nc = len(acts), len(params), len(consts)

    def kern(*refs):
        a = [r[...].astype(F32) for r in refs[:na]]
        p = [r[...] for r in refs[na:na + npar]]
        c = [r[...] for r in refs[na + npar:na + npar + nc]]
        outs = f(a, p, c, pl.program_id(1) * tile)
        for r, o in zip(refs[na + npar + nc:], outs):
            r[...] = o.astype(r.dtype)

    act_specs, par_specs, con_specs = _stage_specs(acts, params, consts, tile, ct)
    o_spec = pl.BlockSpec((tile, ct), lambda j, i: (i, j))
    outs = pl.pallas_call(
        kern, name=name, grid=(width // ct, n_rows // tile),
        in_specs=act_specs + par_specs + con_specs, out_specs=[o_spec] * len(out_dtypes),
        out_shape=[jax.ShapeDtypeStruct((n_rows, width), d) for d in out_dtypes],
        compiler_params=_cparams(("parallel", "parallel")),
    )(*[a for a, _ in acts], *[p for p, _, _ in params], *[c for c, _, _ in consts])
    return tuple(outs)


def _stage_bwd(f, name, n_rows, width, tile, ct, acts, params, consts, couts, dact_dtypes, extra_add=None):
    na, npar, nc, no = len(acts), len(params), len(consts), len(couts)
    nx = 0 if extra_add is None else 1

    def kern(*refs):
        a = [r[...].astype(F32) for r in refs[:na]]
        p = [r[...] for r in refs[na:na + npar]]
        c = [r[...] for r in refs[na + npar:na + npar + nc]]
        base = na + npar + nc
        co = [r[...].astype(F32) for r in refs[base:base + no]]
        base += no
        x_refs = refs[base:base + nx]
        base += nx
        da_refs = refs[base:base + na]
        dp_refs = refs[base + na:]
        row0 = pl.program_id(1) * tile
        _, vjp = jax.vjp(lambda aa, pp: tuple(f(aa, pp, c, row0)), a, p)
        da, dp = vjp(tuple(co))
        for k, (r, d) in enumerate(zip(da_refs, da)):
            if k == 0 and nx:
                d = d + x_refs[0][...].astype(F32)
            r[...] = d.astype(r.dtype)
        first = pl.program_id(1) == 0
        for r, d in zip(dp_refs, dp):
            @pl.when(first)
            def _(r=r, d=d):
                r[...] = d

            @pl.when(jnp.logical_not(first))
            def _(r=r, d=d):
                r[...] += d

    act_specs, par_specs, con_specs = _stage_specs(acts, params, consts, tile, ct)
    t_spec = pl.BlockSpec((tile, ct), lambda j, i: (i, j))
    co_specs = [pl.BlockSpec((tile, ct), functools.partial(lambda j, i, o: (i, o + j), o=off // ct)) for _, off in couts]
    x_specs = [] if extra_add is None else [pl.BlockSpec((tile, ct), functools.partial(lambda j, i, o: (i, o + j), o=extra_add[1] // ct))]
    x_args = [] if extra_add is None else [extra_add[0]]
    outs = pl.pallas_call(
        kern, name=name, grid=(width // ct, n_rows // tile),
        in_specs=act_specs + par_specs + con_specs + co_specs + x_specs,
        out_specs=[t_spec] * na + par_specs,
        out_shape=[jax.ShapeDtypeStruct((n_rows, width), d) for d in dact_dtypes]
        + [jax.ShapeDtypeStruct(p.shape, F32) for p, _, _ in params],
        compiler_params=_cparams(("parallel", "arbitrary")),
    )(*[a for a, _ in acts], *[p for p, _, _ in params], *[c for c, _, _ in consts], *[c for c, _ in couts], *x_args)
    return tuple(outs[:na]), tuple(outs[na:])


def _row(ct):
    return (1, ct), (lambda j: (0, j))


def _f_rmsnorm(a, p, c, row0):
    x, = a
    g, = p
    return (x * lax.rsqrt(jnp.mean(x * x, axis=-1, keepdims=True) + NORM_EPS) * g,)


def _f_lora_act(a, p, c, row0, widths):
    x, = a
    dl, al = widths
    col = lax.broadcasted_iota(jnp.int32, x.shape, 1)
    return (jnp.where(col < dl, jnp.tanh(x), jnp.where(col < dl + al, x, _sigmoid(x))),)


def _f_rwkv_pre(a, p, c, row0):
    k, wlin, alin = a
    w0, a0, k_k, k_a = p
    gsum, = c
    w = -_softplus(-(w0 + wlin)) - 0.5
    lw = -jnp.exp(w)
    alpha = _sigmoid(a0 + alin)
    kk = k * k_k
    ss = _nn(kk * kk, gsum, HI)
    kk = kk * lax.rsqrt(jnp.maximum(ss, 1e-24))
    k2 = k * (1.0 + (alpha - 1.0) * k_a)
    return lw, k2, -kk, kk * alpha


def _f_rwkv_post(a, p, c, row0):
    y, r, k2, v, g = a
    ln_g, ln_b, r_k = p
    gsum, = c
    inv = 1.0 / HEAD
    mean = _nn(y, gsum, HI) * inv
    yc = y - mean
    var = _nn(yc * yc, gsum, HI) * inv
    yn = yc * lax.rsqrt(var + GN_EPS) * ln_g + ln_b
    bonus = _nn(r * k2 * r_k, gsum, HI)
    return ((yn + bonus * v) * g,)


def _f_lru_gates(a, p, c, row0, seq):
    xc, = a
    wr, br, wi, bi, lam = p
    xb = xc.astype(BF16)
    rg = _sigmoid(_nn(xb, wr[0].astype(BF16)) + br)
    ig = _sigmoid(_nn(xb, wi[0].astype(BF16)) + bi)
    log_a = -LRU_C * rg * _softplus(-lam)
    a_t = jnp.exp(log_a)
    mult = jnp.sqrt(_neg_expm1(2.0 * log_a))
    row = row0 + lax.broadcasted_iota(jnp.int32, xc.shape, 0)
    mult = jnp.where(row % seq == 0, 1.0, mult)
    return a_t, mult * ig * xc


def _f_lru_post(a, p, c, row0):
    h, gate = a
    g, = p
    y = h * _gelu(gate)
    return (y * lax.rsqrt(jnp.mean(y * y, axis=-1, keepdims=True) + NORM_EPS) * g,)


def _f_swiglu(a, p, c, row0):
    gate, up = a
    return (gate * _sigmoid(gate) * up,)


def _shift_down(x, s, row):
    return jnp.where(row >= s, pltpu.roll(x, s, 0), 0.0)


def _shift_up(x, s, row):
    n = x.shape[0]
    return jnp.where(row < n - s, pltpu.roll(x, n - s, 0), 0.0)


def _seq_call(kern, name, bl, seq, width, ct, ins, outs, acc_outs=()):
    def spec(off, rows):
        if rows is None:
            return pl.BlockSpec((seq, ct), functools.partial(lambda j, b, o: (b, o + j), o=off // ct))
        return pl.BlockSpec((rows, ct), lambda j, b: (0, j))

    in_specs = [spec(off, rows) for _, off, rows in ins]
    out_specs = [spec(0, None) for _ in outs] + [spec(0, rows) for _, rows in acc_outs]
    out_shape = [jax.ShapeDtypeStruct((bl * seq, width), d) for d in outs] + [jax.ShapeDtypeStruct((rows, width), F32) for _, rows in acc_outs]
    res = pl.pallas_call(
        kern, name=name, grid=(width // ct, bl), in_specs=in_specs, out_specs=out_specs, out_shape=out_shape,
        compiler_params=_cparams(("parallel", "arbitrary")),
    )(*[a for a, _, _ in ins])
    return tuple(res)


def _acc(ref, val):
    first = pl.program_id(1) == 0

    @pl.when(first)
    def _():
        ref[...] = val

    @pl.when(jnp.logical_not(first))
    def _():
        ref[...] += val


def _lerp_fwd(p, off, mu, bl, seq, width, ct):
    def kern(p_ref, mu_ref, o_ref):
        x = p_ref[...]
        row = lax.broadcasted_iota(jnp.int32, x.shape, 0)
        o_ref[...] = x + (_shift_down(x, 1, row) - x) * mu_ref[...]

    return _seq_call(kern, "lerp_fwd", bl, seq, width, ct, [(p, off, None), (mu, 0, 1)], [F32])[0]


def _lerp_bwd(p, off, mu, dps, bl, seq, width, ct, out_dtype):
    def kern(p_ref, mu_ref, d_ref, dp_ref, dmu_ref):
        x = p_ref[...]
        d = d_ref[...].astype(F32)
        m = mu_ref[...]
        row = lax.broadcasted_iota(jnp.int32, x.shape, 0)
        dp_ref[...] = (d * (1.0 - m) + _shift_up(d * m, 1, row)).astype(dp_ref.dtype)
        _acc(dmu_ref, jnp.sum(d * (_shift_down(x, 1, row) - x), axis=0, keepdims=True))

    return _seq_call(kern, "lerp_bwd", bl, seq, width, ct, [(p, off, None), (mu, 0, 1), (dps, 0, None)], [out_dtype], [(None, 1)])


def _conv_fwd(p, off, cw, cb, bl, seq, width, ct):
    nw = cw.shape[0]

    def kern(x_ref, w_ref, b_ref, o_ref):
        x = x_ref[...]
        row = lax.broadcasted_iota(jnp.int32, x.shape, 0)
        acc = b_ref[...] + x * w_ref[pl.ds(nw - 1, 1), :]
        for s in range(1, nw):
            acc = acc + _shift_down(x, s, row) * w_ref[pl.ds(nw - 1 - s, 1), :]
        o_ref[...] = acc

    return _seq_call(kern, "conv_fwd", bl, seq, width, ct, [(p, off, None), (cw, 0, nw), (cb, 0, 1)], [F32])[0]


def _conv_bwd(p, off, cw, dxc, bl, seq, width, ct, out_dtype):
    nw = cw.shape[0]

    def kern(x_ref, w_ref, d_ref, dx_ref, dw_ref, db_ref):
        x = x_ref[...]
        d = d_ref[...]
        row = lax.broadcasted_iota(jnp.int32, x.shape, 0)
        wrow = lax.broadcasted_iota(jnp.int32, dw_ref.shape, 0)
        dx = d * w_ref[pl.ds(nw - 1, 1), :]
        dw = jnp.where(wrow == nw - 1, jnp.sum(d * x, axis=0, keepdims=True), 0.0)
        for s in range(1, nw):
            dx = dx + _shift_up(d, s, row) * w_ref[pl.ds(nw - 1 - s, 1), :]
            dw = jnp.where(wrow == nw - 1 - s, jnp.sum(d * _shift_down(x, s, row), axis=0, keepdims=True), dw)
        dx_ref[...] = dx.astype(dx_ref.dtype)
        _acc(dw_ref, dw)
        _acc(db_ref, jnp.sum(d, axis=0, keepdims=True))

    return _seq_call(kern, "conv_bwd", bl, seq, width, ct, [(p, off, None), (cw, 0, nw), (dxc, 0, None)], [out_dtype], [(None, nw), (None, 1)])


def _lru_scan_fwd(a, bx, bl, seq, width, ct):
    def kern(a_ref, b_ref, h_ref):
        av = a_ref[...]
        bv = b_ref[...]
        row = lax.broadcasted_iota(jnp.int32, av.shape, 0)
        d = 1
        while d < seq:
            a_sh = jnp.where(row >= d, pltpu.roll(av, d, 0), 1.0)
            b_sh = jnp.where(row >= d, pltpu.roll(bv, d, 0), 0.0)
            bv = av * b_sh + bv
            av = av * a_sh
            d *= 2
        h_ref[...] = bv

    return _seq_call(kern, "lru_scan_fwd", bl, seq, width, ct, [(a, 0, None), (bx, 0, None)], [F32])[0]


def _lru_scan_bwd(a, h, dh, bl, seq, width, ct):
    def kern(a_ref, h_ref, d_ref, da_ref, db_ref):
        row = lax.broadcasted_iota(jnp.int32, a_ref.shape, 0)
        al = _shift_up(a_ref[...], 1, row)
        g = d_ref[...]
        d = 1
        while d < seq:
            keep = row < seq - d
            al_sh = jnp.where(keep, pltpu.roll(al, seq - d, 0), 1.0)
            g_sh = jnp.where(keep, pltpu.roll(g, seq - d, 0), 0.0)
            g = al * g_sh + g
            al = al * al_sh
            d *= 2
        db_ref[...] = g
        da_ref[...] = g * _shift_down(h_ref[...], 1, row)

    return _seq_call(kern, "lru_scan_bwd", bl, seq, width, ct, [(a, 0, None), (h, 0, None), (dh, 0, None)], [F32, F32])


def _scan_chunk(S0, r, lw, k, v, a, b):
    C = r.shape[0]
    ri = lax.broadcasted_iota(jnp.int32, (C, C), 0)
    ci = lax.broadcasted_iota(jnp.int32, (C, C), 1)
    incl = ri >= ci
    strict = ri > ci
    eye = (ri == ci).astype(F32)
    lane = lax.broadcasted_iota(jnp.int32, (1, LANE), 1)
    hm = [(lane < HEAD).astype(F32), (lane >= HEAD).astype(F32)]
    bi = lax.broadcasted_iota(jnp.int32, (LANE, LANE), 0) // HEAD
    bj = lax.broadcasted_iota(jnp.int32, (LANE, LANE), 1) // HEAD
    bd = (bi == bj).astype(F32)

    cs = _nn(incl.astype(F32), lw, HI)
    p_incl = jnp.exp(cs)
    p_inv = jnp.exp(-cs)
    rt = r * p_incl
    at = a * jnp.exp(cs - lw)
    bt = b * p_inv
    kt = k * p_inv
    a_s0 = _nt(at, S0, HI)
    y = _nt(rt, S0, HI)
    u = jnp.zeros_like(r)
    per_head = []
    for m in hm:
        ah = at * m
        rh = rt * m
        a_ab = jnp.where(strict, _nt(ah, bt, HI), 0.0)
        a_ak = jnp.where(strict, _nt(ah, kt, HI), 0.0)
        x = eye + a_ab
        pw = a_ab
        n = 2
        while n < C:
            pw = _nn(pw, pw, HI)
            x = x + _nn(x, pw, HI)
            n *= 2
        uh = _nn(x, a_s0 + _nn(a_ak, v, HI), HI)
        u = u + uh * m
        per_head.append((rh, m))
    for rh, m in per_head:
        r_b = jnp.where(incl, _nt(rh, bt, HI), 0.0)
        r_k = jnp.where(incl, _nt(rh, kt, HI), 0.0)
        y = y + (_nn(r_b, u, HI) + _nn(r_k, v, HI)) * m
    p_last = jnp.exp(jnp.sum(lw, axis=0, keepdims=True))
    s_new = (S0 + (_tn(u, bt, HI) + _tn(v, kt, HI)) * bd) * p_last
    return y, s_new


def _rwkv_scan_fwd(r, lw, k2, v, na, bb, p, bl, seq, rw):
    C = _pick(seq, (SCAN_CHUNK,))
    nc, nhp = seq // C, rw // LANE

    def kern(r_ref, lw_ref, k_ref, v_ref, a_ref, b_ref, y_ref, st_ref, s_scr):
        @pl.when(pl.program_id(2) == 0)
        def _():
            s_scr[...] = jnp.zeros_like(s_scr)

        s0 = s_scr[...]
        st_ref[...] = s0
        y, s1 = _scan_chunk(s0, r_ref[...], lw_ref[...], k_ref[...], v_ref[...], a_ref[...], b_ref[...])
        y_ref[...] = y
        s_scr[...] = s1

    def tok(off):
        return pl.BlockSpec((C, LANE), functools.partial(lambda b, h, c, o: (b * nc + c, o + h), o=off // LANE))

    y, st = pl.pallas_call(
        kern, name="rwkv_scan_fwd", grid=(bl, nhp, nc),
        in_specs=[tok(0), tok(0), tok(0), tok(2 * rw), tok(0), tok(0)],
        out_specs=[tok(0), pl.BlockSpec((LANE, LANE), lambda b, h, c: ((b * nhp + h) * nc + c, 0))],
        out_shape=[jax.ShapeDtypeStruct((bl * seq, rw), F32), jax.ShapeDtypeStruct((bl * nhp * nc * LANE, LANE), F32)],
        scratch_shapes=[pltpu.VMEM((LANE, LANE), F32)],
        compiler_params=_cparams(("parallel", "parallel", "arbitrary")),
    )(p, lw, k2, p, na, bb)
    return y, st


def _rwkv_scan_bwd(lw, k2, na, bb, p, st, dy, bl, seq, rw):
    C = _pick(seq, (SCAN_CHUNK,))
    nc, nhp = seq // C, rw // LANE

    def kern(r_ref, lw_ref, k_ref, v_ref, a_ref, b_ref, st_ref, dy_ref, dr_ref, dlw_ref, dk_ref, dv_ref, da_ref, db_ref, ds_scr):
        @pl.when(pl.program_id(2) == 0)
        def _():
            ds_scr[...] = jnp.zeros_like(ds_scr)

        args = (st_ref[...], r_ref[...], lw_ref[...], k_ref[...], v_ref[...], a_ref[...], b_ref[...])
        _, vjp = jax.vjp(_scan_chunk, *args)
        ds0, dr, dlw, dk, dv, da, db = vjp((dy_ref[...], ds_scr[...]))
        ds_scr[...] = ds0
        dr_ref[...] = dr
        dlw_ref[...] = dlw
        dk_ref[...] = dk
        dv_ref[...] = dv
        da_ref[...] = da
        db_ref[...] = db

    def tok(off):
        return pl.BlockSpec((C, LANE), functools.partial(lambda b, h, c, o: (b * nc + (nc - 1 - c), o + h), o=off // LANE))

    st_spec = pl.BlockSpec((LANE, LANE), lambda b, h, c: ((b * nhp + h) * nc + (nc - 1 - c), 0))
    return pl.pallas_call(
        kern, name="rwkv_scan_bwd", grid=(bl, nhp, nc),
        in_specs=[tok(0), tok(0), tok(0), tok(2 * rw), tok(0), tok(0), st_spec, tok(0)],
        out_specs=[tok(0)] * 6,
        out_shape=[jax.ShapeDtypeStruct((bl * seq, rw), F32)] * 6,
        scratch_shapes=[pltpu.VMEM((LANE, LANE), F32)],
        compiler_params=_cparams(("parallel", "parallel", "arbitrary")),
    )(p, lw, k2, p, na, bb, st, dy)


def _loss_head(h2, g_final, target, tile):
    n, d = h2.shape
    nt = n // tile

    def kern(h_ref, g_ref, t_ref, dh_ref, dg_ref, l_ref):
        def f(h, g):
            y = h * lax.rsqrt(jnp.mean(h * h, axis=-1, keepdims=True) + NORM_EPS) * g
            e = y - t_ref[...]
            return 0.5 * jnp.sum(jnp.mean(e * e, axis=-1, keepdims=True))

        loss, (dh, dg) = jax.value_and_grad(f, argnums=(0, 1))(h_ref[...], g_ref[...])
        dh_ref[...] = dh
        first = pl.program_id(0) == 0

        @pl.when(first)
        def _():
            dg_ref[...] = dg
            l_ref[...] = jnp.zeros_like(l_ref) + loss

        @pl.when(jnp.logical_not(first))
        def _():
            dg_ref[...] += dg
            l_ref[...] += loss

    row = pl.BlockSpec((tile, d), lambda i: (i, 0))
    vec = pl.BlockSpec((1, d), lambda i: (0, 0))
    return pl.pallas_call(
        kern, name="loss_head", grid=(nt,), in_specs=[row, vec, row],
        out_specs=[row, vec, pl.BlockSpec((1, LANE), lambda i: (0, 0))],
        out_shape=[jax.ShapeDtypeStruct((n, d), F32), jax.ShapeDtypeStruct((1, d), F32), jax.ShapeDtypeStruct((1, LANE), F32)],
        compiler_params=_cparams(("arbitrary",)),
    )(h2, g_final, target)


def _adamw(parts, w, m, v, name):
    n_parts, R, Cc = parts.shape
    tr = _pick(R, tuple(t for t in (1024, 512, 256, 128, 64, 32, 16) if t * Cc <= 128 * 1024) + (8,))
    c1, c2 = 1.0 - ADAM_B1, 1.0 - ADAM_B2
    bc1, bc2 = 1.0 - ADAM_B1 ** ADAM_STEP, 1.0 - ADAM_B2 ** ADAM_STEP

    def kern(p_ref, w_ref, m_ref, v_ref, g_ref, d_ref, nm_ref, nv_ref):
        g = p_ref[0].astype(F32)
        for s in range(1, n_parts):
            g = g + p_ref[s].astype(F32)
        m2 = ADAM_B1 * m_ref[...] + c1 * g
        v2 = ADAM_B2 * v_ref[...] + c2 * (g * g)
        g_ref[...] = g
        nm_ref[...] = m2
        nv_ref[...] = v2
        d_ref[...] = -ADAM_LR * ((m2 / bc1) / (jnp.sqrt(v2 / bc2) + ADAM_EPS) + ADAM_WD * w_ref[...])

    blk = pl.BlockSpec((tr, Cc), lambda i: (i, 0))
    return pl.pallas_call(
        kern, name=name, grid=(R // tr,),
        in_specs=[pl.BlockSpec((n_parts, tr, Cc), lambda i: (0, i, 0)), blk, blk, blk],
        out_specs=[blk] * 4, out_shape=[jax.ShapeDtypeStruct((R, Cc), F32)] * 4,
        compiler_params=_cparams(("parallel",)),
    )(parts, w, m, v)


def _sum_parts(parts, name):
    n_parts, R, Cc = parts.shape
    tr = _pick(R, (512, 256, 128, 64, 32, 16, 8))

    def kern(p_ref, o_ref):
        g = p_ref[0]
        for s in range(1, n_parts):
            g = g + p_ref[s]
        o_ref[...] = g

    return pl.pallas_call(
        kern, name=name, grid=(R // tr,), in_specs=[pl.BlockSpec((n_parts, tr, Cc), lambda i: (0, i, 0))],
        out_specs=pl.BlockSpec((tr, Cc), lambda i: (i, 0)), out_shape=jax.ShapeDtypeStruct((R, Cc), F32),
        compiler_params=_cparams(("parallel",)),
    )(parts)


def _coords():
    return lax.axis_index("x"), lax.axis_index("y"), lax.axis_index("c")


def _all_gather(xs, name):
    n = len(xs)

    def body(*refs):
        x_refs, o_refs = refs[:n], refs[n:2 * n]
        send_sems, recv_sems, local_sems = refs[2 * n:]
        x, y, c = _coords()
        me, sibling = (x, y, c), (x, y, 1 - c)
        chips = [(1 - x, y), (x, 1 - y), (1 - x, 1 - y)]
        waits = []
        for t in range(n):
            o_ref = o_refs[t]

            def slot(px, py, pc, o_ref=o_ref):
                return o_ref.at[4 * px + 2 * py + pc]

            def copy(k, block, to, src=None, t=t, slot=slot):
                return pltpu.make_async_remote_copy(
                    src_ref=slot(*block) if src is None else src, dst_ref=slot(*block),
                    send_sem=send_sems.at[t, k], recv_sem=recv_sems.at[t, k], device_id=to, device_id_type=MESH)

            mine = pltpu.make_async_copy(x_refs[t], slot(*me), local_sems.at[t])
            mine.start()
            first = [copy(0, me, sibling, src=x_refs[t])]
            first += [copy(1 + j, me, (*chip, c), src=x_refs[t]) for j, chip in enumerate(chips)]
            for cp in first:
                cp.start()
            waits.append((copy, mine, first))
        for t in range(n):
            copy, mine, first = waits[t]
            passed = [copy(4 + j, (*chip, c), sibling) for j, chip in enumerate(chips)]
            for j, chip in enumerate(chips):
                copy(1 + j, (*chip, c), me).wait_recv()
                passed[j].start()
            copy(0, sibling, me).wait_recv()
            for j, chip in enumerate(chips):
                copy(4 + j, (*chip, 1 - c), me).wait_recv()
            for cp in first + passed:
                cp.wait_send()
            mine.wait()

    any_spec = pl.BlockSpec(memory_space=pl.ANY)
    outs = pl.pallas_call(
        body, name=name, in_specs=[any_spec] * n, out_specs=[any_spec] * n,
        out_shape=[jax.ShapeDtypeStruct((N_DEV,) + a.shape, a.dtype) for a in xs],
        scratch_shapes=[pltpu.SemaphoreType.DMA((n, 7)), pltpu.SemaphoreType.DMA((n, 7)), pltpu.SemaphoreType.DMA((n,))],
    )(*xs)
    return list(outs)


def _exchange(xs, name):
    n = len(xs)

    def body(*refs):
        x_refs, o_refs = refs[:n], refs[n:2 * n]
        send_sems, recv_sems, local_sems = refs[2 * n:]
        x, y, c = _coords()
        my = 4 * x + 2 * y + c
        copies = []
        for t in range(n):
            mine = pltpu.make_async_copy(x_refs[t].at[my], o_refs[t].at[my], local_sems.at[t])
            mine.start()
            copies.append(mine)
        rem = []
        for k in range(1, N_DEV):
            fx, fy, fc = (k >> 2) & 1, (k >> 1) & 1, k & 1
            px, py, pc = x ^ fx, y ^ fy, c ^ fc
            peer = 4 * px + 2 * py + pc
            for t in range(n):
                cp = pltpu.make_async_remote_copy(
                    src_ref=x_refs[t].at[peer], dst_ref=o_refs[t].at[my],
                    send_sem=send_sems.at[t, k - 1], recv_sem=recv_sems.at[t, k - 1],
                    device_id=(px, py, pc), device_id_type=MESH)
                cp.start()
                rem.append(cp)
        for cp in rem:
            cp.wait()
        for cp in copies:
            cp.wait()

    any_spec = pl.BlockSpec(memory_space=pl.ANY)
    outs = pl.pallas_call(
        body, name=name, in_specs=[any_spec] * n, out_specs=[any_spec] * n,
        out_shape=[jax.ShapeDtypeStruct(a.shape, a.dtype) for a in xs],
        scratch_shapes=[pltpu.SemaphoreType.DMA((n, 7)), pltpu.SemaphoreType.DMA((n, 7)), pltpu.SemaphoreType.DMA((n,))],
    )(*xs)
    return list(outs)


def _cols_from_shards(g):
    return jnp.transpose(g, (1, 0, 2)).reshape(g.shape[1], N_DEV * g.shape[2])


def _shards_from_cols(w):
    r, n = w.shape
    return jnp.transpose(w.reshape(r, N_DEV, n // N_DEV), (1, 0, 2))


def _pad_cols(w, to):
    return jnp.pad(w, ((0, 0), (0, to - w.shape[1])))


def _pack(arrs):
    flat = jnp.concatenate([a.reshape(-1) for a in arrs])
    n = _rup(flat.shape[0], 8 * LANE)
    return jnp.pad(flat, (0, n - flat.shape[0])).reshape(n // LANE, LANE)


def _unpack(mat, shapes):
    flat = mat.reshape(-1)
    out, o = [], 0
    for s in shapes:
        n = math.prod(s)
        out.append(flat[o:o + n].reshape(s))
        o += n
    return out


_SMALL = ["norm_mix_g", "mu_shift", "rwkv_w0", "rwkv_a0", "rwkv_k_k", "rwkv_k_a", "rwkv_r_k", "rwkv_ln_g", "rwkv_ln_b", "conv_b",
          "lru_wr", "lru_br", "lru_wi", "lru_bi", "lru_lambda", "lru_norm_g", "norm_ffn_g", "norm_final_g"]
_SMALL_SHARDED = ["rwkv_w2", "rwkv_a2", "rwkv_g2", "conv_w"]
_BIG = ["w_in", "w_out", "ffn_w_gate", "ffn_w_up", "ffn_w_down"]
_WEIGHTS = ['norm_mix_g', 'w_in', 'mu_shift', 'rwkv_w0', 'rwkv_w2', 'rwkv_a0', 'rwkv_a2', 'rwkv_g2', 'rwkv_k_k', 'rwkv_k_a', 'rwkv_r_k',
            'rwkv_ln_g', 'rwkv_ln_b', 'conv_w', 'conv_b', 'lru_wr', 'lru_br', 'lru_wi', 'lru_bi', 'lru_lambda', 'lru_norm_g', 'w_out',
            'norm_ffn_g', 'ffn_w_gate', 'ffn_w_up', 'ffn_w_down', 'norm_final_g']


def _step(W, M, V, x, loss_target):
    bl, seq, d = x.shape
    n = bl * seq
    rw = W["rwkv_w0"].shape[1]
    nh = W["rwkv_r_k"].shape[1]
    assert W["rwkv_r_k"].shape[2] == HEAD and nh * HEAD == rw and rw % LANE == 0
    dl, al, gl = W["rwkv_w2"].shape[1], W["rwkv_a2"].shape[1], W["rwkv_g2"].shape[1]
    dlp, alp, glp = _rup(dl, LANE), _rup(al, LANE), _rup(gl, LANE)
    lorap = dlp + alp + glp
    lw_ = W["conv_b"].shape[1]
    nblk, lbw = W["lru_wr"].shape[1], W["lru_wr"].shape[2]
    assert lbw == LANE and nblk * lbw == lw_
    o_xb, o_gate, o_rw = 0, lw_, 2 * lw_
    o_lora = 3 * rw
    rwp = o_lora + lorap
    inp = o_rw + rwp
    nsh_ff = W["ffn_w_gate"].shape[2]
    dff = N_DEV * nsh_ff
    dffp = _rup(dff, 1024) if dff >= 1024 else _rup(dff, LANE)
    x2 = x.reshape(n, d)
    tgt2 = loss_target.reshape(n, d)

    small_sh = _pack([W[k][0] for k in _SMALL_SHARDED])
    g_in, g_out, g_gate, g_up, g_down, g_small = _all_gather(
        [W["w_in"][0].astype(BF16), W["w_out"][0].astype(BF16), W["ffn_w_gate"][0].astype(BF16), W["ffn_w_up"][0].astype(BF16),
         W["ffn_w_down"][0].astype(BF16), small_sh], "gather_weights")
    w_in_l = _cols_from_shards(g_in)
    o1 = 3 * rw
    w_in = jnp.concatenate([w_in_l[:, o1 + dl + al + gl:], w_in_l[:, :o1], _pad_cols(w_in_l[:, o1:o1 + dl], dlp),
                            _pad_cols(w_in_l[:, o1 + dl:o1 + dl + al], alp), _pad_cols(w_in_l[:, o1 + dl + al:o1 + dl + al + gl], glp)], axis=1)
    w_out = g_out.reshape(N_DEV * g_out.shape[1], d)
    w_gu = jnp.concatenate([_pad_cols(_cols_from_shards(g_gate), dffp), _pad_cols(_cols_from_shards(g_up), dffp)], axis=1)
    w_down = jnp.pad(g_down.reshape(dff, d), ((0, dffp - dff), (0, 0)))
    sm_shapes = [W[k][0].shape for k in _SMALL_SHARDED]
    sm = [_unpack(g_small[s], sm_shapes) for s in range(N_DEV)]
    w2, a2, g2, conv_w = [jnp.concatenate([sm[s][i] for s in range(N_DEV)], axis=1) for i in range(4)]
    w_lora = jnp.zeros((lorap, 3 * rw), F32)
    w_lora = w_lora.at[:dl, :rw].set(w2).at[dlp:dlp + al, rw:2 * rw].set(a2).at[dlp + alp:dlp + alp + gl, 2 * rw:].set(g2)
    w_lora = w_lora.astype(BF16)
    mu_l = W["mu_shift"]
    mu = jnp.concatenate([mu_l[:, :o1], _pad_cols(mu_l[:, o1:o1 + dl], dlp), _pad_cols(mu_l[:, o1 + dl:o1 + dl + al], alp),
                          _pad_cols(mu_l[:, o1 + dl + al:], glp)], axis=1)
    r_k = W["rwkv_r_k"].reshape(1, rw)

    tile = _pick(n, (256, 128, 64))
    tile_s = _pick(n, (128, 64))
    ct_seq = _pick(math.gcd(rwp, lw_), (256, 128))
    assert o_rw % ct_seq == 0 and o_gate % lw_ == 0
    ct_h = _pick(rw, (512, 256, 128))
    gi = lax.broadcasted_iota(jnp.int32, (ct_h, ct_h), 0) // HEAD
    gj = lax.broadcasted_iota(jnp.int32, (ct_h, ct_h), 1) // HEAD
    gsum = ((gi == gj).astype(F32), (ct_h, ct_h), lambda j: (0, 0))
    full = lambda a: (a, a.shape, lambda j: (0,) * a.ndim)
    rowp = lambda a, ct: (a,) + _row(ct)

    u1, = _stage_fwd(_f_rmsnorm, "norm_mix_fwd", n, d, tile, d, [(x2, 0)], [full(W["norm_mix_g"])], [], [BF16])
    p = _mm(u1, w_in, name="mm_in")
    ps = _lerp_fwd(p, o_rw, mu, bl, seq, rwp, ct_seq)
    f_lora = functools.partial(_f_lora_act, widths=(dlp, alp))
    lact, = _stage_fwd(f_lora, "lora_act_fwd", n, lorap, tile, lorap, [(ps, o_lora)], [], [], [BF16])
    wag = _mm(lact, w_lora, name="mm_lora")
    pre_par = [rowp(W["rwkv_w0"], ct_h), rowp(W["rwkv_a0"], ct_h), rowp(W["rwkv_k_k"], ct_h), rowp(W["rwkv_k_a"], ct_h)]
    pre_acts = [(ps, rw), (wag, 0), (wag, rw)]
    lw, k2, na, bb = _stage_fwd(_f_rwkv_pre, "rwkv_pre_fwd", n, rw, tile_s, ct_h, pre_acts, pre_par, [gsum], [F32] * 4)
    ysc, st = _rwkv_scan_fwd(None, lw, k2, None, na, bb, ps, bl, seq, rw)
    post_par = [rowp(W["rwkv_ln_g"], ct_h), rowp(W["rwkv_ln_b"], ct_h), rowp(r_k, ct_h)]
    post_acts = [(ysc, 0), (ps, 0), (k2, 0), (ps, 2 * rw), (wag, 2 * rw)]
    ya, = _stage_fwd(_f_rwkv_post, "rwkv_post_fwd", n, rw, tile_s, ct_h, post_acts, post_par, [gsum], [BF16])

    xc = _conv_fwd(p, o_xb, conv_w, W["conv_b"], bl, seq, lw_, ct_seq)
    f_gates = functools.partial(_f_lru_gates, seq=seq)
    blk3 = lambda a: (a[0], (1, LANE, LANE), lambda j: (j, 0, 0))
    gate_par = [blk3(W["lru_wr"]), rowp(W["lru_br"], LANE), blk3(W["lru_wi"]), rowp(W["lru_bi"], LANE), rowp(W["lru_lambda"], LANE)]
    a_l, bx = _stage_fwd(f_gates, "lru_gates_fwd", n, lw_, tile, LANE, [(xc, 0)], gate_par, [], [F32, F32])
    ct_l = _pick(lw_, (256, 128))
    h_l = _lru_scan_fwd(a_l, bx, bl, seq, lw_, ct_l)
    lpost_par = [full(W["lru_norm_g"])]
    yb, = _stage_fwd(_f_lru_post, "lru_post_fwd", n, lw_, tile_s, lw_, [(h_l, 0), (p, o_gate)], lpost_par, [], [BF16])

    ycat = jnp.concatenate([ya, yb], axis=1)
    h1 = _mm(ycat, w_out, name="mm_out", add=x2)
    u2, = _stage_fwd(_f_rmsnorm, "norm_ffn_fwd", n, d, tile, d, [(h1, 0)], [full(W["norm_ffn_g"])], [], [BF16])
    gu = _mm(u2, w_gu, name="mm_gu")
    ct_f = _pick(dffp, (1024, 512, 256, 128))
    act, = _stage_fwd(_f_swiglu, "swiglu_fwd", n, dffp, tile, ct_f, [(gu, 0), (gu, dffp)], [], [], [BF16])
    h2 = _mm(act, w_down, name="mm_down", add=h1)

    dh2, dg_final, lsum = _loss_head(h2, W["norm_final_g"].reshape(1, d), tgt2, tile_s)
    loss = lax.psum(lsum[0, 0], ("x", "y", "c"))
    dh2b = dh2.astype(BF16)
    dact = _mm(dh2b, w_down, name="mm_dact", tb=True, out_dtype=BF16)
    dw_down = _mm(act, dh2b, name="mm_dw_down", ta=True)
    (dgate, dup), _ = _stage_bwd(_f_swiglu, "swiglu_bwd", n, dffp, tile, ct_f, [(gu, 0), (gu, dffp)], [], [], [(dact, 0)], [BF16, BF16])
    dgu = jnp.concatenate([dgate, dup], axis=1)
    du2 = _mm(dgu, w_gu, name="mm_du2", tb=True)
    dw_gu = _mm(u2, dgu, name="mm_dw_gu", ta=True)
    (dh1,), (dg_ffn,) = _stage_bwd(_f_rmsnorm, "norm_ffn_bwd", n, d, tile_s, d, [(h1, 0)], [full(W["norm_ffn_g"])], [], [(du2, 0)], [F32],
                                   extra_add=(dh2, 0))
    dh1b = dh1.astype(BF16)
    dycat = _mm(dh1b, w_out, name="mm_dycat", tb=True)
    dw_out = _mm(ycat, dh1b, name="mm_dw_out", ta=True)

    (dysc, dr_p, dk2_p, dv_p, dg_g), (dln_g, dln_b, dr_k) = _stage_bwd(
        _f_rwkv_post, "rwkv_post_bwd", n, rw, tile_s, ct_h, post_acts, post_par, [gsum], [(dycat, 0)], [F32] * 5)
    dr_s, dlw, dk2_s, dv_s, dna, dbb = _rwkv_scan_bwd(lw, k2, na, bb, ps, st, dysc, bl, seq, rw)
    dk2 = dk2_p + dk2_s
    (dk, dwlin, dalin), (dw0, da0, dk_k, dk_a) = _stage_bwd(
        _f_rwkv_pre, "rwkv_pre_bwd", n, rw, tile_s, ct_h, pre_acts, pre_par, [gsum], [(dlw, 0), (dk2, 0), (dna, 0), (dbb, 0)], [F32] * 3)
    dwag = jnp.concatenate([dwlin, dalin, dg_g], axis=1).astype(BF16)
    dlact = _mm(dwag, w_lora, name="mm_dlact", tb=True)
    dw_lora = _mm(lact, dwag, name="mm_dw_lora", ta=True)
    (dps_lora,), _ = _stage_bwd(f_lora, "lora_act_bwd", n, lorap, tile, lorap, [(ps, o_lora)], [], [], [(dlact, 0)], [F32])
    dps = jnp.concatenate([dr_p + dr_s, dk, dv_p + dv_s, dps_lora], axis=1)
    dp_rwkv, dmu = _lerp_bwd(p, o_rw, mu, dps, bl, seq, rwp, ct_seq, BF16)

    (dh_l, dgate_l), (dlru_norm_g,) = _stage_bwd(_f_lru_post, "lru_post_bwd", n, lw_, tile_s, lw_, [(h_l, 0), (p, o_gate)], lpost_par, [],
                                                 [(dycat, rw)], [F32, BF16])
    da_l, dbx = _lru_scan_bwd(a_l, h_l, dh_l, bl, seq, lw_, ct_l)
    (dxc,), (dwr, dbr, dwi, dbi, dlam) = _stage_bwd(f_gates, "lru_gates_bwd", n, lw_, tile, LANE, [(xc, 0)], gate_par, [],
                                                    [(da_l, 0), (dbx, 0)], [F32])
    dxb, dconv_w, dconv_b = _conv_bwd(p, o_xb, conv_w, dxc, bl, seq, lw_, ct_seq, BF16)

    dp = jnp.concatenate([dxb, dgate_l, dp_rwkv], axis=1)
    du1 = _mm(dp, w_in, name="mm_du1", tb=True)
    dw_in = _mm(u1, dp, name="mm_dw_in", ta=True)
    (grad_x,), (dg_mix,) = _stage_bwd(_f_rmsnorm, "norm_mix_bwd", n, d, tile_s, d, [(x2, 0)], [full(W["norm_mix_g"])], [], [(du1, 0)], [F32],
                                      extra_add=(dh1, 0))

    ol = o_rw + o_lora
    dw_in_l = jnp.concatenate([dw_in[:, o_rw:ol], dw_in[:, ol:ol + dl], dw_in[:, ol + dlp:ol + dlp + al],
                               dw_in[:, ol + dlp + alp:ol + dlp + alp + gl], dw_in[:, :o_rw]], axis=1)
    send = [_shards_from_cols(dw_in_l).astype(BF16), dw_out.reshape(N_DEV, -1, d).astype(BF16),
            _shards_from_cols(dw_gu[:, :dff]).astype(BF16), _shards_from_cols(dw_gu[:, dffp:dffp + dff]).astype(BF16),
            dw_down[:dff].reshape(N_DEV, nsh_ff, d).astype(BF16)]
    recv = _exchange(send, "exchange_grads")
    out = {}
    for k, parts in zip(_BIG, recv):
        out[k] = [o[None] for o in _adamw(parts, W[k][0], M[k][0], V[k][0], "adamw_" + k)]

    dmu_l = jnp.concatenate([dmu[:, :o1], dmu[:, o_lora:o_lora + dl], dmu[:, o_lora + dlp:o_lora + dlp + al],
                             dmu[:, o_lora + dlp + alp:o_lora + dlp + alp + gl]], axis=1)
    small_g = {"norm_mix_g": dg_mix, "mu_shift": dmu_l, "rwkv_w0": dw0, "rwkv_a0": da0, "rwkv_k_k": dk_k, "rwkv_k_a": dk_a,
               "rwkv_r_k": dr_k.reshape(W["rwkv_r_k"].shape), "rwkv_ln_g": dln_g, "rwkv_ln_b": dln_b, "conv_b": dconv_b,
               "lru_wr": dwr[None], "lru_br": dbr, "lru_wi": dwi[None], "lru_bi": dbi, "lru_lambda": dlam, "lru_norm_g": dlru_norm_g,
               "norm_ffn_g": dg_ffn, "norm_final_g": dg_final.reshape(W["norm_final_g"].shape)}
    sh_full = [dw_lora[:dl, :rw], dw_lora[dlp:dlp + al, rw:2 * rw], dw_lora[dlp + alp:dlp + alp + gl, 2 * rw:], dconv_w]
    names = _SMALL + _SMALL_SHARDED
    shapes = [W[k].shape for k in _SMALL] + [a.shape for a in sh_full]
    gpack = _pack([small_g[k] for k in _SMALL] + sh_full)
    gall, = _all_gather([gpack], "gather_small_grads")
    x_i, y_i, c_i = _coords()
    me = 4 * x_i + 2 * y_i + c_i
    g_rep = dict(zip(names, _unpack(_sum_parts(gall, "sum_small_grads"), shapes)))
    wp = _pack([W[k] for k in _SMALL] + [W[k][0] for k in _SMALL_SHARDED])
    mp = _pack([M[k] for k in _SMALL] + [M[k][0] for k in _SMALL_SHARDED])
    vp = _pack([V[k] for k in _SMALL] + [V[k][0] for k in _SMALL_SHARDED])
    g_mine = [g_rep[k] for k in _SMALL]
    for k in _SMALL_SHARDED:
        nsh = W[k].shape[2]
        g_mine.append(lax.dynamic_slice_in_dim(g_rep[k], me * nsh, nsh, axis=1))
    res = _adamw(_pack(g_mine)[None], wp, mp, vp, "adamw_small")
    shapes2 = [W[k].shape for k in _SMALL] + [W[k].shape for k in _SMALL_SHARDED]
    res = [dict(zip(names, _unpack(r, shapes2))) for r in res]
    for k in names:
        out[k] = [r[k] for r in res]
    return loss, grad_x.reshape(x.shape), out


def kernel(x, norm_mix_g, w_in, mu_shift, rwkv_w0, rwkv_w2, rwkv_a0, rwkv_a2, rwkv_g2, rwkv_k_k, rwkv_k_a, rwkv_r_k, rwkv_ln_g, rwkv_ln_b, conv_w, conv_b, lru_wr, lru_br, lru_wi, lru_bi, lru_lambda, lru_norm_g, w_out, norm_ffn_g, ffn_w_gate, ffn_w_up, ffn_w_down, norm_final_g, loss_target, m_norm_mix_g, m_w_in, m_mu_shift, m_rwkv_w0, m_rwkv_w2, m_rwkv_a0, m_rwkv_a2, m_rwkv_g2, m_rwkv_k_k, m_rwkv_k_a, m_rwkv_r_k, m_rwkv_ln_g, m_rwkv_ln_b, m_conv_w, m_conv_b, m_lru_wr, m_lru_br, m_lru_wi, m_lru_bi, m_lru_lambda, m_lru_norm_g, m_w_out, m_norm_ffn_g, m_ffn_w_gate, m_ffn_w_up, m_ffn_w_down, m_norm_final_g, v_norm_mix_g, v_w_in, v_mu_shift, v_rwkv_w0, v_rwkv_w2, v_rwkv_a0, v_rwkv_a2, v_rwkv_g2, v_rwkv_k_k, v_rwkv_k_a, v_rwkv_r_k, v_rwkv_ln_g, v_rwkv_ln_b, v_conv_w, v_conv_b, v_lru_wr, v_lru_br, v_lru_wi, v_lru_bi, v_lru_lambda, v_lru_norm_g, v_w_out, v_norm_ffn_g, v_ffn_w_gate, v_ffn_w_up, v_ffn_w_down, v_norm_final_g):
    a = locals()
    W = {k: a[k] for k in _WEIGHTS}
    M = {k: a["m_" + k] for k in _WEIGHTS}
    V = {k: a["v_" + k] for k in _WEIGHTS}
    loss, grad_x, out = _step(W, M, V, x, loss_target)
    res = [loss, grad_x]
    for i in range(4):
        res += [out[k][i].reshape(W[k].shape) for k in _WEIGHTS]
    return tuple(res)
```

```python
import functools
import math

import jax
import jax.numpy as jnp
from jax import lax
from jax.experimental import pallas as pl
from jax.experimental.pallas import tpu as pltpu

F32 = jnp.float32
BF16 = jnp.bfloat16
HI = lax.Precision.HIGHEST
MESH = pl.DeviceIdType.MESH

N_DEV = 8
LANE = 128
HEAD = 64
MM_MAX_TK = 4096
SCAN_CHUNK = 64
SCAN_GROUP = 4
SCAN_PAIRS = 2
VMEM_LIMIT = 56 * 1024 * 1024

NORM_EPS = 1e-6
GN_EPS = 64e-5
LRU_C = 8.0
ADAM_LR, ADAM_B1, ADAM_B2, ADAM_EPS, ADAM_WD, ADAM_STEP = 0.001, 0.9, 0.999, 1e-08, 0.01, 10


def _pick(n, cands):
    for c in cands:
        if n % c == 0:
            return c
    return n


def _rup(n, m):
    return (n + m - 1) // m * m


def _cparams(dims):
    return pltpu.CompilerParams(dimension_semantics=dims, vmem_limit_bytes=VMEM_LIMIT)


def _sigmoid(x):
    return 1.0 / (1.0 + jnp.exp(-x))


def _softplus(z):
    return jnp.maximum(z, 0.0) + jnp.log(1.0 + jnp.exp(-jnp.abs(z)))


def _neg_expm1(x):
    series = -(x * (1.0 + 0.5 * x * (1.0 + (x / 3.0) * (1.0 + 0.25 * x))))
    return jnp.where(jnp.abs(x) < 0.03, series, 1.0 - jnp.exp(x))


def _gelu(x):
    return 0.5 * x * (1.0 + jnp.tanh(0.7978845608028654 * (x + 0.044715 * (x * x * x))))


def _dot(a, b, dims, precision=None):
    return lax.dot_general(a, b, (dims, ((), ())), precision=precision, preferred_element_type=F32)


def _nn(a, b, precision=None):
    return _dot(a, b, ((1,), (0,)), precision)


def _nt(a, b, precision=None):
    return _dot(a, b, ((1,), (1,)), precision)


def _tn(a, b, precision=None):
    return _dot(a, b, ((0,), (0,)), precision)


def _mm(a, b, *, name, ta=False, tb=False, out_dtype=F32, add=None, tiles=None):
    M, K = (a.shape[1], a.shape[0]) if ta else a.shape
    N = b.shape[0] if tb else b.shape[1]
    assert (b.shape[1] if tb else b.shape[0]) == K, (a.shape, b.shape, ta, tb)
    tk = max(t for t in range(LANE, min(K, MM_MAX_TK) + 1, LANE) if K % t == 0)
    tm, tn, tk = tiles or (_pick(M, (1024, 512, 256, 128)), _pick(N, (512, 256, 128)), tk)
    nk = K // tk
    dims = ((0 if ta else 1,), (1 if tb else 0,))

    def kern(*refs):
        a_ref, b_ref = refs[:2]
        add_ref = refs[2] if add is not None else None
        o_ref = refs[-2] if nk > 1 else refs[-1]

        def finish(r):
            if add is not None:
                r = r + add_ref[...].astype(F32)
            o_ref[...] = r.astype(o_ref.dtype)

        if nk == 1:
            finish(_dot(a_ref[...], b_ref[...], dims))
            return
        acc = refs[-1]
        k = pl.program_id(2)

        @pl.when(k == 0)
        def _():
            acc[...] = jnp.zeros_like(acc)

        acc[...] += _dot(a_ref[...], b_ref[...], dims)

        @pl.when(k == nk - 1)
        def _():
            finish(acc[...])

    a_spec = pl.BlockSpec((tk, tm), lambda i, j, k: (k, i)) if ta else pl.BlockSpec((tm, tk), lambda i, j, k: (i, k))
    b_spec = pl.BlockSpec((tn, tk), lambda i, j, k: (j, k)) if tb else pl.BlockSpec((tk, tn), lambda i, j, k: (k, j))
    o_spec = pl.BlockSpec((tm, tn), lambda i, j, k: (i, j))
    in_specs = [a_spec, b_spec] + ([o_spec] if add is not None else [])
    args = (a, b) + ((add,) if add is not None else ())
    return pl.pallas_call(
        kern, name=name, grid=(M // tm, N // tn, nk), in_specs=in_specs, out_specs=o_spec,
        out_shape=jax.ShapeDtypeStruct((M, N), out_dtype), scratch_shapes=[pltpu.VMEM((tm, tn), F32)] if nk > 1 else [],
        compiler_params=_cparams(("parallel", "parallel", "arbitrary")),
    )(*args)


def _stage_specs(acts, params, consts, tile, ct):
    act_specs = [pl.BlockSpec((tile, ct), functools.partial(lambda j, i, o: (i, o + j), o=off // ct)) for _, off in acts]
    par_specs = [pl.BlockSpec(bs, functools.partial(lambda j, i, im: im(j), im=im)) for _, bs, im in params]
    con_specs = [pl.BlockSpec(bs, functools.partial(lambda j, i, im: im(j), im=im)) for _, bs, im in consts]
    return act_specs, par_specs, con_specs


def _stage_fwd(f, name, n_rows, width, tile, ct, acts, params, consts, out_dtypes):
    for _, off in acts:
        assert off % ct == 0
    na, npar, nc = len(acts), len(params), len(consts)

    def kern(*refs):
        a = [r[...].astype(F32) for r in refs[:na]]
        p = [r[...] for r in refs[na:na + npar]]
        c = [r[...] for r in refs[na + npar:na + npar + nc]]
        outs = f(a, p, c, pl.program_id(1) * tile)
        for r, o in zip(refs[na + npar + nc:], outs):
            r[...] = o.astype(r.dtype)

    act_specs, par_specs, con_specs = _stage_specs(acts, params, consts, tile, ct)
    o_spec = pl.BlockSpec((tile, ct), lambda j, i: (i, j))
    outs = pl.pallas_call(
        kern, name=name, grid=(width // ct, n_rows // tile),
        in_specs=act_specs + par_specs + con_specs, out_specs=[o_spec] * len(out_dtypes),
        out_shape=[jax.ShapeDtypeStruct((n_rows, width), d) for d in out_dtypes],
        compiler_params=_cparams(("parallel", "parallel")),
    )(*[a for a, _ in acts], *[p for p, _, _ in params], *[c for c, _, _ in consts])
    return tuple(outs)


def _stage_bwd(f, name, n_rows, width, tile, ct, acts, params, consts, couts, dact_dtypes, extra_add=None):
    na, npar, nc, no = len(acts), len(params), len(consts), len(couts)
    nx = 0 if extra_add is None else 1

    def kern(*refs):
        a = [r[...].astype(F32) for r in refs[:na]]
        p = [r[...] for r in refs[na:na + npar]]
        c = [r[...] for r in refs[na + npar:na + npar + nc]]
        base = na + npar + nc
        co = [r[...].astype(F32) for r in refs[base:base + no]]
        base += no
        x_refs = refs[base:base + nx]
        base += nx
        da_refs = refs[base:base + na]
        dp_refs = refs[base + na:]
        row0 = pl.program_id(1) * tile
        _, vjp = jax.vjp(lambda aa, pp: tuple(f(aa, pp, c, row0)), a, p)
        da, dp = vjp(tuple(co))
        for k, (r, d) in enumerate(zip(da_refs, da)):
            if k == 0 and nx:
                d = d + x_refs[0][...].astype(F32)
            r[...] = d.astype(r.dtype)
        first = pl.program_id(1) == 0
        for r, d in zip(dp_refs, dp):
            @pl.when(first)
            def _(r=r, d=d):
                r[...] = d

            @pl.when(jnp.logical_not(first))
            def _(r=r, d=d):
                r[...] += d

    act_specs, par_specs, con_specs = _stage_specs(acts, params, consts, tile, ct)
    t_spec = pl.BlockSpec((tile, ct), lambda j, i: (i, j))
    co_specs = [pl.BlockSpec((tile, ct), functools.partial(lambda j, i, o: (i, o + j), o=off // ct)) for _, off in couts]
    x_specs = [] if extra_add is None else [pl.BlockSpec((tile, ct), functools.partial(lambda j, i, o: (i, o + j), o=extra_add[1] // ct))]
    x_args = [] if extra_add is None else [extra_add[0]]
    outs = pl.pallas_call(
        kern, name=name, grid=(width // ct, n_rows // tile),
        in_specs=act_specs + par_specs + con_specs + co_specs + x_specs,
        out_specs=[t_spec] * na + par_specs,
        out_shape=[jax.ShapeDtypeStruct((n_rows, width), d) for d in dact_dtypes]
        + [jax.ShapeDtypeStruct(p.shape, F32) for p, _, _ in params],
        compiler_params=_cparams(("parallel", "arbitrary")),
    )(*[a for a, _ in acts], *[p for p, _, _ in params], *[c for c, _, _ in consts], *[c for c, _ in couts], *x_args)
    return tuple(outs[:na]), tuple(outs[na:])


def _row(ct):
    return (1, ct), (lambda j: (0, j))


def _f_rmsnorm(a, p, c, row0):
    x, = a
    g, = p
    return (x * lax.rsqrt(jnp.mean(x * x, axis=-1, keepdims=True) + NORM_EPS) * g,)


def _f_lora_act(a, p, c, row0, widths):
    x, = a
    dl, al = widths
    col = lax.broadcasted_iota(jnp.int32, x.shape, 1)
    return (jnp.where(col < dl, jnp.tanh(x), jnp.where(col < dl + al, x, _sigmoid(x))),)


def _f_rwkv_pre(a, p, c, row0):
    k, wlin, alin = a
    w0, a0, k_k, k_a = p
    gsum, = c
    w = -_softplus(-(w0 + wlin)) - 0.5
    lw = -jnp.exp(w)
    alpha = _sigmoid(a0 + alin)
    kk = k * k_k
    ss = _nn(kk * kk, gsum, HI)
    kk = kk * lax.rsqrt(jnp.maximum(ss, 1e-24))
    k2 = k * (1.0 + (alpha - 1.0) * k_a)
    return lw, k2, -kk, kk * alpha


def _f_rwkv_post(a, p, c, row0):
    y, r, k2, v, g = a
    ln_g, ln_b, r_k = p
    gsum, = c
    inv = 1.0 / HEAD
    mean = _nn(y, gsum, HI) * inv
    yc = y - mean
    var = _nn(yc * yc, gsum, HI) * inv
    yn = yc * lax.rsqrt(var + GN_EPS) * ln_g + ln_b
    bonus = _nn(r * k2 * r_k, gsum, HI)
    return ((yn + bonus * v) * g,)


def _f_lru_gates(a, p, c, row0, seq):
    xc, = a
    wr, br, wi, bi, lam = p
    xb = xc.astype(BF16)
    rg = _sigmoid(_nn(xb, wr[0].astype(BF16)) + br)
    ig = _sigmoid(_nn(xb, wi[0].astype(BF16)) + bi)
    log_a = -LRU_C * rg * _softplus(-lam)
    a_t = jnp.exp(log_a)
    mult = jnp.sqrt(_neg_expm1(2.0 * log_a))
    row = row0 + lax.broadcasted_iota(jnp.int32, xc.shape, 0)
    mult = jnp.where(row % seq == 0, 1.0, mult)
    return a_t, mult * ig * xc


def _f_lru_post(a, p, c, row0):
    h, gate = a
    g, = p
    y = h * _gelu(gate)
    return (y * lax.rsqrt(jnp.mean(y * y, axis=-1, keepdims=True) + NORM_EPS) * g,)


def _f_swiglu(a, p, c, row0):
    gate, up = a
    return (gate * _sigmoid(gate) * up,)


def _shift_down(x, s, row):
    return jnp.where(row >= s, pltpu.roll(x, s, 0), 0.0)


def _shift_up(x, s, row):
    n = x.shape[0]
    return jnp.where(row < n - s, pltpu.roll(x, n - s, 0), 0.0)


def _seq_call(kern, name, bl, seq, width, ct, ins, outs, acc_outs=()):
    def spec(off, rows):
        if rows is None:
            return pl.BlockSpec((seq, ct), functools.partial(lambda j, b, o: (b, o + j), o=off // ct))
        return pl.BlockSpec((rows, ct), lambda j, b: (0, j))

    in_specs = [spec(off, rows) for _, off, rows in ins]
    out_specs = [spec(0, None) for _ in outs] + [spec(0, rows) for _, rows in acc_outs]
    out_shape = [jax.ShapeDtypeStruct((bl * seq, width), d) for d in outs] + [jax.ShapeDtypeStruct((rows, width), F32) for _, rows in acc_outs]
    res = pl.pallas_call(
        kern, name=name, grid=(width // ct, bl), in_specs=in_specs, out_specs=out_specs, out_shape=out_shape,
        compiler_params=_cparams(("parallel", "arbitrary")),
    )(*[a for a, _, _ in ins])
    return tuple(res)


def _acc(ref, val):
    first = pl.program_id(1) == 0

    @pl.when(first)
    def _():
        ref[...] = val

    @pl.when(jnp.logical_not(first))
    def _():
        ref[...] += val


def _lerp_fwd(p, off, mu, bl, seq, width, ct):
    def kern(p_ref, mu_ref, o_ref):
        x = p_ref[...]
        row = lax.broadcasted_iota(jnp.int32, x.shape, 0)
        o_ref[...] = x + (_shift_down(x, 1, row) - x) * mu_ref[...]

    return _seq_call(kern, "lerp_fwd", bl, seq, width, ct, [(p, off, None), (mu, 0, 1)], [F32])[0]


def _lerp_bwd(p, off, mu, dps, bl, seq, width, ct, out_dtype):
    def kern(p_ref, mu_ref, d_ref, dp_ref, dmu_ref):
        x = p_ref[...]
        d = d_ref[...].astype(F32)
        m = mu_ref[...]
        row = lax.broadcasted_iota(jnp.int32, x.shape, 0)
        dp_ref[...] = (d * (1.0 - m) + _shift_up(d * m, 1, row)).astype(dp_ref.dtype)
        _acc(dmu_ref, jnp.sum(d * (_shift_down(x, 1, row) - x), axis=0, keepdims=True))

    return _seq_call(kern, "lerp_bwd", bl, seq, width, ct, [(p, off, None), (mu, 0, 1), (dps, 0, None)], [out_dtype], [(None, 1)])


def _conv_fwd(p, off, cw, cb, bl, seq, width, ct):
    nw = cw.shape[0]

    def kern(x_ref, w_ref, b_ref, o_ref):
        x = x_ref[...]
        row = lax.broadcasted_iota(jnp.int32, x.shape, 0)
        acc = b_ref[...] + x * w_ref[pl.ds(nw - 1, 1), :]
        for s in range(1, nw):
            acc = acc + _shift_down(x, s, row) * w_ref[pl.ds(nw - 1 - s, 1), :]
        o_ref[...] = acc

    return _seq_call(kern, "conv_fwd", bl, seq, width, ct, [(p, off, None), (cw, 0, nw), (cb, 0, 1)], [F32])[0]


def _conv_bwd(p, off, cw, dxc, bl, seq, width, ct, out_dtype):
    nw = cw.shape[0]

    def kern(x_ref, w_ref, d_ref, dx_ref, dw_ref, db_ref):
        x = x_ref[...]
        d = d_ref[...]
        row = lax.broadcasted_iota(jnp.int32, x.shape, 0)
        wrow = lax.broadcasted_iota(jnp.int32, dw_ref.shape, 0)
        dx = d * w_ref[pl.ds(nw - 1, 1), :]
        dw = jnp.where(wrow == nw - 1, jnp.sum(d * x, axis=0, keepdims=True), 0.0)
        for s in range(1, nw):
            dx = dx + _shift_up(d, s, row) * w_ref[pl.ds(nw - 1 - s, 1), :]
            dw = jnp.where(wrow == nw - 1 - s, jnp.sum(d * _shift_down(x, s, row), axis=0, keepdims=True), dw)
        dx_ref[...] = dx.astype(dx_ref.dtype)
        _acc(dw_ref, dw)
        _acc(db_ref, jnp.sum(d, axis=0, keepdims=True))

    return _seq_call(kern, "conv_bwd", bl, seq, width, ct, [(p, off, None), (cw, 0, nw), (dxc, 0, None)], [out_dtype], [(None, nw), (None, 1)])


def _lru_scan_fwd(a, bx, bl, seq, width, ct):
    def kern(a_ref, b_ref, h_ref):
        av = a_ref[...]
        bv = b_ref[...]
        row = lax.broadcasted_iota(jnp.int32, av.shape, 0)
        d = 1
        while d < seq:
            a_sh = jnp.where(row >= d, pltpu.roll(av, d, 0), 1.0)
            b_sh = jnp.where(row >= d, pltpu.roll(bv, d, 0), 0.0)
            bv = av * b_sh + bv
            av = av * a_sh
            d *= 2
        h_ref[...] = bv

    return _seq_call(kern, "lru_scan_fwd", bl, seq, width, ct, [(a, 0, None), (bx, 0, None)], [F32])[0]


def _lru_scan_bwd(a, h, dh, bl, seq, width, ct):
    def kern(a_ref, h_ref, d_ref, da_ref, db_ref):
        row = lax.broadcasted_iota(jnp.int32, a_ref.shape, 0)
        al = _shift_up(a_ref[...], 1, row)
        g = d_ref[...]
        d = 1
        while d < seq:
            keep = row < seq - d
            al_sh = jnp.where(keep, pltpu.roll(al, seq - d, 0), 1.0)
            g_sh = jnp.where(keep, pltpu.roll(g, seq - d, 0), 0.0)
            g = al * g_sh + g
            al = al * al_sh
            d *= 2
        db_ref[...] = g
        da_ref[...] = g * _shift_down(h_ref[...], 1, row)

    return _seq_call(kern, "lru_scan_bwd", bl, seq, width, ct, [(a, 0, None), (h, 0, None), (dh, 0, None)], [F32, F32])


_FORMS = {"nn": ((1,), (0,)), "nt": ((1,), (1,)), "tn": ((0,), (0,))}
_FORM_GRADS = {"nn": (("nt", "g", "b"), ("tn", "a", "g")),
               "nt": (("nn", "g", "b"), ("tn", "g", "a")),
               "tn": (("nt", "b", "g"), ("nn", "a", "g"))}


def _split_bf16(x):
    hi = x.astype(BF16)
    return hi, (x - hi.astype(F32)).astype(BF16)


def _pdot_raw(a, b, form, passes):
    dims = _FORMS[form]
    if passes == 1:
        return _dot(a.astype(BF16), b.astype(BF16), dims)
    ah, al = _split_bf16(a)
    bh, bl = _split_bf16(b)
    return _dot(ah, bh, dims) + (_dot(ah, bl, dims) + _dot(al, bh, dims))


@functools.partial(jax.custom_vjp, nondiff_argnums=(2, 3))
def _pdot(a, b, form, passes):
    return _pdot_raw(a, b, form, passes)


def _pdot_fwd(a, b, form, passes):
    return _pdot_raw(a, b, form, passes), (a, b)


def _pdot_bwd(form, passes, res, g):
    vals = {"a": res[0], "b": res[1], "g": g}
    (fa, xa, ya), (fb, xb, yb) = _FORM_GRADS[form]
    return _pdot_raw(vals[xa], vals[ya], fa, passes), _pdot_raw(vals[xb], vals[yb], fb, passes)


_pdot.defvjp(_pdot_fwd, _pdot_bwd)


def _scan_chunk2(S0, r, lw, k, v, a, b, p_main=1, p_inv=3):
    y, s = _scan_block([S0], [[(r, lw, k, v, a, b)]], p_main, p_inv)
    return y[0][0], s[0]


def _scan_block(states, units, p_main=1, p_inv=1):
    C = units[0][0][0].shape[0]
    C2 = 2 * C
    ri = lax.broadcasted_iota(jnp.int32, (C, C), 0)
    ci = lax.broadcasted_iota(jnp.int32, (C, C), 1)
    tri = (ri >= ci).astype(F32)
    i2 = lax.broadcasted_iota(jnp.int32, (C2, C2), 0)
    j2 = lax.broadcasted_iota(jnp.int32, (C2, C2), 1)
    same = (i2 // C) == (j2 // C)
    strict = jnp.logical_and(same, (i2 % C) > (j2 % C))
    incl = jnp.logical_and(same, (i2 % C) >= (j2 % C))
    eye = (i2 == j2).astype(F32)
    lane = lax.broadcasted_iota(jnp.int32, (1, LANE), 1)
    m0, m1 = (lane < HEAD).astype(F32), (lane >= HEAD).astype(F32)
    stack = lambda z: jnp.concatenate([z * m0, z * m1], axis=0)
    ids = [(i, g) for g in range(len(units[0])) for i in range(len(units))]

    pre = {}
    for i, g in ids:
        r, lw, k, v, a, b = units[i][g]
        cs = _nn(tri, lw, HI)
        p_incl = jnp.exp(cs)
        p_rec = jnp.exp(-cs)
        xr = jnp.concatenate([stack(a * jnp.exp(cs - lw)), stack(r * p_incl)], axis=0)
        bk = jnp.concatenate([stack(b * p_rec), stack(k * p_rec)], axis=0)
        pre[i, g] = (xr, bk, stack(v), jnp.exp(jnp.sum(lw, axis=0, keepdims=True)))
    gm = {u: _pdot(pre[u][0], pre[u][1], "nt", p_main) for u in ids}
    a_ak = {u: jnp.where(strict, gm[u][:C2, C2:], 0.0) for u in ids}
    r_bk = {u: jnp.concatenate([jnp.where(incl, gm[u][C2:, :C2], 0.0), jnp.where(incl, gm[u][C2:, C2:], 0.0)], axis=1) for u in ids}
    pw = {u: jnp.where(strict, gm[u][:C2, :C2], 0.0) for u in ids}
    x = {u: eye + pw[u] for u in ids}
    n = 2
    while n < C:
        pw = {u: _pdot(pw[u], pw[u], "nn", p_inv) for u in ids}
        x = {u: x[u] + _pdot(x[u], pw[u], "nn", p_inv) for u in ids}
        n *= 2
    akv = {u: _pdot(a_ak[u], pre[u][2], "nn", p_main) for u in ids}

    states = list(states)
    ys = [[None] * len(units[0]) for _ in units]
    for i, g in ids:
        xr, bk, vs, p_last = pre[i, g]
        xs = _pdot(xr, states[i], "nt", p_main)
        us = _pdot(x[i, g], xs[:C2] + akv[i, g], "nn", p_inv)
        uv = jnp.concatenate([us, vs], axis=0)
        y2 = xs[C2:] + _pdot(r_bk[i, g], uv, "nn", p_main)
        ys[i][g] = y2[:C] + y2[C:]
        states[i] = (states[i] + _pdot(uv, bk, "tn", p_main)) * p_last
    return ys, states


def _scan_dims(seq, rw):
    G = _pick(seq // SCAN_CHUNK, (SCAN_GROUP, 2, 1))
    NP = _pick(rw // LANE, (SCAN_PAIRS, 1))
    C = SCAN_CHUNK * G
    return SCAN_CHUNK, G, NP, C, seq // C, rw // (NP * LANE)


def _rwkv_scan_fwd(r, lw, k2, v, na, bb, p, bl, seq, rw):
    cs, G, NP, C, nc, nhg = _scan_dims(seq, rw)

    def kern(r_ref, lw_ref, k_ref, v_ref, a_ref, b_ref, y_ref, st_ref, s_scr):
        @pl.when(pl.program_id(2) == 0)
        def _():
            s_scr[...] = jnp.zeros_like(s_scr)

        in_refs = (r_ref, lw_ref, k_ref, v_ref, a_ref, b_ref)
        st_ref[...] = s_scr[...]
        units = [[tuple(ref[pl.ds(g * cs, cs), pl.ds(i * LANE, LANE)] for ref in in_refs) for g in range(G)] for i in range(NP)]
        ys, s_new = _scan_block([s_scr[i] for i in range(NP)], units)
        for i in range(NP):
            s_scr[i] = s_new[i]
            for g in range(G):
                y_ref[pl.ds(g * cs, cs), pl.ds(i * LANE, LANE)] = ys[i][g]

    def tok(off):
        return pl.BlockSpec((C, NP * LANE), functools.partial(lambda b, h, c, o: (b * nc + c, o + h), o=off // (NP * LANE)))

    y, st = pl.pallas_call(
        kern, name="rwkv_scan_fwd", grid=(bl, nhg, nc),
        in_specs=[tok(0), tok(0), tok(0), tok(2 * rw), tok(0), tok(0)],
        out_specs=[tok(0), pl.BlockSpec((NP, LANE, LANE), lambda b, h, c: ((b * nhg + h) * nc + c, 0, 0))],
        out_shape=[jax.ShapeDtypeStruct((bl * seq, rw), F32), jax.ShapeDtypeStruct((bl * nhg * nc * NP, LANE, LANE), F32)],
        scratch_shapes=[pltpu.VMEM((NP, LANE, LANE), F32)],
        compiler_params=_cparams(("parallel", "parallel", "arbitrary")),
    )(p, lw, k2, p, na, bb)
    return y, st


def _rwkv_scan_bwd(lw, k2, na, bb, p, st, dy, bl, seq, rw):
    cs, G, NP, C, nc, nhg = _scan_dims(seq, rw)

    def kern(r_ref, lw_ref, k_ref, v_ref, a_ref, b_ref, st_ref, dy_ref, dr_ref, dlw_ref, dk_ref, dv_ref, da_ref, db_ref, ds_scr):
        @pl.when(pl.program_id(2) == 0)
        def _():
            ds_scr[...] = jnp.zeros_like(ds_scr)

        in_refs = (r_ref, lw_ref, k_ref, v_ref, a_ref, b_ref)
        out_refs = (dr_ref, dlw_ref, dk_ref, dv_ref, da_ref, db_ref)
        win = lambda ref, i, g: ref[pl.ds(g * cs, cs), pl.ds(i * LANE, LANE)]
        units = [[tuple(win(ref, i, g) for ref in in_refs) for g in range(G)] for i in range(NP)]
        _, vjp = jax.vjp(_scan_block, [st_ref[i] for i in range(NP)], units)
        dys = [[win(dy_ref, i, g) for g in range(G)] for i in range(NP)]
        ds, dunits = vjp((dys, [ds_scr[i] for i in range(NP)]))
        for i in range(NP):
            ds_scr[i] = ds[i]
            for g in range(G):
                for ref, d in zip(out_refs, dunits[i][g]):
                    ref[pl.ds(g * cs, cs), pl.ds(i * LANE, LANE)] = d

    def tok(off):
        return pl.BlockSpec((C, NP * LANE), functools.partial(lambda b, h, c, o: (b * nc + (nc - 1 - c), o + h), o=off // (NP * LANE)))

    st_spec = pl.BlockSpec((NP, LANE, LANE), lambda b, h, c: ((b * nhg + h) * nc + (nc - 1 - c), 0, 0))
    return pl.pallas_call(
        kern, name="rwkv_scan_bwd", grid=(bl, nhg, nc),
        in_specs=[tok(0), tok(0), tok(0), tok(2 * rw), tok(0), tok(0), st_spec, tok(0)],
        out_specs=[tok(0)] * 6,
        out_shape=[jax.ShapeDtypeStruct((bl * seq, rw), F32)] * 6,
        scratch_shapes=[pltpu.VMEM((NP, LANE, LANE), F32)],
        compiler_params=_cparams(("parallel", "parallel", "arbitrary")),
    )(p, lw, k2, p, na, bb, st, dy)


def _loss_head(h2, g_final, target, tile):
    n, d = h2.shape
    nt = n // tile

    def kern(h_ref, g_ref, t_ref, dh_ref, dg_ref, l_ref):
        def f(h, g):
            y = h * lax.rsqrt(jnp.mean(h * h, axis=-1, keepdims=True) + NORM_EPS) * g
            e = y - t_ref[...]
            return 0.5 * jnp.sum(jnp.mean(e * e, axis=-1, keepdims=True))

        loss, (dh, dg) = jax.value_and_grad(f, argnums=(0, 1))(h_ref[...], g_ref[...])
        dh_ref[...] = dh
        first = pl.program_id(0) == 0

        @pl.when(first)
        def _():
            dg_ref[...] = dg
            l_ref[...] = jnp.zeros_like(l_ref) + loss

        @pl.when(jnp.logical_not(first))
        def _():
            dg_ref[...] += dg
            l_ref[...] += loss

    row = pl.BlockSpec((tile, d), lambda i: (i, 0))
    vec = pl.BlockSpec((1, d), lambda i: (0, 0))
    return pl.pallas_call(
        kern, name="loss_head", grid=(nt,), in_specs=[row, vec, row],
        out_specs=[row, vec, pl.BlockSpec((1, LANE), lambda i: (0, 0))],
        out_shape=[jax.ShapeDtypeStruct((n, d), F32), jax.ShapeDtypeStruct((1, d), F32), jax.ShapeDtypeStruct((1, LANE), F32)],
        compiler_params=_cparams(("arbitrary",)),
    )(h2, g_final, target)


def _adamw(parts, w, m, v, name):
    n_parts, R, Cc = parts.shape
    tr = _pick(R, tuple(t for t in (1024, 512, 256, 128, 64, 32, 16) if t * Cc <= 128 * 1024) + (8,))
    c1, c2 = 1.0 - ADAM_B1, 1.0 - ADAM_B2
    bc1, bc2 = 1.0 - ADAM_B1 ** ADAM_STEP, 1.0 - ADAM_B2 ** ADAM_STEP

    def kern(p_ref, w_ref, m_ref, v_ref, g_ref, d_ref, nm_ref, nv_ref):
        g = p_ref[0].astype(F32)
        for s in range(1, n_parts):
            g = g + p_ref[s].astype(F32)
        m2 = ADAM_B1 * m_ref[...] + c1 * g
        v2 = ADAM_B2 * v_ref[...] + c2 * (g * g)
        g_ref[...] = g
        nm_ref[...] = m2
        nv_ref[...] = v2
        d_ref[...] = -ADAM_LR * ((m2 / bc1) / (jnp.sqrt(v2 / bc2) + ADAM_EPS) + ADAM_WD * w_ref[...])

    blk = pl.BlockSpec((tr, Cc), lambda i: (i, 0))
    return pl.pallas_call(
        kern, name=name, grid=(R // tr,),
        in_specs=[pl.BlockSpec((n_parts, tr, Cc), lambda i: (0, i, 0)), blk, blk, blk],
        out_specs=[blk] * 4, out_shape=[jax.ShapeDtypeStruct((R, Cc), F32)] * 4,
        compiler_params=_cparams(("parallel",)),
    )(parts, w, m, v)


def _sum_parts(parts, name):
    n_parts, R, Cc = parts.shape
    tr = _pick(R, (512, 256, 128, 64, 32, 16, 8))

    def kern(p_ref, o_ref):
        g = p_ref[0]
        for s in range(1, n_parts):
            g = g + p_ref[s]
        o_ref[...] = g

    return pl.pallas_call(
        kern, name=name, grid=(R // tr,), in_specs=[pl.BlockSpec((n_parts, tr, Cc), lambda i: (0, i, 0))],
        out_specs=pl.BlockSpec((tr, Cc), lambda i: (i, 0)), out_shape=jax.ShapeDtypeStruct((R, Cc), F32),
        compiler_params=_cparams(("parallel",)),
    )(parts)


def _coords():
    return lax.axis_index("x"), lax.axis_index("y"), lax.axis_index("c")


def _all_gather(xs, name):
    n = len(xs)

    def body(*refs):
        x_refs, o_refs = refs[:n], refs[n:2 * n]
        send_sems, recv_sems, local_sems = refs[2 * n:]
        x, y, c = _coords()
        me, sibling = (x, y, c), (x, y, 1 - c)
        chips = [(1 - x, y), (x, 1 - y), (1 - x, 1 - y)]
        waits = []
        for t in range(n):
            o_ref = o_refs[t]

            def slot(px, py, pc, o_ref=o_ref):
                return o_ref.at[4 * px + 2 * py + pc]

            def copy(k, block, to, src=None, t=t, slot=slot):
                return pltpu.make_async_remote_copy(
                    src_ref=slot(*block) if src is None else src, dst_ref=slot(*block),
                    send_sem=send_sems.at[t, k], recv_sem=recv_sems.at[t, k], device_id=to, device_id_type=MESH)

            mine = pltpu.make_async_copy(x_refs[t], slot(*me), local_sems.at[t])
            mine.start()
            first = [copy(0, me, sibling, src=x_refs[t])]
            first += [copy(1 + j, me, (*chip, c), src=x_refs[t]) for j, chip in enumerate(chips)]
            for cp in first:
                cp.start()
            waits.append((copy, mine, first))
        for t in range(n):
            copy, mine, first = waits[t]
            passed = [copy(4 + j, (*chip, c), sibling) for j, chip in enumerate(chips)]
            for j, chip in enumerate(chips):
                copy(1 + j, (*chip, c), me).wait_recv()
                passed[j].start()
            copy(0, sibling, me).wait_recv()
            for j, chip in enumerate(chips):
                copy(4 + j, (*chip, 1 - c), me).wait_recv()
            for cp in first + passed:
                cp.wait_send()
            mine.wait()

    any_spec = pl.BlockSpec(memory_space=pl.ANY)
    outs = pl.pallas_call(
        body, name=name, in_specs=[any_spec] * n, out_specs=[any_spec] * n,
        out_shape=[jax.ShapeDtypeStruct((N_DEV,) + a.shape, a.dtype) for a in xs],
        scratch_shapes=[pltpu.SemaphoreType.DMA((n, 7)), pltpu.SemaphoreType.DMA((n, 7)), pltpu.SemaphoreType.DMA((n,))],
    )(*xs)
    return list(outs)


def _exchange(xs, name):
    n = len(xs)

    def body(*refs):
        x_refs, o_refs = refs[:n], refs[n:2 * n]
        send_sems, recv_sems, local_sems = refs[2 * n:]
        x, y, c = _coords()
        my = 4 * x + 2 * y + c
        copies = []
        for t in range(n):
            mine = pltpu.make_async_copy(x_refs[t].at[my], o_refs[t].at[my], local_sems.at[t])
            mine.start()
            copies.append(mine)
        rem = []
        for k in range(1, N_DEV):
            fx, fy, fc = (k >> 2) & 1, (k >> 1) & 1, k & 1
            px, py, pc = x ^ fx, y ^ fy, c ^ fc
            peer = 4 * px + 2 * py + pc
            for t in range(n):
                cp = pltpu.make_async_remote_copy(
                    src_ref=x_refs[t].at[peer], dst_ref=o_refs[t].at[my],
                    send_sem=send_sems.at[t, k - 1], recv_sem=recv_sems.at[t, k - 1],
                    device_id=(px, py, pc), device_id_type=MESH)
                cp.start()
                rem.append(cp)
        for cp in rem:
            cp.wait()
        for cp in copies:
            cp.wait()

    any_spec = pl.BlockSpec(memory_space=pl.ANY)
    outs = pl.pallas_call(
        body, name=name, in_specs=[any_spec] * n, out_specs=[any_spec] * n,
        out_shape=[jax.ShapeDtypeStruct(a.shape, a.dtype) for a in xs],
        scratch_shapes=[pltpu.SemaphoreType.DMA((n, 7)), pltpu.SemaphoreType.DMA((n, 7)), pltpu.SemaphoreType.DMA((n,))],
    )(*xs)
    return list(outs)


def _cols_from_shards(g):
    return jnp.transpose(g, (1, 0, 2)).reshape(g.shape[1], N_DEV * g.shape[2])


def _shards_from_cols(w):
    r, n = w.shape
    return jnp.transpose(w.reshape(r, N_DEV, n // N_DEV), (1, 0, 2))


def _pad_cols(w, to):
    return jnp.pad(w, ((0, 0), (0, to - w.shape[1])))


def _pack(arrs):
    flat = jnp.concatenate([a.reshape(-1) for a in arrs])
    n = _rup(flat.shape[0], 8 * LANE)
    return jnp.pad(flat, (0, n - flat.shape[0])).reshape(n // LANE, LANE)


def _unpack(mat, shapes):
    flat = mat.reshape(-1)
    out, o = [], 0
    for s in shapes:
        n = math.prod(s)
        out.append(flat[o:o + n].reshape(s))
        o += n
    return out


_SMALL = ["norm_mix_g", "mu_shift", "rwkv_w0", "rwkv_a0", "rwkv_k_k", "rwkv_k_a", "rwkv_r_k", "rwkv_ln_g", "rwkv_ln_b", "conv_b",
          "lru_wr", "lru_br", "lru_wi", "lru_bi", "lru_lambda", "lru_norm_g", "norm_ffn_g", "norm_final_g"]
_SMALL_SHARDED = ["rwkv_w2", "rwkv_a2", "rwkv_g2", "conv_w"]
_BIG = ["w_in", "w_out", "ffn_w_gate", "ffn_w_up", "ffn_w_down"]
_WEIGHTS = ['norm_mix_g', 'w_in', 'mu_shift', 'rwkv_w0', 'rwkv_w2', 'rwkv_a0', 'rwkv_a2', 'rwkv_g2', 'rwkv_k_k', 'rwkv_k_a', 'rwkv_r_k',
            'rwkv_ln_g', 'rwkv_ln_b', 'conv_w', 'conv_b', 'lru_wr', 'lru_br', 'lru_wi', 'lru_bi', 'lru_lambda', 'lru_norm_g', 'w_out',
            'norm_ffn_g', 'ffn_w_gate', 'ffn_w_up', 'ffn_w_down', 'norm_final_g']


def _step(W, M, V, x, loss_target):
    bl, seq, d = x.shape
    n = bl * seq
    rw = W["rwkv_w0"].shape[1]
    nh = W["rwkv_r_k"].shape[1]
    assert W["rwkv_r_k"].shape[2] == HEAD and nh * HEAD == rw and rw % LANE == 0
    dl, al, gl = W["rwkv_w2"].shape[1], W["rwkv_a2"].shape[1], W["rwkv_g2"].shape[1]
    dlp, alp, glp = _rup(dl, LANE), _rup(al, LANE), _rup(gl, LANE)
    lorap = dlp + alp + glp
    lw_ = W["conv_b"].shape[1]
    nblk, lbw = W["lru_wr"].shape[1], W["lru_wr"].shape[2]
    assert lbw == LANE and nblk * lbw == lw_
    o_xb, o_gate, o_rw = 0, lw_, 2 * lw_
    o_lora = 3 * rw
    rwp = o_lora + lorap
    inp = o_rw + rwp
    nsh_ff = W["ffn_w_gate"].shape[2]
    dff = N_DEV * nsh_ff
    dffp = _rup(dff, 1024) if dff >= 1024 else _rup(dff, LANE)
    x2 = x.reshape(n, d)
    tgt2 = loss_target.reshape(n, d)

    small_sh = _pack([W[k][0] for k in _SMALL_SHARDED])
    g_in, g_out, g_gate, g_up, g_down, g_small = _all_gather(
        [W["w_in"][0].astype(BF16), W["w_out"][0].astype(BF16), W["ffn_w_gate"][0].astype(BF16), W["ffn_w_up"][0].astype(BF16),
         W["ffn_w_down"][0].astype(BF16), small_sh], "gather_weights")
    w_in_l = _cols_from_shards(g_in)
    o1 = 3 * rw
    w_in = jnp.concatenate([w_in_l[:, o1 + dl + al + gl:], w_in_l[:, :o1], _pad_cols(w_in_l[:, o1:o1 + dl], dlp),
                            _pad_cols(w_in_l[:, o1 + dl:o1 + dl + al], alp), _pad_cols(w_in_l[:, o1 + dl + al:o1 + dl + al + gl], glp)], axis=1)
    w_out = g_out.reshape(N_DEV * g_out.shape[1], d)
    w_gu = jnp.concatenate([_pad_cols(_cols_from_shards(g_gate), dffp), _pad_cols(_cols_from_shards(g_up), dffp)], axis=1)
    w_down = jnp.pad(g_down.reshape(dff, d), ((0, dffp - dff), (0, 0)))
    sm_shapes = [W[k][0].shape for k in _SMALL_SHARDED]
    sm = [_unpack(g_small[s], sm_shapes) for s in range(N_DEV)]
    w2, a2, g2, conv_w = [jnp.concatenate([sm[s][i] for s in range(N_DEV)], axis=1) for i in range(4)]
    w_lora = jnp.zeros((lorap, 3 * rw), F32)
    w_lora = w_lora.at[:dl, :rw].set(w2).at[dlp:dlp + al, rw:2 * rw].set(a2).at[dlp + alp:dlp + alp + gl, 2 * rw:].set(g2)
    w_lora = w_lora.astype(BF16)
    mu_l = W["mu_shift"]
    mu = jnp.concatenate([mu_l[:, :o1], _pad_cols(mu_l[:, o1:o1 + dl], dlp), _pad_cols(mu_l[:, o1 + dl:o1 + dl + al], alp),
                          _pad_cols(mu_l[:, o1 + dl + al:], glp)], axis=1)
    r_k = W["rwkv_r_k"].reshape(1, rw)

    tile = _pick(n, (256, 128, 64))
    tile_s = _pick(n, (128, 64))
    ct_seq = _pick(math.gcd(rwp, lw_), (256, 128))
    assert o_rw % ct_seq == 0 and o_gate % lw_ == 0
    ct_h = _pick(rw, (512, 256, 128))
    gi = lax.broadcasted_iota(jnp.int32, (ct_h, ct_h), 0) // HEAD
    gj = lax.broadcasted_iota(jnp.int32, (ct_h, ct_h), 1) // HEAD
    gsum = ((gi == gj).astype(F32), (ct_h, ct_h), lambda j: (0, 0))
    full = lambda a: (a, a.shape, lambda j: (0,) * a.ndim)
    rowp = lambda a, ct: (a,) + _row(ct)

    u1, = _stage_fwd(_f_rmsnorm, "norm_mix_fwd", n, d, tile, d, [(x2, 0)], [full(W["norm_mix_g"])], [], [BF16])
    p = _mm(u1, w_in, name="mm_in")
    ps = _lerp_fwd(p, o_rw, mu, bl, seq, rwp, ct_seq)
    f_lora = functools.partial(_f_lora_act, widths=(dlp, alp))
    lact, = _stage_fwd(f_lora, "lora_act_fwd", n, lorap, tile, lorap, [(ps, o_lora)], [], [], [BF16])
    wag = _mm(lact, w_lora, name="mm_lora")
    pre_par = [rowp(W["rwkv_w0"], ct_h), rowp(W["rwkv_a0"], ct_h), rowp(W["rwkv_k_k"], ct_h), rowp(W["rwkv_k_a"], ct_h)]
    pre_acts = [(ps, rw), (wag, 0), (wag, rw)]
    lw, k2, na, bb = _stage_fwd(_f_rwkv_pre, "rwkv_pre_fwd", n, rw, tile_s, ct_h, pre_acts, pre_par, [gsum], [F32] * 4)
    ysc, st = _rwkv_scan_fwd(None, lw, k2, None, na, bb, ps, bl, seq, rw)
    post_par = [rowp(W["rwkv_ln_g"], ct_h), rowp(W["rwkv_ln_b"], ct_h), rowp(r_k, ct_h)]
    post_acts = [(ysc, 0), (ps, 0), (k2, 0), (ps, 2 * rw), (wag, 2 * rw)]
    ya, = _stage_fwd(_f_rwkv_post, "rwkv_post_fwd", n, rw, tile_s, ct_h, post_acts, post_par, [gsum], [BF16])

    xc = _conv_fwd(p, o_xb, conv_w, W["conv_b"], bl, seq, lw_, ct_seq)
    f_gates = functools.partial(_f_lru_gates, seq=seq)
    blk3 = lambda a: (a[0], (1, LANE, LANE), lambda j: (j, 0, 0))
    gate_par = [blk3(W["lru_wr"]), rowp(W["lru_br"], LANE), blk3(W["lru_wi"]), rowp(W["lru_bi"], LANE), rowp(W["lru_lambda"], LANE)]
    a_l, bx = _stage_fwd(f_gates, "lru_gates_fwd", n, lw_, tile, LANE, [(xc, 0)], gate_par, [], [F32, F32])
    ct_l = _pick(lw_, (256, 128))
    h_l = _lru_scan_fwd(a_l, bx, bl, seq, lw_, ct_l)
    lpost_par = [full(W["lru_norm_g"])]
    yb, = _stage_fwd(_f_lru_post, "lru_post_fwd", n, lw_, tile_s, lw_, [(h_l, 0), (p, o_gate)], lpost_par, [], [BF16])

    ycat = jnp.concatenate([ya, yb], axis=1)
    h1 = _mm(ycat, w_out, name="mm_out", add=x2)
    u2, = _stage_fwd(_f_rmsnorm, "norm_ffn_fwd", n, d, tile, d, [(h1, 0)], [full(W["norm_ffn_g"])], [], [BF16])
    gu = _mm(u2, w_gu, name="mm_gu")
    ct_f = _pick(dffp, (1024, 512, 256, 128))
    act, = _stage_fwd(_f_swiglu, "swiglu_fwd", n, dffp, tile, ct_f, [(gu, 0), (gu, dffp)], [], [], [BF16])
    h2 = _mm(act, w_down, name="mm_down", add=h1)

    dh2, dg_final, lsum = _loss_head(h2, W["norm_final_g"].reshape(1, d), tgt2, tile_s)
    loss = lax.psum(lsum[0, 0], ("x", "y", "c"))
    dh2b = dh2.astype(BF16)
    dact = _mm(dh2b, w_down, name="mm_dact", tb=True, out_dtype=BF16)
    dw_down = _mm(act, dh2b, name="mm_dw_down", ta=True)
    (dgate, dup), _ = _stage_bwd(_f_swiglu, "swiglu_bwd", n, dffp, tile, ct_f, [(gu, 0), (gu, dffp)], [], [], [(dact, 0)], [BF16, BF16])
    dgu = jnp.concatenate([dgate, dup], axis=1)
    du2 = _mm(dgu, w_gu, name="mm_du2", tb=True)
    dw_gu = _mm(u2, dgu, name="mm_dw_gu", ta=True)
    (dh1,), (dg_ffn,) = _stage_bwd(_f_rmsnorm, "norm_ffn_bwd", n, d, tile_s, d, [(h1, 0)], [full(W["norm_ffn_g"])], [], [(du2, 0)], [F32],
                                   extra_add=(dh2, 0))
    dh1b = dh1.astype(BF16)
    dycat = _mm(dh1b, w_out, name="mm_dycat", tb=True)
    dw_out = _mm(ycat, dh1b, name="mm_dw_out", ta=True)

    (dysc, dr_p, dk2_p, dv_p, dg_g), (dln_g, dln_b, dr_k) = _stage_bwd(
        _f_rwkv_post, "rwkv_post_bwd", n, rw, tile_s, ct_h, post_acts, post_par, [gsum], [(dycat, 0)], [F32] * 5)
    dr_s, dlw, dk2_s, dv_s, dna, dbb = _rwkv_scan_bwd(lw, k2, na, bb, ps, st, dysc, bl, seq, rw)
    dk2 = dk2_p + dk2_s
    (dk, dwlin, dalin), (dw0, da0, dk_k, dk_a) = _stage_bwd(
        _f_rwkv_pre, "rwkv_pre_bwd", n, rw, tile_s, ct_h, pre_acts, pre_par, [gsum], [(dlw, 0), (dk2, 0), (dna, 0), (dbb, 0)], [F32] * 3)
    dwag = jnp.concatenate([dwlin, dalin, dg_g], axis=1).astype(BF16)
    dlact = _mm(dwag, w_lora, name="mm_dlact", tb=True)
    dw_lora = _mm(lact, dwag, name="mm_dw_lora", ta=True)
    (dps_lora,), _ = _stage_bwd(f_lora, "lora_act_bwd", n, lorap, tile, lorap, [(ps, o_lora)], [], [], [(dlact, 0)], [F32])
    dps = jnp.concatenate([dr_p + dr_s, dk, dv_p + dv_s, dps_lora], axis=1)
    dp_rwkv, dmu = _lerp_bwd(p, o_rw, mu, dps, bl, seq, rwp, ct_seq, BF16)

    (dh_l, dgate_l), (dlru_norm_g,) = _stage_bwd(_f_lru_post, "lru_post_bwd", n, lw_, tile_s, lw_, [(h_l, 0), (p, o_gate)], lpost_par, [],
                                                 [(dycat, rw)], [F32, BF16])
    da_l, dbx = _lru_scan_bwd(a_l, h_l, dh_l, bl, seq, lw_, ct_l)
    (dxc,), (dwr, dbr, dwi, dbi, dlam) = _stage_bwd(f_gates, "lru_gates_bwd", n, lw_, tile, LANE, [(xc, 0)], gate_par, [],
                                                    [(da_l, 0), (dbx, 0)], [F32])
    dxb, dconv_w, dconv_b = _conv_bwd(p, o_xb, conv_w, dxc, bl, seq, lw_, ct_seq, BF16)

    dp = jnp.concatenate([dxb, dgate_l, dp_rwkv], axis=1)
    du1 = _mm(dp, w_in, name="mm_du1", tb=True)
    dw_in = _mm(u1, dp, name="mm_dw_in", ta=True)
    (grad_x,), (dg_mix,) = _stage_bwd(_f_rmsnorm, "norm_mix_bwd", n, d, tile_s, d, [(x2, 0)], [full(W["norm_mix_g"])], [], [(du1, 0)], [F32],
                                      extra_add=(dh1, 0))

    ol = o_rw + o_lora
    dw_in_l = jnp.concatenate([dw_in[:, o_rw:ol], dw_in[:, ol:ol + dl], dw_in[:, ol + dlp:ol + dlp + al],
                               dw_in[:, ol + dlp + alp:ol + dlp + alp + gl], dw_in[:, :o_rw]], axis=1)
    send = [_shards_from_cols(dw_in_l).astype(BF16), dw_out.reshape(N_DEV, -1, d).astype(BF16),
            _shards_from_cols(dw_gu[:, :dff]).astype(BF16), _shards_from_cols(dw_gu[:, dffp:dffp + dff]).astype(BF16),
            dw_down[:dff].reshape(N_DEV, nsh_ff, d).astype(BF16)]
    recv = _exchange(send, "exchange_grads")
    out = {}
    for k, parts in zip(_BIG, recv):
        out[k] = [o[None] for o in _adamw(parts, W[k][0], M[k][0], V[k][0], "adamw_" + k)]

    dmu_l = jnp.concatenate([dmu[:, :o1], dmu[:, o_lora:o_lora + dl], dmu[:, o_lora + dlp:o_lora + dlp + al],
                             dmu[:, o_lora + dlp + alp:o_lora + dlp + alp + gl]], axis=1)
    small_g = {"norm_mix_g": dg_mix, "mu_shift": dmu_l, "rwkv_w0": dw0, "rwkv_a0": da0, "rwkv_k_k": dk_k, "rwkv_k_a": dk_a,
               "rwkv_r_k": dr_k.reshape(W["rwkv_r_k"].shape), "rwkv_ln_g": dln_g, "rwkv_ln_b": dln_b, "conv_b": dconv_b,
               "lru_wr": dwr[None], "lru_br": dbr, "lru_wi": dwi[None], "lru_bi": dbi, "lru_lambda": dlam, "lru_norm_g": dlru_norm_g,
               "norm_ffn_g": dg_ffn, "norm_final_g": dg_final.reshape(W["norm_final_g"].shape)}
    sh_full = [dw_lora[:dl, :rw], dw_lora[dlp:dlp + al, rw:2 * rw], dw_lora[dlp + alp:dlp + alp + gl, 2 * rw:], dconv_w]
    names = _SMALL + _SMALL_SHARDED
    shapes = [W[k].shape for k in _SMALL] + [a.shape for a in sh_full]
    gpack = _pack([small_g[k] for k in _SMALL] + sh_full)
    gall, = _all_gather([gpack], "gather_small_grads")
    x_i, y_i, c_i = _coords()
    me = 4 * x_i + 2 * y_i + c_i
    g_rep = dict(zip(names, _unpack(_sum_parts(gall, "sum_small_grads"), shapes)))
    wp = _pack([W[k] for k in _SMALL] + [W[k][0] for k in _SMALL_SHARDED])
    mp = _pack([M[k] for k in _SMALL] + [M[k][0] for k in _SMALL_SHARDED])
    vp = _pack([V[k] for k in _SMALL] + [V[k][0] for k in _SMALL_SHARDED])
    g_mine = [g_rep[k] for k in _SMALL]
    for k in _SMALL_SHARDED:
        nsh = W[k].shape[2]
        g_mine.append(lax.dynamic_slice_in_dim(g_rep[k], me * nsh, nsh, axis=1))
    res = _adamw(_pack(g_mine)[None], wp, mp, vp, "adamw_small")
    shapes2 = [W[k].shape for k in _SMALL] + [W[k].shape for k in _SMALL_SHARDED]
    res = [dict(zip(names, _unpack(r, shapes2))) for r in res]
    for k in names:
        out[k] = [r[k] for r in res]
    return loss, grad_x.reshape(x.shape), out


def kernel(x, norm_mix_g, w_in, mu_shift, rwkv_w0, rwkv_w2, rwkv_a0, rwkv_a2, rwkv_g2, rwkv_k_k, rwkv_k_a, rwkv_r_k, rwkv_ln_g, rwkv_ln_b, conv_w, conv_b, lru_wr, lru_br, lru_wi, lru_bi, lru_lambda, lru_norm_g, w_out, norm_ffn_g, ffn_w_gate, ffn_w_up, ffn_w_down, norm_final_g, loss_target, m_norm_mix_g, m_w_in, m_mu_shift, m_rwkv_w0, m_rwkv_w2, m_rwkv_a0, m_rwkv_a2, m_rwkv_g2, m_rwkv_k_k, m_rwkv_k_a, m_rwkv_r_k, m_rwkv_ln_g, m_rwkv_ln_b, m_conv_w, m_conv_b, m_lru_wr, m_lru_br, m_lru_wi, m_lru_bi, m_lru_lambda, m_lru_norm_g, m_w_out, m_norm_ffn_g, m_ffn_w_gate, m_ffn_w_up, m_ffn_w_down, m_norm_final_g, v_norm_mix_g, v_w_in, v_mu_shift, v_rwkv_w0, v_rwkv_w2, v_rwkv_a0, v_rwkv_a2, v_rwkv_g2, v_rwkv_k_k, v_rwkv_k_a, v_rwkv_r_k, v_rwkv_ln_g, v_rwkv_ln_b, v_conv_w, v_conv_b, v_lru_wr, v_lru_br, v_lru_wi, v_lru_bi, v_lru_lambda, v_lru_norm_g, v_w_out, v_norm_ffn_g, v_ffn_w_gate, v_ffn_w_up, v_ffn_w_down, v_norm_final_g):
    a = locals()
    W = {k: a[k] for k in _WEIGHTS}
    M = {k: a["m_" + k] for k in _WEIGHTS}
    V = {k: a["v_" + k] for k in _WEIGHTS}
    loss, grad_x, out = _step(W, M, V, x, loss_target)
    res = [loss, grad_x]
    for i in range(4):
        res += [out[k][i].reshape(W[k].shape) for k in _WEIGHTS]
    return tuple(res)
```

```python
import functools
import math

import jax
import jax.numpy as jnp
from jax import lax
from jax.experimental import pallas as pl
from jax.experimental.pallas import tpu as pltpu

F32 = jnp.float32
BF16 = jnp.bfloat16
HI = lax.Precision.HIGHEST
MESH = pl.DeviceIdType.MESH

N_DEV = 8
LANE = 128
HEAD = 64
MM_MAX_TK = 4096
SCAN_CHUNK = 64
SCAN_GROUP = 4
SCAN_PAIRS = 2
VMEM_LIMIT = 56 * 1024 * 1024

NORM_EPS = 1e-6
GN_EPS = 64e-5
LRU_C = 8.0
ADAM_LR, ADAM_B1, ADAM_B2, ADAM_EPS, ADAM_WD, ADAM_STEP = 0.001, 0.9, 0.999, 1e-08, 0.01, 10


def _pick(n, cands):
    for c in cands:
        if n % c == 0:
            return c
    return n


def _rup(n, m):
    return (n + m - 1) // m * m


def _cparams(dims):
    return pltpu.CompilerParams(dimension_semantics=dims, vmem_limit_bytes=VMEM_LIMIT)


def _sigmoid(x):
    return 1.0 / (1.0 + jnp.exp(-x))


def _softplus(z):
    return jnp.maximum(z, 0.0) + jnp.log(1.0 + jnp.exp(-jnp.abs(z)))


def _neg_expm1(x):
    series = -(x * (1.0 + 0.5 * x * (1.0 + (x / 3.0) * (1.0 + 0.25 * x))))
    return jnp.where(jnp.abs(x) < 0.03, series, 1.0 - jnp.exp(x))


def _gelu(x):
    return 0.5 * x * (1.0 + jnp.tanh(0.7978845608028654 * (x + 0.044715 * (x * x * x))))


def _dot(a, b, dims, precision=None):
    return lax.dot_general(a, b, (dims, ((), ())), precision=precision, preferred_element_type=F32)


def _nn(a, b, precision=None):
    return _dot(a, b, ((1,), (0,)), precision)


def _nt(a, b, precision=None):
    return _dot(a, b, ((1,), (1,)), precision)


def _tn(a, b, precision=None):
    return _dot(a, b, ((0,), (0,)), precision)


def _coords():
    return lax.axis_index("x"), lax.axis_index("y"), lax.axis_index("c")


class _Carry:
    def __init__(self, tensors, items):
        self.tensors, self.items = tensors, items
        nt, ni = len(tensors), len(items)
        any_spec = pl.BlockSpec(memory_space=pl.ANY)
        self.args = [s for s, _ in tensors] + [d for _, d in tensors]
        self.in_specs = [any_spec] * (2 * nt)
        self.out_specs = [any_spec] * nt
        self.out_shape = [jax.ShapeDtypeStruct(d.shape, d.dtype) for _, d in tensors]
        self.scratch = [pltpu.SemaphoreType.DMA((ni, N_DEV - 1)), pltpu.SemaphoreType.DMA((ni, N_DEV - 1)), pltpu.SemaphoreType.DMA((ni,))]

    def aliases(self, first_in, first_out):
        nt = len(self.tensors)
        return {first_in + nt + t: first_out + t for t in range(nt)}

    def _copies(self, src_refs, dst_refs, sems):
        send_sems, recv_sems, local_sems = sems
        x, y, c = _coords()
        my = 4 * x + 2 * y + c
        out = []
        for n, (t, r0, rows) in enumerate(self.items):
            win = pl.ds(r0, rows)
            out.append(pltpu.make_async_copy(src_refs[t].at[my, win], dst_refs[t].at[my, win], local_sems.at[n]))
            for k in range(1, N_DEV):
                px, py, pc = x ^ ((k >> 2) & 1), y ^ ((k >> 1) & 1), c ^ (k & 1)
                out.append(pltpu.make_async_remote_copy(
                    src_ref=src_refs[t].at[4 * px + 2 * py + pc, win], dst_ref=dst_refs[t].at[my, win],
                    send_sem=send_sems.at[n, k - 1], recv_sem=recv_sems.at[n, k - 1],
                    device_id=(px, py, pc), device_id_type=MESH))
        return out

    def hook(self, step, last, src_refs, dst_refs, sems):
        if last == 0:
            for cp in self._copies(src_refs, dst_refs, sems):
                cp.start()
            for cp in self._copies(src_refs, dst_refs, sems):
                cp.wait()
            return

        @pl.when(step == 0)
        def _():
            for cp in self._copies(src_refs, dst_refs, sems):
                cp.start()

        @pl.when(step == last)
        def _():
            for cp in self._copies(src_refs, dst_refs, sems):
                cp.wait()


def _mm(a, b, *, name, ta=False, tb=False, out_dtype=F32, add=None, tiles=None, carry=None):
    M, K = (a.shape[1], a.shape[0]) if ta else a.shape
    N = b.shape[0] if tb else b.shape[1]
    assert (b.shape[1] if tb else b.shape[0]) == K, (a.shape, b.shape, ta, tb)
    tk = max(t for t in range(LANE, min(K, MM_MAX_TK) + 1, LANE) if K % t == 0)
    tm, tn, tk = tiles or (_pick(M, (1024, 512, 256, 128)), _pick(N, (512, 256, 128)), tk)
    nk = K // tk
    dims = ((0 if ta else 1,), (1 if tb else 0,))

    n_in = 2 + (add is not None)
    nt = len(carry.tensors) if carry else 0
    gi, gj = M // tm, N // tn

    def kern(*refs):
        a_ref, b_ref = refs[:2]
        add_ref = refs[2] if add is not None else None
        o_ref = refs[n_in + 2 * nt]
        scr = refs[n_in + 3 * nt + 1:]
        if carry:
            step = (pl.program_id(0) * gj + pl.program_id(1)) * nk + pl.program_id(2)
            carry.hook(step, gi * gj * nk - 1, refs[n_in:n_in + nt], refs[n_in + 2 * nt + 1:n_in + 3 * nt + 1], scr[-3:])

        def finish(r):
            if add is not None:
                r = r + add_ref[...].astype(F32)
            o_ref[...] = r.astype(o_ref.dtype)

        if nk == 1:
            finish(_dot(a_ref[...], b_ref[...], dims))
            return
        acc = scr[0]
        k = pl.program_id(2)

        @pl.when(k == 0)
        def _():
            acc[...] = jnp.zeros_like(acc)

        acc[...] += _dot(a_ref[...], b_ref[...], dims)

        @pl.when(k == nk - 1)
        def _():
            finish(acc[...])

    a_spec = pl.BlockSpec((tk, tm), lambda i, j, k: (k, i)) if ta else pl.BlockSpec((tm, tk), lambda i, j, k: (i, k))
    b_spec = pl.BlockSpec((tn, tk), lambda i, j, k: (j, k)) if tb else pl.BlockSpec((tk, tn), lambda i, j, k: (k, j))
    o_spec = pl.BlockSpec((tm, tn), lambda i, j, k: (i, j))
    in_specs = [a_spec, b_spec] + ([o_spec] if add is not None else [])
    args = (a, b) + ((add,) if add is not None else ())
    scratch = [pltpu.VMEM((tm, tn), F32)] if nk > 1 else []
    o_shape = jax.ShapeDtypeStruct((M, N), out_dtype)
    if not carry:
        return pl.pallas_call(
            kern, name=name, grid=(gi, gj, nk), in_specs=in_specs, out_specs=o_spec, out_shape=o_shape, scratch_shapes=scratch,
            compiler_params=_cparams(("parallel", "parallel", "arbitrary")),
        )(*args)
    res = pl.pallas_call(
        kern, name=name, grid=(gi, gj, nk), in_specs=in_specs + carry.in_specs, out_specs=[o_spec] + carry.out_specs,
        out_shape=[o_shape] + carry.out_shape, scratch_shapes=scratch + carry.scratch,
        input_output_aliases=carry.aliases(n_in, 1), compiler_params=_cparams(("arbitrary", "arbitrary", "arbitrary")),
    )(*args, *carry.args)
    return res[0], list(res[1:])


def _stage_specs(acts, params, consts, tile, ct):
    act_specs = [pl.BlockSpec((tile, ct), functools.partial(lambda j, i, o: (i, o + j), o=off // ct)) for _, off in acts]
    par_specs = [pl.BlockSpec(bs, functools.partial(lambda j, i, im: im(j), im=im)) for _, bs, im in params]
    con_specs = [pl.BlockSpec(bs, functools.partial(lambda j, i, im: im(j), im=im)) for _, bs, im in consts]
    return act_specs, par_specs, con_specs


def _stage_fwd(f, name, n_rows, width, tile, ct, acts, params, consts, out_dtypes):
    for _, off in acts:
        assert off % ct == 0
    na, npar, nc = len(acts), len(params), len(consts)

    def kern(*refs):
        a = [r[...].astype(F32) for r in refs[:na]]
        p = [r[...] for r in refs[na:na + npar]]
        c = [r[...] for r in refs[na + npar:na + npar + nc]]
        outs = f(a, p, c, pl.program_id(1) * tile)
        for r, o in zip(refs[na + npar + nc:], outs):
            r[...] = o.astype(r.dtype)

    act_specs, par_specs, con_specs = _stage_specs(acts, params, consts, tile, ct)
    o_spec = pl.BlockSpec((tile, ct), lambda j, i: (i, j))
    outs = pl.pallas_call(
        kern, name=name, grid=(width // ct, n_rows // tile),
        in_specs=act_specs + par_specs + con_specs, out_specs=[o_spec] * len(out_dtypes),
        out_shape=[jax.ShapeDtypeStruct((n_rows, width), d) for d in out_dtypes],
        compiler_params=_cparams(("parallel", "parallel")),
    )(*[a for a, _ in acts], *[p for p, _, _ in params], *[c for c, _, _ in consts])
    return tuple(outs)


def _stage_bwd(f, name, n_rows, width, tile, ct, acts, params, consts, couts, dact_dtypes, extra_add=None):
    na, npar, nc, no = len(acts), len(params), len(consts), len(couts)
    nx = 0 if extra_add is None else 1

    def kern(*refs):
        a = [r[...].astype(F32) for r in refs[:na]]
        p = [r[...] for r in refs[na:na + npar]]
        c = [r[...] for r in refs[na + npar:na + npar + nc]]
        base = na + npar + nc
        co = [r[...].astype(F32) for r in refs[base:base + no]]
        base += no
        x_refs = refs[base:base + nx]
        base += nx
        da_refs = refs[base:base + na]
        dp_refs = refs[base + na:]
        row0 = pl.program_id(1) * tile
        _, vjp = jax.vjp(lambda aa, pp: tuple(f(aa, pp, c, row0)), a, p)
        da, dp = vjp(tuple(co))
        for k, (r, d) in enumerate(zip(da_refs, da)):
            if k == 0 and nx:
                d = d + x_refs[0][...].astype(F32)
            r[...] = d.astype(r.dtype)
        first = pl.program_id(1) == 0
        for r, d in zip(dp_refs, dp):
            @pl.when(first)
            def _(r=r, d=d):
                r[...] = d

            @pl.when(jnp.logical_not(first))
            def _(r=r, d=d):
                r[...] += d

    act_specs, par_specs, con_specs = _stage_specs(acts, params, consts, tile, ct)
    t_spec = pl.BlockSpec((tile, ct), lambda j, i: (i, j))
    co_specs = [pl.BlockSpec((tile, ct), functools.partial(lambda j, i, o: (i, o + j), o=off // ct)) for _, off in couts]
    x_specs = [] if extra_add is None else [pl.BlockSpec((tile, ct), functools.partial(lambda j, i, o: (i, o + j), o=extra_add[1] // ct))]
    x_args = [] if extra_add is None else [extra_add[0]]
    outs = pl.pallas_call(
        kern, name=name, grid=(width // ct, n_rows // tile),
        in_specs=act_specs + par_specs + con_specs + co_specs + x_specs,
        out_specs=[t_spec] * na + par_specs,
        out_shape=[jax.ShapeDtypeStruct((n_rows, width), d) for d in dact_dtypes]
        + [jax.ShapeDtypeStruct(p.shape, F32) for p, _, _ in params],
        compiler_params=_cparams(("parallel", "arbitrary")),
    )(*[a for a, _ in acts], *[p for p, _, _ in params], *[c for c, _, _ in consts], *[c for c, _ in couts], *x_args)
    return tuple(outs[:na]), tuple(outs[na:])


def _row(ct):
    return (1, ct), (lambda j: (0, j))


def _f_rmsnorm(a, p, c, row0):
    x, = a
    g, = p
    return (x * lax.rsqrt(jnp.mean(x * x, axis=-1, keepdims=True) + NORM_EPS) * g,)


def _f_lora_act(a, p, c, row0, widths):
    x, = a
    dl, al = widths
    col = lax.broadcasted_iota(jnp.int32, x.shape, 1)
    return (jnp.where(col < dl, jnp.tanh(x), jnp.where(col < dl + al, x, _sigmoid(x))),)


def _f_rwkv_pre(a, p, c, row0):
    k, wlin, alin = a
    w0, a0, k_k, k_a = p
    gsum, = c
    w = -_softplus(-(w0 + wlin)) - 0.5
    lw = -jnp.exp(w)
    alpha = _sigmoid(a0 + alin)
    kk = k * k_k
    ss = _nn(kk * kk, gsum, HI)
    kk = kk * lax.rsqrt(jnp.maximum(ss, 1e-24))
    k2 = k * (1.0 + (alpha - 1.0) * k_a)
    return lw, k2, -kk, kk * alpha


def _f_rwkv_post(a, p, c, row0):
    y, r, k2, v, g = a
    ln_g, ln_b, r_k = p
    gsum, = c
    inv = 1.0 / HEAD
    mean = _nn(y, gsum, HI) * inv
    yc = y - mean
    var = _nn(yc * yc, gsum, HI) * inv
    yn = yc * lax.rsqrt(var + GN_EPS) * ln_g + ln_b
    bonus = _nn(r * k2 * r_k, gsum, HI)
    return ((yn + bonus * v) * g,)


def _f_lru_gates(a, p, c, row0, seq):
    xc, = a
    wr, br, wi, bi, lam = p
    xb = xc.astype(BF16)
    rg = _sigmoid(_nn(xb, wr[0].astype(BF16)) + br)
    ig = _sigmoid(_nn(xb, wi[0].astype(BF16)) + bi)
    log_a = -LRU_C * rg * _softplus(-lam)
    a_t = jnp.exp(log_a)
    mult = jnp.sqrt(_neg_expm1(2.0 * log_a))
    row = row0 + lax.broadcasted_iota(jnp.int32, xc.shape, 0)
    mult = jnp.where(row % seq == 0, 1.0, mult)
    return a_t, mult * ig * xc


def _f_lru_post(a, p, c, row0):
    h, gate = a
    g, = p
    y = h * _gelu(gate)
    return (y * lax.rsqrt(jnp.mean(y * y, axis=-1, keepdims=True) + NORM_EPS) * g,)


def _f_swiglu(a, p, c, row0):
    gate, up = a
    return (gate * _sigmoid(gate) * up,)


def _shift_down(x, s, row):
    return jnp.where(row >= s, pltpu.roll(x, s, 0), 0.0)


def _shift_up(x, s, row):
    n = x.shape[0]
    return jnp.where(row < n - s, pltpu.roll(x, n - s, 0), 0.0)


def _seq_call(kern, name, bl, seq, width, ct, ins, outs, acc_outs=()):
    def spec(off, rows):
        if rows is None:
            return pl.BlockSpec((seq, ct), functools.partial(lambda j, b, o: (b, o + j), o=off // ct))
        return pl.BlockSpec((rows, ct), lambda j, b: (0, j))

    in_specs = [spec(off, rows) for _, off, rows in ins]
    out_specs = [spec(0, None) for _ in outs] + [spec(0, rows) for _, rows in acc_outs]
    out_shape = [jax.ShapeDtypeStruct((bl * seq, width), d) for d in outs] + [jax.ShapeDtypeStruct((rows, width), F32) for _, rows in acc_outs]
    res = pl.pallas_call(
        kern, name=name, grid=(width // ct, bl), in_specs=in_specs, out_specs=out_specs, out_shape=out_shape,
        compiler_params=_cparams(("parallel", "arbitrary")),
    )(*[a for a, _, _ in ins])
    return tuple(res)


def _acc(ref, val):
    first = pl.program_id(1) == 0

    @pl.when(first)
    def _():
        ref[...] = val

    @pl.when(jnp.logical_not(first))
    def _():
        ref[...] += val


def _lerp_fwd(p, off, mu, bl, seq, width, ct):
    def kern(p_ref, mu_ref, o_ref):
        x = p_ref[...]
        row = lax.broadcasted_iota(jnp.int32, x.shape, 0)
        o_ref[...] = x + (_shift_down(x, 1, row) - x) * mu_ref[...]

    return _seq_call(kern, "lerp_fwd", bl, seq, width, ct, [(p, off, None), (mu, 0, 1)], [F32])[0]


def _lerp_bwd(p, off, mu, dps, bl, seq, width, ct, out_dtype):
    def kern(p_ref, mu_ref, d_ref, dp_ref, dmu_ref):
        x = p_ref[...]
        d = d_ref[...].astype(F32)
        m = mu_ref[...]
        row = lax.broadcasted_iota(jnp.int32, x.shape, 0)
        dp_ref[...] = (d * (1.0 - m) + _shift_up(d * m, 1, row)).astype(dp_ref.dtype)
        _acc(dmu_ref, jnp.sum(d * (_shift_down(x, 1, row) - x), axis=0, keepdims=True))

    return _seq_call(kern, "lerp_bwd", bl, seq, width, ct, [(p, off, None), (mu, 0, 1), (dps, 0, None)], [out_dtype], [(None, 1)])


def _conv_fwd(p, off, cw, cb, bl, seq, width, ct):
    nw = cw.shape[0]

    def kern(x_ref, w_ref, b_ref, o_ref):
        x = x_ref[...]
        row = lax.broadcasted_iota(jnp.int32, x.shape, 0)
        acc = b_ref[...] + x * w_ref[pl.ds(nw - 1, 1), :]
        for s in range(1, nw):
            acc = acc + _shift_down(x, s, row) * w_ref[pl.ds(nw - 1 - s, 1), :]
        o_ref[...] = acc

    return _seq_call(kern, "conv_fwd", bl, seq, width, ct, [(p, off, None), (cw, 0, nw), (cb, 0, 1)], [F32])[0]


def _conv_bwd(p, off, cw, dxc, bl, seq, width, ct, out_dtype):
    nw = cw.shape[0]

    def kern(x_ref, w_ref, d_ref, dx_ref, dw_ref, db_ref):
        x = x_ref[...]
        d = d_ref[...]
        row = lax.broadcasted_iota(jnp.int32, x.shape, 0)
        wrow = lax.broadcasted_iota(jnp.int32, dw_ref.shape, 0)
        dx = d * w_ref[pl.ds(nw - 1, 1), :]
        dw = jnp.where(wrow == nw - 1, jnp.sum(d * x, axis=0, keepdims=True), 0.0)
        for s in range(1, nw):
            dx = dx + _shift_up(d, s, row) * w_ref[pl.ds(nw - 1 - s, 1), :]
            dw = jnp.where(wrow == nw - 1 - s, jnp.sum(d * _shift_down(x, s, row), axis=0, keepdims=True), dw)
        dx_ref[...] = dx.astype(dx_ref.dtype)
        _acc(dw_ref, dw)
        _acc(db_ref, jnp.sum(d, axis=0, keepdims=True))

    return _seq_call(kern, "conv_bwd", bl, seq, width, ct, [(p, off, None), (cw, 0, nw), (dxc, 0, None)], [out_dtype], [(None, nw), (None, 1)])


def _lru_scan_fwd(a, bx, bl, seq, width, ct):
    def kern(a_ref, b_ref, h_ref):
        av = a_ref[...]
        bv = b_ref[...]
        row = lax.broadcasted_iota(jnp.int32, av.shape, 0)
        d = 1
        while d < seq:
            a_sh = jnp.where(row >= d, pltpu.roll(av, d, 0), 1.0)
            b_sh = jnp.where(row >= d, pltpu.roll(bv, d, 0), 0.0)
            bv = av * b_sh + bv
            av = av * a_sh
            d *= 2
        h_ref[...] = bv

    return _seq_call(kern, "lru_scan_fwd", bl, seq, width, ct, [(a, 0, None), (bx, 0, None)], [F32])[0]


def _lru_scan_bwd(a, h, dh, bl, seq, width, ct):
    def kern(a_ref, h_ref, d_ref, da_ref, db_ref):
        row = lax.broadcasted_iota(jnp.int32, a_ref.shape, 0)
        al = _shift_up(a_ref[...], 1, row)
        g = d_ref[...]
        d = 1
        while d < seq:
            keep = row < seq - d
            al_sh = jnp.where(keep, pltpu.roll(al, seq - d, 0), 1.0)
            g_sh = jnp.where(keep, pltpu.roll(g, seq - d, 0), 0.0)
            g = al * g_sh + g
            al = al * al_sh
            d *= 2
        db_ref[...] = g
        da_ref[...] = g * _shift_down(h_ref[...], 1, row)

    return _seq_call(kern, "lru_scan_bwd", bl, seq, width, ct, [(a, 0, None), (h, 0, None), (dh, 0, None)], [F32, F32])


_FORMS = {"nn": ((1,), (0,)), "nt": ((1,), (1,)), "tn": ((0,), (0,))}
_FORM_GRADS = {"nn": (("nt", "g", "b"), ("tn", "a", "g")),
               "nt": (("nn", "g", "b"), ("tn", "g", "a")),
               "tn": (("nt", "b", "g"), ("nn", "a", "g"))}


def _split_bf16(x):
    hi = x.astype(BF16)
    return hi, (x - hi.astype(F32)).astype(BF16)


def _pdot_raw(a, b, form, passes):
    dims = _FORMS[form]
    if passes == 1:
        return _dot(a.astype(BF16), b.astype(BF16), dims)
    ah, al = _split_bf16(a)
    bh, bl = _split_bf16(b)
    return _dot(ah, bh, dims) + (_dot(ah, bl, dims) + _dot(al, bh, dims))


@functools.partial(jax.custom_vjp, nondiff_argnums=(2, 3))
def _pdot(a, b, form, passes):
    return _pdot_raw(a, b, form, passes)


def _pdot_fwd(a, b, form, passes):
    return _pdot_raw(a, b, form, passes), (a, b)


def _pdot_bwd(form, passes, res, g):
    vals = {"a": res[0], "b": res[1], "g": g}
    (fa, xa, ya), (fb, xb, yb) = _FORM_GRADS[form]
    return _pdot_raw(vals[xa], vals[ya], fa, passes), _pdot_raw(vals[xb], vals[yb], fb, passes)


_pdot.defvjp(_pdot_fwd, _pdot_bwd)


def _scan_chunk2(S0, r, lw, k, v, a, b, p_main=1, p_inv=3):
    y, s = _scan_block([S0], [[(r, lw, k, v, a, b)]], p_main, p_inv)
    return y[0][0], s[0]


def _scan_block(states, units, p_main=1, p_inv=1):
    C = units[0][0][0].shape[0]
    C2 = 2 * C
    ri = lax.broadcasted_iota(jnp.int32, (C, C), 0)
    ci = lax.broadcasted_iota(jnp.int32, (C, C), 1)
    tri = (ri >= ci).astype(F32)
    i2 = lax.broadcasted_iota(jnp.int32, (C2, C2), 0)
    j2 = lax.broadcasted_iota(jnp.int32, (C2, C2), 1)
    same = (i2 // C) == (j2 // C)
    strict = jnp.logical_and(same, (i2 % C) > (j2 % C))
    incl = jnp.logical_and(same, (i2 % C) >= (j2 % C))
    eye = (i2 == j2).astype(F32)
    lane = lax.broadcasted_iota(jnp.int32, (1, LANE), 1)
    m0, m1 = (lane < HEAD).astype(F32), (lane >= HEAD).astype(F32)
    stack = lambda z: jnp.concatenate([z * m0, z * m1], axis=0)
    ids = [(i, g) for g in range(len(units[0])) for i in range(len(units))]

    pre = {}
    for i, g in ids:
        r, lw, k, v, a, b = units[i][g]
        cs = _nn(tri, lw, HI)
        p_incl = jnp.exp(cs)
        p_rec = jnp.exp(-cs)
        xr = jnp.concatenate([stack(a * jnp.exp(cs - lw)), stack(r * p_incl)], axis=0)
        bk = jnp.concatenate([stack(b * p_rec), stack(k * p_rec)], axis=0)
        pre[i, g] = (xr, bk, stack(v), jnp.exp(jnp.sum(lw, axis=0, keepdims=True)))
    gm = {u: _pdot(pre[u][0], pre[u][1], "nt", p_main) for u in ids}
    a_ak = {u: jnp.where(strict, gm[u][:C2, C2:], 0.0) for u in ids}
    r_bk = {u: jnp.concatenate([jnp.where(incl, gm[u][C2:, :C2], 0.0), jnp.where(incl, gm[u][C2:, C2:], 0.0)], axis=1) for u in ids}
    pw = {u: jnp.where(strict, gm[u][:C2, :C2], 0.0) for u in ids}
    x = {u: eye + pw[u] for u in ids}
    n = 2
    while n < C:
        pw = {u: _pdot(pw[u], pw[u], "nn", p_inv) for u in ids}
        x = {u: x[u] + _pdot(x[u], pw[u], "nn", p_inv) for u in ids}
        n *= 2
    akv = {u: _pdot(a_ak[u], pre[u][2], "nn", p_main) for u in ids}

    states = list(states)
    ys = [[None] * len(units[0]) for _ in units]
    for i, g in ids:
        xr, bk, vs, p_last = pre[i, g]
        xs = _pdot(xr, states[i], "nt", p_main)
        us = _pdot(x[i, g], xs[:C2] + akv[i, g], "nn", p_inv)
        uv = jnp.concatenate([us, vs], axis=0)
        y2 = xs[C2:] + _pdot(r_bk[i, g], uv, "nn", p_main)
        ys[i][g] = y2[:C] + y2[C:]
        states[i] = (states[i] + _pdot(uv, bk, "tn", p_main)) * p_last
    return ys, states


def _scan_dims(seq, rw):
    G = _pick(seq // SCAN_CHUNK, (SCAN_GROUP, 2, 1))
    NP = _pick(rw // LANE, (SCAN_PAIRS, 1))
    C = SCAN_CHUNK * G
    return SCAN_CHUNK, G, NP, C, seq // C, rw // (NP * LANE)


def _rwkv_scan_fwd(r, lw, k2, v, na, bb, p, bl, seq, rw):
    cs, G, NP, C, nc, nhg = _scan_dims(seq, rw)

    def kern(r_ref, lw_ref, k_ref, v_ref, a_ref, b_ref, y_ref, st_ref, s_scr):
        @pl.when(pl.program_id(2) == 0)
        def _():
            s_scr[...] = jnp.zeros_like(s_scr)

        in_refs = (r_ref, lw_ref, k_ref, v_ref, a_ref, b_ref)
        st_ref[...] = s_scr[...]
        units = [[tuple(ref[pl.ds(g * cs, cs), pl.ds(i * LANE, LANE)] for ref in in_refs) for g in range(G)] for i in range(NP)]
        ys, s_new = _scan_block([s_scr[i] for i in range(NP)], units)
        for i in range(NP):
            s_scr[i] = s_new[i]
            for g in range(G):
                y_ref[pl.ds(g * cs, cs), pl.ds(i * LANE, LANE)] = ys[i][g]

    def tok(off):
        return pl.BlockSpec((C, NP * LANE), functools.partial(lambda b, h, c, o: (b * nc + c, o + h), o=off // (NP * LANE)))

    y, st = pl.pallas_call(
        kern, name="rwkv_scan_fwd", grid=(bl, nhg, nc),
        in_specs=[tok(0), tok(0), tok(0), tok(2 * rw), tok(0), tok(0)],
        out_specs=[tok(0), pl.BlockSpec((NP, LANE, LANE), lambda b, h, c: ((b * nhg + h) * nc + c, 0, 0))],
        out_shape=[jax.ShapeDtypeStruct((bl * seq, rw), F32), jax.ShapeDtypeStruct((bl * nhg * nc * NP, LANE, LANE), F32)],
        scratch_shapes=[pltpu.VMEM((NP, LANE, LANE), F32)],
        compiler_params=_cparams(("parallel", "parallel", "arbitrary")),
    )(p, lw, k2, p, na, bb)
    return y, st


def _rwkv_scan_bwd(lw, k2, na, bb, p, st, dy, bl, seq, rw, carry=None):
    cs, G, NP, C, nc, nhg = _scan_dims(seq, rw)
    nt = len(carry.tensors) if carry else 0

    def kern(*refs):
        in_refs = refs[:6]
        st_ref, dy_ref = refs[6:8]
        out_refs = refs[8 + 2 * nt:14 + 2 * nt]
        ds_scr = refs[14 + 3 * nt]
        if carry:
            step = (pl.program_id(0) * nhg + pl.program_id(1)) * nc + pl.program_id(2)
            carry.hook(step, bl * nhg * nc - 1, refs[8:8 + nt], refs[14 + 2 * nt:14 + 3 * nt], refs[15 + 3 * nt:])

        @pl.when(pl.program_id(2) == 0)
        def _():
            ds_scr[...] = jnp.zeros_like(ds_scr)

        win = lambda ref, i, g: ref[pl.ds(g * cs, cs), pl.ds(i * LANE, LANE)]
        units = [[tuple(win(ref, i, g) for ref in in_refs) for g in range(G)] for i in range(NP)]
        _, vjp = jax.vjp(_scan_block, [st_ref[i] for i in range(NP)], units)
        dys = [[win(dy_ref, i, g) for g in range(G)] for i in range(NP)]
        ds, dunits = vjp((dys, [ds_scr[i] for i in range(NP)]))
        for i in range(NP):
            ds_scr[i] = ds[i]
            for g in range(G):
                for ref, d in zip(out_refs, dunits[i][g]):
                    ref[pl.ds(g * cs, cs), pl.ds(i * LANE, LANE)] = d

    def tok(off):
        return pl.BlockSpec((C, NP * LANE), functools.partial(lambda b, h, c, o: (b * nc + (nc - 1 - c), o + h), o=off // (NP * LANE)))

    st_spec = pl.BlockSpec((NP, LANE, LANE), lambda b, h, c: ((b * nhg + h) * nc + (nc - 1 - c), 0, 0))
    in_specs = [tok(0), tok(0), tok(0), tok(2 * rw), tok(0), tok(0), st_spec, tok(0)]
    out_shape = [jax.ShapeDtypeStruct((bl * seq, rw), F32)] * 6
    scratch = [pltpu.VMEM((NP, LANE, LANE), F32)]
    if not carry:
        return pl.pallas_call(
            kern, name="rwkv_scan_bwd", grid=(bl, nhg, nc), in_specs=in_specs, out_specs=[tok(0)] * 6, out_shape=out_shape,
            scratch_shapes=scratch, compiler_params=_cparams(("parallel", "parallel", "arbitrary")),
        )(p, lw, k2, p, na, bb, st, dy)
    res = pl.pallas_call(
        kern, name="rwkv_scan_bwd", grid=(bl, nhg, nc), in_specs=in_specs + carry.in_specs, out_specs=[tok(0)] * 6 + carry.out_specs,
        out_shape=out_shape + carry.out_shape, scratch_shapes=scratch + carry.scratch, input_output_aliases=carry.aliases(8, 6),
        compiler_params=_cparams(("arbitrary", "arbitrary", "arbitrary")),
    )(p, lw, k2, p, na, bb, st, dy, *carry.args)
    return res[:6], list(res[6:])


def _loss_head(h2, g_final, target, tile):
    n, d = h2.shape
    nt = n // tile

    def kern(h_ref, g_ref, t_ref, dh_ref, dg_ref, l_ref):
        def f(h, g):
            y = h * lax.rsqrt(jnp.mean(h * h, axis=-1, keepdims=True) + NORM_EPS) * g
            e = y - t_ref[...]
            return 0.5 * jnp.sum(jnp.mean(e * e, axis=-1, keepdims=True))

        loss, (dh, dg) = jax.value_and_grad(f, argnums=(0, 1))(h_ref[...], g_ref[...])
        dh_ref[...] = dh
        first = pl.program_id(0) == 0

        @pl.when(first)
        def _():
            dg_ref[...] = dg
            l_ref[...] = jnp.zeros_like(l_ref) + loss

        @pl.when(jnp.logical_not(first))
        def _():
            dg_ref[...] += dg
            l_ref[...] += loss

    row = pl.BlockSpec((tile, d), lambda i: (i, 0))
    vec = pl.BlockSpec((1, d), lambda i: (0, 0))
    return pl.pallas_call(
        kern, name="loss_head", grid=(nt,), in_specs=[row, vec, row],
        out_specs=[row, vec, pl.BlockSpec((1, LANE), lambda i: (0, 0))],
        out_shape=[jax.ShapeDtypeStruct((n, d), F32), jax.ShapeDtypeStruct((1, d), F32), jax.ShapeDtypeStruct((1, LANE), F32)],
        compiler_params=_cparams(("arbitrary",)),
    )(h2, g_final, target)


def _adamw(parts, w, m, v, name, carry=None):
    n_parts, R, Cc = parts.shape
    tr = _pick(R, tuple(t for t in (1024, 512, 256, 128, 64, 32, 16) if t * Cc <= 128 * 1024) + (8,))
    c1, c2 = 1.0 - ADAM_B1, 1.0 - ADAM_B2
    bc1, bc2 = 1.0 - ADAM_B1 ** ADAM_STEP, 1.0 - ADAM_B2 ** ADAM_STEP

    nt = len(carry.tensors) if carry else 0

    def kern(*refs):
        p_ref, w_ref, m_ref, v_ref = refs[:4]
        g_ref, d_ref, nm_ref, nv_ref = refs[4 + 2 * nt:8 + 2 * nt]
        if carry:
            carry.hook(pl.program_id(0), R // tr - 1, refs[4:4 + nt], refs[8 + 2 * nt:8 + 3 * nt], refs[8 + 3 * nt:])
        g = p_ref[0].astype(F32)
        for s in range(1, n_parts):
            g = g + p_ref[s].astype(F32)
        m2 = ADAM_B1 * m_ref[...] + c1 * g
        v2 = ADAM_B2 * v_ref[...] + c2 * (g * g)
        g_ref[...] = g
        nm_ref[...] = m2
        nv_ref[...] = v2
        d_ref[...] = -ADAM_LR * ((m2 / bc1) / (jnp.sqrt(v2 / bc2) + ADAM_EPS) + ADAM_WD * w_ref[...])

    blk = pl.BlockSpec((tr, Cc), lambda i: (i, 0))
    in_specs = [pl.BlockSpec((n_parts, tr, Cc), lambda i: (0, i, 0)), blk, blk, blk]
    out_shape = [jax.ShapeDtypeStruct((R, Cc), F32)] * 4
    if not carry:
        return pl.pallas_call(
            kern, name=name, grid=(R // tr,), in_specs=in_specs, out_specs=[blk] * 4, out_shape=out_shape,
            compiler_params=_cparams(("parallel",)),
        )(parts, w, m, v)
    res = pl.pallas_call(
        kern, name=name, grid=(R // tr,), in_specs=in_specs + carry.in_specs, out_specs=[blk] * 4 + carry.out_specs,
        out_shape=out_shape + carry.out_shape, scratch_shapes=carry.scratch, input_output_aliases=carry.aliases(4, 4),
        compiler_params=_cparams(("arbitrary",)),
    )(parts, w, m, v, *carry.args)
    return res[:4], list(res[4:])


def _sum_parts(parts, name):
    n_parts, R, Cc = parts.shape
    tr = _pick(R, (512, 256, 128, 64, 32, 16, 8))

    def kern(p_ref, o_ref):
        g = p_ref[0]
        for s in range(1, n_parts):
            g = g + p_ref[s]
        o_ref[...] = g

    return pl.pallas_call(
        kern, name=name, grid=(R // tr,), in_specs=[pl.BlockSpec((n_parts, tr, Cc), lambda i: (0, i, 0))],
        out_specs=pl.BlockSpec((tr, Cc), lambda i: (i, 0)), out_shape=jax.ShapeDtypeStruct((R, Cc), F32),
        compiler_params=_cparams(("parallel",)),
    )(parts)


def _all_gather(xs, name):
    n = len(xs)

    def body(*refs):
        x_refs, o_refs = refs[:n], refs[n:2 * n]
        send_sems, recv_sems, local_sems = refs[2 * n:]
        x, y, c = _coords()
        me, sibling = (x, y, c), (x, y, 1 - c)
        chips = [(1 - x, y), (x, 1 - y), (1 - x, 1 - y)]
        waits = []
        for t in range(n):
            o_ref = o_refs[t]

            def slot(px, py, pc, o_ref=o_ref):
                return o_ref.at[4 * px + 2 * py + pc]

            def copy(k, block, to, src=None, t=t, slot=slot):
                return pltpu.make_async_remote_copy(
                    src_ref=slot(*block) if src is None else src, dst_ref=slot(*block),
                    send_sem=send_sems.at[t, k], recv_sem=recv_sems.at[t, k], device_id=to, device_id_type=MESH)

            mine = pltpu.make_async_copy(x_refs[t], slot(*me), local_sems.at[t])
            mine.start()
            first = [copy(0, me, sibling, src=x_refs[t])]
            first += [copy(1 + j, me, (*chip, c), src=x_refs[t]) for j, chip in enumerate(chips)]
            for cp in first:
                cp.start()
            waits.append((copy, mine, first))
        for t in range(n):
            copy, mine, first = waits[t]
            passed = [copy(4 + j, (*chip, c), sibling) for j, chip in enumerate(chips)]
            for j, chip in enumerate(chips):
                copy(1 + j, (*chip, c), me).wait_recv()
                passed[j].start()
            copy(0, sibling, me).wait_recv()
            for j, chip in enumerate(chips):
                copy(4 + j, (*chip, 1 - c), me).wait_recv()
            for cp in first + passed:
                cp.wait_send()
            mine.wait()

    any_spec = pl.BlockSpec(memory_space=pl.ANY)
    outs = pl.pallas_call(
        body, name=name, in_specs=[any_spec] * n, out_specs=[any_spec] * n,
        out_shape=[jax.ShapeDtypeStruct((N_DEV,) + a.shape, a.dtype) for a in xs],
        scratch_shapes=[pltpu.SemaphoreType.DMA((n, 7)), pltpu.SemaphoreType.DMA((n, 7)), pltpu.SemaphoreType.DMA((n,))],
    )(*xs)
    return list(outs)


def _exchange_now(carry, name):
    nt = len(carry.tensors)

    def body(*refs):
        carry.hook(0, 0, refs[:nt], refs[2 * nt:3 * nt], refs[3 * nt:])

    return list(pl.pallas_call(
        body, name=name, in_specs=carry.in_specs, out_specs=carry.out_specs, out_shape=carry.out_shape,
        scratch_shapes=carry.scratch, input_output_aliases=carry.aliases(0, 0),
    )(*carry.args))


class _GradQueue:
    def __init__(self):
        self.tensors, self.fifo, self.n_flush = {}, [], 0

    def push(self, name, src, n_pieces, cost_us):
        rows = src.shape[1] // n_pieces
        assert rows * n_pieces == src.shape[1] and rows % 16 == 0
        self.tensors[name] = [src, lax.empty(src.shape, src.dtype)]
        self.fifo += [(name, p * rows, rows, cost_us / n_pieces) for p in range(n_pieces)]

    def take(self, capacity_us):
        picked = []
        while self.fifo and capacity_us >= 0.6 * self.fifo[0][3]:
            picked.append(self.fifo.pop(0))
            capacity_us -= picked[-1][3]
        if not picked:
            return None
        names = list(dict.fromkeys(n for n, _, _, _ in picked))
        carry = _Carry([tuple(self.tensors[n]) for n in names], [(names.index(n), r0, rows) for n, r0, rows, _ in picked])
        carry.names = names
        return carry

    def done(self, carry, dsts):
        for n, d in zip(carry.names, dsts):
            self.tensors[n][1] = d

    def flush(self):
        carry = self.take(float("inf"))
        if carry:
            self.done(carry, _exchange_now(carry, "exchange_rest_%d" % self.n_flush))
            self.n_flush += 1

    def result(self, name):
        assert not any(n == name for n, _, _, _ in self.fifo)
        return self.tensors[name][1]


def _cols_from_shards(g):
    return jnp.transpose(g, (1, 0, 2)).reshape(g.shape[1], N_DEV * g.shape[2])


def _shards_from_cols(w):
    r, n = w.shape
    return jnp.transpose(w.reshape(r, N_DEV, n // N_DEV), (1, 0, 2))


def _pad_cols(w, to):
    return jnp.pad(w, ((0, 0), (0, to - w.shape[1])))


def _pack(arrs):
    flat = jnp.concatenate([a.reshape(-1) for a in arrs])
    n = _rup(flat.shape[0], 8 * LANE)
    return jnp.pad(flat, (0, n - flat.shape[0])).reshape(n // LANE, LANE)


def _unpack(mat, shapes):
    flat = mat.reshape(-1)
    out, o = [], 0
    for s in shapes:
        n = math.prod(s)
        out.append(flat[o:o + n].reshape(s))
        o += n
    return out


_SMALL = ["norm_mix_g", "mu_shift", "rwkv_w0", "rwkv_a0", "rwkv_k_k", "rwkv_k_a", "rwkv_r_k", "rwkv_ln_g", "rwkv_ln_b", "conv_b",
          "lru_wr", "lru_br", "lru_wi", "lru_bi", "lru_lambda", "lru_norm_g", "norm_ffn_g", "norm_final_g"]
_SMALL_SHARDED = ["rwkv_w2", "rwkv_a2", "rwkv_g2", "conv_w"]
_BIG = ["w_in", "w_out", "ffn_w_gate", "ffn_w_up", "ffn_w_down"]
_WEIGHTS = ['norm_mix_g', 'w_in', 'mu_shift', 'rwkv_w0', 'rwkv_w2', 'rwkv_a0', 'rwkv_a2', 'rwkv_g2', 'rwkv_k_k', 'rwkv_k_a', 'rwkv_r_k',
            'rwkv_ln_g', 'rwkv_ln_b', 'conv_w', 'conv_b', 'lru_wr', 'lru_br', 'lru_wi', 'lru_bi', 'lru_lambda', 'lru_norm_g', 'w_out',
            'norm_ffn_g', 'ffn_w_gate', 'ffn_w_up', 'ffn_w_down', 'norm_final_g']


def _step(W, M, V, x, loss_target):
    bl, seq, d = x.shape
    n = bl * seq
    rw = W["rwkv_w0"].shape[1]
    nh = W["rwkv_r_k"].shape[1]
    assert W["rwkv_r_k"].shape[2] == HEAD and nh * HEAD == rw and rw % LANE == 0
    dl, al, gl = W["rwkv_w2"].shape[1], W["rwkv_a2"].shape[1], W["rwkv_g2"].shape[1]
    dlp, alp, glp = _rup(dl, LANE), _rup(al, LANE), _rup(gl, LANE)
    lorap = dlp + alp + glp
    lw_ = W["conv_b"].shape[1]
    nblk, lbw = W["lru_wr"].shape[1], W["lru_wr"].shape[2]
    assert lbw == LANE and nblk * lbw == lw_
    o_xb, o_gate, o_rw = 0, lw_, 2 * lw_
    o_lora = 3 * rw
    rwp = o_lora + lorap
    inp = o_rw + rwp
    nsh_ff = W["ffn_w_gate"].shape[2]
    dff = N_DEV * nsh_ff
    dffp = _rup(dff, 1024) if dff >= 1024 else _rup(dff, LANE)
    x2 = x.reshape(n, d)
    tgt2 = loss_target.reshape(n, d)

    small_sh = _pack([W[k][0] for k in _SMALL_SHARDED])
    g_in, g_out, g_gate, g_up, g_down, g_small = _all_gather(
        [W["w_in"][0].astype(BF16), W["w_out"][0].astype(BF16), W["ffn_w_gate"][0].astype(BF16), W["ffn_w_up"][0].astype(BF16),
         W["ffn_w_down"][0].astype(BF16), small_sh], "gather_weights")
    w_in_l = _cols_from_shards(g_in)
    o1 = 3 * rw
    w_in = jnp.concatenate([w_in_l[:, o1 + dl + al + gl:], w_in_l[:, :o1], _pad_cols(w_in_l[:, o1:o1 + dl], dlp),
                            _pad_cols(w_in_l[:, o1 + dl:o1 + dl + al], alp), _pad_cols(w_in_l[:, o1 + dl + al:o1 + dl + al + gl], glp)], axis=1)
    w_out = g_out.reshape(N_DEV * g_out.shape[1], d)
    w_gu = jnp.concatenate([_pad_cols(_cols_from_shards(g_gate), dffp), _pad_cols(_cols_from_shards(g_up), dffp)], axis=1)
    w_down = jnp.pad(g_down.reshape(dff, d), ((0, dffp - dff), (0, 0)))
    sm_shapes = [W[k][0].shape for k in _SMALL_SHARDED]
    sm = [_unpack(g_small[s], sm_shapes) for s in range(N_DEV)]
    w2, a2, g2, conv_w = [jnp.concatenate([sm[s][i] for s in range(N_DEV)], axis=1) for i in range(4)]
    w_lora = jnp.zeros((lorap, 3 * rw), F32)
    w_lora = w_lora.at[:dl, :rw].set(w2).at[dlp:dlp + al, rw:2 * rw].set(a2).at[dlp + alp:dlp + alp + gl, 2 * rw:].set(g2)
    w_lora = w_lora.astype(BF16)
    mu_l = W["mu_shift"]
    mu = jnp.concatenate([mu_l[:, :o1], _pad_cols(mu_l[:, o1:o1 + dl], dlp), _pad_cols(mu_l[:, o1 + dl:o1 + dl + al], alp),
                          _pad_cols(mu_l[:, o1 + dl + al:], glp)], axis=1)
    r_k = W["rwkv_r_k"].reshape(1, rw)

    tile = _pick(n, (256, 128, 64))
    tile_s = _pick(n, (128, 64))
    ct_seq = _pick(math.gcd(rwp, lw_), (256, 128))
    assert o_rw % ct_seq == 0 and o_gate % lw_ == 0
    ct_h = _pick(rw, (512, 256, 128))
    gi = lax.broadcasted_iota(jnp.int32, (ct_h, ct_h), 0) // HEAD
    gj = lax.broadcasted_iota(jnp.int32, (ct_h, ct_h), 1) // HEAD
    gsum = ((gi == gj).astype(F32), (ct_h, ct_h), lambda j: (0, 0))
    full = lambda a: (a, a.shape, lambda j: (0,) * a.ndim)
    rowp = lambda a, ct: (a,) + _row(ct)

    u1, = _stage_fwd(_f_rmsnorm, "norm_mix_fwd", n, d, tile, d, [(x2, 0)], [full(W["norm_mix_g"])], [], [BF16])
    p = _mm(u1, w_in, name="mm_in")
    ps = _lerp_fwd(p, o_rw, mu, bl, seq, rwp, ct_seq)
    f_lora = functools.partial(_f_lora_act, widths=(dlp, alp))
    lact, = _stage_fwd(f_lora, "lora_act_fwd", n, lorap, tile, lorap, [(ps, o_lora)], [], [], [BF16])
    wag = _mm(lact, w_lora, name="mm_lora")
    pre_par = [rowp(W["rwkv_w0"], ct_h), rowp(W["rwkv_a0"], ct_h), rowp(W["rwkv_k_k"], ct_h), rowp(W["rwkv_k_a"], ct_h)]
    pre_acts = [(ps, rw), (wag, 0), (wag, rw)]
    lw, k2, na, bb = _stage_fwd(_f_rwkv_pre, "rwkv_pre_fwd", n, rw, tile_s, ct_h, pre_acts, pre_par, [gsum], [F32] * 4)
    ysc, st = _rwkv_scan_fwd(None, lw, k2, None, na, bb, ps, bl, seq, rw)
    post_par = [rowp(W["rwkv_ln_g"], ct_h), rowp(W["rwkv_ln_b"], ct_h), rowp(r_k, ct_h)]
    post_acts = [(ysc, 0), (ps, 0), (k2, 0), (ps, 2 * rw), (wag, 2 * rw)]
    ya, = _stage_fwd(_f_rwkv_post, "rwkv_post_fwd", n, rw, tile_s, ct_h, post_acts, post_par, [gsum], [BF16])

    xc = _conv_fwd(p, o_xb, conv_w, W["conv_b"], bl, seq, lw_, ct_seq)
    f_gates = functools.partial(_f_lru_gates, seq=seq)
    blk3 = lambda a: (a[0], (1, LANE, LANE), lambda j: (j, 0, 0))
    gate_par = [blk3(W["lru_wr"]), rowp(W["lru_br"], LANE), blk3(W["lru_wi"]), rowp(W["lru_bi"], LANE), rowp(W["lru_lambda"], LANE)]
    a_l, bx = _stage_fwd(f_gates, "lru_gates_fwd", n, lw_, tile, LANE, [(xc, 0)], gate_par, [], [F32, F32])
    ct_l = _pick(lw_, (256, 128))
    h_l = _lru_scan_fwd(a_l, bx, bl, seq, lw_, ct_l)
    lpost_par = [full(W["lru_norm_g"])]
    yb, = _stage_fwd(_f_lru_post, "lru_post_fwd", n, lw_, tile_s, lw_, [(h_l, 0), (p, o_gate)], lpost_par, [], [BF16])

    ycat = jnp.concatenate([ya, yb], axis=1)
    h1 = _mm(ycat, w_out, name="mm_out", add=x2)
    u2, = _stage_fwd(_f_rmsnorm, "norm_ffn_fwd", n, d, tile, d, [(h1, 0)], [full(W["norm_ffn_g"])], [], [BF16])
    gu = _mm(u2, w_gu, name="mm_gu")
    ct_f = _pick(dffp, (1024, 512, 256, 128))
    act, = _stage_fwd(_f_swiglu, "swiglu_fwd", n, dffp, tile, ct_f, [(gu, 0), (gu, dffp)], [], [], [BF16])
    h2 = _mm(act, w_down, name="mm_down", add=h1)

    dh2, dg_final, lsum = _loss_head(h2, W["norm_final_g"].reshape(1, d), tgt2, tile_s)
    loss = lax.psum(lsum[0, 0], ("x", "y", "c"))
    queue = _GradQueue()

    def cmm(capacity_us, *args, **kw):
        carry = queue.take(capacity_us)
        if carry is None:
            return _mm(*args, **kw)
        res, dsts = _mm(*args, carry=carry, **kw)
        queue.done(carry, dsts)
        return res

    dh2b = dh2.astype(BF16)
    dact = _mm(dh2b, w_down, name="mm_dact", tb=True, out_dtype=BF16)
    dw_down = _mm(act, dh2b, name="mm_dw_down", ta=True, out_dtype=BF16)
    queue.push("ffn_w_down", dw_down[:dff].reshape(N_DEV, nsh_ff, d), 2, 1000)
    (dgate, dup), _ = _stage_bwd(_f_swiglu, "swiglu_bwd", n, dffp, tile, ct_f, [(gu, 0), (gu, dffp)], [], [], [(dact, 0)], [BF16, BF16])
    dgu = jnp.concatenate([dgate, dup], axis=1)
    du2 = cmm(930, dgu, w_gu, name="mm_du2", tb=True)
    dw_gu = _mm(u2, dgu, name="mm_dw_gu", ta=True, out_dtype=BF16)
    queue.push("ffn_w_gate", _shards_from_cols(dw_gu[:, :dff]), 4, 1000)
    queue.push("ffn_w_up", _shards_from_cols(dw_gu[:, dffp:dffp + dff]), 4, 1000)
    (dh1,), (dg_ffn,) = _stage_bwd(_f_rmsnorm, "norm_ffn_bwd", n, d, tile_s, d, [(h1, 0)], [full(W["norm_ffn_g"])], [], [(du2, 0)], [F32],
                                   extra_add=(dh2, 0))
    dh1b = dh1.astype(BF16)
    dycat = cmm(160, dh1b, w_out, name="mm_dycat", tb=True)
    dw_out = cmm(200, ycat, dh1b, name="mm_dw_out", ta=True, out_dtype=BF16)
    queue.push("w_out", dw_out.reshape(N_DEV, -1, d), 1, 370)

    (dysc, dr_p, dk2_p, dv_p, dg_g), (dln_g, dln_b, dr_k) = _stage_bwd(
        _f_rwkv_post, "rwkv_post_bwd", n, rw, tile_s, ct_h, post_acts, post_par, [gsum], [(dycat, 0)], [F32] * 5)
    carry = queue.take(1600)
    if carry is None:
        dr_s, dlw, dk2_s, dv_s, dna, dbb = _rwkv_scan_bwd(lw, k2, na, bb, ps, st, dysc, bl, seq, rw)
    else:
        (dr_s, dlw, dk2_s, dv_s, dna, dbb), dsts = _rwkv_scan_bwd(lw, k2, na, bb, ps, st, dysc, bl, seq, rw, carry=carry)
        queue.done(carry, dsts)
    dk2 = dk2_p + dk2_s
    (dk, dwlin, dalin), (dw0, da0, dk_k, dk_a) = _stage_bwd(
        _f_rwkv_pre, "rwkv_pre_bwd", n, rw, tile_s, ct_h, pre_acts, pre_par, [gsum], [(dlw, 0), (dk2, 0), (dna, 0), (dbb, 0)], [F32] * 3)
    dwag = jnp.concatenate([dwlin, dalin, dg_g], axis=1).astype(BF16)
    dlact = cmm(90, dwag, w_lora, name="mm_dlact", tb=True)
    dw_lora = cmm(60, lact, dwag, name="mm_dw_lora", ta=True)
    (dps_lora,), _ = _stage_bwd(f_lora, "lora_act_bwd", n, lorap, tile, lorap, [(ps, o_lora)], [], [], [(dlact, 0)], [F32])
    dps = jnp.concatenate([dr_p + dr_s, dk, dv_p + dv_s, dps_lora], axis=1)
    dp_rwkv, dmu = _lerp_bwd(p, o_rw, mu, dps, bl, seq, rwp, ct_seq, BF16)

    (dh_l, dgate_l), (dlru_norm_g,) = _stage_bwd(_f_lru_post, "lru_post_bwd", n, lw_, tile_s, lw_, [(h_l, 0), (p, o_gate)], lpost_par, [],
                                                 [(dycat, rw)], [F32, BF16])
    da_l, dbx = _lru_scan_bwd(a_l, h_l, dh_l, bl, seq, lw_, ct_l)
    (dxc,), (dwr, dbr, dwi, dbi, dlam) = _stage_bwd(f_gates, "lru_gates_bwd", n, lw_, tile, LANE, [(xc, 0)], gate_par, [],
                                                    [(da_l, 0), (dbx, 0)], [F32])
    dxb, dconv_w, dconv_b = _conv_bwd(p, o_xb, conv_w, dxc, bl, seq, lw_, ct_seq, BF16)

    dp = jnp.concatenate([dxb, dgate_l, dp_rwkv], axis=1)
    dw_in = cmm(400, u1, dp, name="mm_dw_in", ta=True, out_dtype=BF16)
    ol = o_rw + o_lora
    dw_in_l = jnp.concatenate([dw_in[:, o_rw:ol], dw_in[:, ol:ol + dl], dw_in[:, ol + dlp:ol + dlp + al],
                               dw_in[:, ol + dlp + alp:ol + dlp + alp + gl], dw_in[:, :o_rw]], axis=1)
    queue.push("w_in", _shards_from_cols(dw_in_l), 4, 970)
    du1 = cmm(460, dp, w_in, name="mm_du1", tb=True)
    (grad_x,), (dg_mix,) = _stage_bwd(_f_rmsnorm, "norm_mix_bwd", n, d, tile_s, d, [(x2, 0)], [full(W["norm_mix_g"])], [], [(du1, 0)], [F32],
                                      extra_add=(dh1, 0))
    out = {}
    for k in ["ffn_w_down", "ffn_w_gate", "ffn_w_up", "w_out", "w_in"]:
        carry = queue.take(250) if k != "w_in" else None
        if k == "w_in":
            queue.flush()
        res = _adamw(queue.result(k), W[k][0], M[k][0], V[k][0], "adamw_" + k, carry=carry)
        if carry:
            res, dsts = res
            queue.done(carry, dsts)
        out[k] = [o[None] for o in res]

    dmu_l = jnp.concatenate([dmu[:, :o1], dmu[:, o_lora:o_lora + dl], dmu[:, o_lora + dlp:o_lora + dlp + al],
                             dmu[:, o_lora + dlp + alp:o_lora + dlp + alp + gl]], axis=1)
    small_g = {"norm_mix_g": dg_mix, "mu_shift": dmu_l, "rwkv_w0": dw0, "rwkv_a0": da0, "rwkv_k_k": dk_k, "rwkv_k_a": dk_a,
               "rwkv_r_k": dr_k.reshape(W["rwkv_r_k"].shape), "rwkv_ln_g": dln_g, "rwkv_ln_b": dln_b, "conv_b": dconv_b,
               "lru_wr": dwr[None], "lru_br": dbr, "lru_wi": dwi[None], "lru_bi": dbi, "lru_lambda": dlam, "lru_norm_g": dlru_norm_g,
               "norm_ffn_g": dg_ffn, "norm_final_g": dg_final.reshape(W["norm_final_g"].shape)}
    sh_full = [dw_lora[:dl, :rw], dw_lora[dlp:dlp + al, rw:2 * rw], dw_lora[dlp + alp:dlp + alp + gl, 2 * rw:], dconv_w]
    names = _SMALL + _SMALL_SHARDED
    shapes = [W[k].shape for k in _SMALL] + [a.shape for a in sh_full]
    gpack = _pack([small_g[k] for k in _SMALL] + sh_full)
    gall, = _all_gather([gpack], "gather_small_grads")
    x_i, y_i, c_i = _coords()
    me = 4 * x_i + 2 * y_i + c_i
    g_rep = dict(zip(names, _unpack(_sum_parts(gall, "sum_small_grads"), shapes)))
    wp = _pack([W[k] for k in _SMALL] + [W[k][0] for k in _SMALL_SHARDED])
    mp = _pack([M[k] for k in _SMALL] + [M[k][0] for k in _SMALL_SHARDED])
    vp = _pack([V[k] for k in _SMALL] + [V[k][0] for k in _SMALL_SHARDED])
    g_mine = [g_rep[k] for k in _SMALL]
    for k in _SMALL_SHARDED:
        nsh = W[k].shape[2]
        g_mine.append(lax.dynamic_slice_in_dim(g_rep[k], me * nsh, nsh, axis=1))
    res = _adamw(_pack(g_mine)[None], wp, mp, vp, "adamw_small")
    shapes2 = [W[k].shape for k in _SMALL] + [W[k].shape for k in _SMALL_SHARDED]
    res = [dict(zip(names, _unpack(r, shapes2))) for r in res]
    for k in names:
        out[k] = [r[k] for r in res]
    return loss, grad_x.reshape(x.shape), out


def kernel(x, norm_mix_g, w_in, mu_shift, rwkv_w0, rwkv_w2, rwkv_a0, rwkv_a2, rwkv_g2, rwkv_k_k, rwkv_k_a, rwkv_r_k, rwkv_ln_g, rwkv_ln_b, conv_w, conv_b, lru_wr, lru_br, lru_wi, lru_bi, lru_lambda, lru_norm_g, w_out, norm_ffn_g, ffn_w_gate, ffn_w_up, ffn_w_down, norm_final_g, loss_target, m_norm_mix_g, m_w_in, m_mu_shift, m_rwkv_w0, m_rwkv_w2, m_rwkv_a0, m_rwkv_a2, m_rwkv_g2, m_rwkv_k_k, m_rwkv_k_a, m_rwkv_r_k, m_rwkv_ln_g, m_rwkv_ln_b, m_conv_w, m_conv_b, m_lru_wr, m_lru_br, m_lru_wi, m_lru_bi, m_lru_lambda, m_lru_norm_g, m_w_out, m_norm_ffn_g, m_ffn_w_gate, m_ffn_w_up, m_ffn_w_down, m_norm_final_g, v_norm_mix_g, v_w_in, v_mu_shift, v_rwkv_w0, v_rwkv_w2, v_rwkv_a0, v_rwkv_a2, v_rwkv_g2, v_rwkv_k_k, v_rwkv_k_a, v_rwkv_r_k, v_rwkv_ln_g, v_rwkv_ln_b, v_conv_w, v_conv_b, v_lru_wr, v_lru_br, v_lru_wi, v_lru_bi, v_lru_lambda, v_lru_norm_g, v_w_out, v_norm_ffn_g, v_ffn_w_gate, v_ffn_w_up, v_ffn_w_down, v_norm_final_g):
    a = locals()
    W = {k: a[k] for k in _WEIGHTS}
    M = {k: a["m_" + k] for k in _WEIGHTS}
    V = {k: a["v_" + k] for k in _WEIGHTS}
    loss, grad_x, out = _step(W, M, V, x, loss_target)
    res = [loss, grad_x]
    for i in range(4):
        res += [out[k][i].reshape(W[k].shape) for k in _WEIGHTS]
    return tuple(res)
```

```python
import functools
import math

import jax
import jax.numpy as jnp
from jax import lax
from jax.experimental import pallas as pl
from jax.experimental.pallas import tpu as pltpu

F32 = jnp.float32
BF16 = jnp.bfloat16
HI = lax.Precision.HIGHEST
MESH = pl.DeviceIdType.MESH

N_DEV = 8
LANE = 128
HEAD = 64
MM_MAX_TK = 4096
SCAN_CHUNK = 64
SCAN_GROUP = 4
SCAN_PAIRS = 2
VMEM_LIMIT = 56 * 1024 * 1024

NORM_EPS = 1e-6
GN_EPS = 64e-5
LRU_C = 8.0
ADAM_LR, ADAM_B1, ADAM_B2, ADAM_EPS, ADAM_WD, ADAM_STEP = 0.001, 0.9, 0.999, 1e-08, 0.01, 10


def _pick(n, cands):
    for c in cands:
        if n % c == 0:
            return c
    return n


def _rup(n, m):
    return (n + m - 1) // m * m


def _cparams(dims):
    return pltpu.CompilerParams(dimension_semantics=dims, vmem_limit_bytes=VMEM_LIMIT)


def _sigmoid(x):
    return 1.0 / (1.0 + jnp.exp(-x))


def _softplus(z):
    return jnp.maximum(z, 0.0) + jnp.log(1.0 + jnp.exp(-jnp.abs(z)))


def _neg_expm1(x):
    series = -(x * (1.0 + 0.5 * x * (1.0 + (x / 3.0) * (1.0 + 0.25 * x))))
    return jnp.where(jnp.abs(x) < 0.03, series, 1.0 - jnp.exp(x))


def _gelu(x):
    return 0.5 * x * (1.0 + jnp.tanh(0.7978845608028654 * (x + 0.044715 * (x * x * x))))


def _dot(a, b, dims, precision=None):
    return lax.dot_general(a, b, (dims, ((), ())), precision=precision, preferred_element_type=F32)


def _nn(a, b, precision=None):
    return _dot(a, b, ((1,), (0,)), precision)


def _nt(a, b, precision=None):
    return _dot(a, b, ((1,), (1,)), precision)


def _tn(a, b, precision=None):
    return _dot(a, b, ((0,), (0,)), precision)


def _coords():
    return lax.axis_index("x"), lax.axis_index("y"), lax.axis_index("c")


class _Carry:
    def __init__(self, tensors, items):
        self.tensors, self.items = tensors, items
        nt, ni = len(tensors), len(items)
        any_spec = pl.BlockSpec(memory_space=pl.ANY)
        self.args = [s for s, _ in tensors] + [d for _, d in tensors]
        self.in_specs = [any_spec] * (2 * nt)
        self.out_specs = [any_spec] * nt
        self.out_shape = [jax.ShapeDtypeStruct(d.shape, d.dtype) for _, d in tensors]
        self.scratch = [pltpu.SemaphoreType.DMA((ni, N_DEV - 1)), pltpu.SemaphoreType.DMA((ni, N_DEV - 1)), pltpu.SemaphoreType.DMA((ni,))]

    def aliases(self, first_in, first_out):
        nt = len(self.tensors)
        return {first_in + nt + t: first_out + t for t in range(nt)}

    def _copies(self, src_refs, dst_refs, sems):
        send_sems, recv_sems, local_sems = sems
        x, y, c = _coords()
        my = 4 * x + 2 * y + c
        out = []
        for n, (t, r0, rows) in enumerate(self.items):
            win = pl.ds(r0, rows)
            out.append(pltpu.make_async_copy(src_refs[t].at[my, win], dst_refs[t].at[my, win], local_sems.at[n]))
            for k in range(1, N_DEV):
                px, py, pc = x ^ ((k >> 2) & 1), y ^ ((k >> 1) & 1), c ^ (k & 1)
                out.append(pltpu.make_async_remote_copy(
                    src_ref=src_refs[t].at[4 * px + 2 * py + pc, win], dst_ref=dst_refs[t].at[my, win],
                    send_sem=send_sems.at[n, k - 1], recv_sem=recv_sems.at[n, k - 1],
                    device_id=(px, py, pc), device_id_type=MESH))
        return out

    def hook(self, step, last, src_refs, dst_refs, sems):
        if last == 0:
            for cp in self._copies(src_refs, dst_refs, sems):
                cp.start()
            for cp in self._copies(src_refs, dst_refs, sems):
                cp.wait()
            return

        @pl.when(step == 0)
        def _():
            for cp in self._copies(src_refs, dst_refs, sems):
                cp.start()

        @pl.when(step == last)
        def _():
            for cp in self._copies(src_refs, dst_refs, sems):
                cp.wait()


class _GatherCarry(_Carry):
    def hook(self, step, last, src_refs, dst_refs, sems):
        send_sems, recv_sems, local_sems = sems
        x, y, c = _coords()
        me, sibling = (x, y, c), (x, y, 1 - c)
        chips = [(1 - x, y), (x, 1 - y), (1 - x, 1 - y)]

        def per_item(fn):
            for n, (t, r0, rows) in enumerate(self.items):
                win = pl.ds(r0, rows)

                def copy(k, block, to, own=False, n=n, t=t, win=win):
                    slot = dst_refs[t].at[4 * block[0] + 2 * block[1] + block[2], win]
                    return pltpu.make_async_remote_copy(
                        src_ref=src_refs[t].at[win] if own else slot, dst_ref=slot,
                        send_sem=send_sems.at[n, k], recv_sem=recv_sems.at[n, k], device_id=to, device_id_type=MESH)

                mine = pltpu.make_async_copy(src_refs[t].at[win], dst_refs[t].at[4 * x + 2 * y + c, win], local_sems.at[n])
                first = [copy(0, me, sibling, own=True)] + [copy(1 + j, me, (*chip, c), own=True) for j, chip in enumerate(chips)]
                fn(copy, mine, first)

        def begin(copy, mine, first):
            mine.start()
            for cp in first:
                cp.start()

        def pass_on(copy, mine, first):
            for j, chip in enumerate(chips):
                copy(1 + j, (*chip, c), me).wait_recv()
                copy(4 + j, (*chip, c), sibling).start()

        def finish(copy, mine, first):
            copy(0, sibling, me).wait_recv()
            for j, chip in enumerate(chips):
                copy(4 + j, (*chip, 1 - c), me).wait_recv()
            for cp in first + [copy(4 + j, (*chip, c), sibling) for j, chip in enumerate(chips)]:
                cp.wait_send()
            mine.wait()

        if last == 0:
            for fn in (begin, pass_on, finish):
                per_item(fn)
            return
        late = max(1, (7 * last) // 8)
        for at, fn in ((0, begin), (late, pass_on), (last, finish)):
            pl.when(step == at)(functools.partial(per_item, fn))


def _mm(a, b, *, name, ta=False, tb=False, out_dtype=F32, add=None, tiles=None, carry=None):
    M, K = (a.shape[1], a.shape[0]) if ta else a.shape
    N = b.shape[0] if tb else b.shape[1]
    assert (b.shape[1] if tb else b.shape[0]) == K, (a.shape, b.shape, ta, tb)
    tk = max(t for t in range(LANE, min(K, MM_MAX_TK) + 1, LANE) if K % t == 0)
    tm, tn, tk = tiles or (_pick(M, (1024, 512, 256, 128)), _pick(N, (512, 256, 128)), tk)
    nk = K // tk
    dims = ((0 if ta else 1,), (1 if tb else 0,))

    n_in = 2 + (add is not None)
    nt = len(carry.tensors) if carry else 0
    gi, gj = M // tm, N // tn

    def kern(*refs):
        a_ref, b_ref = refs[:2]
        add_ref = refs[2] if add is not None else None
        o_ref = refs[n_in + 2 * nt]
        scr = refs[n_in + 3 * nt + 1:]
        if carry:
            step = (pl.program_id(0) * gj + pl.program_id(1)) * nk + pl.program_id(2)
            carry.hook(step, gi * gj * nk - 1, refs[n_in:n_in + nt], refs[n_in + 2 * nt + 1:n_in + 3 * nt + 1], scr[-3:])

        def finish(r):
            if add is not None:
                r = r + add_ref[...].astype(F32)
            o_ref[...] = r.astype(o_ref.dtype)

        if nk == 1:
            finish(_dot(a_ref[...], b_ref[...], dims))
            return
        acc = scr[0]
        k = pl.program_id(2)

        @pl.when(k == 0)
        def _():
            acc[...] = jnp.zeros_like(acc)

        acc[...] += _dot(a_ref[...], b_ref[...], dims)

        @pl.when(k == nk - 1)
        def _():
            finish(acc[...])

    a_spec = pl.BlockSpec((tk, tm), lambda i, j, k: (k, i)) if ta else pl.BlockSpec((tm, tk), lambda i, j, k: (i, k))
    b_spec = pl.BlockSpec((tn, tk), lambda i, j, k: (j, k)) if tb else pl.BlockSpec((tk, tn), lambda i, j, k: (k, j))
    o_spec = pl.BlockSpec((tm, tn), lambda i, j, k: (i, j))
    in_specs = [a_spec, b_spec] + ([o_spec] if add is not None else [])
    args = (a, b) + ((add,) if add is not None else ())
    scratch = [pltpu.VMEM((tm, tn), F32)] if nk > 1 else []
    o_shape = jax.ShapeDtypeStruct((M, N), out_dtype)
    if not carry:
        return pl.pallas_call(
            kern, name=name, grid=(gi, gj, nk), in_specs=in_specs, out_specs=o_spec, out_shape=o_shape, scratch_shapes=scratch,
            compiler_params=_cparams(("parallel", "parallel", "arbitrary")),
        )(*args)
    res = pl.pallas_call(
        kern, name=name, grid=(gi, gj, nk), in_specs=in_specs + carry.in_specs, out_specs=[o_spec] + carry.out_specs,
        out_shape=[o_shape] + carry.out_shape, scratch_shapes=scratch + carry.scratch,
        input_output_aliases=carry.aliases(n_in, 1), compiler_params=_cparams(("arbitrary", "arbitrary", "arbitrary")),
    )(*args, *carry.args)
    return res[0], list(res[1:])


def _stage_specs(acts, params, consts, tile, ct):
    act_specs = [pl.BlockSpec((tile, ct), functools.partial(lambda j, i, o: (i, o + j), o=off // ct)) for _, off in acts]
    par_specs = [pl.BlockSpec(bs, functools.partial(lambda j, i, im: im(j), im=im)) for _, bs, im in params]
    con_specs = [pl.BlockSpec(bs, functools.partial(lambda j, i, im: im(j), im=im)) for _, bs, im in consts]
    return act_specs, par_specs, con_specs


def _stage_fwd(f, name, n_rows, width, tile, ct, acts, params, consts, out_dtypes):
    for _, off in acts:
        assert off % ct == 0
    na, npar, nc = len(acts), len(params), len(consts)

    def kern(*refs):
        a = [r[...].astype(F32) for r in refs[:na]]
        p = [r[...] for r in refs[na:na + npar]]
        c = [r[...] for r in refs[na + npar:na + npar + nc]]
        outs = f(a, p, c, pl.program_id(1) * tile)
        for r, o in zip(refs[na + npar + nc:], outs):
            r[...] = o.astype(r.dtype)

    act_specs, par_specs, con_specs = _stage_specs(acts, params, consts, tile, ct)
    o_spec = pl.BlockSpec((tile, ct), lambda j, i: (i, j))
    outs = pl.pallas_call(
        kern, name=name, grid=(width // ct, n_rows // tile),
        in_specs=act_specs + par_specs + con_specs, out_specs=[o_spec] * len(out_dtypes),
        out_shape=[jax.ShapeDtypeStruct((n_rows, width), d) for d in out_dtypes],
        compiler_params=_cparams(("parallel", "parallel")),
    )(*[a for a, _ in acts], *[p for p, _, _ in params], *[c for c, _, _ in consts])
    return tuple(outs)


def _stage_bwd(f, name, n_rows, width, tile, ct, acts, params, consts, couts, dact_dtypes, extra_add=None):
    na, npar, nc, no = len(acts), len(params), len(consts), len(couts)
    nx = 0 if extra_add is None else 1

    def kern(*refs):
        a = [r[...].astype(F32) for r in refs[:na]]
        p = [r[...] for r in refs[na:na + npar]]
        c = [r[...] for r in refs[na + npar:na + npar + nc]]
        base = na + npar + nc
        co = [r[...].astype(F32) for r in refs[base:base + no]]
        base += no
        x_refs = refs[base:base + nx]
        base += nx
        da_refs = refs[base:base + na]
        dp_refs = refs[base + na:]
        row0 = pl.program_id(1) * tile
        _, vjp = jax.vjp(lambda aa, pp: tuple(f(aa, pp, c, row0)), a, p)
        da, dp = vjp(tuple(co))
        for k, (r, d) in enumerate(zip(da_refs, da)):
            if k == 0 and nx:
                d = d + x_refs[0][...].astype(F32)
            r[...] = d.astype(r.dtype)
        first = pl.program_id(1) == 0
        for r, d in zip(dp_refs, dp):
            @pl.when(first)
            def _(r=r, d=d):
                r[...] = d

            @pl.when(jnp.logical_not(first))
            def _(r=r, d=d):
                r[...] += d

    act_specs, par_specs, con_specs = _stage_specs(acts, params, consts, tile, ct)
    t_spec = pl.BlockSpec((tile, ct), lambda j, i: (i, j))
    co_specs = [pl.BlockSpec((tile, ct), functools.partial(lambda j, i, o: (i, o + j), o=off // ct)) for _, off in couts]
    x_specs = [] if extra_add is None else [pl.BlockSpec((tile, ct), functools.partial(lambda j, i, o: (i, o + j), o=extra_add[1] // ct))]
    x_args = [] if extra_add is None else [extra_add[0]]
    outs = pl.pallas_call(
        kern, name=name, grid=(width // ct, n_rows // tile),
        in_specs=act_specs + par_specs + con_specs + co_specs + x_specs,
        out_specs=[t_spec] * na + par_specs,
        out_shape=[jax.ShapeDtypeStruct((n_rows, width), d) for d in dact_dtypes]
        + [jax.ShapeDtypeStruct(p.shape, F32) for p, _, _ in params],
        compiler_params=_cparams(("parallel", "arbitrary")),
    )(*[a for a, _ in acts], *[p for p, _, _ in params], *[c for c, _, _ in consts], *[c for c, _ in couts], *x_args)
    return tuple(outs[:na]), tuple(outs[na:])


def _row(ct):
    return (1, ct), (lambda j: (0, j))


def _f_rmsnorm(a, p, c, row0):
    x, = a
    g, = p
    return (x * lax.rsqrt(jnp.mean(x * x, axis=-1, keepdims=True) + NORM_EPS) * g,)


def _f_lora_act(a, p, c, row0, widths):
    x, = a
    dl, al = widths
    col = lax.broadcasted_iota(jnp.int32, x.shape, 1)
    return (jnp.where(col < dl, jnp.tanh(x), jnp.where(col < dl + al, x, _sigmoid(x))),)


def _head_sums_raw(x, ones):
    hi = x.astype(BF16)
    lo = (x - hi.astype(F32)).astype(BF16)
    return _nn(hi, ones) + _nn(lo, ones)


@jax.custom_vjp
def _head_sums(x, ones):
    return _head_sums_raw(x, ones)


_head_sums.defvjp(lambda x, ones: (_head_sums_raw(x, ones), ones),
                  lambda ones, ct: (_head_sums_raw(ct, ones), jnp.zeros_like(ones)))


def _f_rwkv_pre(a, p, c, row0):
    k, wlin, alin = a
    w0, a0, k_k, k_a = p
    gsum, = c
    w = -_softplus(-(w0 + wlin)) - 0.5
    lw = -jnp.exp(w)
    alpha = _sigmoid(a0 + alin)
    kk = k * k_k
    ss = _head_sums(kk * kk, gsum)
    kk = kk * lax.rsqrt(jnp.maximum(ss, 1e-24))
    k2 = k * (1.0 + (alpha - 1.0) * k_a)
    return lw, k2, -kk, kk * alpha


def _f_rwkv_post(a, p, c, row0):
    y, r, k2, v, g = a
    ln_g, ln_b, r_k = p
    gsum, = c
    inv = 1.0 / HEAD
    mean = _head_sums(y, gsum) * inv
    yc = y - mean
    var = _head_sums(yc * yc, gsum) * inv
    yn = yc * lax.rsqrt(var + GN_EPS) * ln_g + ln_b
    bonus = _head_sums(r * k2 * r_k, gsum)
    return ((yn + bonus * v) * g,)


def _f_lru_gates(a, p, c, row0, seq):
    xc, = a
    wr, br, wi, bi, lam = p
    xb = xc.astype(BF16)
    rg = _sigmoid(_nn(xb, wr[0].astype(BF16)) + br)
    ig = _sigmoid(_nn(xb, wi[0].astype(BF16)) + bi)
    log_a = -LRU_C * rg * _softplus(-lam)
    a_t = jnp.exp(log_a)
    mult = jnp.sqrt(_neg_expm1(2.0 * log_a))
    row = row0 + lax.broadcasted_iota(jnp.int32, xc.shape, 0)
    mult = jnp.where(row % seq == 0, 1.0, mult)
    return a_t, mult * ig * xc


def _f_lru_post(a, p, c, row0):
    h, gate = a
    g, = p
    y = h * _gelu(gate)
    return (y * lax.rsqrt(jnp.mean(y * y, axis=-1, keepdims=True) + NORM_EPS) * g,)


def _f_swiglu(a, p, c, row0):
    gate, up = a
    return (gate * _sigmoid(gate) * up,)


def _shift_down(x, s, row):
    return jnp.where(row >= s, pltpu.roll(x, s, 0), 0.0)


def _shift_up(x, s, row):
    n = x.shape[0]
    return jnp.where(row < n - s, pltpu.roll(x, n - s, 0), 0.0)


def _seq_call(kern, name, bl, seq, width, ct, ins, outs, acc_outs=()):
    def spec(off, rows):
        if rows is None:
            return pl.BlockSpec((seq, ct), functools.partial(lambda j, b, o: (b, o + j), o=off // ct))
        return pl.BlockSpec((rows, ct), lambda j, b: (0, j))

    in_specs = [spec(off, rows) for _, off, rows in ins]
    out_specs = [spec(0, None) for _ in outs] + [spec(0, rows) for _, rows in acc_outs]
    out_shape = [jax.ShapeDtypeStruct((bl * seq, width), d) for d in outs] + [jax.ShapeDtypeStruct((rows, width), F32) for _, rows in acc_outs]
    res = pl.pallas_call(
        kern, name=name, grid=(width // ct, bl), in_specs=in_specs, out_specs=out_specs, out_shape=out_shape,
        compiler_params=_cparams(("parallel", "arbitrary")),
    )(*[a for a, _, _ in ins])
    return tuple(res)


def _acc(ref, val):
    first = pl.program_id(1) == 0

    @pl.when(first)
    def _():
        ref[...] = val

    @pl.when(jnp.logical_not(first))
    def _():
        ref[...] += val


def _lerp_fwd(p, off, mu, bl, seq, width, ct):
    def kern(p_ref, mu_ref, o_ref):
        x = p_ref[...]
        row = lax.broadcasted_iota(jnp.int32, x.shape, 0)
        o_ref[...] = x + (_shift_down(x, 1, row) - x) * mu_ref[...]

    return _seq_call(kern, "lerp_fwd", bl, seq, width, ct, [(p, off, None), (mu, 0, 1)], [F32])[0]


def _lerp_bwd(p, off, mu, dps, bl, seq, width, ct, out_dtype):
    def kern(p_ref, mu_ref, d_ref, dp_ref, dmu_ref):
        x = p_ref[...]
        d = d_ref[...].astype(F32)
        m = mu_ref[...]
        row = lax.broadcasted_iota(jnp.int32, x.shape, 0)
        dp_ref[...] = (d * (1.0 - m) + _shift_up(d * m, 1, row)).astype(dp_ref.dtype)
        _acc(dmu_ref, jnp.sum(d * (_shift_down(x, 1, row) - x), axis=0, keepdims=True))

    return _seq_call(kern, "lerp_bwd", bl, seq, width, ct, [(p, off, None), (mu, 0, 1), (dps, 0, None)], [out_dtype], [(None, 1)])


def _conv_fwd(p, off, cw, cb, bl, seq, width, ct):
    nw = cw.shape[0]

    def kern(x_ref, w_ref, b_ref, o_ref):
        x = x_ref[...]
        row = lax.broadcasted_iota(jnp.int32, x.shape, 0)
        acc = b_ref[...] + x * w_ref[pl.ds(nw - 1, 1), :]
        for s in range(1, nw):
            acc = acc + _shift_down(x, s, row) * w_ref[pl.ds(nw - 1 - s, 1), :]
        o_ref[...] = acc

    return _seq_call(kern, "conv_fwd", bl, seq, width, ct, [(p, off, None), (cw, 0, nw), (cb, 0, 1)], [F32])[0]


def _conv_bwd(p, off, cw, dxc, bl, seq, width, ct, out_dtype):
    nw = cw.shape[0]

    def kern(x_ref, w_ref, d_ref, dx_ref, dw_ref, db_ref):
        x = x_ref[...]
        d = d_ref[...]
        row = lax.broadcasted_iota(jnp.int32, x.shape, 0)
        wrow = lax.broadcasted_iota(jnp.int32, dw_ref.shape, 0)
        dx = d * w_ref[pl.ds(nw - 1, 1), :]
        dw = jnp.where(wrow == nw - 1, jnp.sum(d * x, axis=0, keepdims=True), 0.0)
        for s in range(1, nw):
            dx = dx + _shift_up(d, s, row) * w_ref[pl.ds(nw - 1 - s, 1), :]
            dw = jnp.where(wrow == nw - 1 - s, jnp.sum(d * _shift_down(x, s, row), axis=0, keepdims=True), dw)
        dx_ref[...] = dx.astype(dx_ref.dtype)
        _acc(dw_ref, dw)
        _acc(db_ref, jnp.sum(d, axis=0, keepdims=True))

    return _seq_call(kern, "conv_bwd", bl, seq, width, ct, [(p, off, None), (cw, 0, nw), (dxc, 0, None)], [out_dtype], [(None, nw), (None, 1)])


def _lru_scan_fwd(a, bx, bl, seq, width, ct):
    def kern(a_ref, b_ref, h_ref):
        av = a_ref[...]
        bv = b_ref[...]
        row = lax.broadcasted_iota(jnp.int32, av.shape, 0)
        d = 1
        while d < seq:
            a_sh = jnp.where(row >= d, pltpu.roll(av, d, 0), 1.0)
            b_sh = jnp.where(row >= d, pltpu.roll(bv, d, 0), 0.0)
            bv = av * b_sh + bv
            av = av * a_sh
            d *= 2
        h_ref[...] = bv

    return _seq_call(kern, "lru_scan_fwd", bl, seq, width, ct, [(a, 0, None), (bx, 0, None)], [F32])[0]


def _lru_scan_bwd(a, h, dh, bl, seq, width, ct):
    def kern(a_ref, h_ref, d_ref, da_ref, db_ref):
        row = lax.broadcasted_iota(jnp.int32, a_ref.shape, 0)
        al = _shift_up(a_ref[...], 1, row)
        g = d_ref[...]
        d = 1
        while d < seq:
            keep = row < seq - d
            al_sh = jnp.where(keep, pltpu.roll(al, seq - d, 0), 1.0)
            g_sh = jnp.where(keep, pltpu.roll(g, seq - d, 0), 0.0)
            g = al * g_sh + g
            al = al * al_sh
            d *= 2
        db_ref[...] = g
        da_ref[...] = g * _shift_down(h_ref[...], 1, row)

    return _seq_call(kern, "lru_scan_bwd", bl, seq, width, ct, [(a, 0, None), (h, 0, None), (dh, 0, None)], [F32, F32])


_FORMS = {"nn": ((1,), (0,)), "nt": ((1,), (1,)), "tn": ((0,), (0,))}
_FORM_GRADS = {"nn": (("nt", "g", "b"), ("tn", "a", "g")),
               "nt": (("nn", "g", "b"), ("tn", "g", "a")),
               "tn": (("nt", "b", "g"), ("nn", "a", "g"))}


def _split_bf16(x):
    hi = x.astype(BF16)
    return hi, (x - hi.astype(F32)).astype(BF16)


def _pdot_raw(a, b, form, passes):
    dims = _FORMS[form]
    if passes == 1:
        return _dot(a.astype(BF16), b.astype(BF16), dims)
    ah, al = _split_bf16(a)
    bh, bl = _split_bf16(b)
    return _dot(ah, bh, dims) + (_dot(ah, bl, dims) + _dot(al, bh, dims))


@functools.partial(jax.custom_vjp, nondiff_argnums=(2, 3))
def _pdot(a, b, form, passes):
    return _pdot_raw(a, b, form, passes)


def _pdot_fwd(a, b, form, passes):
    return _pdot_raw(a, b, form, passes), (a, b)


def _pdot_bwd(form, passes, res, g):
    vals = {"a": res[0], "b": res[1], "g": g}
    (fa, xa, ya), (fb, xb, yb) = _FORM_GRADS[form]
    return _pdot_raw(vals[xa], vals[ya], fa, passes), _pdot_raw(vals[xb], vals[yb], fb, passes)


_pdot.defvjp(_pdot_fwd, _pdot_bwd)


def _scan_chunk2(S0, r, lw, k, v, a, b, p_main=1, p_inv=3):
    y, s = _scan_block([S0], [[(r, lw, k, v, a, b)]], p_main, p_inv)
    return y[0][0], s[0]


def _scan_block(states, units, p_main=1, p_inv=1):
    C = units[0][0][0].shape[0]
    C2 = 2 * C
    ri = lax.broadcasted_iota(jnp.int32, (C, C), 0)
    ci = lax.broadcasted_iota(jnp.int32, (C, C), 1)
    tri = (ri >= ci).astype(F32)
    i2 = lax.broadcasted_iota(jnp.int32, (C2, C2), 0)
    j2 = lax.broadcasted_iota(jnp.int32, (C2, C2), 1)
    same = (i2 // C) == (j2 // C)
    strict = jnp.logical_and(same, (i2 % C) > (j2 % C))
    incl = jnp.logical_and(same, (i2 % C) >= (j2 % C))
    eye = (i2 == j2).astype(F32)
    lane = lax.broadcasted_iota(jnp.int32, (1, LANE), 1)
    m0, m1 = (lane < HEAD).astype(F32), (lane >= HEAD).astype(F32)
    stack = lambda z: jnp.concatenate([z * m0, z * m1], axis=0)
    ids = [(i, g) for g in range(len(units[0])) for i in range(len(units))]

    pre = {}
    for i, g in ids:
        r, lw, k, v, a, b = units[i][g]
        cs = _nn(tri, lw, HI)
        p_incl = jnp.exp(cs)
        p_rec = jnp.exp(-cs)
        xr = jnp.concatenate([stack(a * jnp.exp(cs - lw)), stack(r * p_incl)], axis=0)
        bk = jnp.concatenate([stack(b * p_rec), stack(k * p_rec)], axis=0)
        pre[i, g] = (xr, bk, stack(v), jnp.exp(jnp.sum(lw, axis=0, keepdims=True)))
    gm = {u: _pdot(pre[u][0], pre[u][1], "nt", p_main) for u in ids}
    a_ak = {u: jnp.where(strict, gm[u][:C2, C2:], 0.0) for u in ids}
    r_bk = {u: jnp.concatenate([jnp.where(incl, gm[u][C2:, :C2], 0.0), jnp.where(incl, gm[u][C2:, C2:], 0.0)], axis=1) for u in ids}
    pw = {u: jnp.where(strict, gm[u][:C2, :C2], 0.0) for u in ids}
    x = {u: eye + pw[u] for u in ids}
    n = 2
    while n < C:
        pw = {u: _pdot(pw[u], pw[u], "nn", p_inv) for u in ids}
        x = {u: x[u] + _pdot(x[u], pw[u], "nn", p_inv) for u in ids}
        n *= 2
    akv = {u: _pdot(a_ak[u], pre[u][2], "nn", p_main) for u in ids}

    states = list(states)
    ys = [[None] * len(units[0]) for _ in units]
    for i, g in ids:
        xr, bk, vs, p_last = pre[i, g]
        xs = _pdot(xr, states[i], "nt", p_main)
        us = _pdot(x[i, g], xs[:C2] + akv[i, g], "nn", p_inv)
        uv = jnp.concatenate([us, vs], axis=0)
        y2 = xs[C2:] + _pdot(r_bk[i, g], uv, "nn", p_main)
        ys[i][g] = y2[:C] + y2[C:]
        states[i] = (states[i] + _pdot(uv, bk, "tn", p_main)) * p_last
    return ys, states


def _scan_dims(seq, rw):
    G = _pick(seq // SCAN_CHUNK, (SCAN_GROUP, 2, 1))
    NP = _pick(rw // LANE, (SCAN_PAIRS, 1))
    C = SCAN_CHUNK * G
    return SCAN_CHUNK, G, NP, C, seq // C, rw // (NP * LANE)


def _rwkv_scan_fwd(r, lw, k2, v, na, bb, p, bl, seq, rw, carry=None):
    cs, G, NP, C, nc, nhg = _scan_dims(seq, rw)
    nt = len(carry.tensors) if carry else 0

    def kern(*refs):
        in_refs = refs[:6]
        y_ref, st_ref = refs[6 + 2 * nt:8 + 2 * nt]
        s_scr = refs[8 + 3 * nt]
        if carry:
            step = (pl.program_id(0) * nhg + pl.program_id(1)) * nc + pl.program_id(2)
            carry.hook(step, bl * nhg * nc - 1, refs[6:6 + nt], refs[8 + 2 * nt:8 + 3 * nt], refs[9 + 3 * nt:])

        @pl.when(pl.program_id(2) == 0)
        def _():
            s_scr[...] = jnp.zeros_like(s_scr)

        st_ref[...] = s_scr[...]
        units = [[tuple(ref[pl.ds(g * cs, cs), pl.ds(i * LANE, LANE)] for ref in in_refs) for g in range(G)] for i in range(NP)]
        ys, s_new = _scan_block([s_scr[i] for i in range(NP)], units)
        for i in range(NP):
            s_scr[i] = s_new[i]
            for g in range(G):
                y_ref[pl.ds(g * cs, cs), pl.ds(i * LANE, LANE)] = ys[i][g]

    def tok(off):
        return pl.BlockSpec((C, NP * LANE), functools.partial(lambda b, h, c, o: (b * nc + c, o + h), o=off // (NP * LANE)))

    in_specs = [tok(0), tok(0), tok(0), tok(2 * rw), tok(0), tok(0)]
    out_specs = [tok(0), pl.BlockSpec((NP, LANE, LANE), lambda b, h, c: ((b * nhg + h) * nc + c, 0, 0))]
    out_shape = [jax.ShapeDtypeStruct((bl * seq, rw), F32), jax.ShapeDtypeStruct((bl * nhg * nc * NP, LANE, LANE), F32)]
    scratch = [pltpu.VMEM((NP, LANE, LANE), F32)]
    if not carry:
        y, st = pl.pallas_call(
            kern, name="rwkv_scan_fwd", grid=(bl, nhg, nc), in_specs=in_specs, out_specs=out_specs, out_shape=out_shape,
            scratch_shapes=scratch, compiler_params=_cparams(("parallel", "parallel", "arbitrary")),
        )(p, lw, k2, p, na, bb)
        return y, st
    res = pl.pallas_call(
        kern, name="rwkv_scan_fwd", grid=(bl, nhg, nc), in_specs=in_specs + carry.in_specs, out_specs=out_specs + carry.out_specs,
        out_shape=out_shape + carry.out_shape, scratch_shapes=scratch + carry.scratch, input_output_aliases=carry.aliases(6, 2),
        compiler_params=_cparams(("arbitrary", "arbitrary", "arbitrary")),
    )(p, lw, k2, p, na, bb, *carry.args)
    return res[0], res[1], list(res[2:])


def _rwkv_scan_bwd(lw, k2, na, bb, p, st, dy, bl, seq, rw, carry=None):
    cs, G, NP, C, nc, nhg = _scan_dims(seq, rw)
    nt = len(carry.tensors) if carry else 0

    def kern(*refs):
        in_refs = refs[:6]
        st_ref, dy_ref = refs[6:8]
        out_refs = refs[8 + 2 * nt:14 + 2 * nt]
        ds_scr = refs[14 + 3 * nt]
        if carry:
            step = (pl.program_id(0) * nhg + pl.program_id(1)) * nc + pl.program_id(2)
            carry.hook(step, bl * nhg * nc - 1, refs[8:8 + nt], refs[14 + 2 * nt:14 + 3 * nt], refs[15 + 3 * nt:])

        @pl.when(pl.program_id(2) == 0)
        def _():
            ds_scr[...] = jnp.zeros_like(ds_scr)

        win = lambda ref, i, g: ref[pl.ds(g * cs, cs), pl.ds(i * LANE, LANE)]
        units = [[tuple(win(ref, i, g) for ref in in_refs) for g in range(G)] for i in range(NP)]
        _, vjp = jax.vjp(_scan_block, [st_ref[i] for i in range(NP)], units)
        dys = [[win(dy_ref, i, g) for g in range(G)] for i in range(NP)]
        ds, dunits = vjp((dys, [ds_scr[i] for i in range(NP)]))
        for i in range(NP):
            ds_scr[i] = ds[i]
            for g in range(G):
                for ref, d in zip(out_refs, dunits[i][g]):
                    ref[pl.ds(g * cs, cs), pl.ds(i * LANE, LANE)] = d

    def tok(off):
        return pl.BlockSpec((C, NP * LANE), functools.partial(lambda b, h, c, o: (b * nc + (nc - 1 - c), o + h), o=off // (NP * LANE)))

    st_spec = pl.BlockSpec((NP, LANE, LANE), lambda b, h, c: ((b * nhg + h) * nc + (nc - 1 - c), 0, 0))
    in_specs = [tok(0), tok(0), tok(0), tok(2 * rw), tok(0), tok(0), st_spec, tok(0)]
    out_shape = [jax.ShapeDtypeStruct((bl * seq, rw), F32)] * 6
    scratch = [pltpu.VMEM((NP, LANE, LANE), F32)]
    if not carry:
        return pl.pallas_call(
            kern, name="rwkv_scan_bwd", grid=(bl, nhg, nc), in_specs=in_specs, out_specs=[tok(0)] * 6, out_shape=out_shape,
            scratch_shapes=scratch, compiler_params=_cparams(("parallel", "parallel", "arbitrary")),
        )(p, lw, k2, p, na, bb, st, dy)
    res = pl.pallas_call(
        kern, name="rwkv_scan_bwd", grid=(bl, nhg, nc), in_specs=in_specs + carry.in_specs, out_specs=[tok(0)] * 6 + carry.out_specs,
        out_shape=out_shape + carry.out_shape, scratch_shapes=scratch + carry.scratch, input_output_aliases=carry.aliases(8, 6),
        compiler_params=_cparams(("arbitrary", "arbitrary", "arbitrary")),
    )(p, lw, k2, p, na, bb, st, dy, *carry.args)
    return res[:6], list(res[6:])


def _loss_head(h2, g_final, target, tile):
    n, d = h2.shape
    nt = n // tile

    def kern(h_ref, g_ref, t_ref, dh_ref, dg_ref, l_ref):
        def f(h, g):
            y = h * lax.rsqrt(jnp.mean(h * h, axis=-1, keepdims=True) + NORM_EPS) * g
            e = y - t_ref[...]
            return 0.5 * jnp.sum(jnp.mean(e * e, axis=-1, keepdims=True))

        loss, (dh, dg) = jax.value_and_grad(f, argnums=(0, 1))(h_ref[...], g_ref[...])
        dh_ref[...] = dh
        first = pl.program_id(0) == 0

        @pl.when(first)
        def _():
            dg_ref[...] = dg
            l_ref[...] = jnp.zeros_like(l_ref) + loss

        @pl.when(jnp.logical_not(first))
        def _():
            dg_ref[...] += dg
            l_ref[...] += loss

    row = pl.BlockSpec((tile, d), lambda i: (i, 0))
    vec = pl.BlockSpec((1, d), lambda i: (0, 0))
    return pl.pallas_call(
        kern, name="loss_head", grid=(nt,), in_specs=[row, vec, row],
        out_specs=[row, vec, pl.BlockSpec((1, LANE), lambda i: (0, 0))],
        out_shape=[jax.ShapeDtypeStruct((n, d), F32), jax.ShapeDtypeStruct((1, d), F32), jax.ShapeDtypeStruct((1, LANE), F32)],
        compiler_params=_cparams(("arbitrary",)),
    )(h2, g_final, target)


def _adamw(parts, w, m, v, name, carry=None):
    n_parts, R, Cc = parts.shape
    tr = _pick(R, tuple(t for t in (1024, 512, 256, 128, 64, 32, 16) if t * Cc <= 128 * 1024) + (8,))
    c1, c2 = 1.0 - ADAM_B1, 1.0 - ADAM_B2
    bc1, bc2 = 1.0 - ADAM_B1 ** ADAM_STEP, 1.0 - ADAM_B2 ** ADAM_STEP

    nt = len(carry.tensors) if carry else 0

    def kern(*refs):
        p_ref, w_ref, m_ref, v_ref = refs[:4]
        g_ref, d_ref, nm_ref, nv_ref = refs[4 + 2 * nt:8 + 2 * nt]
        if carry:
            carry.hook(pl.program_id(0), R // tr - 1, refs[4:4 + nt], refs[8 + 2 * nt:8 + 3 * nt], refs[8 + 3 * nt:])
        g = p_ref[0].astype(F32)
        for s in range(1, n_parts):
            g = g + p_ref[s].astype(F32)
        m2 = ADAM_B1 * m_ref[...] + c1 * g
        v2 = ADAM_B2 * v_ref[...] + c2 * (g * g)
        g_ref[...] = g
        nm_ref[...] = m2
        nv_ref[...] = v2
        d_ref[...] = -ADAM_LR * ((m2 / bc1) / (jnp.sqrt(v2 / bc2) + ADAM_EPS) + ADAM_WD * w_ref[...])

    blk = pl.BlockSpec((tr, Cc), lambda i: (i, 0))
    in_specs = [pl.BlockSpec((n_parts, tr, Cc), lambda i: (0, i, 0)), blk, blk, blk]
    out_shape = [jax.ShapeDtypeStruct((R, Cc), F32)] * 4
    if not carry:
        return pl.pallas_call(
            kern, name=name, grid=(R // tr,), in_specs=in_specs, out_specs=[blk] * 4, out_shape=out_shape,
            compiler_params=_cparams(("parallel",)),
        )(parts, w, m, v)
    res = pl.pallas_call(
        kern, name=name, grid=(R // tr,), in_specs=in_specs + carry.in_specs, out_specs=[blk] * 4 + carry.out_specs,
        out_shape=out_shape + carry.out_shape, scratch_shapes=carry.scratch, input_output_aliases=carry.aliases(4, 4),
        compiler_params=_cparams(("arbitrary",)),
    )(parts, w, m, v, *carry.args)
    return res[:4], list(res[4:])


def _exchange_now(carry, name):
    nt = len(carry.tensors)

    def body(*refs):
        carry.hook(0, 0, refs[:nt], refs[2 * nt:3 * nt], refs[3 * nt:])

    return list(pl.pallas_call(
        body, name=name, in_specs=carry.in_specs, out_specs=carry.out_specs, out_shape=carry.out_shape,
        scratch_shapes=carry.scratch, input_output_aliases=carry.aliases(0, 0),
    )(*carry.args))


class _Queue:
    def __init__(self, gather, label):
        self.gather, self.label = gather, label
        self.tensors, self.fifo, self.n_flush = {}, [], 0

    def push(self, name, src, n_pieces, cost_us):
        n_rows = src.shape[0] if self.gather else src.shape[1]
        rows = n_rows // n_pieces
        assert rows * n_pieces == n_rows and rows % 16 == 0, (name, src.shape)
        self.tensors[name] = [src, lax.empty(((N_DEV,) + src.shape) if self.gather else src.shape, src.dtype)]
        self.fifo += [(name, p * rows, rows, cost_us / n_pieces) for p in range(n_pieces)]

    def take(self, capacity_us):
        picked = []
        while self.fifo and capacity_us >= 0.6 * self.fifo[0][3]:
            picked.append(self.fifo.pop(0))
            capacity_us -= picked[-1][3]
        if not picked:
            return None
        names = list(dict.fromkeys(n for n, _, _, _ in picked))
        cls = _GatherCarry if self.gather else _Carry
        carry = cls([tuple(self.tensors[n]) for n in names], [(names.index(n), r0, rows) for n, r0, rows, _ in picked])
        carry.names = names
        return carry

    def done(self, carry, dsts):
        for n, d in zip(carry.names, dsts):
            self.tensors[n][1] = d

    def flush(self):
        carry = self.take(float("inf"))
        if carry:
            self.done(carry, _exchange_now(carry, "%s_now_%d" % (self.label, self.n_flush)))
            self.n_flush += 1

    def result(self, name):
        assert not any(n == name for n, _, _, _ in self.fifo)
        return self.tensors[name][1]


def _cols_from_shards(g):
    return jnp.transpose(g, (1, 0, 2)).reshape(g.shape[1], N_DEV * g.shape[2])


def _shards_from_cols(w):
    r, n = w.shape
    return jnp.transpose(w.reshape(r, N_DEV, n // N_DEV), (1, 0, 2))


def _pad_cols(w, to):
    return jnp.pad(w, ((0, 0), (0, to - w.shape[1])))


def _pack(arrs):
    flat = jnp.concatenate([a.reshape(-1) for a in arrs])
    n = _rup(flat.shape[0], 256 * LANE)
    return jnp.pad(flat, (0, n - flat.shape[0])).reshape(n // LANE, LANE)


def _unpack(mat, shapes):
    flat = mat.reshape(-1)
    out, o = [], 0
    for s in shapes:
        n = math.prod(s)
        out.append(flat[o:o + n].reshape(s))
        o += n
    return out


_SMALL = ["norm_mix_g", "mu_shift", "rwkv_w0", "rwkv_a0", "rwkv_k_k", "rwkv_k_a", "rwkv_r_k", "rwkv_ln_g", "rwkv_ln_b", "conv_b",
          "lru_wr", "lru_br", "lru_wi", "lru_bi", "lru_lambda", "lru_norm_g", "norm_ffn_g", "norm_final_g"]
_SMALL_SHARDED = ["rwkv_w2", "rwkv_a2", "rwkv_g2", "conv_w"]
_BIG = ["w_in", "w_out", "ffn_w_gate", "ffn_w_up", "ffn_w_down"]
_WEIGHTS = ['norm_mix_g', 'w_in', 'mu_shift', 'rwkv_w0', 'rwkv_w2', 'rwkv_a0', 'rwkv_a2', 'rwkv_g2', 'rwkv_k_k', 'rwkv_k_a', 'rwkv_r_k',
            'rwkv_ln_g', 'rwkv_ln_b', 'conv_w', 'conv_b', 'lru_wr', 'lru_br', 'lru_wi', 'lru_bi', 'lru_lambda', 'lru_norm_g', 'w_out',
            'norm_ffn_g', 'ffn_w_gate', 'ffn_w_up', 'ffn_w_down', 'norm_final_g']


def _step(W, M, V, x, loss_target):
    bl, seq, d = x.shape
    n = bl * seq
    rw = W["rwkv_w0"].shape[1]
    nh = W["rwkv_r_k"].shape[1]
    assert W["rwkv_r_k"].shape[2] == HEAD and nh * HEAD == rw and rw % LANE == 0
    dl, al, gl = W["rwkv_w2"].shape[1], W["rwkv_a2"].shape[1], W["rwkv_g2"].shape[1]
    dlp, alp, glp = _rup(dl, LANE), _rup(al, LANE), _rup(gl, LANE)
    lorap = dlp + alp + glp
    lw_ = W["conv_b"].shape[1]
    nblk, lbw = W["lru_wr"].shape[1], W["lru_wr"].shape[2]
    assert lbw == LANE and nblk * lbw == lw_
    o_xb, o_gate, o_rw = 0, lw_, 2 * lw_
    o_lora = 3 * rw
    rwp = o_lora + lorap
    inp = o_rw + rwp
    nsh_ff = W["ffn_w_gate"].shape[2]
    dff = N_DEV * nsh_ff
    dffp = _rup(dff, 1024) if dff >= 1024 else _rup(dff, LANE)
    x2 = x.reshape(n, d)
    tgt2 = loss_target.reshape(n, d)

    gq = _Queue(True, "gather")
    gq.push("w_in", W["w_in"][0].astype(BF16), 1, 490)
    gq.push("small", _pack([W[k][0] for k in _SMALL_SHARDED]), 1, 10)
    gq.flush()
    gq.push("w_out", W["w_out"][0].astype(BF16), 1, 180)
    gq.push("ffn_w_gate", W["ffn_w_gate"][0].astype(BF16), 2, 490)
    gq.push("ffn_w_up", W["ffn_w_up"][0].astype(BF16), 2, 490)
    gq.push("ffn_w_down", W["ffn_w_down"][0].astype(BF16), 2, 490)

    def gmm(capacity_us, *args, **kw):
        carry = gq.take(capacity_us)
        if carry is None:
            return _mm(*args, **kw)
        res, dsts = _mm(*args, carry=carry, **kw)
        gq.done(carry, dsts)
        return res

    g_small = gq.result("small")
    w_in_l = _cols_from_shards(gq.result("w_in"))
    o1 = 3 * rw
    w_in = jnp.concatenate([w_in_l[:, o1 + dl + al + gl:], w_in_l[:, :o1], _pad_cols(w_in_l[:, o1:o1 + dl], dlp),
                            _pad_cols(w_in_l[:, o1 + dl:o1 + dl + al], alp), _pad_cols(w_in_l[:, o1 + dl + al:o1 + dl + al + gl], glp)], axis=1)
    sm_shapes = [W[k][0].shape for k in _SMALL_SHARDED]
    sm = [_unpack(g_small[s], sm_shapes) for s in range(N_DEV)]
    w2, a2, g2, conv_w = [jnp.concatenate([sm[s][i] for s in range(N_DEV)], axis=1) for i in range(4)]
    w_lora = jnp.zeros((lorap, 3 * rw), F32)
    w_lora = w_lora.at[:dl, :rw].set(w2).at[dlp:dlp + al, rw:2 * rw].set(a2).at[dlp + alp:dlp + alp + gl, 2 * rw:].set(g2)
    w_lora = w_lora.astype(BF16)
    mu_l = W["mu_shift"]
    mu = jnp.concatenate([mu_l[:, :o1], _pad_cols(mu_l[:, o1:o1 + dl], dlp), _pad_cols(mu_l[:, o1 + dl:o1 + dl + al], alp),
                          _pad_cols(mu_l[:, o1 + dl + al:], glp)], axis=1)
    r_k = W["rwkv_r_k"].reshape(1, rw)

    tile = _pick(n, (256, 128, 64))
    tile_s = _pick(n, (128, 64))
    ct_seq = _pick(math.gcd(rwp, lw_), (256, 128))
    assert o_rw % ct_seq == 0 and o_gate % lw_ == 0
    ct_h = _pick(rw, (512, 256, 128))
    gi = lax.broadcasted_iota(jnp.int32, (ct_h, ct_h), 0) // HEAD
    gj = lax.broadcasted_iota(jnp.int32, (ct_h, ct_h), 1) // HEAD
    gsum = ((gi == gj).astype(BF16), (ct_h, ct_h), lambda j: (0, 0))
    full = lambda a: (a, a.shape, lambda j: (0,) * a.ndim)
    rowp = lambda a, ct: (a,) + _row(ct)

    u1, = _stage_fwd(_f_rmsnorm, "norm_mix_fwd", n, d, tile, d, [(x2, 0)], [full(W["norm_mix_g"])], [], [BF16])
    p = gmm(430, u1, w_in, name="mm_in")
    ps = _lerp_fwd(p, o_rw, mu, bl, seq, rwp, ct_seq)
    f_lora = functools.partial(_f_lora_act, widths=(dlp, alp))
    lact, = _stage_fwd(f_lora, "lora_act_fwd", n, lorap, tile, lorap, [(ps, o_lora)], [], [], [BF16])
    wag = _mm(lact, w_lora, name="mm_lora")
    pre_par = [rowp(W["rwkv_w0"], ct_h), rowp(W["rwkv_a0"], ct_h), rowp(W["rwkv_k_k"], ct_h), rowp(W["rwkv_k_a"], ct_h)]
    pre_acts = [(ps, rw), (wag, 0), (wag, rw)]
    lw, k2, na, bb = _stage_fwd(_f_rwkv_pre, "rwkv_pre_fwd", n, rw, tile_s, ct_h, pre_acts, pre_par, [gsum], [F32] * 4)
    carry = gq.take(590)
    if carry is None:
        ysc, st = _rwkv_scan_fwd(None, lw, k2, None, na, bb, ps, bl, seq, rw)
    else:
        ysc, st, dsts = _rwkv_scan_fwd(None, lw, k2, None, na, bb, ps, bl, seq, rw, carry=carry)
        gq.done(carry, dsts)
    post_par = [rowp(W["rwkv_ln_g"], ct_h), rowp(W["rwkv_ln_b"], ct_h), rowp(r_k, ct_h)]
    post_acts = [(ysc, 0), (ps, 0), (k2, 0), (ps, 2 * rw), (wag, 2 * rw)]
    ya, = _stage_fwd(_f_rwkv_post, "rwkv_post_fwd", n, rw, tile_s, ct_h, post_acts, post_par, [gsum], [BF16])

    xc = _conv_fwd(p, o_xb, conv_w, W["conv_b"], bl, seq, lw_, ct_seq)
    f_gates = functools.partial(_f_lru_gates, seq=seq)
    blk3 = lambda a: (a[0], (1, LANE, LANE), lambda j: (j, 0, 0))
    gate_par = [blk3(W["lru_wr"]), rowp(W["lru_br"], LANE), blk3(W["lru_wi"]), rowp(W["lru_bi"], LANE), rowp(W["lru_lambda"], LANE)]
    a_l, bx = _stage_fwd(f_gates, "lru_gates_fwd", n, lw_, tile, LANE, [(xc, 0)], gate_par, [], [F32, F32])
    ct_l = _pick(lw_, (256, 128))
    h_l = _lru_scan_fwd(a_l, bx, bl, seq, lw_, ct_l)
    lpost_par = [full(W["lru_norm_g"])]
    yb, = _stage_fwd(_f_lru_post, "lru_post_fwd", n, lw_, tile_s, lw_, [(h_l, 0), (p, o_gate)], lpost_par, [], [BF16])

    ycat = jnp.concatenate([ya, yb], axis=1)
    g_out = gq.result("w_out")
    w_out = g_out.reshape(N_DEV * g_out.shape[1], d)
    h1 = gmm(160, ycat, w_out, name="mm_out", add=x2)
    u2, = _stage_fwd(_f_rmsnorm, "norm_ffn_fwd", n, d, tile, d, [(h1, 0)], [full(W["norm_ffn_g"])], [], [BF16])
    w_gate = _pad_cols(_cols_from_shards(gq.result("ffn_w_gate")), dffp)
    ff_gate = gmm(410, u2, w_gate, name="mm_gate")
    w_up = _pad_cols(_cols_from_shards(gq.result("ffn_w_up")), dffp)
    ff_up = gmm(410, u2, w_up, name="mm_up")
    ct_f = _pick(dffp, (1024, 512, 256, 128))
    ff_acts = [(ff_gate, 0), (ff_up, 0)]
    act, = _stage_fwd(_f_swiglu, "swiglu_fwd", n, dffp, tile, ct_f, ff_acts, [], [], [BF16])
    gq.flush()
    w_down = jnp.pad(gq.result("ffn_w_down").reshape(dff, d), ((0, dffp - dff), (0, 0)))
    h2 = _mm(act, w_down, name="mm_down", add=h1)

    dh2, dg_final, lsum = _loss_head(h2, W["norm_final_g"].reshape(1, d), tgt2, tile_s)
    loss = lax.psum(lsum[0, 0], ("x", "y", "c"))
    queue = _Queue(False, "exchange")

    def cmm(capacity_us, *args, **kw):
        carry = queue.take(capacity_us)
        if carry is None:
            return _mm(*args, **kw)
        res, dsts = _mm(*args, carry=carry, **kw)
        queue.done(carry, dsts)
        return res

    dh2b = dh2.astype(BF16)
    dact = _mm(dh2b, w_down, name="mm_dact", tb=True, out_dtype=BF16)
    dw_down = _mm(act, dh2b, name="mm_dw_down", ta=True, out_dtype=BF16)
    queue.push("ffn_w_down", dw_down[:dff].reshape(N_DEV, nsh_ff, d), 2, 1000)
    (dgate, dup), _ = _stage_bwd(_f_swiglu, "swiglu_bwd", n, dffp, tile, ct_f, ff_acts, [], [], [(dact, 0)], [BF16, BF16])
    du2 = cmm(465, dgate, w_gate, name="mm_du2_gate", tb=True)
    dw_gate = cmm(415, u2, dgate, name="mm_dw_gate", ta=True, out_dtype=BF16)
    queue.push("ffn_w_gate", _shards_from_cols(dw_gate[:, :dff]), 4, 1000)
    du2 = cmm(465, dup, w_up, name="mm_du2_up", tb=True, add=du2)
    dw_up = cmm(415, u2, dup, name="mm_dw_up", ta=True, out_dtype=BF16)
    queue.push("ffn_w_up", _shards_from_cols(dw_up[:, :dff]), 4, 1000)
    (dh1,), (dg_ffn,) = _stage_bwd(_f_rmsnorm, "norm_ffn_bwd", n, d, tile_s, d, [(h1, 0)], [full(W["norm_ffn_g"])], [], [(du2, 0)], [F32],
                                   extra_add=(dh2, 0))
    dh1b = dh1.astype(BF16)
    dycat = cmm(160, dh1b, w_out, name="mm_dycat", tb=True)
    dw_out = cmm(200, ycat, dh1b, name="mm_dw_out", ta=True, out_dtype=BF16)
    queue.push("w_out", dw_out.reshape(N_DEV, -1, d), 1, 370)

    (dysc, dr_p, dk2_p, dv_p, dg_g), (dln_g, dln_b, dr_k) = _stage_bwd(
        _f_rwkv_post, "rwkv_post_bwd", n, rw, tile_s, ct_h, post_acts, post_par, [gsum], [(dycat, 0)], [F32] * 5)
    carry = queue.take(1600)
    if carry is None:
        dr_s, dlw, dk2_s, dv_s, dna, dbb = _rwkv_scan_bwd(lw, k2, na, bb, ps, st, dysc, bl, seq, rw)
    else:
        (dr_s, dlw, dk2_s, dv_s, dna, dbb), dsts = _rwkv_scan_bwd(lw, k2, na, bb, ps, st, dysc, bl, seq, rw, carry=carry)
        queue.done(carry, dsts)
    dk2 = dk2_p + dk2_s
    (dk, dwlin, dalin), (dw0, da0, dk_k, dk_a) = _stage_bwd(
        _f_rwkv_pre, "rwkv_pre_bwd", n, rw, tile_s, ct_h, pre_acts, pre_par, [gsum], [(dlw, 0), (dk2, 0), (dna, 0), (dbb, 0)], [F32] * 3)
    dwag = jnp.concatenate([dwlin, dalin, dg_g], axis=1).astype(BF16)
    dlact = cmm(90, dwag, w_lora, name="mm_dlact", tb=True)
    dw_lora = cmm(60, lact, dwag, name="mm_dw_lora", ta=True)
    (dps_lora,), _ = _stage_bwd(f_lora, "lora_act_bwd", n, lorap, tile, lorap, [(ps, o_lora)], [], [], [(dlact, 0)], [F32])
    dps = jnp.concatenate([dr_p + dr_s, dk, dv_p + dv_s, dps_lora], axis=1)
    dp_rwkv, dmu = _lerp_bwd(p, o_rw, mu, dps, bl, seq, rwp, ct_seq, BF16)

    (dh_l, dgate_l), (dlru_norm_g,) = _stage_bwd(_f_lru_post, "lru_post_bwd", n, lw_, tile_s, lw_, [(h_l, 0), (p, o_gate)], lpost_par, [],
                                                 [(dycat, rw)], [F32, BF16])
    da_l, dbx = _lru_scan_bwd(a_l, h_l, dh_l, bl, seq, lw_, ct_l)
    (dxc,), (dwr, dbr, dwi, dbi, dlam) = _stage_bwd(f_gates, "lru_gates_bwd", n, lw_, tile, LANE, [(xc, 0)], gate_par, [],
                                                    [(da_l, 0), (dbx, 0)], [F32])
    dxb, dconv_w, dconv_b = _conv_bwd(p, o_xb, conv_w, dxc, bl, seq, lw_, ct_seq, BF16)
    sh_full = [dw_lora[:dl, :rw], dw_lora[dlp:dlp + al, rw:2 * rw], dw_lora[dlp + alp:dlp + alp + gl, 2 * rw:], dconv_w]
    assert rw == lw_
    rows_sh = sum(a.shape[0] for a in sh_full)
    pad_sh = _rup(rows_sh, 16) - rows_sh
    queue.push("small_sharded", jnp.pad(jnp.concatenate([_shards_from_cols(a) for a in sh_full], axis=1), ((0, 0), (0, pad_sh), (0, 0))), 1, 40)
    stack_sh = lambda D: jnp.pad(jnp.concatenate([D[k][0] for k in _SMALL_SHARDED], axis=0), ((0, pad_sh), (0, 0)))

    dp = jnp.concatenate([dxb, dgate_l, dp_rwkv], axis=1)
    dw_in = cmm(400, u1, dp, name="mm_dw_in", ta=True, out_dtype=BF16)
    ol = o_rw + o_lora
    dw_in_l = jnp.concatenate([dw_in[:, o_rw:ol], dw_in[:, ol:ol + dl], dw_in[:, ol + dlp:ol + dlp + al],
                               dw_in[:, ol + dlp + alp:ol + dlp + alp + gl], dw_in[:, :o_rw]], axis=1)
    queue.push("w_in", _shards_from_cols(dw_in_l), 4, 970)
    du1 = cmm(460, dp, w_in, name="mm_du1", tb=True)
    (grad_x,), (dg_mix,) = _stage_bwd(_f_rmsnorm, "norm_mix_bwd", n, d, tile_s, d, [(x2, 0)], [full(W["norm_mix_g"])], [], [(du1, 0)], [F32],
                                      extra_add=(dh1, 0))
    out = {}
    for k in ["ffn_w_down", "ffn_w_gate", "ffn_w_up", "w_out", "w_in"]:
        carry = queue.take(250) if k != "w_in" else None
        if k == "w_in":
            queue.flush()
        res = _adamw(queue.result(k), W[k][0], M[k][0], V[k][0], "adamw_" + k, carry=carry)
        if carry:
            res, dsts = res
            queue.done(carry, dsts)
        out[k] = [o[None] for o in res]

    dmu_l = jnp.concatenate([dmu[:, :o1], dmu[:, o_lora:o_lora + dl], dmu[:, o_lora + dlp:o_lora + dlp + al],
                             dmu[:, o_lora + dlp + alp:o_lora + dlp + alp + gl]], axis=1)
    small_g = {"norm_mix_g": dg_mix, "mu_shift": dmu_l, "rwkv_w0": dw0, "rwkv_a0": da0, "rwkv_k_k": dk_k, "rwkv_k_a": dk_a,
               "rwkv_r_k": dr_k.reshape(W["rwkv_r_k"].shape), "rwkv_ln_g": dln_g, "rwkv_ln_b": dln_b, "conv_b": dconv_b,
               "lru_wr": dwr[None], "lru_br": dbr, "lru_wi": dwi[None], "lru_bi": dbi, "lru_lambda": dlam, "lru_norm_g": dlru_norm_g,
               "norm_ffn_g": dg_ffn, "norm_final_g": dg_final.reshape(W["norm_final_g"].shape)}
    gq.push("small_grads", _pack([small_g[k] for k in _SMALL]), 1, 50)
    gq.flush()
    pk = lambda D: _pack([D[k] for k in _SMALL])
    res = _adamw(gq.result("small_grads"), pk(W), pk(M), pk(V), "adamw_small")
    shapes = [W[k].shape for k in _SMALL]
    for i, r in enumerate(res):
        for k, a in zip(_SMALL, _unpack(r, shapes)):
            out.setdefault(k, [None] * 4)[i] = a
    res = _adamw(queue.result("small_sharded"), stack_sh(W), stack_sh(M), stack_sh(V), "adamw_small_sharded")
    for i, r in enumerate(res):
        o = 0
        for k in _SMALL_SHARDED:
            rows = W[k].shape[1]
            out.setdefault(k, [None] * 4)[i] = r[o:o + rows][None]
            o += rows
    return loss, grad_x.reshape(x.shape), out


def kernel(x, norm_mix_g, w_in, mu_shift, rwkv_w0, rwkv_w2, rwkv_a0, rwkv_a2, rwkv_g2, rwkv_k_k, rwkv_k_a, rwkv_r_k, rwkv_ln_g, rwkv_ln_b, conv_w, conv_b, lru_wr, lru_br, lru_wi, lru_bi, lru_lambda, lru_norm_g, w_out, norm_ffn_g, ffn_w_gate, ffn_w_up, ffn_w_down, norm_final_g, loss_target, m_norm_mix_g, m_w_in, m_mu_shift, m_rwkv_w0, m_rwkv_w2, m_rwkv_a0, m_rwkv_a2, m_rwkv_g2, m_rwkv_k_k, m_rwkv_k_a, m_rwkv_r_k, m_rwkv_ln_g, m_rwkv_ln_b, m_conv_w, m_conv_b, m_lru_wr, m_lru_br, m_lru_wi, m_lru_bi, m_lru_lambda, m_lru_norm_g, m_w_out, m_norm_ffn_g, m_ffn_w_gate, m_ffn_w_up, m_ffn_w_down, m_norm_final_g, v_norm_mix_g, v_w_in, v_mu_shift, v_rwkv_w0, v_rwkv_w2, v_rwkv_a0, v_rwkv_a2, v_rwkv_g2, v_rwkv_k_k, v_rwkv_k_a, v_rwkv_r_k, v_rwkv_ln_g, v_rwkv_ln_b, v_conv_w, v_conv_b, v_lru_wr, v_lru_br, v_lru_wi, v_lru_bi, v_lru_lambda, v_lru_norm_g, v_w_out, v_norm_ffn_g, v_ffn_w_gate, v_ffn_w_up, v_ffn_w_down, v_norm_final_g):
    a = locals()
    W = {k: a[k] for k in _WEIGHTS}
    M = {k: a["m_" + k] for k in _WEIGHTS}
    V = {k: a["v_" + k] for k in _WEIGHTS}
    loss, grad_x, out = _step(W, M, V, x, loss_target)
    res = [loss, grad_x]
    for i in range(4):
        res += [out[k][i].reshape(W[k].shape) for k in _WEIGHTS]
    return tuple(res)
```

```python
import functools
import math

import jax
import jax.numpy as jnp
from jax import lax
from jax.experimental import pallas as pl
from jax.experimental.pallas import tpu as pltpu

F32 = jnp.float32
BF16 = jnp.bfloat16
HI = lax.Precision.HIGHEST
MESH = pl.DeviceIdType.MESH

N_DEV = 8
LANE = 128
HEAD = 64
MM_MAX_TK = 4096
SCAN_CHUNK = 64
SCAN_GROUP = 2
SCAN_PAIRS = 8
VMEM_LIMIT = 56 * 1024 * 1024

NORM_EPS = 1e-6
GN_EPS = 64e-5
LRU_C = 8.0
ADAM_LR, ADAM_B1, ADAM_B2, ADAM_EPS, ADAM_WD, ADAM_STEP = 0.001, 0.9, 0.999, 1e-08, 0.01, 10


def _pick(n, cands):
    for c in cands:
        if n % c == 0:
            return c
    return n


def _rup(n, m):
    return (n + m - 1) // m * m


def _cparams(dims):
    return pltpu.CompilerParams(dimension_semantics=dims, vmem_limit_bytes=VMEM_LIMIT)


def _sigmoid(x):
    return 1.0 / (1.0 + jnp.exp(-x))


def _softplus(z):
    return jnp.maximum(z, 0.0) + jnp.log(1.0 + jnp.exp(-jnp.abs(z)))


def _neg_expm1(x):
    series = -(x * (1.0 + 0.5 * x * (1.0 + (x / 3.0) * (1.0 + 0.25 * x))))
    return jnp.where(jnp.abs(x) < 0.03, series, 1.0 - jnp.exp(x))


def _gelu(x):
    return 0.5 * x * (1.0 + jnp.tanh(0.7978845608028654 * (x + 0.044715 * (x * x * x))))


def _dot(a, b, dims, precision=None):
    return lax.dot_general(a, b, (dims, ((), ())), precision=precision, preferred_element_type=F32)


def _nn(a, b, precision=None):
    return _dot(a, b, ((1,), (0,)), precision)


def _nt(a, b, precision=None):
    return _dot(a, b, ((1,), (1,)), precision)


def _tn(a, b, precision=None):
    return _dot(a, b, ((0,), (0,)), precision)


def _coords():
    return lax.axis_index("x"), lax.axis_index("y"), lax.axis_index("c")


class _Carry:
    def __init__(self, tensors, items):
        self.tensors, self.items = tensors, items
        nt, ni = len(tensors), len(items)
        any_spec = pl.BlockSpec(memory_space=pl.ANY)
        self.args = [s for s, _ in tensors] + [d for _, d in tensors]
        self.in_specs = [any_spec] * (2 * nt)
        self.out_specs = [any_spec] * nt
        self.out_shape = [jax.ShapeDtypeStruct(d.shape, d.dtype) for _, d in tensors]
        self.scratch = [pltpu.SemaphoreType.DMA((ni, N_DEV - 1)), pltpu.SemaphoreType.DMA((ni, N_DEV - 1)), pltpu.SemaphoreType.DMA((ni,))]

    def aliases(self, first_in, first_out):
        nt = len(self.tensors)
        return {first_in + nt + t: first_out + t for t in range(nt)}

    def _copies(self, src_refs, dst_refs, sems):
        send_sems, recv_sems, local_sems = sems
        x, y, c = _coords()
        my = 4 * x + 2 * y + c
        out = []
        for n, (t, r0, rows) in enumerate(self.items):
            win = pl.ds(r0, rows)
            out.append(pltpu.make_async_copy(src_refs[t].at[my, win], dst_refs[t].at[my, win], local_sems.at[n]))
            for k in range(1, N_DEV):
                px, py, pc = x ^ ((k >> 2) & 1), y ^ ((k >> 1) & 1), c ^ (k & 1)
                out.append(pltpu.make_async_remote_copy(
                    src_ref=src_refs[t].at[4 * px + 2 * py + pc, win], dst_ref=dst_refs[t].at[my, win],
                    send_sem=send_sems.at[n, k - 1], recv_sem=recv_sems.at[n, k - 1],
                    device_id=(px, py, pc), device_id_type=MESH))
        return out

    def hook(self, step, last, src_refs, dst_refs, sems):
        if last == 0:
            for cp in self._copies(src_refs, dst_refs, sems):
                cp.start()
            for cp in self._copies(src_refs, dst_refs, sems):
                cp.wait()
            return

        @pl.when(step == 0)
        def _():
            for cp in self._copies(src_refs, dst_refs, sems):
                cp.start()

        @pl.when(step == last)
        def _():
            for cp in self._copies(src_refs, dst_refs, sems):
                cp.wait()


class _GatherCarry(_Carry):
    def hook(self, step, last, src_refs, dst_refs, sems):
        send_sems, recv_sems, local_sems = sems
        x, y, c = _coords()
        me, sibling = (x, y, c), (x, y, 1 - c)
        chips = [(1 - x, y), (x, 1 - y), (1 - x, 1 - y)]

        def per_item(fn):
            for n, (t, r0, rows) in enumerate(self.items):
                win = pl.ds(r0, rows)

                def copy(k, block, to, own=False, n=n, t=t, win=win):
                    slot = dst_refs[t].at[4 * block[0] + 2 * block[1] + block[2], win]
                    return pltpu.make_async_remote_copy(
                        src_ref=src_refs[t].at[win] if own else slot, dst_ref=slot,
                        send_sem=send_sems.at[n, k], recv_sem=recv_sems.at[n, k], device_id=to, device_id_type=MESH)

                mine = pltpu.make_async_copy(src_refs[t].at[win], dst_refs[t].at[4 * x + 2 * y + c, win], local_sems.at[n])
                first = [copy(0, me, sibling, own=True)] + [copy(1 + j, me, (*chip, c), own=True) for j, chip in enumerate(chips)]
                fn(copy, mine, first)

        def begin(copy, mine, first):
            mine.start()
            for cp in first:
                cp.start()

        def pass_on(copy, mine, first):
            for j, chip in enumerate(chips):
                copy(1 + j, (*chip, c), me).wait_recv()
                copy(4 + j, (*chip, c), sibling).start()

        def finish(copy, mine, first):
            copy(0, sibling, me).wait_recv()
            for j, chip in enumerate(chips):
                copy(4 + j, (*chip, 1 - c), me).wait_recv()
            for cp in first + [copy(4 + j, (*chip, c), sibling) for j, chip in enumerate(chips)]:
                cp.wait_send()
            mine.wait()

        if last == 0:
            for fn in (begin, pass_on, finish):
                per_item(fn)
            return
        late = max(1, (7 * last) // 8)
        for at, fn in ((0, begin), (late, pass_on), (last, finish)):
            pl.when(step == at)(functools.partial(per_item, fn))


def _mm(a, b, *, name, ta=False, tb=False, out_dtype=F32, add=None, tiles=None, carry=None):
    M, K = (a.shape[1], a.shape[0]) if ta else a.shape
    N = b.shape[0] if tb else b.shape[1]
    assert (b.shape[1] if tb else b.shape[0]) == K, (a.shape, b.shape, ta, tb)
    tk = max(t for t in range(LANE, min(K, MM_MAX_TK) + 1, LANE) if K % t == 0)
    tm, tn, tk = tiles or (_pick(M, (1024, 512, 256, 128)), _pick(N, (512, 256, 128)), tk)
    nk = K // tk
    dims = ((0 if ta else 1,), (1 if tb else 0,))

    n_in = 2 + (add is not None)
    nt = len(carry.tensors) if carry else 0
    gi, gj = M // tm, N // tn

    def kern(*refs):
        a_ref, b_ref = refs[:2]
        add_ref = refs[2] if add is not None else None
        o_ref = refs[n_in + 2 * nt]
        scr = refs[n_in + 3 * nt + 1:]
        if carry:
            step = (pl.program_id(0) * gj + pl.program_id(1)) * nk + pl.program_id(2)
            carry.hook(step, gi * gj * nk - 1, refs[n_in:n_in + nt], refs[n_in + 2 * nt + 1:n_in + 3 * nt + 1], scr[-3:])

        def finish(r):
            if add is not None:
                r = r + add_ref[...].astype(F32)
            o_ref[...] = r.astype(o_ref.dtype)

        if nk == 1:
            finish(_dot(a_ref[...], b_ref[...], dims))
            return
        acc = scr[0]
        k = pl.program_id(2)

        @pl.when(k == 0)
        def _():
            acc[...] = jnp.zeros_like(acc)

        acc[...] += _dot(a_ref[...], b_ref[...], dims)

        @pl.when(k == nk - 1)
        def _():
            finish(acc[...])

    a_spec = pl.BlockSpec((tk, tm), lambda i, j, k: (k, i)) if ta else pl.BlockSpec((tm, tk), lambda i, j, k: (i, k))
    b_spec = pl.BlockSpec((tn, tk), lambda i, j, k: (j, k)) if tb else pl.BlockSpec((tk, tn), lambda i, j, k: (k, j))
    o_spec = pl.BlockSpec((tm, tn), lambda i, j, k: (i, j))
    in_specs = [a_spec, b_spec] + ([o_spec] if add is not None else [])
    args = (a, b) + ((add,) if add is not None else ())
    scratch = [pltpu.VMEM((tm, tn), F32)] if nk > 1 else []
    o_shape = jax.ShapeDtypeStruct((M, N), out_dtype)
    if not carry:
        return pl.pallas_call(
            kern, name=name, grid=(gi, gj, nk), in_specs=in_specs, out_specs=o_spec, out_shape=o_shape, scratch_shapes=scratch,
            compiler_params=_cparams(("parallel", "parallel", "arbitrary")),
        )(*args)
    res = pl.pallas_call(
        kern, name=name, grid=(gi, gj, nk), in_specs=in_specs + carry.in_specs, out_specs=[o_spec] + carry.out_specs,
        out_shape=[o_shape] + carry.out_shape, scratch_shapes=scratch + carry.scratch,
        input_output_aliases=carry.aliases(n_in, 1), compiler_params=_cparams(("arbitrary", "arbitrary", "arbitrary")),
    )(*args, *carry.args)
    return res[0], list(res[1:])


def _stage_specs(acts, params, consts, tile, ct):
    act_specs = [pl.BlockSpec((tile, ct), functools.partial(lambda j, i, o: (i, o + j), o=off // ct)) for _, off in acts]
    par_specs = [pl.BlockSpec(bs, functools.partial(lambda j, i, im: im(j), im=im)) for _, bs, im in params]
    con_specs = [pl.BlockSpec(bs, functools.partial(lambda j, i, im: im(j), im=im)) for _, bs, im in consts]
    return act_specs, par_specs, con_specs


def _stage_fwd(f, name, n_rows, width, tile, ct, acts, params, consts, out_dtypes, carry=None):
    for _, off in acts:
        assert off % ct == 0
    na, npar, nc, no = len(acts), len(params), len(consts), len(out_dtypes)
    n_in = na + npar + nc
    nt = len(carry.tensors) if carry else 0
    gj, gi = width // ct, n_rows // tile

    def kern(*refs):
        if carry:
            step = pl.program_id(0) * gi + pl.program_id(1)
            carry.hook(step, gj * gi - 1, refs[n_in:n_in + nt], refs[n_in + 2 * nt + no:n_in + 3 * nt + no], refs[n_in + 3 * nt + no:])
        a = [r[...].astype(F32) for r in refs[:na]]
        p = [r[...] for r in refs[na:na + npar]]
        c = [r[...] for r in refs[na + npar:n_in]]
        outs = f(a, p, c, pl.program_id(1) * tile)
        for r, o in zip(refs[n_in + 2 * nt:n_in + 2 * nt + no], outs):
            r[...] = o.astype(r.dtype)

    act_specs, par_specs, con_specs = _stage_specs(acts, params, consts, tile, ct)
    o_spec = pl.BlockSpec((tile, ct), lambda j, i: (i, j))
    in_specs = act_specs + par_specs + con_specs
    out_shape = [jax.ShapeDtypeStruct((n_rows, width), d) for d in out_dtypes]
    args = [a for a, _ in acts] + [p for p, _, _ in params] + [c for c, _, _ in consts]
    if not carry:
        return tuple(pl.pallas_call(
            kern, name=name, grid=(gj, gi), in_specs=in_specs, out_specs=[o_spec] * no, out_shape=out_shape,
            compiler_params=_cparams(("parallel", "parallel")),
        )(*args))
    res = pl.pallas_call(
        kern, name=name, grid=(gj, gi), in_specs=in_specs + carry.in_specs, out_specs=[o_spec] * no + carry.out_specs,
        out_shape=out_shape + carry.out_shape, scratch_shapes=carry.scratch, input_output_aliases=carry.aliases(n_in, no),
        compiler_params=_cparams(("arbitrary", "arbitrary")),
    )(*args, *carry.args)
    return tuple(res[:no]), list(res[no:])


def _stage_bwd(f, name, n_rows, width, tile, ct, acts, params, consts, couts, dact_dtypes, extra_add=None, carry=None):
    na, npar, nc, no = len(acts), len(params), len(consts), len(couts)
    nx = 0 if extra_add is None else 1
    nt = len(carry.tensors) if carry else 0
    n_in = na + npar + nc + no + nx
    gj, gi = width // ct, n_rows // tile

    def kern(*refs):
        if carry:
            step = pl.program_id(0) * gi + pl.program_id(1)
            n_out = n_in + 2 * nt + na + npar
            carry.hook(step, gj * gi - 1, refs[n_in:n_in + nt], refs[n_out:n_out + nt], refs[n_out + nt:])
        a = [r[...].astype(F32) for r in refs[:na]]
        p = [r[...] for r in refs[na:na + npar]]
        c = [r[...] for r in refs[na + npar:na + npar + nc]]
        base = na + npar + nc
        co = [r[...].astype(F32) for r in refs[base:base + no]]
        base += no
        x_refs = refs[base:base + nx]
        base += nx + 2 * nt
        da_refs = refs[base:base + na]
        dp_refs = refs[base + na:base + na + npar]
        row0 = pl.program_id(1) * tile
        _, vjp = jax.vjp(lambda aa, pp: tuple(f(aa, pp, c, row0)), a, p)
        da, dp = vjp(tuple(co))
        for k, (r, d) in enumerate(zip(da_refs, da)):
            if k == 0 and nx:
                d = d + x_refs[0][...].astype(F32)
            r[...] = d.astype(r.dtype)
        first = pl.program_id(1) == 0
        for r, d in zip(dp_refs, dp):
            @pl.when(first)
            def _(r=r, d=d):
                r[...] = d

            @pl.when(jnp.logical_not(first))
            def _(r=r, d=d):
                r[...] += d

    act_specs, par_specs, con_specs = _stage_specs(acts, params, consts, tile, ct)
    t_spec = pl.BlockSpec((tile, ct), lambda j, i: (i, j))
    co_specs = [pl.BlockSpec((tile, ct), functools.partial(lambda j, i, o: (i, o + j), o=off // ct)) for _, off in couts]
    x_specs = [] if extra_add is None else [pl.BlockSpec((tile, ct), functools.partial(lambda j, i, o: (i, o + j), o=extra_add[1] // ct))]
    x_args = [] if extra_add is None else [extra_add[0]]
    in_specs = act_specs + par_specs + con_specs + co_specs + x_specs
    out_specs = [t_spec] * na + par_specs
    out_shape = [jax.ShapeDtypeStruct((n_rows, width), d) for d in dact_dtypes] + [jax.ShapeDtypeStruct(p.shape, F32) for p, _, _ in params]
    args = [a for a, _ in acts] + [p for p, _, _ in params] + [c for c, _, _ in consts] + [c for c, _ in couts] + x_args
    if not carry:
        outs = pl.pallas_call(
            kern, name=name, grid=(gj, gi), in_specs=in_specs, out_specs=out_specs, out_shape=out_shape,
            compiler_params=_cparams(("parallel", "arbitrary")),
        )(*args)
        return tuple(outs[:na]), tuple(outs[na:])
    outs = pl.pallas_call(
        kern, name=name, grid=(gj, gi), in_specs=in_specs + carry.in_specs, out_specs=out_specs + carry.out_specs,
        out_shape=out_shape + carry.out_shape, scratch_shapes=carry.scratch, input_output_aliases=carry.aliases(n_in, na + npar),
        compiler_params=_cparams(("arbitrary", "arbitrary")),
    )(*args, *carry.args)
    return tuple(outs[:na]), tuple(outs[na:na + npar]), list(outs[na + npar:])


def _row(ct):
    return (1, ct), (lambda j: (0, j))


def _f_rmsnorm(a, p, c, row0):
    x, = a
    g, = p
    return (x * lax.rsqrt(jnp.mean(x * x, axis=-1, keepdims=True) + NORM_EPS) * g,)


def _f_lora_act(a, p, c, row0, widths):
    x, = a
    dl, al = widths
    col = lax.broadcasted_iota(jnp.int32, x.shape, 1)
    return (jnp.where(col < dl, jnp.tanh(x), jnp.where(col < dl + al, x, _sigmoid(x))),)


def _head_sums_raw(x, ones):
    hi = x.astype(BF16)
    lo = (x - hi.astype(F32)).astype(BF16)
    return _nn(hi, ones) + _nn(lo, ones)


@jax.custom_vjp
def _head_sums(x, ones):
    return _head_sums_raw(x, ones)


_head_sums.defvjp(lambda x, ones: (_head_sums_raw(x, ones), ones),
                  lambda ones, ct: (_head_sums_raw(ct, ones), jnp.zeros_like(ones)))


def _f_rwkv_pre(a, p, c, row0):
    k, wlin, alin = a
    w0, a0, k_k, k_a = p
    gsum, = c
    w = -_softplus(-(w0 + wlin)) - 0.5
    lw = -jnp.exp(w)
    alpha = _sigmoid(a0 + alin)
    kk = k * k_k
    ss = _head_sums(kk * kk, gsum)
    kk = kk * lax.rsqrt(jnp.maximum(ss, 1e-24))
    k2 = k * (1.0 + (alpha - 1.0) * k_a)
    return lw, k2, -kk, kk * alpha


def _f_rwkv_post(a, p, c, row0):
    y, r, k2, v, g = a
    ln_g, ln_b, r_k = p
    gsum, = c
    inv = 1.0 / HEAD
    mean = _head_sums(y, gsum) * inv
    yc = y - mean
    var = _head_sums(yc * yc, gsum) * inv
    yn = yc * lax.rsqrt(var + GN_EPS) * ln_g + ln_b
    bonus = _head_sums(r * k2 * r_k, gsum)
    return ((yn + bonus * v) * g,)


def _f_lru_gates(a, p, c, row0, seq):
    xc, = a
    wr, br, wi, bi, lam = p
    xb = xc.astype(BF16)
    rg = _sigmoid(_nn(xb, wr[0].astype(BF16)) + br)
    ig = _sigmoid(_nn(xb, wi[0].astype(BF16)) + bi)
    log_a = -LRU_C * rg * _softplus(-lam)
    a_t = jnp.exp(log_a)
    mult = jnp.sqrt(_neg_expm1(2.0 * log_a))
    row = row0 + lax.broadcasted_iota(jnp.int32, xc.shape, 0)
    mult = jnp.where(row % seq == 0, 1.0, mult)
    return a_t, mult * ig * xc


def _f_lru_post(a, p, c, row0):
    h, gate = a
    g, = p
    y = h * _gelu(gate)
    return (y * lax.rsqrt(jnp.mean(y * y, axis=-1, keepdims=True) + NORM_EPS) * g,)


def _f_swiglu(a, p, c, row0):
    gate, up = a
    return (gate * _sigmoid(gate) * up,)


def _shift_down(x, s, row):
    return jnp.where(row >= s, pltpu.roll(x, s, 0), 0.0)


def _shift_up(x, s, row):
    n = x.shape[0]
    return jnp.where(row < n - s, pltpu.roll(x, n - s, 0), 0.0)


def _seq_call(kern, name, bl, seq, width, ct, ins, outs, acc_outs=()):
    def spec(off, rows):
        if rows is None:
            return pl.BlockSpec((seq, ct), functools.partial(lambda j, b, o: (b, o + j), o=off // ct))
        return pl.BlockSpec((rows, ct), lambda j, b: (0, j))

    in_specs = [spec(off, rows) for _, off, rows in ins]
    out_specs = [spec(0, None) for _ in outs] + [spec(0, rows) for _, rows in acc_outs]
    out_shape = [jax.ShapeDtypeStruct((bl * seq, width), d) for d in outs] + [jax.ShapeDtypeStruct((rows, width), F32) for _, rows in acc_outs]
    res = pl.pallas_call(
        kern, name=name, grid=(width // ct, bl), in_specs=in_specs, out_specs=out_specs, out_shape=out_shape,
        compiler_params=_cparams(("parallel", "arbitrary")),
    )(*[a for a, _, _ in ins])
    return tuple(res)


def _acc(ref, val):
    first = pl.program_id(1) == 0

    @pl.when(first)
    def _():
        ref[...] = val

    @pl.when(jnp.logical_not(first))
    def _():
        ref[...] += val


def _lerp_fwd(p, off, mu, bl, seq, width, ct):
    def kern(p_ref, mu_ref, o_ref):
        x = p_ref[...]
        row = lax.broadcasted_iota(jnp.int32, x.shape, 0)
        o_ref[...] = x + (_shift_down(x, 1, row) - x) * mu_ref[...]

    return _seq_call(kern, "lerp_fwd", bl, seq, width, ct, [(p, off, None), (mu, 0, 1)], [F32])[0]


def _lerp_bwd(p, off, mu, dps, bl, seq, width, ct, out_dtype):
    def kern(p_ref, mu_ref, d_ref, dp_ref, dmu_ref):
        x = p_ref[...]
        d = d_ref[...].astype(F32)
        m = mu_ref[...]
        row = lax.broadcasted_iota(jnp.int32, x.shape, 0)
        dp_ref[...] = (d * (1.0 - m) + _shift_up(d * m, 1, row)).astype(dp_ref.dtype)
        _acc(dmu_ref, jnp.sum(d * (_shift_down(x, 1, row) - x), axis=0, keepdims=True))

    return _seq_call(kern, "lerp_bwd", bl, seq, width, ct, [(p, off, None), (mu, 0, 1), (dps, 0, None)], [out_dtype], [(None, 1)])


def _conv_fwd(p, off, cw, cb, bl, seq, width, ct):
    nw = cw.shape[0]

    def kern(x_ref, w_ref, b_ref, o_ref):
        x = x_ref[...]
        row = lax.broadcasted_iota(jnp.int32, x.shape, 0)
        acc = b_ref[...] + x * w_ref[pl.ds(nw - 1, 1), :]
        for s in range(1, nw):
            acc = acc + _shift_down(x, s, row) * w_ref[pl.ds(nw - 1 - s, 1), :]
        o_ref[...] = acc

    return _seq_call(kern, "conv_fwd", bl, seq, width, ct, [(p, off, None), (cw, 0, nw), (cb, 0, 1)], [F32])[0]


def _conv_bwd(p, off, cw, dxc, bl, seq, width, ct, out_dtype):
    nw = cw.shape[0]

    def kern(x_ref, w_ref, d_ref, dx_ref, dw_ref, db_ref):
        x = x_ref[...]
        d = d_ref[...]
        row = lax.broadcasted_iota(jnp.int32, x.shape, 0)
        wrow = lax.broadcasted_iota(jnp.int32, dw_ref.shape, 0)
        dx = d * w_ref[pl.ds(nw - 1, 1), :]
        dw = jnp.where(wrow == nw - 1, jnp.sum(d * x, axis=0, keepdims=True), 0.0)
        for s in range(1, nw):
            dx = dx + _shift_up(d, s, row) * w_ref[pl.ds(nw - 1 - s, 1), :]
            dw = jnp.where(wrow == nw - 1 - s, jnp.sum(d * _shift_down(x, s, row), axis=0, keepdims=True), dw)
        dx_ref[...] = dx.astype(dx_ref.dtype)
        _acc(dw_ref, dw)
        _acc(db_ref, jnp.sum(d, axis=0, keepdims=True))

    return _seq_call(kern, "conv_bwd", bl, seq, width, ct, [(p, off, None), (cw, 0, nw), (dxc, 0, None)], [out_dtype], [(None, nw), (None, 1)])


def _lru_scan_fwd(a, bx, bl, seq, width, ct):
    def kern(a_ref, b_ref, h_ref):
        av = a_ref[...]
        bv = b_ref[...]
        row = lax.broadcasted_iota(jnp.int32, av.shape, 0)
        d = 1
        while d < seq:
            a_sh = jnp.where(row >= d, pltpu.roll(av, d, 0), 1.0)
            b_sh = jnp.where(row >= d, pltpu.roll(bv, d, 0), 0.0)
            bv = av * b_sh + bv
            av = av * a_sh
            d *= 2
        h_ref[...] = bv

    return _seq_call(kern, "lru_scan_fwd", bl, seq, width, ct, [(a, 0, None), (bx, 0, None)], [F32])[0]


def _lru_scan_bwd(a, h, dh, bl, seq, width, ct):
    def kern(a_ref, h_ref, d_ref, da_ref, db_ref):
        row = lax.broadcasted_iota(jnp.int32, a_ref.shape, 0)
        al = _shift_up(a_ref[...], 1, row)
        g = d_ref[...]
        d = 1
        while d < seq:
            keep = row < seq - d
            al_sh = jnp.where(keep, pltpu.roll(al, seq - d, 0), 1.0)
            g_sh = jnp.where(keep, pltpu.roll(g, seq - d, 0), 0.0)
            g = al * g_sh + g
            al = al * al_sh
            d *= 2
        db_ref[...] = g
        da_ref[...] = g * _shift_down(h_ref[...], 1, row)

    return _seq_call(kern, "lru_scan_bwd", bl, seq, width, ct, [(a, 0, None), (h, 0, None), (dh, 0, None)], [F32, F32])


_FORMS = {"nn": ((1,), (0,)), "nt": ((1,), (1,)), "tn": ((0,), (0,))}
_FORM_GRADS = {"nn": (("nt", "g", "b"), ("tn", "a", "g")),
               "nt": (("nn", "g", "b"), ("tn", "g", "a")),
               "tn": (("nt", "b", "g"), ("nn", "a", "g"))}


def _split_bf16(x):
    hi = x.astype(BF16)
    return hi, (x - hi.astype(F32)).astype(BF16)


def _pdot_raw(a, b, form, passes):
    dims = _FORMS[form]
    if passes == 1:
        return _dot(a.astype(BF16), b.astype(BF16), dims)
    ah, al = _split_bf16(a)
    bh, bl = _split_bf16(b)
    return _dot(ah, bh, dims) + (_dot(ah, bl, dims) + _dot(al, bh, dims))


@functools.partial(jax.custom_vjp, nondiff_argnums=(2, 3))
def _pdot(a, b, form, passes):
    return _pdot_raw(a, b, form, passes)


def _pdot_fwd(a, b, form, passes):
    return _pdot_raw(a, b, form, passes), (a, b)


def _pdot_bwd(form, passes, res, g):
    vals = {"a": res[0], "b": res[1], "g": g}
    (fa, xa, ya), (fb, xb, yb) = _FORM_GRADS[form]
    return _pdot_raw(vals[xa], vals[ya], fa, passes), _pdot_raw(vals[xb], vals[yb], fb, passes)


_pdot.defvjp(_pdot_fwd, _pdot_bwd)


def _neumann_raw(a_list, n_levels, passes):
    eye = (lax.broadcasted_iota(jnp.int32, a_list[0].shape, 0) == lax.broadcasted_iota(jnp.int32, a_list[0].shape, 1)).astype(F32)
    pw = list(a_list)
    x = [eye + a for a in a_list]
    for _ in range(n_levels):
        pw = [_pdot_raw(p, p, "nn", passes) for p in pw]
        x = [xi + _pdot_raw(xi, p, "nn", passes) for xi, p in zip(x, pw)]
    return x


@functools.partial(jax.custom_vjp, nondiff_argnums=(1, 2))
def _neumann_inverse(a_list, n_levels, passes):
    return _neumann_raw(a_list, n_levels, passes)


def _neumann_fwd(a_list, n_levels, passes):
    x = _neumann_raw(a_list, n_levels, passes)
    return x, x


def _neumann_bwd(n_levels, passes, x, ct):
    return ([_pdot_raw(xi, _pdot_raw(c, xi, "nt", passes), "tn", passes) for xi, c in zip(x, ct)],)


_neumann_inverse.defvjp(_neumann_fwd, _neumann_bwd)


def _scan_chunk2(S0, r, lw, k, v, a, b, p_main=1, p_inv=3):
    y, s = _scan_block([S0], [[(r, lw, k, v, a, b)]], p_main, p_inv)
    return y[0][0], s[0]


def _scan_block(states, units, p_main=1, p_inv=1):
    C = units[0][0][0].shape[0]
    C2 = 2 * C
    ri = lax.broadcasted_iota(jnp.int32, (C, C), 0)
    ci = lax.broadcasted_iota(jnp.int32, (C, C), 1)
    tri = (ri >= ci).astype(F32)
    i2 = lax.broadcasted_iota(jnp.int32, (C2, C2), 0)
    j2 = lax.broadcasted_iota(jnp.int32, (C2, C2), 1)
    same = (i2 // C) == (j2 // C)
    strict = jnp.logical_and(same, (i2 % C) > (j2 % C))
    incl = jnp.logical_and(same, (i2 % C) >= (j2 % C))
    eye = (i2 == j2).astype(F32)
    lane = lax.broadcasted_iota(jnp.int32, (1, LANE), 1)
    m0, m1 = (lane < HEAD).astype(F32), (lane >= HEAD).astype(F32)
    stack = lambda z: jnp.concatenate([z * m0, z * m1], axis=0)
    ids = [(i, g) for g in range(len(units[0])) for i in range(len(units))]

    pre = {}
    for i, g in ids:
        r, lw, k, v, a, b = units[i][g]
        cs = _nn(tri, lw, HI)
        p_incl = jnp.exp(cs)
        p_rec = jnp.exp(-cs)
        xr = jnp.concatenate([stack(a * jnp.exp(cs - lw)), stack(r * p_incl)], axis=0)
        bk = jnp.concatenate([stack(b * p_rec), stack(k * p_rec)], axis=0)
        pre[i, g] = (xr, bk, stack(v), jnp.exp(jnp.sum(lw, axis=0, keepdims=True)))
    gm = {u: _pdot(pre[u][0], pre[u][1], "nt", p_main) for u in ids}
    a_ak = {u: jnp.where(strict, gm[u][:C2, C2:], 0.0) for u in ids}
    r_bk = {u: jnp.concatenate([jnp.where(incl, gm[u][C2:, :C2], 0.0), jnp.where(incl, gm[u][C2:, C2:], 0.0)], axis=1) for u in ids}
    a_ab = [jnp.where(strict, gm[u][:C2, :C2], 0.0) for u in ids]
    x = dict(zip(ids, _neumann_inverse(a_ab, int(math.log2(C)) - 1, p_inv)))
    akv = {u: _pdot(a_ak[u], pre[u][2], "nn", p_main) for u in ids}

    states = list(states)
    pairs = range(len(units))
    ys = [[None] * len(units[0]) for _ in units]
    for g in range(len(units[0])):
        xs = [_pdot(pre[i, g][0], states[i], "nt", p_main) for i in pairs]
        us = [_pdot(x[i, g], xs[i][:C2] + akv[i, g], "nn", p_inv) for i in pairs]
        uv = [jnp.concatenate([us[i], pre[i, g][2]], axis=0) for i in pairs]
        y2 = [xs[i][C2:] + _pdot(r_bk[i, g], uv[i], "nn", p_main) for i in pairs]
        for i in pairs:
            ys[i][g] = y2[i][:C] + y2[i][C:]
        states = [(states[i] + _pdot(uv[i], pre[i, g][1], "tn", p_main)) * pre[i, g][3] for i in pairs]
    return ys, states


def _scan_dims(seq, rw):
    G = _pick(seq // SCAN_CHUNK, (SCAN_GROUP, 2, 1))
    NP = _pick(rw // LANE, (SCAN_PAIRS, 4, 2, 1))
    C = SCAN_CHUNK * G
    return SCAN_CHUNK, G, NP, C, seq // C, rw // (NP * LANE)


def _rwkv_scan_fwd(r, lw, k2, v, na, bb, p, bl, seq, rw, carry=None):
    cs, G, NP, C, nc, nhg = _scan_dims(seq, rw)
    nt = len(carry.tensors) if carry else 0

    def kern(*refs):
        in_refs = refs[:6]
        y_ref, st_ref = refs[6 + 2 * nt:8 + 2 * nt]
        s_scr = refs[8 + 3 * nt]
        if carry:
            step = (pl.program_id(0) * nhg + pl.program_id(1)) * nc + pl.program_id(2)
            carry.hook(step, bl * nhg * nc - 1, refs[6:6 + nt], refs[8 + 2 * nt:8 + 3 * nt], refs[9 + 3 * nt:])

        @pl.when(pl.program_id(2) == 0)
        def _():
            s_scr[...] = jnp.zeros_like(s_scr)

        st_ref[...] = s_scr[...]
        units = [[tuple(ref[pl.ds(g * cs, cs), pl.ds(i * LANE, LANE)] for ref in in_refs) for g in range(G)] for i in range(NP)]
        ys, s_new = _scan_block([s_scr[i] for i in range(NP)], units)
        for i in range(NP):
            s_scr[i] = s_new[i]
            for g in range(G):
                y_ref[pl.ds(g * cs, cs), pl.ds(i * LANE, LANE)] = ys[i][g]

    def tok(off):
        return pl.BlockSpec((C, NP * LANE), functools.partial(lambda b, h, c, o: (b * nc + c, o + h), o=off // (NP * LANE)))

    in_specs = [tok(0), tok(0), tok(0), tok(2 * rw), tok(0), tok(0)]
    out_specs = [tok(0), pl.BlockSpec((NP, LANE, LANE), lambda b, h, c: ((b * nhg + h) * nc + c, 0, 0))]
    out_shape = [jax.ShapeDtypeStruct((bl * seq, rw), F32), jax.ShapeDtypeStruct((bl * nhg * nc * NP, LANE, LANE), F32)]
    scratch = [pltpu.VMEM((NP, LANE, LANE), F32)]
    if not carry:
        y, st = pl.pallas_call(
            kern, name="rwkv_scan_fwd", grid=(bl, nhg, nc), in_specs=in_specs, out_specs=out_specs, out_shape=out_shape,
            scratch_shapes=scratch, compiler_params=_cparams(("parallel", "parallel", "arbitrary")),
        )(p, lw, k2, p, na, bb)
        return y, st
    res = pl.pallas_call(
        kern, name="rwkv_scan_fwd", grid=(bl, nhg, nc), in_specs=in_specs + carry.in_specs, out_specs=out_specs + carry.out_specs,
        out_shape=out_shape + carry.out_shape, scratch_shapes=scratch + carry.scratch, input_output_aliases=carry.aliases(6, 2),
        compiler_params=_cparams(("arbitrary", "arbitrary", "arbitrary")),
    )(p, lw, k2, p, na, bb, *carry.args)
    return res[0], res[1], list(res[2:])


def _rwkv_scan_bwd(lw, k2, na, bb, p, st, dy, bl, seq, rw, carry=None):
    cs, G, NP, C, nc, nhg = _scan_dims(seq, rw)
    nt = len(carry.tensors) if carry else 0

    def kern(*refs):
        in_refs = refs[:6]
        st_ref, dy_ref = refs[6:8]
        out_refs = refs[8 + 2 * nt:14 + 2 * nt]
        ds_scr = refs[14 + 3 * nt]
        if carry:
            step = (pl.program_id(0) * nhg + pl.program_id(1)) * nc + pl.program_id(2)
            carry.hook(step, bl * nhg * nc - 1, refs[8:8 + nt], refs[14 + 2 * nt:14 + 3 * nt], refs[15 + 3 * nt:])

        @pl.when(pl.program_id(2) == 0)
        def _():
            ds_scr[...] = jnp.zeros_like(ds_scr)

        win = lambda ref, i, g: ref[pl.ds(g * cs, cs), pl.ds(i * LANE, LANE)]
        units = [[tuple(win(ref, i, g) for ref in in_refs) for g in range(G)] for i in range(NP)]
        _, vjp = jax.vjp(_scan_block, [st_ref[i] for i in range(NP)], units)
        dys = [[win(dy_ref, i, g) for g in range(G)] for i in range(NP)]
        ds, dunits = vjp((dys, [ds_scr[i] for i in range(NP)]))
        for i in range(NP):
            ds_scr[i] = ds[i]
            for g in range(G):
                for ref, d in zip(out_refs, dunits[i][g]):
                    ref[pl.ds(g * cs, cs), pl.ds(i * LANE, LANE)] = d

    def tok(off):
        return pl.BlockSpec((C, NP * LANE), functools.partial(lambda b, h, c, o: (b * nc + (nc - 1 - c), o + h), o=off // (NP * LANE)))

    st_spec = pl.BlockSpec((NP, LANE, LANE), lambda b, h, c: ((b * nhg + h) * nc + (nc - 1 - c), 0, 0))
    in_specs = [tok(0), tok(0), tok(0), tok(2 * rw), tok(0), tok(0), st_spec, tok(0)]
    out_shape = [jax.ShapeDtypeStruct((bl * seq, rw), F32)] * 6
    scratch = [pltpu.VMEM((NP, LANE, LANE), F32)]
    if not carry:
        return pl.pallas_call(
            kern, name="rwkv_scan_bwd", grid=(bl, nhg, nc), in_specs=in_specs, out_specs=[tok(0)] * 6, out_shape=out_shape,
            scratch_shapes=scratch, compiler_params=_cparams(("parallel", "parallel", "arbitrary")),
        )(p, lw, k2, p, na, bb, st, dy)
    res = pl.pallas_call(
        kern, name="rwkv_scan_bwd", grid=(bl, nhg, nc), in_specs=in_specs + carry.in_specs, out_specs=[tok(0)] * 6 + carry.out_specs,
        out_shape=out_shape + carry.out_shape, scratch_shapes=scratch + carry.scratch, input_output_aliases=carry.aliases(8, 6),
        compiler_params=_cparams(("arbitrary", "arbitrary", "arbitrary")),
    )(p, lw, k2, p, na, bb, st, dy, *carry.args)
    return res[:6], list(res[6:])


def _loss_head(h2, g_final, target, tile):
    n, d = h2.shape
    nt = n // tile

    def kern(h_ref, g_ref, t_ref, dh_ref, dg_ref, l_ref):
        def f(h, g):
            y = h * lax.rsqrt(jnp.mean(h * h, axis=-1, keepdims=True) + NORM_EPS) * g
            e = y - t_ref[...]
            return 0.5 * jnp.sum(jnp.mean(e * e, axis=-1, keepdims=True))

        loss, (dh, dg) = jax.value_and_grad(f, argnums=(0, 1))(h_ref[...], g_ref[...])
        dh_ref[...] = dh
        first = pl.program_id(0) == 0

        @pl.when(first)
        def _():
            dg_ref[...] = dg
            l_ref[...] = jnp.zeros_like(l_ref) + loss

        @pl.when(jnp.logical_not(first))
        def _():
            dg_ref[...] += dg
            l_ref[...] += loss

    row = pl.BlockSpec((tile, d), lambda i: (i, 0))
    vec = pl.BlockSpec((1, d), lambda i: (0, 0))
    return pl.pallas_call(
        kern, name="loss_head", grid=(nt,), in_specs=[row, vec, row],
        out_specs=[row, vec, pl.BlockSpec((1, LANE), lambda i: (0, 0))],
        out_shape=[jax.ShapeDtypeStruct((n, d), F32), jax.ShapeDtypeStruct((1, d), F32), jax.ShapeDtypeStruct((1, LANE), F32)],
        compiler_params=_cparams(("arbitrary",)),
    )(h2, g_final, target)


def _adamw(parts, w, m, v, name, carry=None):
    n_parts, R, Cc = parts.shape
    tr = _pick(R, tuple(t for t in (1024, 512, 256, 128, 64, 32, 16) if t * Cc <= 128 * 1024) + (8,))
    c1, c2 = 1.0 - ADAM_B1, 1.0 - ADAM_B2
    bc1, bc2 = 1.0 - ADAM_B1 ** ADAM_STEP, 1.0 - ADAM_B2 ** ADAM_STEP

    nt = len(carry.tensors) if carry else 0

    def kern(*refs):
        p_ref, w_ref, m_ref, v_ref = refs[:4]
        g_ref, d_ref, nm_ref, nv_ref = refs[4 + 2 * nt:8 + 2 * nt]
        if carry:
            carry.hook(pl.program_id(0), R // tr - 1, refs[4:4 + nt], refs[8 + 2 * nt:8 + 3 * nt], refs[8 + 3 * nt:])
        g = p_ref[0].astype(F32)
        for s in range(1, n_parts):
            g = g + p_ref[s].astype(F32)
        m2 = ADAM_B1 * m_ref[...] + c1 * g
        v2 = ADAM_B2 * v_ref[...] + c2 * (g * g)
        g_ref[...] = g
        nm_ref[...] = m2
        nv_ref[...] = v2
        d_ref[...] = -ADAM_LR * ((m2 / bc1) / (jnp.sqrt(v2 / bc2) + ADAM_EPS) + ADAM_WD * w_ref[...])

    blk = pl.BlockSpec((tr, Cc), lambda i: (i, 0))
    in_specs = [pl.BlockSpec((n_parts, tr, Cc), lambda i: (0, i, 0)), blk, blk, blk]
    out_shape = [jax.ShapeDtypeStruct((R, Cc), F32)] * 4
    if not carry:
        return pl.pallas_call(
            kern, name=name, grid=(R // tr,), in_specs=in_specs, out_specs=[blk] * 4, out_shape=out_shape,
            compiler_params=_cparams(("parallel",)),
        )(parts, w, m, v)
    res = pl.pallas_call(
        kern, name=name, grid=(R // tr,), in_specs=in_specs + carry.in_specs, out_specs=[blk] * 4 + carry.out_specs,
        out_shape=out_shape + carry.out_shape, scratch_shapes=carry.scratch, input_output_aliases=carry.aliases(4, 4),
        compiler_params=_cparams(("arbitrary",)),
    )(parts, w, m, v, *carry.args)
    return res[:4], list(res[4:])


def _exchange_now(carry, name):
    nt = len(carry.tensors)

    def body(*refs):
        carry.hook(0, 0, refs[:nt], refs[2 * nt:3 * nt], refs[3 * nt:])

    return list(pl.pallas_call(
        body, name=name, in_specs=carry.in_specs, out_specs=carry.out_specs, out_shape=carry.out_shape,
        scratch_shapes=carry.scratch, input_output_aliases=carry.aliases(0, 0),
    )(*carry.args))


def _carried(queue, capacity_us, *args, fn, **kw):
    carry = queue.take(capacity_us)
    if carry is None:
        return fn(*args, **kw)
    res = fn(*args, carry=carry, **kw)
    queue.done(carry, res[-1])
    return res[0] if len(res) == 2 else res[:-1]


class _Queue:
    def __init__(self, gather, label):
        self.gather, self.label = gather, label
        self.tensors, self.fifo, self.n_flush = {}, [], 0

    def push(self, name, src, n_pieces, cost_us):
        n_rows = src.shape[0] if self.gather else src.shape[1]
        rows = n_rows // n_pieces
        assert rows * n_pieces == n_rows and rows % 16 == 0, (name, src.shape)
        self.tensors[name] = [src, lax.empty(((N_DEV,) + src.shape) if self.gather else src.shape, src.dtype)]
        self.fifo += [(name, p * rows, rows, cost_us / n_pieces) for p in range(n_pieces)]

    def take(self, capacity_us):
        picked = []
        while self.fifo and capacity_us >= 0.6 * self.fifo[0][3]:
            picked.append(self.fifo.pop(0))
            capacity_us -= picked[-1][3]
        if not picked:
            return None
        names = list(dict.fromkeys(n for n, _, _, _ in picked))
        cls = _GatherCarry if self.gather else _Carry
        carry = cls([tuple(self.tensors[n]) for n in names], [(names.index(n), r0, rows) for n, r0, rows, _ in picked])
        carry.names = names
        return carry

    def done(self, carry, dsts):
        for n, d in zip(carry.names, dsts):
            self.tensors[n][1] = d

    def flush(self):
        carry = self.take(float("inf"))
        if carry:
            self.done(carry, _exchange_now(carry, "%s_now_%d" % (self.label, self.n_flush)))
            self.n_flush += 1

    def result(self, name):
        assert not any(n == name for n, _, _, _ in self.fifo)
        return self.tensors[name][1]


def _cols_from_shards(g):
    return jnp.transpose(g, (1, 0, 2)).reshape(g.shape[1], N_DEV * g.shape[2])


def _shards_from_cols(w):
    r, n = w.shape
    return jnp.transpose(w.reshape(r, N_DEV, n // N_DEV), (1, 0, 2))


def _pad_cols(w, to):
    return jnp.pad(w, ((0, 0), (0, to - w.shape[1])))


def _pack(arrs):
    flat = jnp.concatenate([a.reshape(-1) for a in arrs])
    n = _rup(flat.shape[0], 256 * LANE)
    return jnp.pad(flat, (0, n - flat.shape[0])).reshape(n // LANE, LANE)


def _unpack(mat, shapes):
    flat = mat.reshape(-1)
    out, o = [], 0
    for s in shapes:
        n = math.prod(s)
        out.append(flat[o:o + n].reshape(s))
        o += n
    return out


_SMALL = ["norm_mix_g", "mu_shift", "rwkv_w0", "rwkv_a0", "rwkv_k_k", "rwkv_k_a", "rwkv_r_k", "rwkv_ln_g", "rwkv_ln_b", "conv_b",
          "lru_wr", "lru_br", "lru_wi", "lru_bi", "lru_lambda", "lru_norm_g", "norm_ffn_g", "norm_final_g"]
_SMALL_SHARDED = ["rwkv_w2", "rwkv_a2", "rwkv_g2", "conv_w"]
_BIG = ["w_in", "w_out", "ffn_w_gate", "ffn_w_up", "ffn_w_down"]
_WEIGHTS = ['norm_mix_g', 'w_in', 'mu_shift', 'rwkv_w0', 'rwkv_w2', 'rwkv_a0', 'rwkv_a2', 'rwkv_g2', 'rwkv_k_k', 'rwkv_k_a', 'rwkv_r_k',
            'rwkv_ln_g', 'rwkv_ln_b', 'conv_w', 'conv_b', 'lru_wr', 'lru_br', 'lru_wi', 'lru_bi', 'lru_lambda', 'lru_norm_g', 'w_out',
            'norm_ffn_g', 'ffn_w_gate', 'ffn_w_up', 'ffn_w_down', 'norm_final_g']


def _step(W, M, V, x, loss_target):
    bl, seq, d = x.shape
    n = bl * seq
    rw = W["rwkv_w0"].shape[1]
    nh = W["rwkv_r_k"].shape[1]
    assert W["rwkv_r_k"].shape[2] == HEAD and nh * HEAD == rw and rw % LANE == 0
    dl, al, gl = W["rwkv_w2"].shape[1], W["rwkv_a2"].shape[1], W["rwkv_g2"].shape[1]
    dlp, alp, glp = _rup(dl, LANE), _rup(al, LANE), _rup(gl, LANE)
    lorap = dlp + alp + glp
    lw_ = W["conv_b"].shape[1]
    nblk, lbw = W["lru_wr"].shape[1], W["lru_wr"].shape[2]
    assert lbw == LANE and nblk * lbw == lw_
    o_xb, o_gate, o_rw = 0, lw_, 2 * lw_
    o_lora = 3 * rw
    rwp = o_lora + lorap
    inp = o_rw + rwp
    nsh_ff = W["ffn_w_gate"].shape[2]
    dff = N_DEV * nsh_ff
    dffp = _rup(dff, 1024) if dff >= 1024 else _rup(dff, LANE)
    x2 = x.reshape(n, d)
    tgt2 = loss_target.reshape(n, d)

    gq = _Queue(True, "gather")
    gq.push("w_in", W["w_in"][0].astype(BF16), 1, 490)
    gq.push("small", _pack([W[k][0] for k in _SMALL_SHARDED]), 1, 10)
    gq.flush()
    gq.push("w_out", W["w_out"][0].astype(BF16), 1, 180)
    gq.push("ffn_w_gate", W["ffn_w_gate"][0].astype(BF16), 4, 490)
    gq.push("ffn_w_up", W["ffn_w_up"][0].astype(BF16), 4, 490)
    gq.push("ffn_w_down", W["ffn_w_down"][0].astype(BF16), 2, 490)
    gmm = functools.partial(_carried, gq, fn=_mm)
    gstage = functools.partial(_carried, gq, fn=_stage_fwd)

    g_small = gq.result("small")
    w_in_l = _cols_from_shards(gq.result("w_in"))
    o1 = 3 * rw
    w_in = jnp.concatenate([w_in_l[:, o1 + dl + al + gl:], w_in_l[:, :o1], _pad_cols(w_in_l[:, o1:o1 + dl], dlp),
                            _pad_cols(w_in_l[:, o1 + dl:o1 + dl + al], alp), _pad_cols(w_in_l[:, o1 + dl + al:o1 + dl + al + gl], glp)], axis=1)
    sm_shapes = [W[k][0].shape for k in _SMALL_SHARDED]
    sm = [_unpack(g_small[s], sm_shapes) for s in range(N_DEV)]
    w2, a2, g2, conv_w = [jnp.concatenate([sm[s][i] for s in range(N_DEV)], axis=1) for i in range(4)]
    w_lora = jnp.zeros((lorap, 3 * rw), F32)
    w_lora = w_lora.at[:dl, :rw].set(w2).at[dlp:dlp + al, rw:2 * rw].set(a2).at[dlp + alp:dlp + alp + gl, 2 * rw:].set(g2)
    w_lora = w_lora.astype(BF16)
    mu_l = W["mu_shift"]
    mu = jnp.concatenate([mu_l[:, :o1], _pad_cols(mu_l[:, o1:o1 + dl], dlp), _pad_cols(mu_l[:, o1 + dl:o1 + dl + al], alp),
                          _pad_cols(mu_l[:, o1 + dl + al:], glp)], axis=1)
    r_k = W["rwkv_r_k"].reshape(1, rw)

    tile = _pick(n, (256, 128, 64))
    tile_s = _pick(n, (128, 64))
    ct_seq = _pick(math.gcd(rwp, lw_), (256, 128))
    assert o_rw % ct_seq == 0 and o_gate % lw_ == 0
    ct_h = _pick(rw, (512, 256, 128))
    gi = lax.broadcasted_iota(jnp.int32, (ct_h, ct_h), 0) // HEAD
    gj = lax.broadcasted_iota(jnp.int32, (ct_h, ct_h), 1) // HEAD
    gsum = ((gi == gj).astype(BF16), (ct_h, ct_h), lambda j: (0, 0))
    full = lambda a: (a, a.shape, lambda j: (0,) * a.ndim)
    rowp = lambda a, ct: (a,) + _row(ct)

    u1, = _stage_fwd(_f_rmsnorm, "norm_mix_fwd", n, d, tile, d, [(x2, 0)], [full(W["norm_mix_g"])], [], [BF16])
    p = gmm(360, u1, w_in, name="mm_in")
    ps = _lerp_fwd(p, o_rw, mu, bl, seq, rwp, ct_seq)
    f_lora = functools.partial(_f_lora_act, widths=(dlp, alp))
    lact, = _stage_fwd(f_lora, "lora_act_fwd", n, lorap, tile, lorap, [(ps, o_lora)], [], [], [BF16])
    wag = _mm(lact, w_lora, name="mm_lora")
    pre_par = [rowp(W["rwkv_w0"], ct_h), rowp(W["rwkv_a0"], ct_h), rowp(W["rwkv_k_k"], ct_h), rowp(W["rwkv_k_a"], ct_h)]
    pre_acts = [(ps, rw), (wag, 0), (wag, rw)]
    lw, k2, na, bb = gstage(120, _f_rwkv_pre, "rwkv_pre_fwd", n, rw, tile_s, ct_h, pre_acts, pre_par, [gsum], [F32] * 4)
    ysc, st = _carried(gq, 230, None, lw, k2, None, na, bb, ps, bl, seq, rw, fn=_rwkv_scan_fwd)
    post_par = [rowp(W["rwkv_ln_g"], ct_h), rowp(W["rwkv_ln_b"], ct_h), rowp(r_k, ct_h)]
    post_acts = [(ysc, 0), (ps, 0), (k2, 0), (ps, 2 * rw), (wag, 2 * rw)]
    ya, = gstage(120, _f_rwkv_post, "rwkv_post_fwd", n, rw, tile_s, ct_h, post_acts, post_par, [gsum], [BF16])

    xc = _conv_fwd(p, o_xb, conv_w, W["conv_b"], bl, seq, lw_, ct_seq)
    f_gates = functools.partial(_f_lru_gates, seq=seq)
    blk3 = lambda a: (a[0], (1, LANE, LANE), lambda j: (j, 0, 0))
    gate_par = [blk3(W["lru_wr"]), rowp(W["lru_br"], LANE), blk3(W["lru_wi"]), rowp(W["lru_bi"], LANE), rowp(W["lru_lambda"], LANE)]
    a_l, bx = gstage(160, f_gates, "lru_gates_fwd", n, lw_, tile, LANE, [(xc, 0)], gate_par, [], [F32, F32])
    ct_l = _pick(lw_, (256, 128))
    h_l = _lru_scan_fwd(a_l, bx, bl, seq, lw_, ct_l)
    lpost_par = [full(W["lru_norm_g"])]
    yb, = _stage_fwd(_f_lru_post, "lru_post_fwd", n, lw_, tile_s, lw_, [(h_l, 0), (p, o_gate)], lpost_par, [], [BF16])

    ycat = jnp.concatenate([ya, yb], axis=1)
    g_out = gq.result("w_out")
    w_out = g_out.reshape(N_DEV * g_out.shape[1], d)
    h1 = gmm(130, ycat, w_out, name="mm_out", add=x2)
    u2, = _stage_fwd(_f_rmsnorm, "norm_ffn_fwd", n, d, tile, d, [(h1, 0)], [full(W["norm_ffn_g"])], [], [BF16])
    w_gate = _pad_cols(_cols_from_shards(gq.result("ffn_w_gate")), dffp)
    ff_gate = gmm(340, u2, w_gate, name="mm_gate", out_dtype=BF16)
    w_up = _pad_cols(_cols_from_shards(gq.result("ffn_w_up")), dffp)
    ff_up = gmm(340, u2, w_up, name="mm_up", out_dtype=BF16)
    ct_f = _pick(dffp, (1024, 512, 256, 128))
    ff_acts = [(ff_gate, 0), (ff_up, 0)]
    act, = _stage_fwd(_f_swiglu, "swiglu_fwd", n, dffp, tile, ct_f, ff_acts, [], [], [BF16])
    gq.flush()
    w_down = jnp.pad(gq.result("ffn_w_down").reshape(dff, d), ((0, dffp - dff), (0, 0)))
    h2 = _mm(act, w_down, name="mm_down", add=h1)

    dh2, dg_final, lsum = _loss_head(h2, W["norm_final_g"].reshape(1, d), tgt2, tile_s)
    loss = lax.psum(lsum[0, 0], ("x", "y", "c"))
    queue = _Queue(False, "exchange")

    cmm = functools.partial(_carried, queue, fn=_mm)
    cstage = functools.partial(_carried, queue, fn=_stage_bwd)
    dh2b = dh2.astype(BF16)
    dact = _mm(dh2b, w_down, name="mm_dact", tb=True, out_dtype=BF16)
    dw_down = _mm(act, dh2b, name="mm_dw_down", ta=True, out_dtype=BF16)
    queue.push("ffn_w_down", dw_down[:dff].reshape(N_DEV, nsh_ff, d), 2, 1000)
    (dgate, dup), _ = _stage_bwd(_f_swiglu, "swiglu_bwd", n, dffp, tile, ct_f, ff_acts, [], [], [(dact, 0)], [BF16, BF16])
    du2 = cmm(400, dgate, w_gate, name="mm_du2_gate", tb=True)
    dw_gate = cmm(350, u2, dgate, name="mm_dw_gate", ta=True, out_dtype=BF16)
    queue.push("ffn_w_gate", _shards_from_cols(dw_gate[:, :dff]), 8, 1000)
    du2 = cmm(400, dup, w_up, name="mm_du2_up", tb=True, add=du2)
    dw_up = cmm(350, u2, dup, name="mm_dw_up", ta=True, out_dtype=BF16)
    queue.push("ffn_w_up", _shards_from_cols(dw_up[:, :dff]), 8, 1000)
    (dh1,), (dg_ffn,) = cstage(130, _f_rmsnorm, "norm_ffn_bwd", n, d, tile_s, d, [(h1, 0)], [full(W["norm_ffn_g"])], [], [(du2, 0)], [F32],
                               extra_add=(dh2, 0))
    dh1b = dh1.astype(BF16)
    dycat = cmm(135, dh1b, w_out, name="mm_dycat", tb=True)
    dw_out = cmm(170, ycat, dh1b, name="mm_dw_out", ta=True, out_dtype=BF16)
    queue.push("w_out", dw_out.reshape(N_DEV, -1, d), 2, 370)

    (dysc, dr_p, dk2_p, dv_p, dg_g), (dln_g, dln_b, dr_k) = cstage(
        195, _f_rwkv_post, "rwkv_post_bwd", n, rw, tile_s, ct_h, post_acts, post_par, [gsum], [(dycat, 0)], [F32] * 5)
    dr_s, dlw, dk2_s, dv_s, dna, dbb = _carried(queue, 650, lw, k2, na, bb, ps, st, dysc, bl, seq, rw, fn=_rwkv_scan_bwd)
    dk2 = dk2_p + dk2_s
    (dk, dwlin, dalin), (dw0, da0, dk_k, dk_a) = cstage(
        180, _f_rwkv_pre, "rwkv_pre_bwd", n, rw, tile_s, ct_h, pre_acts, pre_par, [gsum], [(dlw, 0), (dk2, 0), (dna, 0), (dbb, 0)], [F32] * 3)
    dwag = jnp.concatenate([dwlin, dalin, dg_g], axis=1).astype(BF16)
    dlact = cmm(75, dwag, w_lora, name="mm_dlact", tb=True)
    dw_lora = cmm(50, lact, dwag, name="mm_dw_lora", ta=True)
    (dps_lora,), _ = _stage_bwd(f_lora, "lora_act_bwd", n, lorap, tile, lorap, [(ps, o_lora)], [], [], [(dlact, 0)], [F32])
    dps = jnp.concatenate([dr_p + dr_s, dk, dv_p + dv_s, dps_lora], axis=1)
    dp_rwkv, dmu = _lerp_bwd(p, o_rw, mu, dps, bl, seq, rwp, ct_seq, BF16)

    (dh_l, dgate_l), (dlru_norm_g,) = _stage_bwd(_f_lru_post, "lru_post_bwd", n, lw_, tile_s, lw_, [(h_l, 0), (p, o_gate)], lpost_par, [],
                                                 [(dycat, rw)], [F32, BF16])
    da_l, dbx = _lru_scan_bwd(a_l, h_l, dh_l, bl, seq, lw_, ct_l)
    (dxc,), (dwr, dbr, dwi, dbi, dlam) = cstage(240, f_gates, "lru_gates_bwd", n, lw_, tile, LANE, [(xc, 0)], gate_par, [],
                                                [(da_l, 0), (dbx, 0)], [F32])
    dxb, dconv_w, dconv_b = _conv_bwd(p, o_xb, conv_w, dxc, bl, seq, lw_, ct_seq, BF16)
    sh_full = [dw_lora[:dl, :rw], dw_lora[dlp:dlp + al, rw:2 * rw], dw_lora[dlp + alp:dlp + alp + gl, 2 * rw:], dconv_w]
    assert rw == lw_
    rows_sh = sum(a.shape[0] for a in sh_full)
    pad_sh = _rup(rows_sh, 16) - rows_sh
    queue.push("small_sharded", jnp.pad(jnp.concatenate([_shards_from_cols(a) for a in sh_full], axis=1), ((0, 0), (0, pad_sh), (0, 0))), 1, 40)
    stack_sh = lambda D: jnp.pad(jnp.concatenate([D[k][0] for k in _SMALL_SHARDED], axis=0), ((0, pad_sh), (0, 0)))

    dp = jnp.concatenate([dxb, dgate_l, dp_rwkv], axis=1)
    dw_in = cmm(340, u1, dp, name="mm_dw_in", ta=True, out_dtype=BF16)
    ol = o_rw + o_lora
    dw_in_l = jnp.concatenate([dw_in[:, o_rw:ol], dw_in[:, ol:ol + dl], dw_in[:, ol + dlp:ol + dlp + al],
                               dw_in[:, ol + dlp + alp:ol + dlp + alp + gl], dw_in[:, :o_rw]], axis=1)
    queue.push("w_in", _shards_from_cols(dw_in_l), 8, 970)
    du1 = cmm(390, dp, w_in, name="mm_du1", tb=True)
    (grad_x,), (dg_mix,) = cstage(90, _f_rmsnorm, "norm_mix_bwd", n, d, tile_s, d, [(x2, 0)], [full(W["norm_mix_g"])], [], [(du1, 0)], [F32],
                                  extra_add=(dh1, 0))
    out = {}
    for k in ["ffn_w_down", "ffn_w_gate", "ffn_w_up", "w_out", "w_in"]:
        if k == "w_in":
            queue.flush()
        res = _carried(queue, 250, queue.result(k), W[k][0], M[k][0], V[k][0], "adamw_" + k, fn=_adamw)
        out[k] = [o[None] for o in res]

    dmu_l = jnp.concatenate([dmu[:, :o1], dmu[:, o_lora:o_lora + dl], dmu[:, o_lora + dlp:o_lora + dlp + al],
                             dmu[:, o_lora + dlp + alp:o_lora + dlp + alp + gl]], axis=1)
    small_g = {"norm_mix_g": dg_mix, "mu_shift": dmu_l, "rwkv_w0": dw0, "rwkv_a0": da0, "rwkv_k_k": dk_k, "rwkv_k_a": dk_a,
               "rwkv_r_k": dr_k.reshape(W["rwkv_r_k"].shape), "rwkv_ln_g": dln_g, "rwkv_ln_b": dln_b, "conv_b": dconv_b,
               "lru_wr": dwr[None], "lru_br": dbr, "lru_wi": dwi[None], "lru_bi": dbi, "lru_lambda": dlam, "lru_norm_g": dlru_norm_g,
               "norm_ffn_g": dg_ffn, "norm_final_g": dg_final.reshape(W["norm_final_g"].shape)}
    gq.push("small_grads", _pack([small_g[k] for k in _SMALL]), 1, 50)
    gq.flush()
    pk = lambda D: _pack([D[k] for k in _SMALL])
    res = _adamw(gq.result("small_grads"), pk(W), pk(M), pk(V), "adamw_small")
    shapes = [W[k].shape for k in _SMALL]
    for i, r in enumerate(res):
        for k, a in zip(_SMALL, _unpack(r, shapes)):
            out.setdefault(k, [None] * 4)[i] = a
    res = _adamw(queue.result("small_sharded"), stack_sh(W), stack_sh(M), stack_sh(V), "adamw_small_sharded")
    for i, r in enumerate(res):
        o = 0
        for k in _SMALL_SHARDED:
            rows = W[k].shape[1]
            out.setdefault(k, [None] * 4)[i] = r[o:o + rows][None]
            o += rows
    return loss, grad_x.reshape(x.shape), out


def kernel(x, norm_mix_g, w_in, mu_shift, rwkv_w0, rwkv_w2, rwkv_a0, rwkv_a2, rwkv_g2, rwkv_k_k, rwkv_k_a, rwkv_r_k, rwkv_ln_g, rwkv_ln_b, conv_w, conv_b, lru_wr, lru_br, lru_wi, lru_bi, lru_lambda, lru_norm_g, w_out, norm_ffn_g, ffn_w_gate, ffn_w_up, ffn_w_down, norm_final_g, loss_target, m_norm_mix_g, m_w_in, m_mu_shift, m_rwkv_w0, m_rwkv_w2, m_rwkv_a0, m_rwkv_a2, m_rwkv_g2, m_rwkv_k_k, m_rwkv_k_a, m_rwkv_r_k, m_rwkv_ln_g, m_rwkv_ln_b, m_conv_w, m_conv_b, m_lru_wr, m_lru_br, m_lru_wi, m_lru_bi, m_lru_lambda, m_lru_norm_g, m_w_out, m_norm_ffn_g, m_ffn_w_gate, m_ffn_w_up, m_ffn_w_down, m_norm_final_g, v_norm_mix_g, v_w_in, v_mu_shift, v_rwkv_w0, v_rwkv_w2, v_rwkv_a0, v_rwkv_a2, v_rwkv_g2, v_rwkv_k_k, v_rwkv_k_a, v_rwkv_r_k, v_rwkv_ln_g, v_rwkv_ln_b, v_conv_w, v_conv_b, v_lru_wr, v_lru_br, v_lru_wi, v_lru_bi, v_lru_lambda, v_lru_norm_g, v_w_out, v_norm_ffn_g, v_ffn_w_gate, v_ffn_w_up, v_ffn_w_down, v_norm_final_g):
    a = locals()
    W = {k: a[k] for k in _WEIGHTS}
    M = {k: a["m_" + k] for k in _WEIGHTS}
    V = {k: a["v_" + k] for k in _WEIGHTS}
    loss, grad_x, out = _step(W, M, V, x, loss_target)
    res = [loss, grad_x]
    for i in range(4):
        res += [out[k][i].reshape(W[k].shape) for k in _WEIGHTS]
    return tuple(res)
```

```python
import functools
import math

import jax
import jax.numpy as jnp
from jax import lax
from jax.experimental import pallas as pl
from jax.experimental.pallas import tpu as pltpu

F32 = jnp.float32
BF16 = jnp.bfloat16
HI = lax.Precision.HIGHEST
MESH = pl.DeviceIdType.MESH

N_DEV = 8
LANE = 128
HEAD = 64
MM_MAX_TK = 4096
SCAN_CHUNK = 64
SCAN_GROUP = 2
SCAN_PAIRS = 8
VMEM_LIMIT = 56 * 1024 * 1024

NORM_EPS = 1e-6
GN_EPS = 64e-5
LRU_C = 8.0
ADAM_LR, ADAM_B1, ADAM_B2, ADAM_EPS, ADAM_WD, ADAM_STEP = 0.001, 0.9, 0.999, 1e-08, 0.01, 10


def _pick(n, cands):
    for c in cands:
        if n % c == 0:
            return c
    return n


def _rup(n, m):
    return (n + m - 1) // m * m


def _cparams(dims):
    return pltpu.CompilerParams(dimension_semantics=dims, vmem_limit_bytes=VMEM_LIMIT)


def _sigmoid(x):
    return 1.0 / (1.0 + jnp.exp(-x))


def _softplus(z):
    return jnp.maximum(z, 0.0) + jnp.log(1.0 + jnp.exp(-jnp.abs(z)))


def _neg_expm1(x):
    series = -(x * (1.0 + 0.5 * x * (1.0 + (x / 3.0) * (1.0 + 0.25 * x))))
    return jnp.where(jnp.abs(x) < 0.03, series, 1.0 - jnp.exp(x))


def _gelu(x):
    return 0.5 * x * (1.0 + jnp.tanh(0.7978845608028654 * (x + 0.044715 * (x * x * x))))


def _dot(a, b, dims, precision=None):
    return lax.dot_general(a, b, (dims, ((), ())), precision=precision, preferred_element_type=F32)


def _nn(a, b, precision=None):
    return _dot(a, b, ((1,), (0,)), precision)


def _nt(a, b, precision=None):
    return _dot(a, b, ((1,), (1,)), precision)


def _tn(a, b, precision=None):
    return _dot(a, b, ((0,), (0,)), precision)


def _coords():
    return lax.axis_index("x"), lax.axis_index("y"), lax.axis_index("c")


class _Carry:
    def __init__(self, tensors, items):
        self.tensors, self.items = tensors, items
        nt, ni = len(tensors), len(items)
        any_spec = pl.BlockSpec(memory_space=pl.ANY)
        self.args = [t[0] for t in tensors] + [t[1] for t in tensors]
        self.in_specs = [any_spec] * (2 * nt)
        self.out_specs = [any_spec] * nt
        self.out_shape = [jax.ShapeDtypeStruct(t[1].shape, t[1].dtype) for t in tensors]
        self.scratch = [pltpu.SemaphoreType.DMA((ni, N_DEV - 1)), pltpu.SemaphoreType.DMA((ni, N_DEV - 1)), pltpu.SemaphoreType.DMA((ni,))]

    def aliases(self, first_in, first_out):
        nt = len(self.tensors)
        return {first_in + nt + t: first_out + t for t in range(nt)}

    def _slot(self, ref, t, idx, win):
        cw = self.tensors[t][2]
        return ref.at[idx, win] if cw is None else ref.at[win, pl.ds(pl.multiple_of(idx * cw, LANE), cw)]

    def _copies(self, src_refs, dst_refs, sems):
        send_sems, recv_sems, local_sems = sems
        x, y, c = _coords()
        my = 4 * x + 2 * y + c
        out = []
        for n, (t, r0, rows) in enumerate(self.items):
            win = pl.ds(r0, rows)
            out.append(pltpu.make_async_copy(self._slot(src_refs[t], t, my, win), dst_refs[t].at[my, win], local_sems.at[n]))
            for k in range(1, N_DEV):
                px, py, pc = x ^ ((k >> 2) & 1), y ^ ((k >> 1) & 1), c ^ (k & 1)
                out.append(pltpu.make_async_remote_copy(
                    src_ref=self._slot(src_refs[t], t, 4 * px + 2 * py + pc, win), dst_ref=dst_refs[t].at[my, win],
                    send_sem=send_sems.at[n, k - 1], recv_sem=recv_sems.at[n, k - 1],
                    device_id=(px, py, pc), device_id_type=MESH))
        return out

    def hook(self, step, last, src_refs, dst_refs, sems):
        if last == 0:
            for cp in self._copies(src_refs, dst_refs, sems):
                cp.start()
            for cp in self._copies(src_refs, dst_refs, sems):
                cp.wait()
            return

        @pl.when(step == 0)
        def _():
            for cp in self._copies(src_refs, dst_refs, sems):
                cp.start()

        @pl.when(step == last)
        def _():
            for cp in self._copies(src_refs, dst_refs, sems):
                cp.wait()


class _GatherCarry(_Carry):
    def hook(self, step, last, src_refs, dst_refs, sems):
        send_sems, recv_sems, local_sems = sems
        x, y, c = _coords()
        me, sibling = (x, y, c), (x, y, 1 - c)
        chips = [(1 - x, y), (x, 1 - y), (1 - x, 1 - y)]

        def per_item(fn):
            for n, (t, r0, rows) in enumerate(self.items):
                win = pl.ds(r0, rows)

                def copy(k, block, to, own=False, n=n, t=t, win=win):
                    slot = self._slot(dst_refs[t], t, 4 * block[0] + 2 * block[1] + block[2], win)
                    return pltpu.make_async_remote_copy(
                        src_ref=src_refs[t].at[win] if own else slot, dst_ref=slot,
                        send_sem=send_sems.at[n, k], recv_sem=recv_sems.at[n, k], device_id=to, device_id_type=MESH)

                mine = pltpu.make_async_copy(src_refs[t].at[win], self._slot(dst_refs[t], t, 4 * x + 2 * y + c, win), local_sems.at[n])
                first = [copy(0, me, sibling, own=True)] + [copy(1 + j, me, (*chip, c), own=True) for j, chip in enumerate(chips)]
                fn(copy, mine, first)

        def begin(copy, mine, first):
            mine.start()
            for cp in first:
                cp.start()

        def pass_on(copy, mine, first):
            for j, chip in enumerate(chips):
                copy(1 + j, (*chip, c), me).wait_recv()
                copy(4 + j, (*chip, c), sibling).start()

        def finish(copy, mine, first):
            copy(0, sibling, me).wait_recv()
            for j, chip in enumerate(chips):
                copy(4 + j, (*chip, 1 - c), me).wait_recv()
            for cp in first + [copy(4 + j, (*chip, c), sibling) for j, chip in enumerate(chips)]:
                cp.wait_send()
            mine.wait()

        if last == 0:
            for fn in (begin, pass_on, finish):
                per_item(fn)
            return
        late = max(1, (7 * last) // 8)
        for at, fn in ((0, begin), (late, pass_on), (last, finish)):
            pl.when(step == at)(functools.partial(per_item, fn))


def _mm(a, b, *, name, ta=False, tb=False, out_dtype=F32, add=None, tiles=None, carry=None):
    M, K = (a.shape[1], a.shape[0]) if ta else a.shape
    N = b.shape[0] if tb else b.shape[1]
    assert (b.shape[1] if tb else b.shape[0]) == K, (a.shape, b.shape, ta, tb)
    tk = max(t for t in range(LANE, min(K, MM_MAX_TK) + 1, LANE) if K % t == 0)
    tm, tn, tk = tiles or (_pick(M, (1024, 512, 256, 128)), _pick(N, (512, 256, 128)), tk)
    nk = K // tk
    dims = ((0 if ta else 1,), (1 if tb else 0,))

    n_in = 2 + (add is not None)
    nt = len(carry.tensors) if carry else 0
    gi, gj = M // tm, N // tn

    def kern(*refs):
        a_ref, b_ref = refs[:2]
        add_ref = refs[2] if add is not None else None
        o_ref = refs[n_in + 2 * nt]
        scr = refs[n_in + 3 * nt + 1:]
        if carry:
            step = (pl.program_id(0) * gj + pl.program_id(1)) * nk + pl.program_id(2)
            carry.hook(step, gi * gj * nk - 1, refs[n_in:n_in + nt], refs[n_in + 2 * nt + 1:n_in + 3 * nt + 1], scr[-3:])

        def finish(r):
            if add is not None:
                r = r + add_ref[...].astype(F32)
            o_ref[...] = r.astype(o_ref.dtype)

        if nk == 1:
            finish(_dot(a_ref[...], b_ref[...], dims))
            return
        acc = scr[0]
        k = pl.program_id(2)

        @pl.when(k == 0)
        def _():
            acc[...] = jnp.zeros_like(acc)

        acc[...] += _dot(a_ref[...], b_ref[...], dims)

        @pl.when(k == nk - 1)
        def _():
            finish(acc[...])

    a_spec = pl.BlockSpec((tk, tm), lambda i, j, k: (k, i)) if ta else pl.BlockSpec((tm, tk), lambda i, j, k: (i, k))
    b_spec = pl.BlockSpec((tn, tk), lambda i, j, k: (j, k)) if tb else pl.BlockSpec((tk, tn), lambda i, j, k: (k, j))
    o_spec = pl.BlockSpec((tm, tn), lambda i, j, k: (i, j))
    in_specs = [a_spec, b_spec] + ([o_spec] if add is not None else [])
    args = (a, b) + ((add,) if add is not None else ())
    scratch = [pltpu.VMEM((tm, tn), F32)] if nk > 1 else []
    o_shape = jax.ShapeDtypeStruct((M, N), out_dtype)
    if not carry:
        return pl.pallas_call(
            kern, name=name, grid=(gi, gj, nk), in_specs=in_specs, out_specs=o_spec, out_shape=o_shape, scratch_shapes=scratch,
            compiler_params=_cparams(("parallel", "parallel", "arbitrary")),
        )(*args)
    res = pl.pallas_call(
        kern, name=name, grid=(gi, gj, nk), in_specs=in_specs + carry.in_specs, out_specs=[o_spec] + carry.out_specs,
        out_shape=[o_shape] + carry.out_shape, scratch_shapes=scratch + carry.scratch,
        input_output_aliases=carry.aliases(n_in, 1), compiler_params=_cparams(("arbitrary", "arbitrary", "arbitrary")),
    )(*args, *carry.args)
    return res[0], list(res[1:])


def _stage_specs(acts, params, consts, tile, ct):
    act_specs = [pl.BlockSpec((tile, ct), functools.partial(lambda j, i, o: (i, o + j), o=off // ct)) for _, off in acts]
    par_specs = [pl.BlockSpec(bs, functools.partial(lambda j, i, im: im(j), im=im)) for _, bs, im in params]
    con_specs = [pl.BlockSpec(bs, functools.partial(lambda j, i, im: im(j), im=im)) for _, bs, im in consts]
    return act_specs, par_specs, con_specs


def _stage_fwd(f, name, n_rows, width, tile, ct, acts, params, consts, out_dtypes, carry=None):
    for _, off in acts:
        assert off % ct == 0
    na, npar, nc, no = len(acts), len(params), len(consts), len(out_dtypes)
    n_in = na + npar + nc
    nt = len(carry.tensors) if carry else 0
    gj, gi = width // ct, n_rows // tile

    def kern(*refs):
        if carry:
            step = pl.program_id(0) * gi + pl.program_id(1)
            carry.hook(step, gj * gi - 1, refs[n_in:n_in + nt], refs[n_in + 2 * nt + no:n_in + 3 * nt + no], refs[n_in + 3 * nt + no:])
        a = [r[...].astype(F32) for r in refs[:na]]
        p = [r[...] for r in refs[na:na + npar]]
        c = [r[...] for r in refs[na + npar:n_in]]
        outs = f(a, p, c, pl.program_id(1) * tile)
        for r, o in zip(refs[n_in + 2 * nt:n_in + 2 * nt + no], outs):
            r[...] = o.astype(r.dtype)

    act_specs, par_specs, con_specs = _stage_specs(acts, params, consts, tile, ct)
    o_spec = pl.BlockSpec((tile, ct), lambda j, i: (i, j))
    in_specs = act_specs + par_specs + con_specs
    out_shape = [jax.ShapeDtypeStruct((n_rows, width), d) for d in out_dtypes]
    args = [a for a, _ in acts] + [p for p, _, _ in params] + [c for c, _, _ in consts]
    if not carry:
        return tuple(pl.pallas_call(
            kern, name=name, grid=(gj, gi), in_specs=in_specs, out_specs=[o_spec] * no, out_shape=out_shape,
            compiler_params=_cparams(("parallel", "parallel")),
        )(*args))
    res = pl.pallas_call(
        kern, name=name, grid=(gj, gi), in_specs=in_specs + carry.in_specs, out_specs=[o_spec] * no + carry.out_specs,
        out_shape=out_shape + carry.out_shape, scratch_shapes=carry.scratch, input_output_aliases=carry.aliases(n_in, no),
        compiler_params=_cparams(("arbitrary", "arbitrary")),
    )(*args, *carry.args)
    return tuple(res[:no]), list(res[no:])


def _stage_bwd(f, name, n_rows, width, tile, ct, acts, params, consts, couts, dact_dtypes, extra_add=None, carry=None):
    na, npar, nc, no = len(acts), len(params), len(consts), len(couts)
    nx = 0 if extra_add is None else 1
    nt = len(carry.tensors) if carry else 0
    n_in = na + npar + nc + no + nx
    gj, gi = width // ct, n_rows // tile

    def kern(*refs):
        if carry:
            step = pl.program_id(0) * gi + pl.program_id(1)
            n_out = n_in + 2 * nt + na + npar
            carry.hook(step, gj * gi - 1, refs[n_in:n_in + nt], refs[n_out:n_out + nt], refs[n_out + nt:])
        a = [r[...].astype(F32) for r in refs[:na]]
        p = [r[...] for r in refs[na:na + npar]]
        c = [r[...] for r in refs[na + npar:na + npar + nc]]
        base = na + npar + nc
        co = [r[...].astype(F32) for r in refs[base:base + no]]
        base += no
        x_refs = refs[base:base + nx]
        base += nx + 2 * nt
        da_refs = refs[base:base + na]
        dp_refs = refs[base + na:base + na + npar]
        row0 = pl.program_id(1) * tile
        _, vjp = jax.vjp(lambda aa, pp: tuple(f(aa, pp, c, row0)), a, p)
        da, dp = vjp(tuple(co))
        for k, (r, d) in enumerate(zip(da_refs, da)):
            if k == 0 and nx:
                d = d + x_refs[0][...].astype(F32)
            r[...] = d.astype(r.dtype)
        first = pl.program_id(1) == 0
        for r, d in zip(dp_refs, dp):
            @pl.when(first)
            def _(r=r, d=d):
                r[...] = d

            @pl.when(jnp.logical_not(first))
            def _(r=r, d=d):
                r[...] += d

    act_specs, par_specs, con_specs = _stage_specs(acts, params, consts, tile, ct)
    t_spec = pl.BlockSpec((tile, ct), lambda j, i: (i, j))
    co_specs = [pl.BlockSpec((tile, ct), functools.partial(lambda j, i, o: (i, o + j), o=off // ct)) for _, off in couts]
    x_specs = [] if extra_add is None else [pl.BlockSpec((tile, ct), functools.partial(lambda j, i, o: (i, o + j), o=extra_add[1] // ct))]
    x_args = [] if extra_add is None else [extra_add[0]]
    in_specs = act_specs + par_specs + con_specs + co_specs + x_specs
    out_specs = [t_spec] * na + par_specs
    out_shape = [jax.ShapeDtypeStruct((n_rows, width), d) for d in dact_dtypes] + [jax.ShapeDtypeStruct(p.shape, F32) for p, _, _ in params]
    args = [a for a, _ in acts] + [p for p, _, _ in params] + [c for c, _, _ in consts] + [c for c, _ in couts] + x_args
    if not carry:
        outs = pl.pallas_call(
            kern, name=name, grid=(gj, gi), in_specs=in_specs, out_specs=out_specs, out_shape=out_shape,
            compiler_params=_cparams(("parallel", "arbitrary")),
        )(*args)
        return tuple(outs[:na]), tuple(outs[na:])
    outs = pl.pallas_call(
        kern, name=name, grid=(gj, gi), in_specs=in_specs + carry.in_specs, out_specs=out_specs + carry.out_specs,
        out_shape=out_shape + carry.out_shape, scratch_shapes=carry.scratch, input_output_aliases=carry.aliases(n_in, na + npar),
        compiler_params=_cparams(("arbitrary", "arbitrary")),
    )(*args, *carry.args)
    return tuple(outs[:na]), tuple(outs[na:na + npar]), list(outs[na + npar:])


def _row(ct):
    return (1, ct), (lambda j: (0, j))


def _f_rmsnorm(a, p, c, row0):
    x, = a
    g, = p
    return (x * lax.rsqrt(jnp.mean(x * x, axis=-1, keepdims=True) + NORM_EPS) * g,)


def _f_lora_act(a, p, c, row0, widths):
    x, = a
    dl, al = widths
    col = lax.broadcasted_iota(jnp.int32, x.shape, 1)
    return (jnp.where(col < dl, jnp.tanh(x), jnp.where(col < dl + al, x, _sigmoid(x))),)


def _head_sums_raw(x, ones):
    hi = x.astype(BF16)
    lo = (x - hi.astype(F32)).astype(BF16)
    return _nn(hi, ones) + _nn(lo, ones)


@jax.custom_vjp
def _head_sums(x, ones):
    return _head_sums_raw(x, ones)


_head_sums.defvjp(lambda x, ones: (_head_sums_raw(x, ones), ones),
                  lambda ones, ct: (_head_sums_raw(ct, ones), jnp.zeros_like(ones)))


def _f_rwkv_pre(a, p, c, row0):
    k, wlin, alin = a
    w0, a0, k_k, k_a = p
    gsum, = c
    w = -_softplus(-(w0 + wlin)) - 0.5
    lw = -jnp.exp(w)
    alpha = _sigmoid(a0 + alin)
    kk = k * k_k
    ss = _head_sums(kk * kk, gsum)
    kk = kk * lax.rsqrt(jnp.maximum(ss, 1e-24))
    k2 = k * (1.0 + (alpha - 1.0) * k_a)
    return lw, k2, -kk, kk * alpha


def _f_rwkv_post(a, p, c, row0):
    y, r, k2, v, g = a
    ln_g, ln_b, r_k = p
    gsum, = c
    inv = 1.0 / HEAD
    mean = _head_sums(y, gsum) * inv
    yc = y - mean
    var = _head_sums(yc * yc, gsum) * inv
    yn = yc * lax.rsqrt(var + GN_EPS) * ln_g + ln_b
    bonus = _head_sums(r * k2 * r_k, gsum)
    return ((yn + bonus * v) * g,)


def _f_lru_gates(a, p, c, row0, seq):
    xc, = a
    wr, br, wi, bi, lam = p
    xb = xc.astype(BF16)
    rg = _sigmoid(_nn(xb, wr[0].astype(BF16)) + br)
    ig = _sigmoid(_nn(xb, wi[0].astype(BF16)) + bi)
    log_a = -LRU_C * rg * _softplus(-lam)
    a_t = jnp.exp(log_a)
    mult = jnp.sqrt(_neg_expm1(2.0 * log_a))
    row = row0 + lax.broadcasted_iota(jnp.int32, xc.shape, 0)
    mult = jnp.where(row % seq == 0, 1.0, mult)
    return a_t, mult * ig * xc


def _f_lru_post(a, p, c, row0):
    h, gate = a
    g, = p
    y = h * _gelu(gate)
    return (y * lax.rsqrt(jnp.mean(y * y, axis=-1, keepdims=True) + NORM_EPS) * g,)


def _f_swiglu(a, p, c, row0):
    gate, up = a
    return (gate * _sigmoid(gate) * up,)


def _shift_down(x, s, row):
    return jnp.where(row >= s, pltpu.roll(x, s, 0), 0.0)


def _shift_up(x, s, row):
    n = x.shape[0]
    return jnp.where(row < n - s, pltpu.roll(x, n - s, 0), 0.0)


def _seq_call(kern, name, bl, seq, width, ct, ins, outs, acc_outs=()):
    def spec(off, rows):
        if rows is None:
            return pl.BlockSpec((seq, ct), functools.partial(lambda j, b, o: (b, o + j), o=off // ct))
        return pl.BlockSpec((rows, ct), lambda j, b: (0, j))

    in_specs = [spec(off, rows) for _, off, rows in ins]
    out_specs = [spec(0, None) for _ in outs] + [spec(0, rows) for _, rows in acc_outs]
    out_shape = [jax.ShapeDtypeStruct((bl * seq, width), d) for d in outs] + [jax.ShapeDtypeStruct((rows, width), F32) for _, rows in acc_outs]
    res = pl.pallas_call(
        kern, name=name, grid=(width // ct, bl), in_specs=in_specs, out_specs=out_specs, out_shape=out_shape,
        compiler_params=_cparams(("parallel", "arbitrary")),
    )(*[a for a, _, _ in ins])
    return tuple(res)


def _acc(ref, val):
    first = pl.program_id(1) == 0

    @pl.when(first)
    def _():
        ref[...] = val

    @pl.when(jnp.logical_not(first))
    def _():
        ref[...] += val


def _lerp_fwd(p, off, mu, bl, seq, width, ct):
    def kern(p_ref, mu_ref, o_ref):
        x = p_ref[...]
        row = lax.broadcasted_iota(jnp.int32, x.shape, 0)
        o_ref[...] = x + (_shift_down(x, 1, row) - x) * mu_ref[...]

    return _seq_call(kern, "lerp_fwd", bl, seq, width, ct, [(p, off, None), (mu, 0, 1)], [F32])[0]


def _lerp_bwd(p, off, mu, dps, bl, seq, width, ct, out_dtype):
    def kern(p_ref, mu_ref, d_ref, dp_ref, dmu_ref):
        x = p_ref[...]
        d = d_ref[...].astype(F32)
        m = mu_ref[...]
        row = lax.broadcasted_iota(jnp.int32, x.shape, 0)
        dp_ref[...] = (d * (1.0 - m) + _shift_up(d * m, 1, row)).astype(dp_ref.dtype)
        _acc(dmu_ref, jnp.sum(d * (_shift_down(x, 1, row) - x), axis=0, keepdims=True))

    return _seq_call(kern, "lerp_bwd", bl, seq, width, ct, [(p, off, None), (mu, 0, 1), (dps, 0, None)], [out_dtype], [(None, 1)])


def _conv_fwd(p, off, cw, cb, bl, seq, width, ct):
    nw = cw.shape[0]

    def kern(x_ref, w_ref, b_ref, o_ref):
        x = x_ref[...]
        row = lax.broadcasted_iota(jnp.int32, x.shape, 0)
        acc = b_ref[...] + x * w_ref[pl.ds(nw - 1, 1), :]
        for s in range(1, nw):
            acc = acc + _shift_down(x, s, row) * w_ref[pl.ds(nw - 1 - s, 1), :]
        o_ref[...] = acc

    return _seq_call(kern, "conv_fwd", bl, seq, width, ct, [(p, off, None), (cw, 0, nw), (cb, 0, 1)], [F32])[0]


def _conv_bwd(p, off, cw, dxc, bl, seq, width, ct, out_dtype):
    nw = cw.shape[0]

    def kern(x_ref, w_ref, d_ref, dx_ref, dw_ref, db_ref):
        x = x_ref[...]
        d = d_ref[...]
        row = lax.broadcasted_iota(jnp.int32, x.shape, 0)
        wrow = lax.broadcasted_iota(jnp.int32, dw_ref.shape, 0)
        dx = d * w_ref[pl.ds(nw - 1, 1), :]
        dw = jnp.where(wrow == nw - 1, jnp.sum(d * x, axis=0, keepdims=True), 0.0)
        for s in range(1, nw):
            dx = dx + _shift_up(d, s, row) * w_ref[pl.ds(nw - 1 - s, 1), :]
            dw = jnp.where(wrow == nw - 1 - s, jnp.sum(d * _shift_down(x, s, row), axis=0, keepdims=True), dw)
        dx_ref[...] = dx.astype(dx_ref.dtype)
        _acc(dw_ref, dw)
        _acc(db_ref, jnp.sum(d, axis=0, keepdims=True))

    return _seq_call(kern, "conv_bwd", bl, seq, width, ct, [(p, off, None), (cw, 0, nw), (dxc, 0, None)], [out_dtype], [(None, nw), (None, 1)])


def _lru_scan_fwd(a, bx, bl, seq, width, ct):
    def kern(a_ref, b_ref, h_ref):
        av = a_ref[...]
        bv = b_ref[...]
        row = lax.broadcasted_iota(jnp.int32, av.shape, 0)
        d = 1
        while d < seq:
            a_sh = jnp.where(row >= d, pltpu.roll(av, d, 0), 1.0)
            b_sh = jnp.where(row >= d, pltpu.roll(bv, d, 0), 0.0)
            bv = av * b_sh + bv
            av = av * a_sh
            d *= 2
        h_ref[...] = bv

    return _seq_call(kern, "lru_scan_fwd", bl, seq, width, ct, [(a, 0, None), (bx, 0, None)], [F32])[0]


def _lru_scan_bwd(a, h, dh, bl, seq, width, ct):
    def kern(a_ref, h_ref, d_ref, da_ref, db_ref):
        row = lax.broadcasted_iota(jnp.int32, a_ref.shape, 0)
        al = _shift_up(a_ref[...], 1, row)
        g = d_ref[...]
        d = 1
        while d < seq:
            keep = row < seq - d
            al_sh = jnp.where(keep, pltpu.roll(al, seq - d, 0), 1.0)
            g_sh = jnp.where(keep, pltpu.roll(g, seq - d, 0), 0.0)
            g = al * g_sh + g
            al = al * al_sh
            d *= 2
        db_ref[...] = g
        da_ref[...] = g * _shift_down(h_ref[...], 1, row)

    return _seq_call(kern, "lru_scan_bwd", bl, seq, width, ct, [(a, 0, None), (h, 0, None), (dh, 0, None)], [F32, F32])


_FORMS = {"nn": ((1,), (0,)), "nt": ((1,), (1,)), "tn": ((0,), (0,))}
_FORM_GRADS = {"nn": (("nt", "g", "b"), ("tn", "a", "g")),
               "nt": (("nn", "g", "b"), ("tn", "g", "a")),
               "tn": (("nt", "b", "g"), ("nn", "a", "g"))}


def _split_bf16(x):
    hi = x.astype(BF16)
    return hi, (x - hi.astype(F32)).astype(BF16)


def _pdot_raw(a, b, form, passes):
    dims = _FORMS[form]
    if passes == 1:
        return _dot(a.astype(BF16), b.astype(BF16), dims)
    ah, al = _split_bf16(a)
    bh, bl = _split_bf16(b)
    return _dot(ah, bh, dims) + (_dot(ah, bl, dims) + _dot(al, bh, dims))


@functools.partial(jax.custom_vjp, nondiff_argnums=(2, 3))
def _pdot(a, b, form, passes):
    return _pdot_raw(a, b, form, passes)


def _pdot_fwd(a, b, form, passes):
    return _pdot_raw(a, b, form, passes), (a, b)


def _pdot_bwd(form, passes, res, g):
    vals = {"a": res[0], "b": res[1], "g": g}
    (fa, xa, ya), (fb, xb, yb) = _FORM_GRADS[form]
    return _pdot_raw(vals[xa], vals[ya], fa, passes), _pdot_raw(vals[xb], vals[yb], fb, passes)


_pdot.defvjp(_pdot_fwd, _pdot_bwd)


def _neumann_raw(a_list, n_levels, passes):
    eye = (lax.broadcasted_iota(jnp.int32, a_list[0].shape, 0) == lax.broadcasted_iota(jnp.int32, a_list[0].shape, 1)).astype(F32)
    pw = list(a_list)
    x = [eye + a for a in a_list]
    for _ in range(n_levels):
        pw = [_pdot_raw(p, p, "nn", passes) for p in pw]
        x = [xi + _pdot_raw(xi, p, "nn", passes) for xi, p in zip(x, pw)]
    return x


@functools.partial(jax.custom_vjp, nondiff_argnums=(1, 2))
def _neumann_inverse(a_list, n_levels, passes):
    return _neumann_raw(a_list, n_levels, passes)


def _neumann_fwd(a_list, n_levels, passes):
    x = _neumann_raw(a_list, n_levels, passes)
    return x, x


def _neumann_bwd(n_levels, passes, x, ct):
    return ([_pdot_raw(xi, _pdot_raw(c, xi, "nt", passes), "tn", passes) for xi, c in zip(x, ct)],)


_neumann_inverse.defvjp(_neumann_fwd, _neumann_bwd)


def _scan_chunk2(S0, r, lw, k, v, a, b, p_main=1, p_inv=3):
    y, s = _scan_block([S0], [[(r, lw, k, v, a, b)]], p_main, p_inv)
    return y[0][0], s[0]


def _scan_block(states, units, p_main=1, p_inv=1):
    C = units[0][0][0].shape[0]
    C2 = 2 * C
    ri = lax.broadcasted_iota(jnp.int32, (C, C), 0)
    ci = lax.broadcasted_iota(jnp.int32, (C, C), 1)
    tri = (ri >= ci).astype(F32)
    i2 = lax.broadcasted_iota(jnp.int32, (C2, C2), 0)
    j2 = lax.broadcasted_iota(jnp.int32, (C2, C2), 1)
    same = (i2 // C) == (j2 // C)
    strict = jnp.logical_and(same, (i2 % C) > (j2 % C))
    incl = jnp.logical_and(same, (i2 % C) >= (j2 % C))
    eye = (i2 == j2).astype(F32)
    lane = lax.broadcasted_iota(jnp.int32, (1, LANE), 1)
    m0, m1 = (lane < HEAD).astype(F32), (lane >= HEAD).astype(F32)
    stack = lambda z: jnp.concatenate([z * m0, z * m1], axis=0)
    ids = [(i, g) for g in range(len(units[0])) for i in range(len(units))]

    pre = {}
    for i, g in ids:
        r, lw, k, v, a, b = units[i][g]
        cs = _nn(tri, lw, HI)
        p_incl = jnp.exp(cs)
        p_rec = jnp.exp(-cs)
        xr = jnp.concatenate([stack(a * jnp.exp(cs - lw)), stack(r * p_incl)], axis=0)
        bk = jnp.concatenate([stack(b * p_rec), stack(k * p_rec)], axis=0)
        pre[i, g] = (xr, bk, stack(v), jnp.exp(jnp.sum(lw, axis=0, keepdims=True)))
    gm = {u: _pdot(pre[u][0], pre[u][1], "nt", p_main) for u in ids}
    a_ak = {u: jnp.where(strict, gm[u][:C2, C2:], 0.0) for u in ids}
    r_bk = {u: jnp.concatenate([jnp.where(incl, gm[u][C2:, :C2], 0.0), jnp.where(incl, gm[u][C2:, C2:], 0.0)], axis=1) for u in ids}
    a_ab = [jnp.where(strict, gm[u][:C2, :C2], 0.0) for u in ids]
    x = dict(zip(ids, _neumann_inverse(a_ab, int(math.log2(C)) - 1, p_inv)))
    akv = {u: _pdot(a_ak[u], pre[u][2], "nn", p_main) for u in ids}

    states = list(states)
    pairs = range(len(units))
    ys = [[None] * len(units[0]) for _ in units]
    for g in range(len(units[0])):
        xs = [_pdot(pre[i, g][0], states[i], "nt", p_main) for i in pairs]
        us = [_pdot(x[i, g], xs[i][:C2] + akv[i, g], "nn", p_inv) for i in pairs]
        uv = [jnp.concatenate([us[i], pre[i, g][2]], axis=0) for i in pairs]
        y2 = [xs[i][C2:] + _pdot(r_bk[i, g], uv[i], "nn", p_main) for i in pairs]
        for i in pairs:
            ys[i][g] = y2[i][:C] + y2[i][C:]
        states = [(states[i] + _pdot(uv[i], pre[i, g][1], "tn", p_main)) * pre[i, g][3] for i in pairs]
    return ys, states


def _scan_dims(seq, rw):
    G = _pick(seq // SCAN_CHUNK, (SCAN_GROUP, 2, 1))
    NP = _pick(rw // LANE, (SCAN_PAIRS, 4, 2, 1))
    C = SCAN_CHUNK * G
    return SCAN_CHUNK, G, NP, C, seq // C, rw // (NP * LANE)


def _rwkv_scan_fwd(r, lw, k2, v, na, bb, p, bl, seq, rw, carry=None):
    cs, G, NP, C, nc, nhg = _scan_dims(seq, rw)
    nt = len(carry.tensors) if carry else 0

    def kern(*refs):
        in_refs = refs[:6]
        y_ref, st_ref = refs[6 + 2 * nt:8 + 2 * nt]
        s_scr = refs[8 + 3 * nt]
        if carry:
            step = (pl.program_id(0) * nhg + pl.program_id(1)) * nc + pl.program_id(2)
            carry.hook(step, bl * nhg * nc - 1, refs[6:6 + nt], refs[8 + 2 * nt:8 + 3 * nt], refs[9 + 3 * nt:])

        @pl.when(pl.program_id(2) == 0)
        def _():
            s_scr[...] = jnp.zeros_like(s_scr)

        st_ref[...] = s_scr[...]
        units = [[tuple(ref[pl.ds(g * cs, cs), pl.ds(i * LANE, LANE)] for ref in in_refs) for g in range(G)] for i in range(NP)]
        ys, s_new = _scan_block([s_scr[i] for i in range(NP)], units)
        for i in range(NP):
            s_scr[i] = s_new[i]
            for g in range(G):
                y_ref[pl.ds(g * cs, cs), pl.ds(i * LANE, LANE)] = ys[i][g]

    def tok(off):
        return pl.BlockSpec((C, NP * LANE), functools.partial(lambda b, h, c, o: (b * nc + c, o + h), o=off // (NP * LANE)))

    in_specs = [tok(0), tok(0), tok(0), tok(2 * rw), tok(0), tok(0)]
    out_specs = [tok(0), pl.BlockSpec((NP, LANE, LANE), lambda b, h, c: ((b * nhg + h) * nc + c, 0, 0))]
    out_shape = [jax.ShapeDtypeStruct((bl * seq, rw), F32), jax.ShapeDtypeStruct((bl * nhg * nc * NP, LANE, LANE), F32)]
    scratch = [pltpu.VMEM((NP, LANE, LANE), F32)]
    if not carry:
        y, st = pl.pallas_call(
            kern, name="rwkv_scan_fwd", grid=(bl, nhg, nc), in_specs=in_specs, out_specs=out_specs, out_shape=out_shape,
            scratch_shapes=scratch, compiler_params=_cparams(("parallel", "parallel", "arbitrary")),
        )(p, lw, k2, p, na, bb)
        return y, st
    res = pl.pallas_call(
        kern, name="rwkv_scan_fwd", grid=(bl, nhg, nc), in_specs=in_specs + carry.in_specs, out_specs=out_specs + carry.out_specs,
        out_shape=out_shape + carry.out_shape, scratch_shapes=scratch + carry.scratch, input_output_aliases=carry.aliases(6, 2),
        compiler_params=_cparams(("arbitrary", "arbitrary", "arbitrary")),
    )(p, lw, k2, p, na, bb, *carry.args)
    return res[0], res[1], list(res[2:])


def _rwkv_scan_bwd(lw, k2, na, bb, p, st, dy, bl, seq, rw, carry=None):
    cs, G, NP, C, nc, nhg = _scan_dims(seq, rw)
    nt = len(carry.tensors) if carry else 0

    def kern(*refs):
        in_refs = refs[:6]
        st_ref, dy_ref = refs[6:8]
        out_refs = refs[8 + 2 * nt:14 + 2 * nt]
        ds_scr = refs[14 + 3 * nt]
        if carry:
            step = (pl.program_id(0) * nhg + pl.program_id(1)) * nc + pl.program_id(2)
            carry.hook(step, bl * nhg * nc - 1, refs[8:8 + nt], refs[14 + 2 * nt:14 + 3 * nt], refs[15 + 3 * nt:])

        @pl.when(pl.program_id(2) == 0)
        def _():
            ds_scr[...] = jnp.zeros_like(ds_scr)

        win = lambda ref, i, g: ref[pl.ds(g * cs, cs), pl.ds(i * LANE, LANE)]
        units = [[tuple(win(ref, i, g) for ref in in_refs) for g in range(G)] for i in range(NP)]
        _, vjp = jax.vjp(_scan_block, [st_ref[i] for i in range(NP)], units)
        dys = [[win(dy_ref, i, g) for g in range(G)] for i in range(NP)]
        ds, dunits = vjp((dys, [ds_scr[i] for i in range(NP)]))
        for i in range(NP):
            ds_scr[i] = ds[i]
            for g in range(G):
                for ref, d in zip(out_refs, dunits[i][g]):
                    ref[pl.ds(g * cs, cs), pl.ds(i * LANE, LANE)] = d

    def tok(off):
        return pl.BlockSpec((C, NP * LANE), functools.partial(lambda b, h, c, o: (b * nc + (nc - 1 - c), o + h), o=off // (NP * LANE)))

    st_spec = pl.BlockSpec((NP, LANE, LANE), lambda b, h, c: ((b * nhg + h) * nc + (nc - 1 - c), 0, 0))
    in_specs = [tok(0), tok(0), tok(0), tok(2 * rw), tok(0), tok(0), st_spec, tok(0)]
    out_shape = [jax.ShapeDtypeStruct((bl * seq, rw), F32)] * 6
    scratch = [pltpu.VMEM((NP, LANE, LANE), F32)]
    if not carry:
        return pl.pallas_call(
            kern, name="rwkv_scan_bwd", grid=(bl, nhg, nc), in_specs=in_specs, out_specs=[tok(0)] * 6, out_shape=out_shape,
            scratch_shapes=scratch, compiler_params=_cparams(("parallel", "parallel", "arbitrary")),
        )(p, lw, k2, p, na, bb, st, dy)
    res = pl.pallas_call(
        kern, name="rwkv_scan_bwd", grid=(bl, nhg, nc), in_specs=in_specs + carry.in_specs, out_specs=[tok(0)] * 6 + carry.out_specs,
        out_shape=out_shape + carry.out_shape, scratch_shapes=scratch + carry.scratch, input_output_aliases=carry.aliases(8, 6),
        compiler_params=_cparams(("arbitrary", "arbitrary", "arbitrary")),
    )(p, lw, k2, p, na, bb, st, dy, *carry.args)
    return res[:6], list(res[6:])


def _loss_head(h2, g_final, target, tile):
    n, d = h2.shape
    nt = n // tile

    def kern(h_ref, g_ref, t_ref, dh_ref, dg_ref, l_ref):
        def f(h, g):
            y = h * lax.rsqrt(jnp.mean(h * h, axis=-1, keepdims=True) + NORM_EPS) * g
            e = y - t_ref[...]
            return 0.5 * jnp.sum(jnp.mean(e * e, axis=-1, keepdims=True))

        loss, (dh, dg) = jax.value_and_grad(f, argnums=(0, 1))(h_ref[...], g_ref[...])
        dh_ref[...] = dh
        first = pl.program_id(0) == 0

        @pl.when(first)
        def _():
            dg_ref[...] = dg
            l_ref[...] = jnp.zeros_like(l_ref) + loss

        @pl.when(jnp.logical_not(first))
        def _():
            dg_ref[...] += dg
            l_ref[...] += loss

    row = pl.BlockSpec((tile, d), lambda i: (i, 0))
    vec = pl.BlockSpec((1, d), lambda i: (0, 0))
    return pl.pallas_call(
        kern, name="loss_head", grid=(nt,), in_specs=[row, vec, row],
        out_specs=[row, vec, pl.BlockSpec((1, LANE), lambda i: (0, 0))],
        out_shape=[jax.ShapeDtypeStruct((n, d), F32), jax.ShapeDtypeStruct((1, d), F32), jax.ShapeDtypeStruct((1, LANE), F32)],
        compiler_params=_cparams(("arbitrary",)),
    )(h2, g_final, target)


def _adamw(parts, w, m, v, name, carry=None):
    n_parts, Rp, Cp = parts.shape
    R, Cc = w.shape
    assert Rp >= R and Cp >= Cc
    tr = _pick(R, tuple(t for t in (1024, 512, 256, 128, 64, 32, 16) if t * Cp <= 128 * 1024) + (8,))
    part = (lambda ref, s: ref[s]) if Cp == Cc else (lambda ref, s: ref[s, :, pl.ds(0, Cc)])
    c1, c2 = 1.0 - ADAM_B1, 1.0 - ADAM_B2
    bc1, bc2 = 1.0 - ADAM_B1 ** ADAM_STEP, 1.0 - ADAM_B2 ** ADAM_STEP

    nt = len(carry.tensors) if carry else 0

    def kern(*refs):
        p_ref, w_ref, m_ref, v_ref = refs[:4]
        g_ref, d_ref, nm_ref, nv_ref = refs[4 + 2 * nt:8 + 2 * nt]
        if carry:
            carry.hook(pl.program_id(0), R // tr - 1, refs[4:4 + nt], refs[8 + 2 * nt:8 + 3 * nt], refs[8 + 3 * nt:])
        g = part(p_ref, 0).astype(F32)
        for s in range(1, n_parts):
            g = g + part(p_ref, s).astype(F32)
        m2 = ADAM_B1 * m_ref[...] + c1 * g
        v2 = ADAM_B2 * v_ref[...] + c2 * (g * g)
        g_ref[...] = g
        nm_ref[...] = m2
        nv_ref[...] = v2
        d_ref[...] = -ADAM_LR * ((m2 / bc1) / (jnp.sqrt(v2 / bc2) + ADAM_EPS) + ADAM_WD * w_ref[...])

    blk = pl.BlockSpec((tr, Cc), lambda i: (i, 0))
    in_specs = [pl.BlockSpec((n_parts, tr, Cp), lambda i: (0, i, 0)), blk, blk, blk]
    out_shape = [jax.ShapeDtypeStruct((R, Cc), F32)] * 4
    if not carry:
        return pl.pallas_call(
            kern, name=name, grid=(R // tr,), in_specs=in_specs, out_specs=[blk] * 4, out_shape=out_shape,
            compiler_params=_cparams(("parallel",)),
        )(parts, w, m, v)
    res = pl.pallas_call(
        kern, name=name, grid=(R // tr,), in_specs=in_specs + carry.in_specs, out_specs=[blk] * 4 + carry.out_specs,
        out_shape=out_shape + carry.out_shape, scratch_shapes=carry.scratch, input_output_aliases=carry.aliases(4, 4),
        compiler_params=_cparams(("arbitrary",)),
    )(parts, w, m, v, *carry.args)
    return res[:4], list(res[4:])


def _exchange_now(carry, name):
    nt = len(carry.tensors)

    def body(*refs):
        carry.hook(0, 0, refs[:nt], refs[2 * nt:3 * nt], refs[3 * nt:])

    return list(pl.pallas_call(
        body, name=name, in_specs=carry.in_specs, out_specs=carry.out_specs, out_shape=carry.out_shape,
        scratch_shapes=carry.scratch, input_output_aliases=carry.aliases(0, 0),
    )(*carry.args))


def _carried(queue, capacity_us, *args, fn, **kw):
    carry = queue.take(capacity_us)
    if carry is None:
        return fn(*args, **kw)
    res = fn(*args, carry=carry, **kw)
    queue.done(carry, res[-1])
    return res[0] if len(res) == 2 else res[:-1]


class _Queue:
    def __init__(self, gather, label):
        self.gather, self.label = gather, label
        self.tensors, self.fifo, self.n_flush = {}, [], 0

    def push(self, name, src, n_pieces, cost_us, cols=False):
        cw = None
        if cols:
            cw = src.shape[1] if self.gather else src.shape[1] // N_DEV
            assert cw % LANE == 0
            dst_shape = (src.shape[0], N_DEV * cw) if self.gather else (N_DEV, src.shape[0], cw)
        else:
            dst_shape = ((N_DEV,) + src.shape) if self.gather else src.shape
        n_rows = src.shape[0] if (self.gather or cols) else src.shape[1]
        rows = n_rows // n_pieces
        assert rows * n_pieces == n_rows and rows % 16 == 0, (name, src.shape)
        self.tensors[name] = [src, lax.empty(dst_shape, src.dtype), cw]
        self.fifo += [(name, p * rows, rows, cost_us / n_pieces) for p in range(n_pieces)]

    def take(self, capacity_us, count=None):
        picked = []
        while self.fifo and (len(picked) < count if count is not None else capacity_us >= 0.6 * self.fifo[0][3]):
            picked.append(self.fifo.pop(0))
            capacity_us -= picked[-1][3]
        if not picked:
            return None
        names = list(dict.fromkeys(n for n, _, _, _ in picked))
        cls = _GatherCarry if self.gather else _Carry
        carry = cls([tuple(self.tensors[n]) for n in names], [(names.index(n), r0, rows) for n, r0, rows, _ in picked])
        carry.names = names
        return carry

    def done(self, carry, dsts):
        for n, d in zip(carry.names, dsts):
            self.tensors[n][1] = d

    def flush(self, count=None):
        carry = self.take(float("inf"), count)
        if carry:
            self.done(carry, _exchange_now(carry, "%s_now_%d" % (self.label, self.n_flush)))
            self.n_flush += 1

    def result(self, name, r0=0, r1=None):
        late = [i for i, (n, p0, rows, _) in enumerate(self.fifo) if n == name and p0 < (r1 or p0 + rows) and p0 + rows > r0]
        if late:
            self.flush(late[-1] + 1)
        return self.tensors[name][1]


def _cols_from_shards(g):
    return jnp.transpose(g, (1, 0, 2)).reshape(g.shape[1], N_DEV * g.shape[2])


def _shards_from_cols(w):
    r, n = w.shape
    return jnp.transpose(w.reshape(r, N_DEV, n // N_DEV), (1, 0, 2))


def _pad_cols(w, to):
    return jnp.pad(w, ((0, 0), (0, to - w.shape[1])))


def _pack(arrs):
    flat = jnp.concatenate([a.reshape(-1) for a in arrs])
    n = _rup(flat.shape[0], 256 * LANE)
    return jnp.pad(flat, (0, n - flat.shape[0])).reshape(n // LANE, LANE)


def _unpack(mat, shapes):
    flat = mat.reshape(-1)
    out, o = [], 0
    for s in shapes:
        n = math.prod(s)
        out.append(flat[o:o + n].reshape(s))
        o += n
    return out


_SMALL = ["norm_mix_g", "mu_shift", "rwkv_w0", "rwkv_a0", "rwkv_k_k", "rwkv_k_a", "rwkv_r_k", "rwkv_ln_g", "rwkv_ln_b", "conv_b",
          "lru_wr", "lru_br", "lru_wi", "lru_bi", "lru_lambda", "lru_norm_g", "norm_ffn_g", "norm_final_g"]
_SMALL_SHARDED = ["rwkv_w2", "rwkv_a2", "rwkv_g2", "conv_w"]
_BIG = ["w_in", "w_out", "ffn_w_gate", "ffn_w_up", "ffn_w_down"]
_WEIGHTS = ['norm_mix_g', 'w_in', 'mu_shift', 'rwkv_w0', 'rwkv_w2', 'rwkv_a0', 'rwkv_a2', 'rwkv_g2', 'rwkv_k_k', 'rwkv_k_a', 'rwkv_r_k',
            'rwkv_ln_g', 'rwkv_ln_b', 'conv_w', 'conv_b', 'lru_wr', 'lru_br', 'lru_wi', 'lru_bi', 'lru_lambda', 'lru_norm_g', 'w_out',
            'norm_ffn_g', 'ffn_w_gate', 'ffn_w_up', 'ffn_w_down', 'norm_final_g']


def _step(W, M, V, x, loss_target):
    bl, seq, d = x.shape
    n = bl * seq
    rw = W["rwkv_w0"].shape[1]
    nh = W["rwkv_r_k"].shape[1]
    assert W["rwkv_r_k"].shape[2] == HEAD and nh * HEAD == rw and rw % LANE == 0
    dl, al, gl = W["rwkv_w2"].shape[1], W["rwkv_a2"].shape[1], W["rwkv_g2"].shape[1]
    dlp, alp, glp = _rup(dl, LANE), _rup(al, LANE), _rup(gl, LANE)
    lorap = dlp + alp + glp
    lw_ = W["conv_b"].shape[1]
    nblk, lbw = W["lru_wr"].shape[1], W["lru_wr"].shape[2]
    assert lbw == LANE and nblk * lbw == lw_
    o_xb, o_gate, o_rw = 0, lw_, 2 * lw_
    o_lora = 3 * rw
    rwp = o_lora + lorap
    inp = o_rw + rwp
    nsh_ff = W["ffn_w_gate"].shape[2]
    nshp = _rup(nsh_ff, LANE)
    dffp = N_DEV * nshp
    x2 = x.reshape(n, d)
    tgt2 = loss_target.reshape(n, d)

    gq = _Queue(True, "gather")
    kp = 4 if d % (4 * LANE) == 0 else 1
    gq.push("w_in", W["w_in"][0].astype(BF16), kp, 490)
    gq.push("small", _pack([W[k][0] for k in _SMALL_SHARDED]), 1, 10)
    gq.push("w_out", W["w_out"][0].astype(BF16), 1, 180)
    pad_ff = nshp - nsh_ff
    gq.push("ffn_w_gate", jnp.pad(W["ffn_w_gate"][0].astype(BF16), ((0, 0), (0, pad_ff))), 4, 490, cols=True)
    gq.push("ffn_w_up", jnp.pad(W["ffn_w_up"][0].astype(BF16), ((0, 0), (0, pad_ff))), 4, 490, cols=True)
    gq.push("ffn_w_down", jnp.pad(W["ffn_w_down"][0].astype(BF16), ((0, pad_ff), (0, 0))), 2, 490)
    gmm = functools.partial(_carried, gq, fn=_mm)
    gstage = functools.partial(_carried, gq, fn=_stage_fwd)

    o1 = 3 * rw

    def my_cols(g):
        w_l = _cols_from_shards(g)
        return jnp.concatenate([w_l[:, o1 + dl + al + gl:], w_l[:, :o1], _pad_cols(w_l[:, o1:o1 + dl], dlp),
                                _pad_cols(w_l[:, o1 + dl:o1 + dl + al], alp), _pad_cols(w_l[:, o1 + dl + al:o1 + dl + al + gl], glp)], axis=1)

    mu_l = W["mu_shift"]
    mu = jnp.concatenate([mu_l[:, :o1], _pad_cols(mu_l[:, o1:o1 + dl], dlp), _pad_cols(mu_l[:, o1 + dl:o1 + dl + al], alp),
                          _pad_cols(mu_l[:, o1 + dl + al:], glp)], axis=1)
    r_k = W["rwkv_r_k"].reshape(1, rw)

    tile = _pick(n, (256, 128, 64))
    tile_s = _pick(n, (128, 64))
    ct_seq = _pick(math.gcd(rwp, lw_), (256, 128))
    assert o_rw % ct_seq == 0 and o_gate % lw_ == 0
    ct_h = _pick(rw, (512, 256, 128))
    gi = lax.broadcasted_iota(jnp.int32, (ct_h, ct_h), 0) // HEAD
    gj = lax.broadcasted_iota(jnp.int32, (ct_h, ct_h), 1) // HEAD
    gsum = ((gi == gj).astype(BF16), (ct_h, ct_h), lambda j: (0, 0))
    full = lambda a: (a, a.shape, lambda j: (0,) * a.ndim)
    rowp = lambda a, ct: (a,) + _row(ct)

    u1, = gstage(80, _f_rmsnorm, "norm_mix_fwd", n, d, tile, d, [(x2, 0)], [full(W["norm_mix_g"])], [], [BF16])
    p = None
    for i in range(kp):
        rows = slice(i * (d // kp), (i + 1) * (d // kp))
        g_in = gq.result("w_in", rows.start, rows.stop)
        p = gmm(100, u1[:, rows], my_cols(g_in[:, rows, :]), name="mm_in_%d" % i, add=p)
    w_in = my_cols(gq.result("w_in"))
    g_small = gq.result("small")
    sm_shapes = [W[k][0].shape for k in _SMALL_SHARDED]
    sm = [_unpack(g_small[s], sm_shapes) for s in range(N_DEV)]
    w2, a2, g2, conv_w = [jnp.concatenate([sm[s][i] for s in range(N_DEV)], axis=1) for i in range(4)]
    w_lora = jnp.zeros((lorap, 3 * rw), F32)
    w_lora = w_lora.at[:dl, :rw].set(w2).at[dlp:dlp + al, rw:2 * rw].set(a2).at[dlp + alp:dlp + alp + gl, 2 * rw:].set(g2)
    w_lora = w_lora.astype(BF16)
    ps = _lerp_fwd(p, o_rw, mu, bl, seq, rwp, ct_seq)
    f_lora = functools.partial(_f_lora_act, widths=(dlp, alp))
    lact, = _stage_fwd(f_lora, "lora_act_fwd", n, lorap, tile, lorap, [(ps, o_lora)], [], [], [BF16])
    wag = _mm(lact, w_lora, name="mm_lora")
    pre_par = [rowp(W["rwkv_w0"], ct_h), rowp(W["rwkv_a0"], ct_h), rowp(W["rwkv_k_k"], ct_h), rowp(W["rwkv_k_a"], ct_h)]
    pre_acts = [(ps, rw), (wag, 0), (wag, rw)]
    lw, k2, na, bb = gstage(120, _f_rwkv_pre, "rwkv_pre_fwd", n, rw, tile_s, ct_h, pre_acts, pre_par, [gsum], [F32] * 4)
    ysc, st = _carried(gq, 230, None, lw, k2, None, na, bb, ps, bl, seq, rw, fn=_rwkv_scan_fwd)
    post_par = [rowp(W["rwkv_ln_g"], ct_h), rowp(W["rwkv_ln_b"], ct_h), rowp(r_k, ct_h)]
    post_acts = [(ysc, 0), (ps, 0), (k2, 0), (ps, 2 * rw), (wag, 2 * rw)]
    ya, = gstage(120, _f_rwkv_post, "rwkv_post_fwd", n, rw, tile_s, ct_h, post_acts, post_par, [gsum], [BF16])

    xc = _conv_fwd(p, o_xb, conv_w, W["conv_b"], bl, seq, lw_, ct_seq)
    f_gates = functools.partial(_f_lru_gates, seq=seq)
    blk3 = lambda a: (a[0], (1, LANE, LANE), lambda j: (j, 0, 0))
    gate_par = [blk3(W["lru_wr"]), rowp(W["lru_br"], LANE), blk3(W["lru_wi"]), rowp(W["lru_bi"], LANE), rowp(W["lru_lambda"], LANE)]
    tile_g = _pick(n, (1024, 512, 256, 128, 64))
    a_l, bx = gstage(160, f_gates, "lru_gates_fwd", n, lw_, tile_g, LANE, [(xc, 0)], gate_par, [], [F32, F32])
    ct_l = _pick(lw_, (256, 128))
    h_l = _lru_scan_fwd(a_l, bx, bl, seq, lw_, ct_l)
    lpost_par = [full(W["lru_norm_g"])]
    yb, = _stage_fwd(_f_lru_post, "lru_post_fwd", n, lw_, tile_s, lw_, [(h_l, 0), (p, o_gate)], lpost_par, [], [BF16])

    ycat = jnp.concatenate([ya, yb], axis=1)
    g_out = gq.result("w_out")
    w_out = g_out.reshape(N_DEV * g_out.shape[1], d)
    h1 = gmm(130, ycat, w_out, name="mm_out", add=x2)
    u2, = _stage_fwd(_f_rmsnorm, "norm_ffn_fwd", n, d, tile, d, [(h1, 0)], [full(W["norm_ffn_g"])], [], [BF16])
    w_gate = gq.result("ffn_w_gate")
    ff_gate = gmm(340, u2, w_gate, name="mm_gate", out_dtype=BF16)
    w_up = gq.result("ffn_w_up")
    ff_up = gmm(340, u2, w_up, name="mm_up", out_dtype=BF16)
    ct_f = _pick(dffp, (1024, 512, 256, 128))
    ff_acts = [(ff_gate, 0), (ff_up, 0)]
    act, = _stage_fwd(_f_swiglu, "swiglu_fwd", n, dffp, tile, ct_f, ff_acts, [], [], [BF16])
    gq.flush()
    w_down = gq.result("ffn_w_down").reshape(dffp, d)
    h2 = _mm(act, w_down, name="mm_down", add=h1)

    dh2, dg_final, lsum = _loss_head(h2, W["norm_final_g"].reshape(1, d), tgt2, tile_s)
    loss = lax.psum(lsum[0, 0], ("x", "y", "c"))
    queue = _Queue(False, "exchange")

    cmm = functools.partial(_carried, queue, fn=_mm)
    cstage = functools.partial(_carried, queue, fn=_stage_bwd)
    dh2b = dh2.astype(BF16)
    dact = _mm(dh2b, w_down, name="mm_dact", tb=True, out_dtype=BF16)
    dw_down = _mm(act, dh2b, name="mm_dw_down", ta=True, out_dtype=BF16)
    queue.push("ffn_w_down", dw_down.reshape(N_DEV, nshp, d), 2, 1000)
    (dgate, dup), _ = _stage_bwd(_f_swiglu, "swiglu_bwd", n, dffp, tile, ct_f, ff_acts, [], [], [(dact, 0)], [BF16, BF16])
    du2 = cmm(400, dgate, w_gate, name="mm_du2_gate", tb=True)
    dw_gate = cmm(350, u2, dgate, name="mm_dw_gate", ta=True, out_dtype=BF16)
    queue.push("ffn_w_gate", dw_gate, 8, 1000, cols=True)
    du2 = cmm(400, dup, w_up, name="mm_du2_up", tb=True, add=du2)
    dw_up = cmm(350, u2, dup, name="mm_dw_up", ta=True, out_dtype=BF16)
    queue.push("ffn_w_up", dw_up, 8, 1000, cols=True)
    (dh1,), (dg_ffn,) = cstage(130, _f_rmsnorm, "norm_ffn_bwd", n, d, tile_s, d, [(h1, 0)], [full(W["norm_ffn_g"])], [], [(du2, 0)], [F32],
                               extra_add=(dh2, 0))
    dh1b = dh1.astype(BF16)
    dycat = cmm(135, dh1b, w_out, name="mm_dycat", tb=True)
    dw_out = cmm(170, ycat, dh1b, name="mm_dw_out", ta=True, out_dtype=BF16)
    queue.push("w_out", dw_out.reshape(N_DEV, -1, d), 2, 370)

    (dysc, dr_p, dk2_p, dv_p, dg_g), (dln_g, dln_b, dr_k) = cstage(
        195, _f_rwkv_post, "rwkv_post_bwd", n, rw, tile_s, ct_h, post_acts, post_par, [gsum], [(dycat, 0)], [F32] * 5)
    dr_s, dlw, dk2_s, dv_s, dna, dbb = _carried(queue, 650, lw, k2, na, bb, ps, st, dysc, bl, seq, rw, fn=_rwkv_scan_bwd)
    dk2 = dk2_p + dk2_s
    (dk, dwlin, dalin), (dw0, da0, dk_k, dk_a) = cstage(
        180, _f_rwkv_pre, "rwkv_pre_bwd", n, rw, tile_s, ct_h, pre_acts, pre_par, [gsum], [(dlw, 0), (dk2, 0), (dna, 0), (dbb, 0)], [F32] * 3)
    dwag = jnp.concatenate([dwlin, dalin, dg_g], axis=1).astype(BF16)
    dlact = cmm(75, dwag, w_lora, name="mm_dlact", tb=True)
    dw_lora = cmm(50, lact, dwag, name="mm_dw_lora", ta=True)
    (dps_lora,), _ = _stage_bwd(f_lora, "lora_act_bwd", n, lorap, tile, lorap, [(ps, o_lora)], [], [], [(dlact, 0)], [F32])
    dps = jnp.concatenate([dr_p + dr_s, dk, dv_p + dv_s, dps_lora], axis=1)
    dp_rwkv, dmu = _lerp_bwd(p, o_rw, mu, dps, bl, seq, rwp, ct_seq, BF16)

    (dh_l, dgate_l), (dlru_norm_g,) = _stage_bwd(_f_lru_post, "lru_post_bwd", n, lw_, tile_s, lw_, [(h_l, 0), (p, o_gate)], lpost_par, [],
                                                 [(dycat, rw)], [F32, BF16])
    da_l, dbx = _lru_scan_bwd(a_l, h_l, dh_l, bl, seq, lw_, ct_l)
    (dxc,), (dwr, dbr, dwi, dbi, dlam) = cstage(240, f_gates, "lru_gates_bwd", n, lw_, tile_g, LANE, [(xc, 0)], gate_par, [],
                                                [(da_l, 0), (dbx, 0)], [F32])
    dxb, dconv_w, dconv_b = _conv_bwd(p, o_xb, conv_w, dxc, bl, seq, lw_, ct_seq, BF16)
    sh_full = [dw_lora[:dl, :rw], dw_lora[dlp:dlp + al, rw:2 * rw], dw_lora[dlp + alp:dlp + alp + gl, 2 * rw:], dconv_w]
    assert rw == lw_
    rows_sh = sum(a.shape[0] for a in sh_full)
    pad_sh = _rup(rows_sh, 16) - rows_sh
    queue.push("small_sharded", jnp.pad(jnp.concatenate([_shards_from_cols(a) for a in sh_full], axis=1), ((0, 0), (0, pad_sh), (0, 0))), 1, 40)
    stack_sh = lambda D: jnp.pad(jnp.concatenate([D[k][0] for k in _SMALL_SHARDED], axis=0), ((0, pad_sh), (0, 0)))

    dp = jnp.concatenate([dxb, dgate_l, dp_rwkv], axis=1)
    dw_in = cmm(340, u1, dp, name="mm_dw_in", ta=True, out_dtype=BF16)
    ol = o_rw + o_lora
    dw_in_l = jnp.concatenate([dw_in[:, o_rw:ol], dw_in[:, ol:ol + dl], dw_in[:, ol + dlp:ol + dlp + al],
                               dw_in[:, ol + dlp + alp:ol + dlp + alp + gl], dw_in[:, :o_rw]], axis=1)
    queue.push("w_in", _shards_from_cols(dw_in_l), 8, 970)
    du1 = cmm(390, dp, w_in, name="mm_du1", tb=True)
    (grad_x,), (dg_mix,) = cstage(90, _f_rmsnorm, "norm_mix_bwd", n, d, tile_s, d, [(x2, 0)], [full(W["norm_mix_g"])], [], [(du1, 0)], [F32],
                                  extra_add=(dh1, 0))
    out = {}
    for k in ["ffn_w_down", "ffn_w_gate", "ffn_w_up", "w_out", "w_in"]:
        if k == "w_in":
            queue.flush()
        res = _carried(queue, 250, queue.result(k), W[k][0], M[k][0], V[k][0], "adamw_" + k, fn=_adamw)
        out[k] = [o[None] for o in res]

    dmu_l = jnp.concatenate([dmu[:, :o1], dmu[:, o_lora:o_lora + dl], dmu[:, o_lora + dlp:o_lora + dlp + al],
                             dmu[:, o_lora + dlp + alp:o_lora + dlp + alp + gl]], axis=1)
    small_g = {"norm_mix_g": dg_mix, "mu_shift": dmu_l, "rwkv_w0": dw0, "rwkv_a0": da0, "rwkv_k_k": dk_k, "rwkv_k_a": dk_a,
               "rwkv_r_k": dr_k.reshape(W["rwkv_r_k"].shape), "rwkv_ln_g": dln_g, "rwkv_ln_b": dln_b, "conv_b": dconv_b,
               "lru_wr": dwr[None], "lru_br": dbr, "lru_wi": dwi[None], "lru_bi": dbi, "lru_lambda": dlam, "lru_norm_g": dlru_norm_g,
               "norm_ffn_g": dg_ffn, "norm_final_g": dg_final.reshape(W["norm_final_g"].shape)}
    gq.push("small_grads", _pack([small_g[k] for k in _SMALL]), 1, 50)
    gq.flush()
    pk = lambda D: _pack([D[k] for k in _SMALL])
    res = _adamw(gq.result("small_grads"), pk(W), pk(M), pk(V), "adamw_small")
    shapes = [W[k].shape for k in _SMALL]
    for i, r in enumerate(res):
        for k, a in zip(_SMALL, _unpack(r, shapes)):
            out.setdefault(k, [None] * 4)[i] = a
    res = _adamw(queue.result("small_sharded"), stack_sh(W), stack_sh(M), stack_sh(V), "adamw_small_sharded")
    for i, r in enumerate(res):
        o = 0
        for k in _SMALL_SHARDED:
            rows = W[k].shape[1]
            out.setdefault(k, [None] * 4)[i] = r[o:o + rows][None]
            o += rows
    return loss, grad_x.reshape(x.shape), out


def kernel(x, norm_mix_g, w_in, mu_shift, rwkv_w0, rwkv_w2, rwkv_a0, rwkv_a2, rwkv_g2, rwkv_k_k, rwkv_k_a, rwkv_r_k, rwkv_ln_g, rwkv_ln_b, conv_w, conv_b, lru_wr, lru_br, lru_wi, lru_bi, lru_lambda, lru_norm_g, w_out, norm_ffn_g, ffn_w_gate, ffn_w_up, ffn_w_down, norm_final_g, loss_target, m_norm_mix_g, m_w_in, m_mu_shift, m_rwkv_w0, m_rwkv_w2, m_rwkv_a0, m_rwkv_a2, m_rwkv_g2, m_rwkv_k_k, m_rwkv_k_a, m_rwkv_r_k, m_rwkv_ln_g, m_rwkv_ln_b, m_conv_w, m_conv_b, m_lru_wr, m_lru_br, m_lru_wi, m_lru_bi, m_lru_lambda, m_lru_norm_g, m_w_out, m_norm_ffn_g, m_ffn_w_gate, m_ffn_w_up, m_ffn_w_down, m_norm_final_g, v_norm_mix_g, v_w_in, v_mu_shift, v_rwkv_w0, v_rwkv_w2, v_rwkv_a0, v_rwkv_a2, v_rwkv_g2, v_rwkv_k_k, v_rwkv_k_a, v_rwkv_r_k, v_rwkv_ln_g, v_rwkv_ln_b, v_conv_w, v_conv_b, v_lru_wr, v_lru_br, v_lru_wi, v_lru_bi, v_lru_lambda, v_lru_norm_g, v_w_out, v_norm_ffn_g, v_ffn_w_gate, v_ffn_w_up, v_ffn_w_down, v_norm_final_g):
    a = locals()
    W = {k: a[k] for k in _WEIGHTS}
    M = {k: a["m_" + k] for k in _WEIGHTS}
    V = {k: a["v_" + k] for k in _WEIGHTS}
    loss, grad_x, out = _step(W, M, V, x, loss_target)
    res = [loss, grad_x]
    for i in range(4):
        res += [out[k][i].reshape(W[k].shape) for k in _WEIGHTS]
    return tuple(res)
```

```python
import functools
import math

import jax
import jax.numpy as jnp
from jax import lax
from jax.experimental import pallas as pl
from jax.experimental.pallas import tpu as pltpu

F32 = jnp.float32
BF16 = jnp.bfloat16
HI = lax.Precision.HIGHEST
MESH = pl.DeviceIdType.MESH

N_DEV = 8
LANE = 128
HEAD = 64
MM_MAX_TK = 4096
CARRY_FILL = 1.2
SCAN_CHUNK = 64
SCAN_GROUP = 2
SCAN_PAIRS = 8
VMEM_LIMIT = 56 * 1024 * 1024

NORM_EPS = 1e-6
GN_EPS = 64e-5
LRU_C = 8.0
ADAM_LR, ADAM_B1, ADAM_B2, ADAM_EPS, ADAM_WD, ADAM_STEP = 0.001, 0.9, 0.999, 1e-08, 0.01, 10


def _pick(n, cands):
    for c in cands:
        if n % c == 0:
            return c
    return n


def _rup(n, m):
    return (n + m - 1) // m * m


def _cparams(dims):
    return pltpu.CompilerParams(dimension_semantics=dims, vmem_limit_bytes=VMEM_LIMIT)


def _sigmoid(x):
    return 1.0 / (1.0 + jnp.exp(-x))


def _softplus(z):
    return jnp.maximum(z, 0.0) + jnp.log(1.0 + jnp.exp(-jnp.abs(z)))


def _neg_expm1(x):
    series = -(x * (1.0 + 0.5 * x * (1.0 + (x / 3.0) * (1.0 + 0.25 * x))))
    return jnp.where(jnp.abs(x) < 0.03, series, 1.0 - jnp.exp(x))


def _gelu(x):
    return 0.5 * x * (1.0 + jnp.tanh(0.7978845608028654 * (x + 0.044715 * (x * x * x))))


def _dot(a, b, dims, precision=None):
    return lax.dot_general(a, b, (dims, ((), ())), precision=precision, preferred_element_type=F32)


def _nn(a, b, precision=None):
    return _dot(a, b, ((1,), (0,)), precision)


def _nt(a, b, precision=None):
    return _dot(a, b, ((1,), (1,)), precision)


def _tn(a, b, precision=None):
    return _dot(a, b, ((0,), (0,)), precision)


def _coords():
    return lax.axis_index("x"), lax.axis_index("y"), lax.axis_index("c")


class _Carry:
    def __init__(self, tensors, items):
        self.tensors, self.items = tensors, items
        nt, ni = len(tensors), len(items)
        any_spec = pl.BlockSpec(memory_space=pl.ANY)
        self.args = [t[0] for t in tensors] + [t[1] for t in tensors]
        self.in_specs = [any_spec] * (2 * nt)
        self.out_specs = [any_spec] * nt
        self.out_shape = [jax.ShapeDtypeStruct(t[1].shape, t[1].dtype) for t in tensors]
        self.scratch = [pltpu.SemaphoreType.DMA((ni, N_DEV - 1)), pltpu.SemaphoreType.DMA((ni, N_DEV - 1)), pltpu.SemaphoreType.DMA((ni,))]

    def aliases(self, first_in, first_out):
        nt = len(self.tensors)
        return {first_in + nt + t: first_out + t for t in range(nt)}

    def _slot(self, ref, t, idx, win):
        cw = self.tensors[t][2]
        return ref.at[idx, win] if cw is None else ref.at[win, pl.ds(pl.multiple_of(idx * cw, LANE), cw)]

    def _copies(self, src_refs, dst_refs, sems):
        send_sems, recv_sems, local_sems = sems
        x, y, c = _coords()
        my = 4 * x + 2 * y + c
        out = []
        for n, (t, r0, rows) in enumerate(self.items):
            win = pl.ds(r0, rows)
            out.append(pltpu.make_async_copy(self._slot(src_refs[t], t, my, win), dst_refs[t].at[my, win], local_sems.at[n]))
            for k in range(1, N_DEV):
                px, py, pc = x ^ ((k >> 2) & 1), y ^ ((k >> 1) & 1), c ^ (k & 1)
                out.append(pltpu.make_async_remote_copy(
                    src_ref=self._slot(src_refs[t], t, 4 * px + 2 * py + pc, win), dst_ref=dst_refs[t].at[my, win],
                    send_sem=send_sems.at[n, k - 1], recv_sem=recv_sems.at[n, k - 1],
                    device_id=(px, py, pc), device_id_type=MESH))
        return out

    def hook(self, step, last, src_refs, dst_refs, sems):
        if last == 0:
            for cp in self._copies(src_refs, dst_refs, sems):
                cp.start()
            for cp in self._copies(src_refs, dst_refs, sems):
                cp.wait()
            return

        @pl.when(step == 0)
        def _():
            for cp in self._copies(src_refs, dst_refs, sems):
                cp.start()

        @pl.when(step == last)
        def _():
            for cp in self._copies(src_refs, dst_refs, sems):
                cp.wait()


class _GatherCarry(_Carry):
    def hook(self, step, last, src_refs, dst_refs, sems):
        send_sems, recv_sems, local_sems = sems
        x, y, c = _coords()
        me, sibling = (x, y, c), (x, y, 1 - c)
        chips = [(1 - x, y), (x, 1 - y), (1 - x, 1 - y)]

        def per_item(fn):
            for n, (t, r0, rows) in enumerate(self.items):
                win = pl.ds(r0, rows)

                def copy(k, block, to, own=False, n=n, t=t, win=win):
                    slot = self._slot(dst_refs[t], t, 4 * block[0] + 2 * block[1] + block[2], win)
                    return pltpu.make_async_remote_copy(
                        src_ref=src_refs[t].at[win] if own else slot, dst_ref=slot,
                        send_sem=send_sems.at[n, k], recv_sem=recv_sems.at[n, k], device_id=to, device_id_type=MESH)

                mine = pltpu.make_async_copy(src_refs[t].at[win], self._slot(dst_refs[t], t, 4 * x + 2 * y + c, win), local_sems.at[n])
                first = [copy(0, me, sibling, own=True)] + [copy(1 + j, me, (*chip, c), own=True) for j, chip in enumerate(chips)]
                fn(copy, mine, first)

        def begin(copy, mine, first):
            mine.start()
            for cp in first:
                cp.start()

        def pass_on(copy, mine, first):
            for j, chip in enumerate(chips):
                copy(1 + j, (*chip, c), me).wait_recv()
                copy(4 + j, (*chip, c), sibling).start()

        def finish(copy, mine, first):
            copy(0, sibling, me).wait_recv()
            for j, chip in enumerate(chips):
                copy(4 + j, (*chip, 1 - c), me).wait_recv()
            for cp in first + [copy(4 + j, (*chip, c), sibling) for j, chip in enumerate(chips)]:
                cp.wait_send()
            mine.wait()

        if last == 0:
            for fn in (begin, pass_on, finish):
                per_item(fn)
            return
        late = max(1, (7 * last) // 8)
        for at, fn in ((0, begin), (late, pass_on), (last, finish)):
            pl.when(step == at)(functools.partial(per_item, fn))


def _mm(a, b, *, name, ta=False, tb=False, out_dtype=F32, add=None, tiles=None, carry=None):
    M, K = (a.shape[1], a.shape[0]) if ta else a.shape
    N = b.shape[0] if tb else b.shape[1]
    assert (b.shape[1] if tb else b.shape[0]) == K, (a.shape, b.shape, ta, tb)
    tk = max(t for t in range(LANE, min(K, MM_MAX_TK) + 1, LANE) if K % t == 0)
    tm, tn, tk = tiles or (_pick(M, (1024, 512, 256, 128)), _pick(N, (512, 256, 128)), tk)
    nk = K // tk
    dims = ((0 if ta else 1,), (1 if tb else 0,))

    n_in = 2 + (add is not None)
    nt = len(carry.tensors) if carry else 0
    gi, gj = M // tm, N // tn

    def kern(*refs):
        a_ref, b_ref = refs[:2]
        add_ref = refs[2] if add is not None else None
        o_ref = refs[n_in + 2 * nt]
        scr = refs[n_in + 3 * nt + 1:]
        if carry:
            step = (pl.program_id(0) * gj + pl.program_id(1)) * nk + pl.program_id(2)
            carry.hook(step, gi * gj * nk - 1, refs[n_in:n_in + nt], refs[n_in + 2 * nt + 1:n_in + 3 * nt + 1], scr[-3:])

        def finish(r):
            if add is not None:
                r = r + add_ref[...].astype(F32)
            o_ref[...] = r.astype(o_ref.dtype)

        if nk == 1:
            finish(_dot(a_ref[...], b_ref[...], dims))
            return
        acc = scr[0]
        k = pl.program_id(2)

        @pl.when(k == 0)
        def _():
            acc[...] = jnp.zeros_like(acc)

        acc[...] += _dot(a_ref[...], b_ref[...], dims)

        @pl.when(k == nk - 1)
        def _():
            finish(acc[...])

    a_spec = pl.BlockSpec((tk, tm), lambda i, j, k: (k, i)) if ta else pl.BlockSpec((tm, tk), lambda i, j, k: (i, k))
    b_spec = pl.BlockSpec((tn, tk), lambda i, j, k: (j, k)) if tb else pl.BlockSpec((tk, tn), lambda i, j, k: (k, j))
    o_spec = pl.BlockSpec((tm, tn), lambda i, j, k: (i, j))
    in_specs = [a_spec, b_spec] + ([o_spec] if add is not None else [])
    args = (a, b) + ((add,) if add is not None else ())
    scratch = [pltpu.VMEM((tm, tn), F32)] if nk > 1 else []
    o_shape = jax.ShapeDtypeStruct((M, N), out_dtype)
    if not carry:
        return pl.pallas_call(
            kern, name=name, grid=(gi, gj, nk), in_specs=in_specs, out_specs=o_spec, out_shape=o_shape, scratch_shapes=scratch,
            compiler_params=_cparams(("parallel", "parallel", "arbitrary")),
        )(*args)
    res = pl.pallas_call(
        kern, name=name, grid=(gi, gj, nk), in_specs=in_specs + carry.in_specs, out_specs=[o_spec] + carry.out_specs,
        out_shape=[o_shape] + carry.out_shape, scratch_shapes=scratch + carry.scratch,
        input_output_aliases=carry.aliases(n_in, 1), compiler_params=_cparams(("arbitrary", "arbitrary", "arbitrary")),
    )(*args, *carry.args)
    return res[0], list(res[1:])


def _stage_specs(acts, params, consts, tile, ct):
    act_specs = [pl.BlockSpec((tile, ct), functools.partial(lambda j, i, o: (i, o + j), o=off // ct)) for _, off in acts]
    par_specs = [pl.BlockSpec(bs, functools.partial(lambda j, i, im: im(j), im=im)) for _, bs, im in params]
    con_specs = [pl.BlockSpec(bs, functools.partial(lambda j, i, im: im(j), im=im)) for _, bs, im in consts]
    return act_specs, par_specs, con_specs


def _stage_fwd(f, name, n_rows, width, tile, ct, acts, params, consts, out_dtypes, carry=None):
    for _, off in acts:
        assert off % ct == 0
    na, npar, nc, no = len(acts), len(params), len(consts), len(out_dtypes)
    n_in = na + npar + nc
    nt = len(carry.tensors) if carry else 0
    gj, gi = width // ct, n_rows // tile

    def kern(*refs):
        if carry:
            step = pl.program_id(0) * gi + pl.program_id(1)
            carry.hook(step, gj * gi - 1, refs[n_in:n_in + nt], refs[n_in + 2 * nt + no:n_in + 3 * nt + no], refs[n_in + 3 * nt + no:])
        a = [r[...].astype(F32) for r in refs[:na]]
        p = [r[...] for r in refs[na:na + npar]]
        c = [r[...] for r in refs[na + npar:n_in]]
        outs = f(a, p, c, pl.program_id(1) * tile)
        for r, o in zip(refs[n_in + 2 * nt:n_in + 2 * nt + no], outs):
            r[...] = o.astype(r.dtype)

    act_specs, par_specs, con_specs = _stage_specs(acts, params, consts, tile, ct)
    o_spec = pl.BlockSpec((tile, ct), lambda j, i: (i, j))
    in_specs = act_specs + par_specs + con_specs
    out_shape = [jax.ShapeDtypeStruct((n_rows, width), d) for d in out_dtypes]
    args = [a for a, _ in acts] + [p for p, _, _ in params] + [c for c, _, _ in consts]
    if not carry:
        return tuple(pl.pallas_call(
            kern, name=name, grid=(gj, gi), in_specs=in_specs, out_specs=[o_spec] * no, out_shape=out_shape,
            compiler_params=_cparams(("parallel", "parallel")),
        )(*args))
    res = pl.pallas_call(
        kern, name=name, grid=(gj, gi), in_specs=in_specs + carry.in_specs, out_specs=[o_spec] * no + carry.out_specs,
        out_shape=out_shape + carry.out_shape, scratch_shapes=carry.scratch, input_output_aliases=carry.aliases(n_in, no),
        compiler_params=_cparams(("arbitrary", "arbitrary")),
    )(*args, *carry.args)
    return tuple(res[:no]), list(res[no:])


def _stage_bwd(f, name, n_rows, width, tile, ct, acts, params, consts, couts, dact_dtypes, extra_add=None, carry=None):
    na, npar, nc, no = len(acts), len(params), len(consts), len(couts)
    nx = 0 if extra_add is None else 1
    nt = len(carry.tensors) if carry else 0
    n_in = na + npar + nc + no + nx
    gj, gi = width // ct, n_rows // tile

    def kern(*refs):
        if carry:
            step = pl.program_id(0) * gi + pl.program_id(1)
            n_out = n_in + 2 * nt + na + npar
            carry.hook(step, gj * gi - 1, refs[n_in:n_in + nt], refs[n_out:n_out + nt], refs[n_out + nt:])
        a = [r[...].astype(F32) for r in refs[:na]]
        p = [r[...] for r in refs[na:na + npar]]
        c = [r[...] for r in refs[na + npar:na + npar + nc]]
        base = na + npar + nc
        co = [r[...].astype(F32) for r in refs[base:base + no]]
        base += no
        x_refs = refs[base:base + nx]
        base += nx + 2 * nt
        da_refs = refs[base:base + na]
        dp_refs = refs[base + na:base + na + npar]
        row0 = pl.program_id(1) * tile
        _, vjp = jax.vjp(lambda aa, pp: tuple(f(aa, pp, c, row0)), a, p)
        da, dp = vjp(tuple(co))
        for k, (r, d) in enumerate(zip(da_refs, da)):
            if k == 0 and nx:
                d = d + x_refs[0][...].astype(F32)
            r[...] = d.astype(r.dtype)
        first = pl.program_id(1) == 0
        for r, d in zip(dp_refs, dp):
            @pl.when(first)
            def _(r=r, d=d):
                r[...] = d

            @pl.when(jnp.logical_not(first))
            def _(r=r, d=d):
                r[...] += d

    act_specs, par_specs, con_specs = _stage_specs(acts, params, consts, tile, ct)
    t_spec = pl.BlockSpec((tile, ct), lambda j, i: (i, j))
    co_specs = [pl.BlockSpec((tile, ct), functools.partial(lambda j, i, o: (i, o + j), o=off // ct)) for _, off in couts]
    x_specs = [] if extra_add is None else [pl.BlockSpec((tile, ct), functools.partial(lambda j, i, o: (i, o + j), o=extra_add[1] // ct))]
    x_args = [] if extra_add is None else [extra_add[0]]
    in_specs = act_specs + par_specs + con_specs + co_specs + x_specs
    out_specs = [t_spec] * na + par_specs
    out_shape = [jax.ShapeDtypeStruct((n_rows, width), d) for d in dact_dtypes] + [jax.ShapeDtypeStruct(p.shape, F32) for p, _, _ in params]
    args = [a for a, _ in acts] + [p for p, _, _ in params] + [c for c, _, _ in consts] + [c for c, _ in couts] + x_args
    if not carry:
        outs = pl.pallas_call(
            kern, name=name, grid=(gj, gi), in_specs=in_specs, out_specs=out_specs, out_shape=out_shape,
            compiler_params=_cparams(("parallel", "arbitrary")),
        )(*args)
        return tuple(outs[:na]), tuple(outs[na:])
    outs = pl.pallas_call(
        kern, name=name, grid=(gj, gi), in_specs=in_specs + carry.in_specs, out_specs=out_specs + carry.out_specs,
        out_shape=out_shape + carry.out_shape, scratch_shapes=carry.scratch, input_output_aliases=carry.aliases(n_in, na + npar),
        compiler_params=_cparams(("arbitrary", "arbitrary")),
    )(*args, *carry.args)
    return tuple(outs[:na]), tuple(outs[na:na + npar]), list(outs[na + npar:])


def _row(ct):
    return (1, ct), (lambda j: (0, j))


def _f_rmsnorm(a, p, c, row0):
    x, = a
    g, = p
    return (x * lax.rsqrt(jnp.mean(x * x, axis=-1, keepdims=True) + NORM_EPS) * g,)


def _f_lora_act(a, p, c, row0, widths):
    x, = a
    dl, al = widths
    col = lax.broadcasted_iota(jnp.int32, x.shape, 1)
    return (jnp.where(col < dl, jnp.tanh(x), jnp.where(col < dl + al, x, _sigmoid(x))),)


def _head_sums_raw(x, ones):
    hi = x.astype(BF16)
    lo = (x - hi.astype(F32)).astype(BF16)
    return _nn(hi, ones) + _nn(lo, ones)


@jax.custom_vjp
def _head_sums(x, ones):
    return _head_sums_raw(x, ones)


_head_sums.defvjp(lambda x, ones: (_head_sums_raw(x, ones), ones),
                  lambda ones, ct: (_head_sums_raw(ct, ones), jnp.zeros_like(ones)))


def _f_rwkv_pre(a, p, c, row0):
    k, wlin, alin = a
    w0, a0, k_k, k_a = p
    gsum, = c
    w = -_softplus(-(w0 + wlin)) - 0.5
    lw = -jnp.exp(w)
    alpha = _sigmoid(a0 + alin)
    kk = k * k_k
    ss = _head_sums(kk * kk, gsum)
    kk = kk * lax.rsqrt(jnp.maximum(ss, 1e-24))
    k2 = k * (1.0 + (alpha - 1.0) * k_a)
    return lw, k2, -kk, kk * alpha


def _f_rwkv_post(a, p, c, row0):
    y, r, k2, v, g = a
    ln_g, ln_b, r_k = p
    gsum, = c
    inv = 1.0 / HEAD
    mean = _head_sums(y, gsum) * inv
    yc = y - mean
    var = _head_sums(yc * yc, gsum) * inv
    yn = yc * lax.rsqrt(var + GN_EPS) * ln_g + ln_b
    bonus = _head_sums(r * k2 * r_k, gsum)
    return ((yn + bonus * v) * g,)


def _f_lru_gates(a, p, c, row0, seq):
    xc, = a
    wr, br, wi, bi, lam = p
    xb = xc.astype(BF16)
    rg = _sigmoid(_nn(xb, wr[0].astype(BF16)) + br)
    ig = _sigmoid(_nn(xb, wi[0].astype(BF16)) + bi)
    log_a = -LRU_C * rg * _softplus(-lam)
    a_t = jnp.exp(log_a)
    mult = jnp.sqrt(_neg_expm1(2.0 * log_a))
    row = row0 + lax.broadcasted_iota(jnp.int32, xc.shape, 0)
    mult = jnp.where(row % seq == 0, 1.0, mult)
    return a_t, mult * ig * xc


def _f_lru_post(a, p, c, row0):
    h, gate = a
    g, = p
    y = h * _gelu(gate)
    return (y * lax.rsqrt(jnp.mean(y * y, axis=-1, keepdims=True) + NORM_EPS) * g,)


def _f_swiglu(a, p, c, row0):
    gate, up = a
    return (gate * _sigmoid(gate) * up,)


def _shift_down(x, s, row):
    return jnp.where(row >= s, pltpu.roll(x, s, 0), 0.0)


def _shift_up(x, s, row):
    n = x.shape[0]
    return jnp.where(row < n - s, pltpu.roll(x, n - s, 0), 0.0)


def _seq_call(kern, name, bl, seq, width, ct, ins, outs, acc_outs=()):
    def spec(off, rows):
        if rows is None:
            return pl.BlockSpec((seq, ct), functools.partial(lambda j, b, o: (b, o + j), o=off // ct))
        return pl.BlockSpec((rows, ct), lambda j, b: (0, j))

    in_specs = [spec(off, rows) for _, off, rows in ins]
    out_specs = [spec(0, None) for _ in outs] + [spec(0, rows) for _, rows in acc_outs]
    out_shape = [jax.ShapeDtypeStruct((bl * seq, width), d) for d in outs] + [jax.ShapeDtypeStruct((rows, width), F32) for _, rows in acc_outs]
    res = pl.pallas_call(
        kern, name=name, grid=(width // ct, bl), in_specs=in_specs, out_specs=out_specs, out_shape=out_shape,
        compiler_params=_cparams(("parallel", "arbitrary")),
    )(*[a for a, _, _ in ins])
    return tuple(res)


def _acc(ref, val):
    first = pl.program_id(1) == 0

    @pl.when(first)
    def _():
        ref[...] = val

    @pl.when(jnp.logical_not(first))
    def _():
        ref[...] += val


def _lerp_fwd(p, off, mu, bl, seq, width, ct):
    def kern(p_ref, mu_ref, o_ref):
        x = p_ref[...]
        row = lax.broadcasted_iota(jnp.int32, x.shape, 0)
        o_ref[...] = x + (_shift_down(x, 1, row) - x) * mu_ref[...]

    return _seq_call(kern, "lerp_fwd", bl, seq, width, ct, [(p, off, None), (mu, 0, 1)], [F32])[0]


def _lerp_bwd(p, off, mu, dps, bl, seq, width, ct, out_dtype):
    def kern(p_ref, mu_ref, d_ref, dp_ref, dmu_ref):
        x = p_ref[...]
        d = d_ref[...].astype(F32)
        m = mu_ref[...]
        row = lax.broadcasted_iota(jnp.int32, x.shape, 0)
        dp_ref[...] = (d * (1.0 - m) + _shift_up(d * m, 1, row)).astype(dp_ref.dtype)
        _acc(dmu_ref, jnp.sum(d * (_shift_down(x, 1, row) - x), axis=0, keepdims=True))

    return _seq_call(kern, "lerp_bwd", bl, seq, width, ct, [(p, off, None), (mu, 0, 1), (dps, 0, None)], [out_dtype], [(None, 1)])


def _conv_fwd(p, off, cw, cb, bl, seq, width, ct):
    nw = cw.shape[0]

    def kern(x_ref, w_ref, b_ref, o_ref):
        x = x_ref[...]
        row = lax.broadcasted_iota(jnp.int32, x.shape, 0)
        acc = b_ref[...] + x * w_ref[pl.ds(nw - 1, 1), :]
        for s in range(1, nw):
            acc = acc + _shift_down(x, s, row) * w_ref[pl.ds(nw - 1 - s, 1), :]
        o_ref[...] = acc

    return _seq_call(kern, "conv_fwd", bl, seq, width, ct, [(p, off, None), (cw, 0, nw), (cb, 0, 1)], [F32])[0]


def _conv_bwd(p, off, cw, dxc, bl, seq, width, ct, out_dtype):
    nw = cw.shape[0]

    def kern(x_ref, w_ref, d_ref, dx_ref, dw_ref, db_ref):
        x = x_ref[...]
        d = d_ref[...]
        row = lax.broadcasted_iota(jnp.int32, x.shape, 0)
        wrow = lax.broadcasted_iota(jnp.int32, dw_ref.shape, 0)
        dx = d * w_ref[pl.ds(nw - 1, 1), :]
        dw = jnp.where(wrow == nw - 1, jnp.sum(d * x, axis=0, keepdims=True), 0.0)
        for s in range(1, nw):
            dx = dx + _shift_up(d, s, row) * w_ref[pl.ds(nw - 1 - s, 1), :]
            dw = jnp.where(wrow == nw - 1 - s, jnp.sum(d * _shift_down(x, s, row), axis=0, keepdims=True), dw)
        dx_ref[...] = dx.astype(dx_ref.dtype)
        _acc(dw_ref, dw)
        _acc(db_ref, jnp.sum(d, axis=0, keepdims=True))

    return _seq_call(kern, "conv_bwd", bl, seq, width, ct, [(p, off, None), (cw, 0, nw), (dxc, 0, None)], [out_dtype], [(None, nw), (None, 1)])


def _lru_scan_fwd(a, bx, bl, seq, width, ct):
    def kern(a_ref, b_ref, h_ref):
        av = a_ref[...]
        bv = b_ref[...]
        row = lax.broadcasted_iota(jnp.int32, av.shape, 0)
        d = 1
        while d < seq:
            a_sh = jnp.where(row >= d, pltpu.roll(av, d, 0), 1.0)
            b_sh = jnp.where(row >= d, pltpu.roll(bv, d, 0), 0.0)
            bv = av * b_sh + bv
            av = av * a_sh
            d *= 2
        h_ref[...] = bv

    return _seq_call(kern, "lru_scan_fwd", bl, seq, width, ct, [(a, 0, None), (bx, 0, None)], [F32])[0]


def _lru_scan_bwd(a, h, dh, bl, seq, width, ct):
    def kern(a_ref, h_ref, d_ref, da_ref, db_ref):
        row = lax.broadcasted_iota(jnp.int32, a_ref.shape, 0)
        al = _shift_up(a_ref[...], 1, row)
        g = d_ref[...]
        d = 1
        while d < seq:
            keep = row < seq - d
            al_sh = jnp.where(keep, pltpu.roll(al, seq - d, 0), 1.0)
            g_sh = jnp.where(keep, pltpu.roll(g, seq - d, 0), 0.0)
            g = al * g_sh + g
            al = al * al_sh
            d *= 2
        db_ref[...] = g
        da_ref[...] = g * _shift_down(h_ref[...], 1, row)

    return _seq_call(kern, "lru_scan_bwd", bl, seq, width, ct, [(a, 0, None), (h, 0, None), (dh, 0, None)], [F32, F32])


_FORMS = {"nn": ((1,), (0,)), "nt": ((1,), (1,)), "tn": ((0,), (0,))}
_FORM_GRADS = {"nn": (("nt", "g", "b"), ("tn", "a", "g")),
               "nt": (("nn", "g", "b"), ("tn", "g", "a")),
               "tn": (("nt", "b", "g"), ("nn", "a", "g"))}


def _split_bf16(x):
    hi = x.astype(BF16)
    return hi, (x - hi.astype(F32)).astype(BF16)


def _pdot_raw(a, b, form, passes):
    dims = _FORMS[form]
    if passes == 1:
        return _dot(a.astype(BF16), b.astype(BF16), dims)
    ah, al = _split_bf16(a)
    bh, bl = _split_bf16(b)
    return _dot(ah, bh, dims) + (_dot(ah, bl, dims) + _dot(al, bh, dims))


@functools.partial(jax.custom_vjp, nondiff_argnums=(2, 3))
def _pdot(a, b, form, passes):
    return _pdot_raw(a, b, form, passes)


def _pdot_fwd(a, b, form, passes):
    return _pdot_raw(a, b, form, passes), (a, b)


def _pdot_bwd(form, passes, res, g):
    vals = {"a": res[0], "b": res[1], "g": g}
    (fa, xa, ya), (fb, xb, yb) = _FORM_GRADS[form]
    return _pdot_raw(vals[xa], vals[ya], fa, passes), _pdot_raw(vals[xb], vals[yb], fb, passes)


_pdot.defvjp(_pdot_fwd, _pdot_bwd)


def _neumann_raw(a_list, n_levels, passes):
    eye = (lax.broadcasted_iota(jnp.int32, a_list[0].shape, 0) == lax.broadcasted_iota(jnp.int32, a_list[0].shape, 1)).astype(F32)
    pw = list(a_list)
    x = [eye + a for a in a_list]
    for _ in range(n_levels):
        pw = [_pdot_raw(p, p, "nn", passes) for p in pw]
        x = [xi + _pdot_raw(xi, p, "nn", passes) for xi, p in zip(x, pw)]
    return x


@functools.partial(jax.custom_vjp, nondiff_argnums=(1, 2))
def _neumann_inverse(a_list, n_levels, passes):
    return _neumann_raw(a_list, n_levels, passes)


def _neumann_fwd(a_list, n_levels, passes):
    x = _neumann_raw(a_list, n_levels, passes)
    return x, x


def _neumann_bwd(n_levels, passes, x, ct):
    return ([_pdot_raw(xi, _pdot_raw(c, xi, "nt", passes), "tn", passes) for xi, c in zip(x, ct)],)


_neumann_inverse.defvjp(_neumann_fwd, _neumann_bwd)


def _scan_chunk2(S0, r, lw, k, v, a, b, p_main=1, p_inv=3):
    y, s = _scan_block([S0], [[(r, lw, k, v, a, b)]], p_main, p_inv)
    return y[0][0], s[0]


def _scan_block(states, units, p_main=1, p_inv=1):
    C = units[0][0][0].shape[0]
    C2 = 2 * C
    ri = lax.broadcasted_iota(jnp.int32, (C, C), 0)
    ci = lax.broadcasted_iota(jnp.int32, (C, C), 1)
    tri = (ri >= ci).astype(F32)
    i2 = lax.broadcasted_iota(jnp.int32, (C2, C2), 0)
    j2 = lax.broadcasted_iota(jnp.int32, (C2, C2), 1)
    same = (i2 // C) == (j2 // C)
    strict = jnp.logical_and(same, (i2 % C) > (j2 % C))
    incl = jnp.logical_and(same, (i2 % C) >= (j2 % C))
    eye = (i2 == j2).astype(F32)
    lane = lax.broadcasted_iota(jnp.int32, (1, LANE), 1)
    m0, m1 = (lane < HEAD).astype(F32), (lane >= HEAD).astype(F32)
    stack = lambda z: jnp.concatenate([z * m0, z * m1], axis=0)
    ids = [(i, g) for g in range(len(units[0])) for i in range(len(units))]

    pre = {}
    for i, g in ids:
        r, lw, k, v, a, b = units[i][g]
        cs = _nn(tri, lw, HI)
        p_incl = jnp.exp(cs)
        p_rec = jnp.exp(-cs)
        xr = jnp.concatenate([stack(a * jnp.exp(cs - lw)), stack(r * p_incl)], axis=0)
        bk = jnp.concatenate([stack(b * p_rec), stack(k * p_rec)], axis=0)
        pre[i, g] = (xr, bk, stack(v), jnp.exp(jnp.sum(lw, axis=0, keepdims=True)))
    gm = {u: _pdot(pre[u][0], pre[u][1], "nt", p_main) for u in ids}
    a_ak = {u: jnp.where(strict, gm[u][:C2, C2:], 0.0) for u in ids}
    r_bk = {u: jnp.concatenate([jnp.where(incl, gm[u][C2:, :C2], 0.0), jnp.where(incl, gm[u][C2:, C2:], 0.0)], axis=1) for u in ids}
    a_ab = [jnp.where(strict, gm[u][:C2, :C2], 0.0) for u in ids]
    x = dict(zip(ids, _neumann_inverse(a_ab, int(math.log2(C)) - 1, p_inv)))
    akv = {u: _pdot(a_ak[u], pre[u][2], "nn", p_main) for u in ids}

    states = list(states)
    pairs = range(len(units))
    ys = [[None] * len(units[0]) for _ in units]
    for g in range(len(units[0])):
        xs = [_pdot(pre[i, g][0], states[i], "nt", p_main) for i in pairs]
        us = [_pdot(x[i, g], xs[i][:C2] + akv[i, g], "nn", p_inv) for i in pairs]
        uv = [jnp.concatenate([us[i], pre[i, g][2]], axis=0) for i in pairs]
        y2 = [xs[i][C2:] + _pdot(r_bk[i, g], uv[i], "nn", p_main) for i in pairs]
        for i in pairs:
            ys[i][g] = y2[i][:C] + y2[i][C:]
        states = [(states[i] + _pdot(uv[i], pre[i, g][1], "tn", p_main)) * pre[i, g][3] for i in pairs]
    return ys, states


def _scan_dims(seq, rw):
    G = _pick(seq // SCAN_CHUNK, (SCAN_GROUP, 2, 1))
    NP = _pick(rw // LANE, (SCAN_PAIRS, 4, 2, 1))
    C = SCAN_CHUNK * G
    return SCAN_CHUNK, G, NP, C, seq // C, rw // (NP * LANE)


def _rwkv_scan_fwd(r, lw, k2, v, na, bb, p, bl, seq, rw, carry=None):
    cs, G, NP, C, nc, nhg = _scan_dims(seq, rw)
    nt = len(carry.tensors) if carry else 0

    def kern(*refs):
        in_refs = refs[:6]
        y_ref, st_ref = refs[6 + 2 * nt:8 + 2 * nt]
        s_scr = refs[8 + 3 * nt]
        if carry:
            step = (pl.program_id(0) * nhg + pl.program_id(1)) * nc + pl.program_id(2)
            carry.hook(step, bl * nhg * nc - 1, refs[6:6 + nt], refs[8 + 2 * nt:8 + 3 * nt], refs[9 + 3 * nt:])

        @pl.when(pl.program_id(2) == 0)
        def _():
            s_scr[...] = jnp.zeros_like(s_scr)

        st_ref[...] = s_scr[...]
        units = [[tuple(ref[pl.ds(g * cs, cs), pl.ds(i * LANE, LANE)] for ref in in_refs) for g in range(G)] for i in range(NP)]
        ys, s_new = _scan_block([s_scr[i] for i in range(NP)], units)
        for i in range(NP):
            s_scr[i] = s_new[i]
            for g in range(G):
                y_ref[pl.ds(g * cs, cs), pl.ds(i * LANE, LANE)] = ys[i][g]

    def tok(off):
        return pl.BlockSpec((C, NP * LANE), functools.partial(lambda b, h, c, o: (b * nc + c, o + h), o=off // (NP * LANE)))

    in_specs = [tok(0), tok(0), tok(0), tok(2 * rw), tok(0), tok(0)]
    out_specs = [tok(0), pl.BlockSpec((NP, LANE, LANE), lambda b, h, c: ((b * nhg + h) * nc + c, 0, 0))]
    out_shape = [jax.ShapeDtypeStruct((bl * seq, rw), F32), jax.ShapeDtypeStruct((bl * nhg * nc * NP, LANE, LANE), F32)]
    scratch = [pltpu.VMEM((NP, LANE, LANE), F32)]
    if not carry:
        y, st = pl.pallas_call(
            kern, name="rwkv_scan_fwd", grid=(bl, nhg, nc), in_specs=in_specs, out_specs=out_specs, out_shape=out_shape,
            scratch_shapes=scratch, compiler_params=_cparams(("parallel", "parallel", "arbitrary")),
        )(p, lw, k2, p, na, bb)
        return y, st
    res = pl.pallas_call(
        kern, name="rwkv_scan_fwd", grid=(bl, nhg, nc), in_specs=in_specs + carry.in_specs, out_specs=out_specs + carry.out_specs,
        out_shape=out_shape + carry.out_shape, scratch_shapes=scratch + carry.scratch, input_output_aliases=carry.aliases(6, 2),
        compiler_params=_cparams(("arbitrary", "arbitrary", "arbitrary")),
    )(p, lw, k2, p, na, bb, *carry.args)
    return res[0], res[1], list(res[2:])


def _rwkv_scan_bwd(lw, k2, na, bb, p, st, dy, bl, seq, rw, carry=None):
    cs, G, NP, C, nc, nhg = _scan_dims(seq, rw)
    nt = len(carry.tensors) if carry else 0

    def kern(*refs):
        in_refs = refs[:6]
        st_ref, dy_ref = refs[6:8]
        out_refs = refs[8 + 2 * nt:14 + 2 * nt]
        ds_scr = refs[14 + 3 * nt]
        if carry:
            step = (pl.program_id(0) * nhg + pl.program_id(1)) * nc + pl.program_id(2)
            carry.hook(step, bl * nhg * nc - 1, refs[8:8 + nt], refs[14 + 2 * nt:14 + 3 * nt], refs[15 + 3 * nt:])

        @pl.when(pl.program_id(2) == 0)
        def _():
            ds_scr[...] = jnp.zeros_like(ds_scr)

        win = lambda ref, i, g: ref[pl.ds(g * cs, cs), pl.ds(i * LANE, LANE)]
        units = [[tuple(win(ref, i, g) for ref in in_refs) for g in range(G)] for i in range(NP)]
        _, vjp = jax.vjp(_scan_block, [st_ref[i] for i in range(NP)], units)
        dys = [[win(dy_ref, i, g) for g in range(G)] for i in range(NP)]
        ds, dunits = vjp((dys, [ds_scr[i] for i in range(NP)]))
        for i in range(NP):
            ds_scr[i] = ds[i]
            for g in range(G):
                for ref, d in zip(out_refs, dunits[i][g]):
                    ref[pl.ds(g * cs, cs), pl.ds(i * LANE, LANE)] = d

    def tok(off):
        return pl.BlockSpec((C, NP * LANE), functools.partial(lambda b, h, c, o: (b * nc + (nc - 1 - c), o + h), o=off // (NP * LANE)))

    st_spec = pl.BlockSpec((NP, LANE, LANE), lambda b, h, c: ((b * nhg + h) * nc + (nc - 1 - c), 0, 0))
    in_specs = [tok(0), tok(0), tok(0), tok(2 * rw), tok(0), tok(0), st_spec, tok(0)]
    out_shape = [jax.ShapeDtypeStruct((bl * seq, rw), F32)] * 6
    scratch = [pltpu.VMEM((NP, LANE, LANE), F32)]
    if not carry:
        return pl.pallas_call(
            kern, name="rwkv_scan_bwd", grid=(bl, nhg, nc), in_specs=in_specs, out_specs=[tok(0)] * 6, out_shape=out_shape,
            scratch_shapes=scratch, compiler_params=_cparams(("parallel", "parallel", "arbitrary")),
        )(p, lw, k2, p, na, bb, st, dy)
    res = pl.pallas_call(
        kern, name="rwkv_scan_bwd", grid=(bl, nhg, nc), in_specs=in_specs + carry.in_specs, out_specs=[tok(0)] * 6 + carry.out_specs,
        out_shape=out_shape + carry.out_shape, scratch_shapes=scratch + carry.scratch, input_output_aliases=carry.aliases(8, 6),
        compiler_params=_cparams(("arbitrary", "arbitrary", "arbitrary")),
    )(p, lw, k2, p, na, bb, st, dy, *carry.args)
    return res[:6], list(res[6:])


def _loss_head(h2, g_final, target, tile):
    n, d = h2.shape
    nt = n // tile

    def kern(h_ref, g_ref, t_ref, dh_ref, dg_ref, l_ref):
        def f(h, g):
            y = h * lax.rsqrt(jnp.mean(h * h, axis=-1, keepdims=True) + NORM_EPS) * g
            e = y - t_ref[...]
            return 0.5 * jnp.sum(jnp.mean(e * e, axis=-1, keepdims=True))

        loss, (dh, dg) = jax.value_and_grad(f, argnums=(0, 1))(h_ref[...], g_ref[...])
        dh_ref[...] = dh
        first = pl.program_id(0) == 0

        @pl.when(first)
        def _():
            dg_ref[...] = dg
            l_ref[...] = jnp.zeros_like(l_ref) + loss

        @pl.when(jnp.logical_not(first))
        def _():
            dg_ref[...] += dg
            l_ref[...] += loss

    row = pl.BlockSpec((tile, d), lambda i: (i, 0))
    vec = pl.BlockSpec((1, d), lambda i: (0, 0))
    return pl.pallas_call(
        kern, name="loss_head", grid=(nt,), in_specs=[row, vec, row],
        out_specs=[row, vec, pl.BlockSpec((1, LANE), lambda i: (0, 0))],
        out_shape=[jax.ShapeDtypeStruct((n, d), F32), jax.ShapeDtypeStruct((1, d), F32), jax.ShapeDtypeStruct((1, LANE), F32)],
        compiler_params=_cparams(("arbitrary",)),
    )(h2, g_final, target)


def _adamw(parts, w, m, v, name, carry=None):
    n_parts, Rp, Cp = parts.shape
    R, Cc = w.shape
    assert Rp >= R and Cp >= Cc
    tr = _pick(R, tuple(t for t in (1024, 512, 256, 128, 64, 32, 16) if t * Cp <= 128 * 1024) + (8,))
    part = (lambda ref, s: ref[s]) if Cp == Cc else (lambda ref, s: ref[s, :, pl.ds(0, Cc)])
    c1, c2 = 1.0 - ADAM_B1, 1.0 - ADAM_B2
    bc1, bc2 = 1.0 - ADAM_B1 ** ADAM_STEP, 1.0 - ADAM_B2 ** ADAM_STEP

    nt = len(carry.tensors) if carry else 0

    def kern(*refs):
        p_ref, w_ref, m_ref, v_ref = refs[:4]
        g_ref, d_ref, nm_ref, nv_ref = refs[4 + 2 * nt:8 + 2 * nt]
        if carry:
            carry.hook(pl.program_id(0), R // tr - 1, refs[4:4 + nt], refs[8 + 2 * nt:8 + 3 * nt], refs[8 + 3 * nt:])
        g = part(p_ref, 0).astype(F32)
        for s in range(1, n_parts):
            g = g + part(p_ref, s).astype(F32)
        m2 = ADAM_B1 * m_ref[...] + c1 * g
        v2 = ADAM_B2 * v_ref[...] + c2 * (g * g)
        g_ref[...] = g
        nm_ref[...] = m2
        nv_ref[...] = v2
        d_ref[...] = -ADAM_LR * ((m2 / bc1) / (jnp.sqrt(v2 / bc2) + ADAM_EPS) + ADAM_WD * w_ref[...])

    blk = pl.BlockSpec((tr, Cc), lambda i: (i, 0))
    in_specs = [pl.BlockSpec((n_parts, tr, Cp), lambda i: (0, i, 0)), blk, blk, blk]
    out_shape = [jax.ShapeDtypeStruct((R, Cc), F32)] * 4
    if not carry:
        return pl.pallas_call(
            kern, name=name, grid=(R // tr,), in_specs=in_specs, out_specs=[blk] * 4, out_shape=out_shape,
            compiler_params=_cparams(("parallel",)),
        )(parts, w, m, v)
    res = pl.pallas_call(
        kern, name=name, grid=(R // tr,), in_specs=in_specs + carry.in_specs, out_specs=[blk] * 4 + carry.out_specs,
        out_shape=out_shape + carry.out_shape, scratch_shapes=carry.scratch, input_output_aliases=carry.aliases(4, 4),
        compiler_params=_cparams(("arbitrary",)),
    )(parts, w, m, v, *carry.args)
    return res[:4], list(res[4:])


def _exchange_now(carry, name):
    nt = len(carry.tensors)

    def body(*refs):
        carry.hook(0, 0, refs[:nt], refs[2 * nt:3 * nt], refs[3 * nt:])

    return list(pl.pallas_call(
        body, name=name, in_specs=carry.in_specs, out_specs=carry.out_specs, out_shape=carry.out_shape,
        scratch_shapes=carry.scratch, input_output_aliases=carry.aliases(0, 0),
    )(*carry.args))


def _carried(queue, capacity_us, *args, fn, **kw):
    carry = queue.take(capacity_us * CARRY_FILL)
    if carry is None:
        return fn(*args, **kw)
    res = fn(*args, carry=carry, **kw)
    queue.done(carry, res[-1])
    return res[0] if len(res) == 2 else res[:-1]


class _Queue:
    def __init__(self, gather, label):
        self.gather, self.label = gather, label
        self.tensors, self.fifo, self.n_flush = {}, [], 0

    def push(self, name, src, n_pieces, cost_us, cols=False):
        cw = None
        if cols:
            cw = src.shape[1] if self.gather else src.shape[1] // N_DEV
            assert cw % LANE == 0
            dst_shape = (src.shape[0], N_DEV * cw) if self.gather else (N_DEV, src.shape[0], cw)
        else:
            dst_shape = ((N_DEV,) + src.shape) if self.gather else src.shape
        n_rows = src.shape[0] if (self.gather or cols) else src.shape[1]
        rows = n_rows // n_pieces
        assert rows * n_pieces == n_rows and rows % 16 == 0, (name, src.shape)
        self.tensors[name] = [src, lax.empty(dst_shape, src.dtype), cw]
        self.fifo += [(name, p * rows, rows, cost_us / n_pieces) for p in range(n_pieces)]

    def take(self, capacity_us, count=None):
        picked = []
        while self.fifo and (len(picked) < count if count is not None else capacity_us >= 0.85 * self.fifo[0][3]):
            picked.append(self.fifo.pop(0))
            capacity_us -= picked[-1][3]
        if not picked:
            return None
        names = list(dict.fromkeys(n for n, _, _, _ in picked))
        cls = _GatherCarry if self.gather else _Carry
        carry = cls([tuple(self.tensors[n]) for n in names], [(names.index(n), r0, rows) for n, r0, rows, _ in picked])
        carry.names = names
        return carry

    def done(self, carry, dsts):
        for n, d in zip(carry.names, dsts):
            self.tensors[n][1] = d

    def flush(self, count=None):
        carry = self.take(float("inf"), count)
        if carry:
            self.done(carry, _exchange_now(carry, "%s_now_%d" % (self.label, self.n_flush)))
            self.n_flush += 1

    def result(self, name, r0=0, r1=None):
        late = [i for i, (n, p0, rows, _) in enumerate(self.fifo) if n == name and p0 < (r1 or p0 + rows) and p0 + rows > r0]
        if late:
            self.flush(late[-1] + 1)
        return self.tensors[name][1]


def _cols_from_shards(g):
    return jnp.transpose(g, (1, 0, 2)).reshape(g.shape[1], N_DEV * g.shape[2])


def _shards_from_cols(w):
    r, n = w.shape
    return jnp.transpose(w.reshape(r, N_DEV, n // N_DEV), (1, 0, 2))


def _pad_cols(w, to):
    return jnp.pad(w, ((0, 0), (0, to - w.shape[1])))


def _pack(arrs):
    flat = jnp.concatenate([a.reshape(-1) for a in arrs])
    n = _rup(flat.shape[0], 256 * LANE)
    return jnp.pad(flat, (0, n - flat.shape[0])).reshape(n // LANE, LANE)


def _unpack(mat, shapes):
    flat = mat.reshape(-1)
    out, o = [], 0
    for s in shapes:
        n = math.prod(s)
        out.append(flat[o:o + n].reshape(s))
        o += n
    return out


_SMALL = ["norm_mix_g", "mu_shift", "rwkv_w0", "rwkv_a0", "rwkv_k_k", "rwkv_k_a", "rwkv_r_k", "rwkv_ln_g", "rwkv_ln_b", "conv_b",
          "lru_wr", "lru_br", "lru_wi", "lru_bi", "lru_lambda", "lru_norm_g", "norm_ffn_g", "norm_final_g"]
_SMALL_SHARDED = ["rwkv_w2", "rwkv_a2", "rwkv_g2", "conv_w"]
_BIG = ["w_in", "w_out", "ffn_w_gate", "ffn_w_up", "ffn_w_down"]
_WEIGHTS = ['norm_mix_g', 'w_in', 'mu_shift', 'rwkv_w0', 'rwkv_w2', 'rwkv_a0', 'rwkv_a2', 'rwkv_g2', 'rwkv_k_k', 'rwkv_k_a', 'rwkv_r_k',
            'rwkv_ln_g', 'rwkv_ln_b', 'conv_w', 'conv_b', 'lru_wr', 'lru_br', 'lru_wi', 'lru_bi', 'lru_lambda', 'lru_norm_g', 'w_out',
            'norm_ffn_g', 'ffn_w_gate', 'ffn_w_up', 'ffn_w_down', 'norm_final_g']


def _step(W, M, V, x, loss_target):
    bl, seq, d = x.shape
    n = bl * seq
    rw = W["rwkv_w0"].shape[1]
    nh = W["rwkv_r_k"].shape[1]
    assert W["rwkv_r_k"].shape[2] == HEAD and nh * HEAD == rw and rw % LANE == 0
    dl, al, gl = W["rwkv_w2"].shape[1], W["rwkv_a2"].shape[1], W["rwkv_g2"].shape[1]
    dlp, alp, glp = _rup(dl, LANE), _rup(al, LANE), _rup(gl, LANE)
    lorap = dlp + alp + glp
    lw_ = W["conv_b"].shape[1]
    nblk, lbw = W["lru_wr"].shape[1], W["lru_wr"].shape[2]
    assert lbw == LANE and nblk * lbw == lw_
    o_xb, o_gate, o_rw = 0, lw_, 2 * lw_
    o_lora = 3 * rw
    rwp = o_lora + lorap
    inp = o_rw + rwp
    nsh_ff = W["ffn_w_gate"].shape[2]
    nshp = _rup(nsh_ff, LANE)
    dffp = N_DEV * nshp
    x2 = x.reshape(n, d)
    tgt2 = loss_target.reshape(n, d)

    gq = _Queue(True, "gather")
    kp = 2 if d % (2 * LANE) == 0 else 1
    gq.push("w_in", W["w_in"][0].astype(BF16), kp, 490)
    gq.push("small", _pack([W[k][0] for k in _SMALL_SHARDED]), 1, 10)
    gq.push("w_out", W["w_out"][0].astype(BF16), 2, 180)
    pad_ff = nshp - nsh_ff
    gq.push("ffn_w_gate", jnp.pad(W["ffn_w_gate"][0].astype(BF16), ((0, 0), (0, pad_ff))), 4, 490, cols=True)
    gq.push("ffn_w_up", jnp.pad(W["ffn_w_up"][0].astype(BF16), ((0, 0), (0, pad_ff))), 4, 490, cols=True)
    gq.push("ffn_w_down", jnp.pad(W["ffn_w_down"][0].astype(BF16), ((0, pad_ff), (0, 0))), 4, 490)
    gmm = functools.partial(_carried, gq, fn=_mm)
    gstage = functools.partial(_carried, gq, fn=_stage_fwd)

    o1 = 3 * rw

    def my_cols(g):
        w_l = _cols_from_shards(g)
        return jnp.concatenate([w_l[:, o1 + dl + al + gl:], w_l[:, :o1], _pad_cols(w_l[:, o1:o1 + dl], dlp),
                                _pad_cols(w_l[:, o1 + dl:o1 + dl + al], alp), _pad_cols(w_l[:, o1 + dl + al:o1 + dl + al + gl], glp)], axis=1)

    mu_l = W["mu_shift"]
    mu = jnp.concatenate([mu_l[:, :o1], _pad_cols(mu_l[:, o1:o1 + dl], dlp), _pad_cols(mu_l[:, o1 + dl:o1 + dl + al], alp),
                          _pad_cols(mu_l[:, o1 + dl + al:], glp)], axis=1)
    r_k = W["rwkv_r_k"].reshape(1, rw)

    tile = _pick(n, (256, 128, 64))
    tile_s = _pick(n, (128, 64))
    ct_seq = _pick(math.gcd(rwp, lw_), (256, 128))
    assert o_rw % ct_seq == 0 and o_gate % lw_ == 0
    ct_h = _pick(rw, (512, 256, 128))
    gi = lax.broadcasted_iota(jnp.int32, (ct_h, ct_h), 0) // HEAD
    gj = lax.broadcasted_iota(jnp.int32, (ct_h, ct_h), 1) // HEAD
    gsum = ((gi == gj).astype(BF16), (ct_h, ct_h), lambda j: (0, 0))
    full = lambda a: (a, a.shape, lambda j: (0,) * a.ndim)
    rowp = lambda a, ct: (a,) + _row(ct)

    u1, = gstage(210, _f_rmsnorm, "norm_mix_fwd", n, d, tile, d, [(x2, 0)], [full(W["norm_mix_g"])], [], [BF16])
    p, w_rows = None, []
    for i in range(kp):
        rows = slice(i * (d // kp), (i + 1) * (d // kp))
        g_in = gq.result("w_in", rows.start, rows.stop)
        w_rows.append(my_cols(g_in[:, rows, :]))
        p = gmm(190, u1[:, rows], w_rows[-1], name="mm_in_%d" % i, add=p)
    w_in = jnp.concatenate(w_rows, axis=0)
    g_small = gq.result("small")
    sm_shapes = [W[k][0].shape for k in _SMALL_SHARDED]
    sm = [_unpack(g_small[s], sm_shapes) for s in range(N_DEV)]
    w2, a2, g2, conv_w = [jnp.concatenate([sm[s][i] for s in range(N_DEV)], axis=1) for i in range(4)]
    w_lora = jnp.zeros((lorap, 3 * rw), F32)
    w_lora = w_lora.at[:dl, :rw].set(w2).at[dlp:dlp + al, rw:2 * rw].set(a2).at[dlp + alp:dlp + alp + gl, 2 * rw:].set(g2)
    w_lora = w_lora.astype(BF16)
    ps = _lerp_fwd(p, o_rw, mu, bl, seq, rwp, ct_seq)
    f_lora = functools.partial(_f_lora_act, widths=(dlp, alp))
    lact, = _stage_fwd(f_lora, "lora_act_fwd", n, lorap, tile, lorap, [(ps, o_lora)], [], [], [BF16])
    wag = _mm(lact, w_lora, name="mm_lora")
    pre_par = [rowp(W["rwkv_w0"], ct_h), rowp(W["rwkv_a0"], ct_h), rowp(W["rwkv_k_k"], ct_h), rowp(W["rwkv_k_a"], ct_h)]
    pre_acts = [(ps, rw), (wag, 0), (wag, rw)]
    lw, k2, na, bb = gstage(120, _f_rwkv_pre, "rwkv_pre_fwd", n, rw, tile_s, ct_h, pre_acts, pre_par, [gsum], [F32] * 4)
    ysc, st = _carried(gq, 230, None, lw, k2, None, na, bb, ps, bl, seq, rw, fn=_rwkv_scan_fwd)
    post_par = [rowp(W["rwkv_ln_g"], ct_h), rowp(W["rwkv_ln_b"], ct_h), rowp(r_k, ct_h)]
    post_acts = [(ysc, 0), (ps, 0), (k2, 0), (ps, 2 * rw), (wag, 2 * rw)]
    ya, = gstage(120, _f_rwkv_post, "rwkv_post_fwd", n, rw, tile_s, ct_h, post_acts, post_par, [gsum], [BF16])

    xc = _conv_fwd(p, o_xb, conv_w, W["conv_b"], bl, seq, lw_, ct_seq)
    f_gates = functools.partial(_f_lru_gates, seq=seq)
    blk3 = lambda a: (a[0], (1, LANE, LANE), lambda j: (j, 0, 0))
    gate_par = [blk3(W["lru_wr"]), rowp(W["lru_br"], LANE), blk3(W["lru_wi"]), rowp(W["lru_bi"], LANE), rowp(W["lru_lambda"], LANE)]
    tile_g = _pick(n, (1024, 512, 256, 128, 64))
    a_l, bx = gstage(160, f_gates, "lru_gates_fwd", n, lw_, tile_g, LANE, [(xc, 0)], gate_par, [], [F32, F32])
    ct_l = _pick(lw_, (256, 128))
    h_l = _lru_scan_fwd(a_l, bx, bl, seq, lw_, ct_l)
    lpost_par = [full(W["lru_norm_g"])]
    yb, = _stage_fwd(_f_lru_post, "lru_post_fwd", n, lw_, tile_s, lw_, [(h_l, 0), (p, o_gate)], lpost_par, [], [BF16])

    ycat = jnp.concatenate([ya, yb], axis=1)
    g_out = gq.result("w_out")
    w_out = g_out.reshape(N_DEV * g_out.shape[1], d)
    h1 = gmm(130, ycat, w_out, name="mm_out", add=x2)
    u2, = _stage_fwd(_f_rmsnorm, "norm_ffn_fwd", n, d, tile, d, [(h1, 0)], [full(W["norm_ffn_g"])], [], [BF16])
    w_gate = gq.result("ffn_w_gate")
    ff_gate = gmm(340, u2, w_gate, name="mm_gate", out_dtype=BF16)
    w_up = gq.result("ffn_w_up")
    ff_up = gmm(340, u2, w_up, name="mm_up", out_dtype=BF16)
    ct_f = _pick(dffp, (1024, 512, 256, 128))
    ff_acts = [(ff_gate, 0), (ff_up, 0)]
    act, = _stage_fwd(_f_swiglu, "swiglu_fwd", n, dffp, tile, ct_f, ff_acts, [], [], [BF16])
    gq.flush()
    w_down = gq.result("ffn_w_down").reshape(dffp, d)
    h2 = _mm(act, w_down, name="mm_down", add=h1)

    dh2, dg_final, lsum = _loss_head(h2, W["norm_final_g"].reshape(1, d), tgt2, tile_s)
    loss = lax.psum(lsum[0, 0], ("x", "y", "c"))
    queue = _Queue(False, "exchange")

    cmm = functools.partial(_carried, queue, fn=_mm)
    cstage = functools.partial(_carried, queue, fn=_stage_bwd)
    dh2b = dh2.astype(BF16)
    dact = _mm(dh2b, w_down, name="mm_dact", tb=True, out_dtype=BF16)
    dw_down = _mm(act, dh2b, name="mm_dw_down", ta=True, out_dtype=BF16)
    queue.push("ffn_w_down", dw_down.reshape(N_DEV, nshp, d), 8, 1000)
    (dgate, dup), _ = _stage_bwd(_f_swiglu, "swiglu_bwd", n, dffp, tile, ct_f, ff_acts, [], [], [(dact, 0)], [BF16, BF16])
    du2 = cmm(400, dgate, w_gate, name="mm_du2_gate", tb=True)
    dw_gate = cmm(350, u2, dgate, name="mm_dw_gate", ta=True, out_dtype=BF16)
    queue.push("ffn_w_gate", dw_gate, 8, 1000, cols=True)
    du2 = cmm(400, dup, w_up, name="mm_du2_up", tb=True, add=du2)
    dw_up = cmm(350, u2, dup, name="mm_dw_up", ta=True, out_dtype=BF16)
    queue.push("ffn_w_up", dw_up, 8, 1000, cols=True)
    (dh1,), (dg_ffn,) = cstage(130, _f_rmsnorm, "norm_ffn_bwd", n, d, tile_s, d, [(h1, 0)], [full(W["norm_ffn_g"])], [], [(du2, 0)], [F32],
                               extra_add=(dh2, 0))
    dh1b = dh1.astype(BF16)
    dycat = cmm(135, dh1b, w_out, name="mm_dycat", tb=True)
    dw_out = cmm(170, ycat, dh1b, name="mm_dw_out", ta=True, out_dtype=BF16)
    queue.push("w_out", dw_out.reshape(N_DEV, -1, d), 4, 370)

    (dysc, dr_p, dk2_p, dv_p, dg_g), (dln_g, dln_b, dr_k) = cstage(
        195, _f_rwkv_post, "rwkv_post_bwd", n, rw, tile_s, ct_h, post_acts, post_par, [gsum], [(dycat, 0)], [F32] * 5)
    dr_s, dlw, dk2_s, dv_s, dna, dbb = _carried(queue, 650, lw, k2, na, bb, ps, st, dysc, bl, seq, rw, fn=_rwkv_scan_bwd)
    dk2 = dk2_p + dk2_s
    (dk, dwlin, dalin), (dw0, da0, dk_k, dk_a) = cstage(
        180, _f_rwkv_pre, "rwkv_pre_bwd", n, rw, tile_s, ct_h, pre_acts, pre_par, [gsum], [(dlw, 0), (dk2, 0), (dna, 0), (dbb, 0)], [F32] * 3)
    dwag = jnp.concatenate([dwlin, dalin, dg_g], axis=1).astype(BF16)
    dlact = cmm(75, dwag, w_lora, name="mm_dlact", tb=True)
    dw_lora = cmm(50, lact, dwag, name="mm_dw_lora", ta=True)
    (dps_lora,), _ = _stage_bwd(f_lora, "lora_act_bwd", n, lorap, tile, lorap, [(ps, o_lora)], [], [], [(dlact, 0)], [F32])
    dps = jnp.concatenate([dr_p + dr_s, dk, dv_p + dv_s, dps_lora], axis=1)
    dp_rwkv, dmu = _lerp_bwd(p, o_rw, mu, dps, bl, seq, rwp, ct_seq, BF16)

    (dh_l, dgate_l), (dlru_norm_g,) = _stage_bwd(_f_lru_post, "lru_post_bwd", n, lw_, tile_s, lw_, [(h_l, 0), (p, o_gate)], lpost_par, [],
                                                 [(dycat, rw)], [F32, BF16])
    da_l, dbx = _lru_scan_bwd(a_l, h_l, dh_l, bl, seq, lw_, ct_l)
    (dxc,), (dwr, dbr, dwi, dbi, dlam) = cstage(240, f_gates, "lru_gates_bwd", n, lw_, tile_g, LANE, [(xc, 0)], gate_par, [],
                                                [(da_l, 0), (dbx, 0)], [F32])
    dxb, dconv_w, dconv_b = _conv_bwd(p, o_xb, conv_w, dxc, bl, seq, lw_, ct_seq, BF16)
    sh_full = [dw_lora[:dl, :rw], dw_lora[dlp:dlp + al, rw:2 * rw], dw_lora[dlp + alp:dlp + alp + gl, 2 * rw:], dconv_w]
    assert rw == lw_
    rows_sh = sum(a.shape[0] for a in sh_full)
    pad_sh = _rup(rows_sh, 16) - rows_sh
    queue.push("small_sharded", jnp.pad(jnp.concatenate([_shards_from_cols(a) for a in sh_full], axis=1), ((0, 0), (0, pad_sh), (0, 0))), 1, 40)
    stack_sh = lambda D: jnp.pad(jnp.concatenate([D[k][0] for k in _SMALL_SHARDED], axis=0), ((0, pad_sh), (0, 0)))

    dp = jnp.concatenate([dxb, dgate_l, dp_rwkv], axis=1)
    dw_in = cmm(340, u1, dp, name="mm_dw_in", ta=True, out_dtype=BF16)
    ol = o_rw + o_lora
    dw_in_l = jnp.concatenate([dw_in[:, o_rw:ol], dw_in[:, ol:ol + dl], dw_in[:, ol + dlp:ol + dlp + al],
                               dw_in[:, ol + dlp + alp:ol + dlp + alp + gl], dw_in[:, :o_rw]], axis=1)
    queue.push("w_in", _shards_from_cols(dw_in_l), 8, 970)
    du1 = cmm(390, dp, w_in, name="mm_du1", tb=True)
    (grad_x,), (dg_mix,) = cstage(90, _f_rmsnorm, "norm_mix_bwd", n, d, tile_s, d, [(x2, 0)], [full(W["norm_mix_g"])], [], [(du1, 0)], [F32],
                                  extra_add=(dh1, 0))
    out = {}
    for k in ["ffn_w_down", "ffn_w_gate", "ffn_w_up", "w_out", "w_in"]:
        if k == "w_in":
            queue.flush()
        res = _carried(queue, 250, queue.result(k), W[k][0], M[k][0], V[k][0], "adamw_" + k, fn=_adamw)
        out[k] = [o[None] for o in res]

    dmu_l = jnp.concatenate([dmu[:, :o1], dmu[:, o_lora:o_lora + dl], dmu[:, o_lora + dlp:o_lora + dlp + al],
                             dmu[:, o_lora + dlp + alp:o_lora + dlp + alp + gl]], axis=1)
    small_g = {"norm_mix_g": dg_mix, "mu_shift": dmu_l, "rwkv_w0": dw0, "rwkv_a0": da0, "rwkv_k_k": dk_k, "rwkv_k_a": dk_a,
               "rwkv_r_k": dr_k.reshape(W["rwkv_r_k"].shape), "rwkv_ln_g": dln_g, "rwkv_ln_b": dln_b, "conv_b": dconv_b,
               "lru_wr": dwr[None], "lru_br": dbr, "lru_wi": dwi[None], "lru_bi": dbi, "lru_lambda": dlam, "lru_norm_g": dlru_norm_g,
               "norm_ffn_g": dg_ffn, "norm_final_g": dg_final.reshape(W["norm_final_g"].shape)}
    gq.push("small_grads", _pack([small_g[k] for k in _SMALL]), 1, 50)
    gq.flush()
    pk = lambda D: _pack([D[k] for k in _SMALL])
    res = _adamw(gq.result("small_grads"), pk(W), pk(M), pk(V), "adamw_small")
    shapes = [W[k].shape for k in _SMALL]
    for i, r in enumerate(res):
        for k, a in zip(_SMALL, _unpack(r, shapes)):
            out.setdefault(k, [None] * 4)[i] = a
    res = _adamw(queue.result("small_sharded"), stack_sh(W), stack_sh(M), stack_sh(V), "adamw_small_sharded")
    for i, r in enumerate(res):
        o = 0
        for k in _SMALL_SHARDED:
            rows = W[k].shape[1]
            out.setdefault(k, [None] * 4)[i] = r[o:o + rows][None]
            o += rows
    return loss, grad_x.reshape(x.shape), out


def kernel(x, norm_mix_g, w_in, mu_shift, rwkv_w0, rwkv_w2, rwkv_a0, rwkv_a2, rwkv_g2, rwkv_k_k, rwkv_k_a, rwkv_r_k, rwkv_ln_g, rwkv_ln_b, conv_w, conv_b, lru_wr, lru_br, lru_wi, lru_bi, lru_lambda, lru_norm_g, w_out, norm_ffn_g, ffn_w_gate, ffn_w_up, ffn_w_down, norm_final_g, loss_target, m_norm_mix_g, m_w_in, m_mu_shift, m_rwkv_w0, m_rwkv_w2, m_rwkv_a0, m_rwkv_a2, m_rwkv_g2, m_rwkv_k_k, m_rwkv_k_a, m_rwkv_r_k, m_rwkv_ln_g, m_rwkv_ln_b, m_conv_w, m_conv_b, m_lru_wr, m_lru_br, m_lru_wi, m_lru_bi, m_lru_lambda, m_lru_norm_g, m_w_out, m_norm_ffn_g, m_ffn_w_gate, m_ffn_w_up, m_ffn_w_down, m_norm_final_g, v_norm_mix_g, v_w_in, v_mu_shift, v_rwkv_w0, v_rwkv_w2, v_rwkv_a0, v_rwkv_a2, v_rwkv_g2, v_rwkv_k_k, v_rwkv_k_a, v_rwkv_r_k, v_rwkv_ln_g, v_rwkv_ln_b, v_conv_w, v_conv_b, v_lru_wr, v_lru_br, v_lru_wi, v_lru_bi, v_lru_lambda, v_lru_norm_g, v_w_out, v_norm_ffn_g, v_ffn_w_gate, v_ffn_w_up, v_ffn_w_down, v_norm_final_g):
    a = locals()
    W = {k: a[k] for k in _WEIGHTS}
    M = {k: a["m_" + k] for k in _WEIGHTS}
    V = {k: a["v_" + k] for k in _WEIGHTS}
    loss, grad_x, out = _step(W, M, V, x, loss_target)
    res = [loss, grad_x]
    for i in range(4):
        res += [out[k][i].reshape(W[k].shape) for k in _WEIGHTS]
    return tuple(res)
```

```python
import functools
import math

import jax
import jax.numpy as jnp
from jax import lax
from jax.experimental import pallas as pl
from jax.experimental.pallas import tpu as pltpu

F32 = jnp.float32
BF16 = jnp.bfloat16
HI = lax.Precision.HIGHEST
MESH = pl.DeviceIdType.MESH

N_DEV = 8
LANE = 128
HEAD = 64
MM_MAX_TK = 4096
CARRY_FILL = 1.2
SCAN_CHUNK = 64
SCAN_GROUP = 2
SCAN_PAIRS = 8
VMEM_LIMIT = 56 * 1024 * 1024

NORM_EPS = 1e-6
GN_EPS = 64e-5
LRU_C = 8.0
ADAM_LR, ADAM_B1, ADAM_B2, ADAM_EPS, ADAM_WD, ADAM_STEP = 0.001, 0.9, 0.999, 1e-08, 0.01, 10


def _pick(n, cands):
    for c in cands:
        if n % c == 0:
            return c
    return n


def _rup(n, m):
    return (n + m - 1) // m * m


def _cparams(dims):
    return pltpu.CompilerParams(dimension_semantics=dims, vmem_limit_bytes=VMEM_LIMIT)


def _sigmoid(x):
    return 1.0 / (1.0 + jnp.exp(-x))


def _softplus(z):
    return jnp.maximum(z, 0.0) + jnp.log(1.0 + jnp.exp(-jnp.abs(z)))


def _neg_expm1(x):
    series = -(x * (1.0 + 0.5 * x * (1.0 + (x / 3.0) * (1.0 + 0.25 * x))))
    return jnp.where(jnp.abs(x) < 0.03, series, 1.0 - jnp.exp(x))


def _gelu(x):
    return 0.5 * x * (1.0 + jnp.tanh(0.7978845608028654 * (x + 0.044715 * (x * x * x))))


def _dot(a, b, dims, precision=None):
    return lax.dot_general(a, b, (dims, ((), ())), precision=precision, preferred_element_type=F32)


def _nn(a, b, precision=None):
    return _dot(a, b, ((1,), (0,)), precision)


def _nt(a, b, precision=None):
    return _dot(a, b, ((1,), (1,)), precision)


def _tn(a, b, precision=None):
    return _dot(a, b, ((0,), (0,)), precision)


def _coords():
    return lax.axis_index("x"), lax.axis_index("y"), lax.axis_index("c")


class _Carry:
    def __init__(self, tensors, items):
        self.tensors, self.items = tensors, items
        nt, ni = len(tensors), len(items)
        any_spec = pl.BlockSpec(memory_space=pl.ANY)
        self.args = [t[0] for t in tensors] + [t[1] for t in tensors]
        self.in_specs = [any_spec] * (2 * nt)
        self.out_specs = [any_spec] * nt
        self.out_shape = [jax.ShapeDtypeStruct(t[1].shape, t[1].dtype) for t in tensors]
        self.scratch = [pltpu.SemaphoreType.DMA((ni, N_DEV - 1)), pltpu.SemaphoreType.DMA((ni, N_DEV - 1)), pltpu.SemaphoreType.DMA((ni,))]

    def aliases(self, first_in, first_out):
        nt = len(self.tensors)
        return {first_in + nt + t: first_out + t for t in range(nt)}

    def _slot(self, ref, t, idx, win):
        cw = self.tensors[t][2]
        return ref.at[idx, win] if cw is None else ref.at[win, pl.ds(pl.multiple_of(idx * cw, LANE), cw)]

    def _copies(self, src_refs, dst_refs, sems):
        send_sems, recv_sems, local_sems = sems
        x, y, c = _coords()
        my = 4 * x + 2 * y + c
        out = []
        for n, (t, r0, rows) in enumerate(self.items):
            win = pl.ds(r0, rows)
            out.append(pltpu.make_async_copy(self._slot(src_refs[t], t, my, win), dst_refs[t].at[my, win], local_sems.at[n]))
            for k in range(1, N_DEV):
                px, py, pc = x ^ ((k >> 2) & 1), y ^ ((k >> 1) & 1), c ^ (k & 1)
                out.append(pltpu.make_async_remote_copy(
                    src_ref=self._slot(src_refs[t], t, 4 * px + 2 * py + pc, win), dst_ref=dst_refs[t].at[my, win],
                    send_sem=send_sems.at[n, k - 1], recv_sem=recv_sems.at[n, k - 1],
                    device_id=(px, py, pc), device_id_type=MESH))
        return out

    def hook(self, step, last, src_refs, dst_refs, sems):
        if last == 0:
            for cp in self._copies(src_refs, dst_refs, sems):
                cp.start()
            for cp in self._copies(src_refs, dst_refs, sems):
                cp.wait()
            return

        @pl.when(step == 0)
        def _():
            for cp in self._copies(src_refs, dst_refs, sems):
                cp.start()

        @pl.when(step == last)
        def _():
            for cp in self._copies(src_refs, dst_refs, sems):
                cp.wait()


class _GatherCarry(_Carry):
    def hook(self, step, last, src_refs, dst_refs, sems):
        send_sems, recv_sems, local_sems = sems
        x, y, c = _coords()
        me, sibling = (x, y, c), (x, y, 1 - c)
        chips = [(1 - x, y), (x, 1 - y), (1 - x, 1 - y)]

        def per_item(fn):
            for n, (t, r0, rows) in enumerate(self.items):
                win = pl.ds(r0, rows)

                def copy(k, block, to, own=False, n=n, t=t, win=win):
                    slot = self._slot(dst_refs[t], t, 4 * block[0] + 2 * block[1] + block[2], win)
                    return pltpu.make_async_remote_copy(
                        src_ref=src_refs[t].at[win] if own else slot, dst_ref=slot,
                        send_sem=send_sems.at[n, k], recv_sem=recv_sems.at[n, k], device_id=to, device_id_type=MESH)

                mine = pltpu.make_async_copy(src_refs[t].at[win], self._slot(dst_refs[t], t, 4 * x + 2 * y + c, win), local_sems.at[n])
                first = [copy(0, me, sibling, own=True)] + [copy(1 + j, me, (*chip, c), own=True) for j, chip in enumerate(chips)]
                fn(copy, mine, first)

        def begin(copy, mine, first):
            mine.start()
            for cp in first:
                cp.start()

        def pass_on(copy, mine, first):
            for j, chip in enumerate(chips):
                copy(1 + j, (*chip, c), me).wait_recv()
                copy(4 + j, (*chip, c), sibling).start()

        def finish(copy, mine, first):
            copy(0, sibling, me).wait_recv()
            for j, chip in enumerate(chips):
                copy(4 + j, (*chip, 1 - c), me).wait_recv()
            for cp in first + [copy(4 + j, (*chip, c), sibling) for j, chip in enumerate(chips)]:
                cp.wait_send()
            mine.wait()

        if last == 0:
            for fn in (begin, pass_on, finish):
                per_item(fn)
            return
        late = max(1, (7 * last) // 8)
        for at, fn in ((0, begin), (late, pass_on), (last, finish)):
            pl.when(step == at)(functools.partial(per_item, fn))


def _mm(a, b, *, name, ta=False, tb=False, out_dtype=F32, add=None, tiles=None, carry=None):
    M, K = (a.shape[1], a.shape[0]) if ta else a.shape
    N = b.shape[0] if tb else b.shape[1]
    assert (b.shape[1] if tb else b.shape[0]) == K, (a.shape, b.shape, ta, tb)
    tk = max(t for t in range(LANE, min(K, MM_MAX_TK) + 1, LANE) if K % t == 0)
    tm, tn, tk = tiles or (_pick(M, (1024, 512, 256, 128)), _pick(N, (512, 256, 128)), tk)
    nk = K // tk
    dims = ((0 if ta else 1,), (1 if tb else 0,))

    n_in = 2 + (add is not None)
    nt = len(carry.tensors) if carry else 0
    gi, gj = M // tm, N // tn

    def kern(*refs):
        a_ref, b_ref = refs[:2]
        add_ref = refs[2] if add is not None else None
        o_ref = refs[n_in + 2 * nt]
        scr = refs[n_in + 3 * nt + 1:]
        if carry:
            step = (pl.program_id(0) * gj + pl.program_id(1)) * nk + pl.program_id(2)
            carry.hook(step, gi * gj * nk - 1, refs[n_in:n_in + nt], refs[n_in + 2 * nt + 1:n_in + 3 * nt + 1], scr[-3:])

        def finish(r):
            if add is not None:
                r = r + add_ref[...].astype(F32)
            o_ref[...] = r.astype(o_ref.dtype)

        if nk == 1:
            finish(_dot(a_ref[...], b_ref[...], dims))
            return
        acc = scr[0]
        k = pl.program_id(2)

        @pl.when(k == 0)
        def _():
            acc[...] = jnp.zeros_like(acc)

        acc[...] += _dot(a_ref[...], b_ref[...], dims)

        @pl.when(k == nk - 1)
        def _():
            finish(acc[...])

    a_spec = pl.BlockSpec((tk, tm), lambda i, j, k: (k, i)) if ta else pl.BlockSpec((tm, tk), lambda i, j, k: (i, k))
    b_spec = pl.BlockSpec((tn, tk), lambda i, j, k: (j, k)) if tb else pl.BlockSpec((tk, tn), lambda i, j, k: (k, j))
    o_spec = pl.BlockSpec((tm, tn), lambda i, j, k: (i, j))
    in_specs = [a_spec, b_spec] + ([o_spec] if add is not None else [])
    args = (a, b) + ((add,) if add is not None else ())
    scratch = [pltpu.VMEM((tm, tn), F32)] if nk > 1 else []
    o_shape = jax.ShapeDtypeStruct((M, N), out_dtype)
    if not carry:
        return pl.pallas_call(
            kern, name=name, grid=(gi, gj, nk), in_specs=in_specs, out_specs=o_spec, out_shape=o_shape, scratch_shapes=scratch,
            compiler_params=_cparams(("parallel", "parallel", "arbitrary")),
        )(*args)
    res = pl.pallas_call(
        kern, name=name, grid=(gi, gj, nk), in_specs=in_specs + carry.in_specs, out_specs=[o_spec] + carry.out_specs,
        out_shape=[o_shape] + carry.out_shape, scratch_shapes=scratch + carry.scratch,
        input_output_aliases=carry.aliases(n_in, 1), compiler_params=_cparams(("arbitrary", "arbitrary", "arbitrary")),
    )(*args, *carry.args)
    return res[0], list(res[1:])


def _stage_specs(acts, params, consts, tile, ct):
    act_specs = [pl.BlockSpec((tile, ct), functools.partial(lambda j, i, o: (i, o + j), o=off // ct)) for _, off in acts]
    par_specs = [pl.BlockSpec(bs, functools.partial(lambda j, i, im: im(j), im=im)) for _, bs, im in params]
    con_specs = [pl.BlockSpec(bs, functools.partial(lambda j, i, im: im(j), im=im)) for _, bs, im in consts]
    return act_specs, par_specs, con_specs


def _stage_fwd(f, name, n_rows, width, tile, ct, acts, params, consts, out_dtypes, carry=None):
    for _, off in acts:
        assert off % ct == 0
    na, npar, nc, no = len(acts), len(params), len(consts), len(out_dtypes)
    n_in = na + npar + nc
    nt = len(carry.tensors) if carry else 0
    gj, gi = width // ct, n_rows // tile

    def kern(*refs):
        if carry:
            step = pl.program_id(0) * gi + pl.program_id(1)
            carry.hook(step, gj * gi - 1, refs[n_in:n_in + nt], refs[n_in + 2 * nt + no:n_in + 3 * nt + no], refs[n_in + 3 * nt + no:])
        a = [r[...].astype(F32) for r in refs[:na]]
        p = [r[...] for r in refs[na:na + npar]]
        c = [r[...] for r in refs[na + npar:n_in]]
        outs = f(a, p, c, pl.program_id(1) * tile)
        for r, o in zip(refs[n_in + 2 * nt:n_in + 2 * nt + no], outs):
            r[...] = o.astype(r.dtype)

    act_specs, par_specs, con_specs = _stage_specs(acts, params, consts, tile, ct)
    o_spec = pl.BlockSpec((tile, ct), lambda j, i: (i, j))
    in_specs = act_specs + par_specs + con_specs
    out_shape = [jax.ShapeDtypeStruct((n_rows, width), d) for d in out_dtypes]
    args = [a for a, _ in acts] + [p for p, _, _ in params] + [c for c, _, _ in consts]
    if not carry:
        return tuple(pl.pallas_call(
            kern, name=name, grid=(gj, gi), in_specs=in_specs, out_specs=[o_spec] * no, out_shape=out_shape,
            compiler_params=_cparams(("parallel", "parallel")),
        )(*args))
    res = pl.pallas_call(
        kern, name=name, grid=(gj, gi), in_specs=in_specs + carry.in_specs, out_specs=[o_spec] * no + carry.out_specs,
        out_shape=out_shape + carry.out_shape, scratch_shapes=carry.scratch, input_output_aliases=carry.aliases(n_in, no),
        compiler_params=_cparams(("arbitrary", "arbitrary")),
    )(*args, *carry.args)
    return tuple(res[:no]), list(res[no:])


def _stage_bwd(f, name, n_rows, width, tile, ct, acts, params, consts, couts, dact_dtypes, extra_add=None, carry=None):
    na, npar, nc, no = len(acts), len(params), len(consts), len(couts)
    nx = 0 if extra_add is None else 1
    nt = len(carry.tensors) if carry else 0
    n_in = na + npar + nc + no + nx
    gj, gi = width // ct, n_rows // tile

    def kern(*refs):
        if carry:
            step = pl.program_id(0) * gi + pl.program_id(1)
            n_out = n_in + 2 * nt + na + npar
            carry.hook(step, gj * gi - 1, refs[n_in:n_in + nt], refs[n_out:n_out + nt], refs[n_out + nt:])
        a = [r[...].astype(F32) for r in refs[:na]]
        p = [r[...] for r in refs[na:na + npar]]
        c = [r[...] for r in refs[na + npar:na + npar + nc]]
        base = na + npar + nc
        co = [r[...].astype(F32) for r in refs[base:base + no]]
        base += no
        x_refs = refs[base:base + nx]
        base += nx + 2 * nt
        da_refs = refs[base:base + na]
        dp_refs = refs[base + na:base + na + npar]
        row0 = pl.program_id(1) * tile
        _, vjp = jax.vjp(lambda aa, pp: tuple(f(aa, pp, c, row0)), a, p)
        da, dp = vjp(tuple(co))
        for k, (r, d) in enumerate(zip(da_refs, da)):
            if k == 0 and nx:
                d = d + x_refs[0][...].astype(F32)
            r[...] = d.astype(r.dtype)
        first = pl.program_id(1) == 0
        for r, d in zip(dp_refs, dp):
            @pl.when(first)
            def _(r=r, d=d):
                r[...] = d

            @pl.when(jnp.logical_not(first))
            def _(r=r, d=d):
                r[...] += d

    act_specs, par_specs, con_specs = _stage_specs(acts, params, consts, tile, ct)
    t_spec = pl.BlockSpec((tile, ct), lambda j, i: (i, j))
    co_specs = [pl.BlockSpec((tile, ct), functools.partial(lambda j, i, o: (i, o + j), o=off // ct)) for _, off in couts]
    x_specs = [] if extra_add is None else [pl.BlockSpec((tile, ct), functools.partial(lambda j, i, o: (i, o + j), o=extra_add[1] // ct))]
    x_args = [] if extra_add is None else [extra_add[0]]
    in_specs = act_specs + par_specs + con_specs + co_specs + x_specs
    out_specs = [t_spec] * na + par_specs
    out_shape = [jax.ShapeDtypeStruct((n_rows, width), d) for d in dact_dtypes] + [jax.ShapeDtypeStruct(p.shape, F32) for p, _, _ in params]
    args = [a for a, _ in acts] + [p for p, _, _ in params] + [c for c, _, _ in consts] + [c for c, _ in couts] + x_args
    if not carry:
        outs = pl.pallas_call(
            kern, name=name, grid=(gj, gi), in_specs=in_specs, out_specs=out_specs, out_shape=out_shape,
            compiler_params=_cparams(("parallel", "arbitrary")),
        )(*args)
        return tuple(outs[:na]), tuple(outs[na:])
    outs = pl.pallas_call(
        kern, name=name, grid=(gj, gi), in_specs=in_specs + carry.in_specs, out_specs=out_specs + carry.out_specs,
        out_shape=out_shape + carry.out_shape, scratch_shapes=carry.scratch, input_output_aliases=carry.aliases(n_in, na + npar),
        compiler_params=_cparams(("arbitrary", "arbitrary")),
    )(*args, *carry.args)
    return tuple(outs[:na]), tuple(outs[na:na + npar]), list(outs[na + npar:])


def _row(ct):
    return (1, ct), (lambda j: (0, j))


def _f_rmsnorm(a, p, c, row0):
    x, = a
    g, = p
    return (x * lax.rsqrt(jnp.mean(x * x, axis=-1, keepdims=True) + NORM_EPS) * g,)


def _f_lora_act(a, p, c, row0, widths):
    x, = a
    dl, al = widths
    col = lax.broadcasted_iota(jnp.int32, x.shape, 1)
    return (jnp.where(col < dl, jnp.tanh(x), jnp.where(col < dl + al, x, _sigmoid(x))),)


def _head_sums_raw(x, ones):
    hi = x.astype(BF16)
    lo = (x - hi.astype(F32)).astype(BF16)
    return _nn(hi, ones) + _nn(lo, ones)


@jax.custom_vjp
def _head_sums(x, ones):
    return _head_sums_raw(x, ones)


_head_sums.defvjp(lambda x, ones: (_head_sums_raw(x, ones), ones),
                  lambda ones, ct: (_head_sums_raw(ct, ones), jnp.zeros_like(ones)))


def _f_rwkv_pre(a, p, c, row0):
    k, wlin, alin = a
    w0, a0, k_k, k_a = p
    gsum, = c
    w = -_softplus(-(w0 + wlin)) - 0.5
    lw = -jnp.exp(w)
    alpha = _sigmoid(a0 + alin)
    kk = k * k_k
    ss = _head_sums(kk * kk, gsum)
    kk = kk * lax.rsqrt(jnp.maximum(ss, 1e-24))
    k2 = k * (1.0 + (alpha - 1.0) * k_a)
    return lw, k2, -kk, kk * alpha


def _f_rwkv_post(a, p, c, row0):
    y, r, k2, v, g = a
    ln_g, ln_b, r_k = p
    gsum, = c
    inv = 1.0 / HEAD
    mean = _head_sums(y, gsum) * inv
    yc = y - mean
    var = _head_sums(yc * yc, gsum) * inv
    yn = yc * lax.rsqrt(var + GN_EPS) * ln_g + ln_b
    bonus = _head_sums(r * k2 * r_k, gsum)
    return ((yn + bonus * v) * g,)


def _f_lru_gates(a, p, c, row0, seq):
    xc, = a
    wr, br, wi, bi, lam = p
    xb = xc.astype(BF16)
    rg = _sigmoid(_nn(xb, wr[0].astype(BF16)) + br)
    ig = _sigmoid(_nn(xb, wi[0].astype(BF16)) + bi)
    log_a = -LRU_C * rg * _softplus(-lam)
    a_t = jnp.exp(log_a)
    mult = jnp.sqrt(_neg_expm1(2.0 * log_a))
    row = row0 + lax.broadcasted_iota(jnp.int32, xc.shape, 0)
    mult = jnp.where(row % seq == 0, 1.0, mult)
    return a_t, mult * ig * xc


def _f_lru_post(a, p, c, row0):
    h, gate = a
    g, = p
    y = h * _gelu(gate)
    return (y * lax.rsqrt(jnp.mean(y * y, axis=-1, keepdims=True) + NORM_EPS) * g,)


def _f_swiglu(a, p, c, row0):
    gate, up = a
    return (gate * _sigmoid(gate) * up,)


def _shift_down(x, s, row):
    return jnp.where(row >= s, pltpu.roll(x, s, 0), 0.0)


def _shift_up(x, s, row):
    n = x.shape[0]
    return jnp.where(row < n - s, pltpu.roll(x, n - s, 0), 0.0)


def _seq_call(kern, name, bl, seq, width, ct, ins, outs, acc_outs=()):
    def spec(off, rows):
        if rows is None:
            return pl.BlockSpec((seq, ct), functools.partial(lambda j, b, o: (b, o + j), o=off // ct))
        return pl.BlockSpec((rows, ct), lambda j, b: (0, j))

    in_specs = [spec(off, rows) for _, off, rows in ins]
    out_specs = [spec(0, None) for _ in outs] + [spec(0, rows) for _, rows in acc_outs]
    out_shape = [jax.ShapeDtypeStruct((bl * seq, width), d) for d in outs] + [jax.ShapeDtypeStruct((rows, width), F32) for _, rows in acc_outs]
    res = pl.pallas_call(
        kern, name=name, grid=(width // ct, bl), in_specs=in_specs, out_specs=out_specs, out_shape=out_shape,
        compiler_params=_cparams(("parallel", "arbitrary")),
    )(*[a for a, _, _ in ins])
    return tuple(res)


def _acc(ref, val):
    first = pl.program_id(1) == 0

    @pl.when(first)
    def _():
        ref[...] = val

    @pl.when(jnp.logical_not(first))
    def _():
        ref[...] += val


def _lerp_fwd(p, off, mu, bl, seq, width, ct):
    def kern(p_ref, mu_ref, o_ref):
        x = p_ref[...]
        row = lax.broadcasted_iota(jnp.int32, x.shape, 0)
        o_ref[...] = x + (_shift_down(x, 1, row) - x) * mu_ref[...]

    return _seq_call(kern, "lerp_fwd", bl, seq, width, ct, [(p, off, None), (mu, 0, 1)], [F32])[0]


def _lerp_bwd(p, off, mu, mu_off, dps_parts, name, bl, seq, width, ct, out_dtype):
    nd = len(dps_parts)

    def kern(*refs):
        p_ref, mu_ref = refs[:2]
        dp_ref, dmu_ref = refs[2 + nd:]
        x = p_ref[...]
        d = refs[2][...].astype(F32)
        for r in refs[3:2 + nd]:
            d = d + r[...].astype(F32)
        m = mu_ref[...]
        row = lax.broadcasted_iota(jnp.int32, x.shape, 0)
        dp_ref[...] = (d * (1.0 - m) + _shift_up(d * m, 1, row)).astype(dp_ref.dtype)
        _acc(dmu_ref, jnp.sum(d * (_shift_down(x, 1, row) - x), axis=0, keepdims=True))

    ins = [(p, off, None), (mu[:, mu_off:mu_off + width], 0, 1)] + [(a, 0, None) for a in dps_parts]
    return _seq_call(kern, name, bl, seq, width, ct, ins, [out_dtype], [(None, 1)])


def _conv_fwd(p, off, cw, cb, bl, seq, width, ct):
    nw = cw.shape[0]

    def kern(x_ref, w_ref, b_ref, o_ref):
        x = x_ref[...]
        row = lax.broadcasted_iota(jnp.int32, x.shape, 0)
        acc = b_ref[...] + x * w_ref[pl.ds(nw - 1, 1), :]
        for s in range(1, nw):
            acc = acc + _shift_down(x, s, row) * w_ref[pl.ds(nw - 1 - s, 1), :]
        o_ref[...] = acc

    return _seq_call(kern, "conv_fwd", bl, seq, width, ct, [(p, off, None), (cw, 0, nw), (cb, 0, 1)], [F32])[0]


def _conv_bwd(p, off, cw, dxc, bl, seq, width, ct, out_dtype):
    nw = cw.shape[0]

    def kern(x_ref, w_ref, d_ref, dx_ref, dw_ref, db_ref):
        x = x_ref[...]
        d = d_ref[...]
        row = lax.broadcasted_iota(jnp.int32, x.shape, 0)
        wrow = lax.broadcasted_iota(jnp.int32, dw_ref.shape, 0)
        dx = d * w_ref[pl.ds(nw - 1, 1), :]
        dw = jnp.where(wrow == nw - 1, jnp.sum(d * x, axis=0, keepdims=True), 0.0)
        for s in range(1, nw):
            dx = dx + _shift_up(d, s, row) * w_ref[pl.ds(nw - 1 - s, 1), :]
            dw = jnp.where(wrow == nw - 1 - s, jnp.sum(d * _shift_down(x, s, row), axis=0, keepdims=True), dw)
        dx_ref[...] = dx.astype(dx_ref.dtype)
        _acc(dw_ref, dw)
        _acc(db_ref, jnp.sum(d, axis=0, keepdims=True))

    return _seq_call(kern, "conv_bwd", bl, seq, width, ct, [(p, off, None), (cw, 0, nw), (dxc, 0, None)], [out_dtype], [(None, nw), (None, 1)])


def _lru_scan_fwd(a, bx, bl, seq, width, ct):
    def kern(a_ref, b_ref, h_ref):
        av = a_ref[...]
        bv = b_ref[...]
        row = lax.broadcasted_iota(jnp.int32, av.shape, 0)
        d = 1
        while d < seq:
            a_sh = jnp.where(row >= d, pltpu.roll(av, d, 0), 1.0)
            b_sh = jnp.where(row >= d, pltpu.roll(bv, d, 0), 0.0)
            bv = av * b_sh + bv
            av = av * a_sh
            d *= 2
        h_ref[...] = bv

    return _seq_call(kern, "lru_scan_fwd", bl, seq, width, ct, [(a, 0, None), (bx, 0, None)], [F32])[0]


def _lru_scan_bwd(a, h, dh, bl, seq, width, ct):
    def kern(a_ref, h_ref, d_ref, da_ref, db_ref):
        row = lax.broadcasted_iota(jnp.int32, a_ref.shape, 0)
        al = _shift_up(a_ref[...], 1, row)
        g = d_ref[...]
        d = 1
        while d < seq:
            keep = row < seq - d
            al_sh = jnp.where(keep, pltpu.roll(al, seq - d, 0), 1.0)
            g_sh = jnp.where(keep, pltpu.roll(g, seq - d, 0), 0.0)
            g = al * g_sh + g
            al = al * al_sh
            d *= 2
        db_ref[...] = g
        da_ref[...] = g * _shift_down(h_ref[...], 1, row)

    return _seq_call(kern, "lru_scan_bwd", bl, seq, width, ct, [(a, 0, None), (h, 0, None), (dh, 0, None)], [F32, F32])


_FORMS = {"nn": ((1,), (0,)), "nt": ((1,), (1,)), "tn": ((0,), (0,))}
_FORM_GRADS = {"nn": (("nt", "g", "b"), ("tn", "a", "g")),
               "nt": (("nn", "g", "b"), ("tn", "g", "a")),
               "tn": (("nt", "b", "g"), ("nn", "a", "g"))}


def _split_bf16(x):
    hi = x.astype(BF16)
    return hi, (x - hi.astype(F32)).astype(BF16)


def _pdot_raw(a, b, form, passes):
    dims = _FORMS[form]
    if passes == 1:
        return _dot(a.astype(BF16), b.astype(BF16), dims)
    ah, al = _split_bf16(a)
    bh, bl = _split_bf16(b)
    return _dot(ah, bh, dims) + (_dot(ah, bl, dims) + _dot(al, bh, dims))


@functools.partial(jax.custom_vjp, nondiff_argnums=(2, 3))
def _pdot(a, b, form, passes):
    return _pdot_raw(a, b, form, passes)


def _pdot_fwd(a, b, form, passes):
    return _pdot_raw(a, b, form, passes), (a, b)


def _pdot_bwd(form, passes, res, g):
    vals = {"a": res[0], "b": res[1], "g": g}
    (fa, xa, ya), (fb, xb, yb) = _FORM_GRADS[form]
    return _pdot_raw(vals[xa], vals[ya], fa, passes), _pdot_raw(vals[xb], vals[yb], fb, passes)


_pdot.defvjp(_pdot_fwd, _pdot_bwd)


def _neumann_raw(a_list, n_levels, passes):
    eye = (lax.broadcasted_iota(jnp.int32, a_list[0].shape, 0) == lax.broadcasted_iota(jnp.int32, a_list[0].shape, 1)).astype(F32)
    pw = list(a_list)
    x = [eye + a for a in a_list]
    for _ in range(n_levels):
        pw = [_pdot_raw(p, p, "nn", passes) for p in pw]
        x = [xi + _pdot_raw(xi, p, "nn", passes) for xi, p in zip(x, pw)]
    return x


@functools.partial(jax.custom_vjp, nondiff_argnums=(1, 2))
def _neumann_inverse(a_list, n_levels, passes):
    return _neumann_raw(a_list, n_levels, passes)


def _neumann_fwd(a_list, n_levels, passes):
    x = _neumann_raw(a_list, n_levels, passes)
    return x, x


def _neumann_bwd(n_levels, passes, x, ct):
    return ([_pdot_raw(xi, _pdot_raw(c, xi, "nt", passes), "tn", passes) for xi, c in zip(x, ct)],)


_neumann_inverse.defvjp(_neumann_fwd, _neumann_bwd)


def _scan_chunk2(S0, r, lw, k, v, a, b, p_main=1, p_inv=3):
    y, s = _scan_block([S0], [[(r, lw, k, v, a, b)]], p_main, p_inv)
    return y[0][0], s[0]


def _scan_block(states, units, p_main=1, p_inv=1):
    C = units[0][0][0].shape[0]
    C2 = 2 * C
    ri = lax.broadcasted_iota(jnp.int32, (C, C), 0)
    ci = lax.broadcasted_iota(jnp.int32, (C, C), 1)
    tri = (ri >= ci).astype(F32)
    i2 = lax.broadcasted_iota(jnp.int32, (C2, C2), 0)
    j2 = lax.broadcasted_iota(jnp.int32, (C2, C2), 1)
    same = (i2 // C) == (j2 // C)
    strict = jnp.logical_and(same, (i2 % C) > (j2 % C))
    incl = jnp.logical_and(same, (i2 % C) >= (j2 % C))
    eye = (i2 == j2).astype(F32)
    lane = lax.broadcasted_iota(jnp.int32, (1, LANE), 1)
    m0, m1 = (lane < HEAD).astype(F32), (lane >= HEAD).astype(F32)
    stack = lambda z: jnp.concatenate([z * m0, z * m1], axis=0)
    ids = [(i, g) for g in range(len(units[0])) for i in range(len(units))]

    pre = {}
    for i, g in ids:
        r, lw, k, v, a, b = units[i][g]
        cs = _nn(tri, lw, HI)
        p_incl = jnp.exp(cs)
        p_rec = jnp.exp(-cs)
        xr = jnp.concatenate([stack(a * jnp.exp(cs - lw)), stack(r * p_incl)], axis=0)
        bk = jnp.concatenate([stack(b * p_rec), stack(k * p_rec)], axis=0)
        pre[i, g] = (xr, bk, stack(v), jnp.exp(jnp.sum(lw, axis=0, keepdims=True)))
    gm = {u: _pdot(pre[u][0], pre[u][1], "nt", p_main) for u in ids}
    a_ak = {u: jnp.where(strict, gm[u][:C2, C2:], 0.0) for u in ids}
    r_bk = {u: jnp.concatenate([jnp.where(incl, gm[u][C2:, :C2], 0.0), jnp.where(incl, gm[u][C2:, C2:], 0.0)], axis=1) for u in ids}
    a_ab = [jnp.where(strict, gm[u][:C2, :C2], 0.0) for u in ids]
    x = dict(zip(ids, _neumann_inverse(a_ab, int(math.log2(C)) - 1, p_inv)))
    akv = {u: _pdot(a_ak[u], pre[u][2], "nn", p_main) for u in ids}

    states = list(states)
    pairs = range(len(units))
    ys = [[None] * len(units[0]) for _ in units]
    for g in range(len(units[0])):
        xs = [_pdot(pre[i, g][0], states[i], "nt", p_main) for i in pairs]
        us = [_pdot(x[i, g], xs[i][:C2] + akv[i, g], "nn", p_inv) for i in pairs]
        uv = [jnp.concatenate([us[i], pre[i, g][2]], axis=0) for i in pairs]
        y2 = [xs[i][C2:] + _pdot(r_bk[i, g], uv[i], "nn", p_main) for i in pairs]
        for i in pairs:
            ys[i][g] = y2[i][:C] + y2[i][C:]
        states = [(states[i] + _pdot(uv[i], pre[i, g][1], "tn", p_main)) * pre[i, g][3] for i in pairs]
    return ys, states


def _scan_dims(seq, rw):
    G = _pick(seq // SCAN_CHUNK, (SCAN_GROUP, 2, 1))
    NP = _pick(rw // LANE, (SCAN_PAIRS, 4, 2, 1))
    C = SCAN_CHUNK * G
    return SCAN_CHUNK, G, NP, C, seq // C, rw // (NP * LANE)


def _rwkv_scan_fwd(r, lw, k2, v, na, bb, p, bl, seq, rw, carry=None):
    cs, G, NP, C, nc, nhg = _scan_dims(seq, rw)
    nt = len(carry.tensors) if carry else 0

    def kern(*refs):
        in_refs = refs[:6]
        y_ref, st_ref = refs[6 + 2 * nt:8 + 2 * nt]
        s_scr = refs[8 + 3 * nt]
        if carry:
            step = (pl.program_id(0) * nhg + pl.program_id(1)) * nc + pl.program_id(2)
            carry.hook(step, bl * nhg * nc - 1, refs[6:6 + nt], refs[8 + 2 * nt:8 + 3 * nt], refs[9 + 3 * nt:])

        @pl.when(pl.program_id(2) == 0)
        def _():
            s_scr[...] = jnp.zeros_like(s_scr)

        st_ref[...] = s_scr[...]
        units = [[tuple(ref[pl.ds(g * cs, cs), pl.ds(i * LANE, LANE)] for ref in in_refs) for g in range(G)] for i in range(NP)]
        ys, s_new = _scan_block([s_scr[i] for i in range(NP)], units)
        for i in range(NP):
            s_scr[i] = s_new[i]
            for g in range(G):
                y_ref[pl.ds(g * cs, cs), pl.ds(i * LANE, LANE)] = ys[i][g]

    def tok(off):
        return pl.BlockSpec((C, NP * LANE), functools.partial(lambda b, h, c, o: (b * nc + c, o + h), o=off // (NP * LANE)))

    in_specs = [tok(0), tok(0), tok(0), tok(2 * rw), tok(0), tok(0)]
    out_specs = [tok(0), pl.BlockSpec((NP, LANE, LANE), lambda b, h, c: ((b * nhg + h) * nc + c, 0, 0))]
    out_shape = [jax.ShapeDtypeStruct((bl * seq, rw), F32), jax.ShapeDtypeStruct((bl * nhg * nc * NP, LANE, LANE), F32)]
    scratch = [pltpu.VMEM((NP, LANE, LANE), F32)]
    if not carry:
        y, st = pl.pallas_call(
            kern, name="rwkv_scan_fwd", grid=(bl, nhg, nc), in_specs=in_specs, out_specs=out_specs, out_shape=out_shape,
            scratch_shapes=scratch, compiler_params=_cparams(("parallel", "parallel", "arbitrary")),
        )(p, lw, k2, p, na, bb)
        return y, st
    res = pl.pallas_call(
        kern, name="rwkv_scan_fwd", grid=(bl, nhg, nc), in_specs=in_specs + carry.in_specs, out_specs=out_specs + carry.out_specs,
        out_shape=out_shape + carry.out_shape, scratch_shapes=scratch + carry.scratch, input_output_aliases=carry.aliases(6, 2),
        compiler_params=_cparams(("arbitrary", "arbitrary", "arbitrary")),
    )(p, lw, k2, p, na, bb, *carry.args)
    return res[0], res[1], list(res[2:])


def _rwkv_scan_bwd(lw, k2, na, bb, p, st, dy, bl, seq, rw, carry=None):
    cs, G, NP, C, nc, nhg = _scan_dims(seq, rw)
    nt = len(carry.tensors) if carry else 0

    def kern(*refs):
        in_refs = refs[:6]
        st_ref, dy_ref = refs[6:8]
        out_refs = refs[8 + 2 * nt:14 + 2 * nt]
        ds_scr = refs[14 + 3 * nt]
        if carry:
            step = (pl.program_id(0) * nhg + pl.program_id(1)) * nc + pl.program_id(2)
            carry.hook(step, bl * nhg * nc - 1, refs[8:8 + nt], refs[14 + 2 * nt:14 + 3 * nt], refs[15 + 3 * nt:])

        @pl.when(pl.program_id(2) == 0)
        def _():
            ds_scr[...] = jnp.zeros_like(ds_scr)

        win = lambda ref, i, g: ref[pl.ds(g * cs, cs), pl.ds(i * LANE, LANE)]
        units = [[tuple(win(ref, i, g) for ref in in_refs) for g in range(G)] for i in range(NP)]
        _, vjp = jax.vjp(_scan_block, [st_ref[i] for i in range(NP)], units)
        dys = [[win(dy_ref, i, g) for g in range(G)] for i in range(NP)]
        ds, dunits = vjp((dys, [ds_scr[i] for i in range(NP)]))
        for i in range(NP):
            ds_scr[i] = ds[i]
            for g in range(G):
                for ref, d in zip(out_refs, dunits[i][g]):
                    ref[pl.ds(g * cs, cs), pl.ds(i * LANE, LANE)] = d

    def tok(off):
        return pl.BlockSpec((C, NP * LANE), functools.partial(lambda b, h, c, o: (b * nc + (nc - 1 - c), o + h), o=off // (NP * LANE)))

    st_spec = pl.BlockSpec((NP, LANE, LANE), lambda b, h, c: ((b * nhg + h) * nc + (nc - 1 - c), 0, 0))
    in_specs = [tok(0), tok(0), tok(0), tok(2 * rw), tok(0), tok(0), st_spec, tok(0)]
    out_shape = [jax.ShapeDtypeStruct((bl * seq, rw), F32)] * 6
    scratch = [pltpu.VMEM((NP, LANE, LANE), F32)]
    if not carry:
        return pl.pallas_call(
            kern, name="rwkv_scan_bwd", grid=(bl, nhg, nc), in_specs=in_specs, out_specs=[tok(0)] * 6, out_shape=out_shape,
            scratch_shapes=scratch, compiler_params=_cparams(("parallel", "parallel", "arbitrary")),
        )(p, lw, k2, p, na, bb, st, dy)
    res = pl.pallas_call(
        kern, name="rwkv_scan_bwd", grid=(bl, nhg, nc), in_specs=in_specs + carry.in_specs, out_specs=[tok(0)] * 6 + carry.out_specs,
        out_shape=out_shape + carry.out_shape, scratch_shapes=scratch + carry.scratch, input_output_aliases=carry.aliases(8, 6),
        compiler_params=_cparams(("arbitrary", "arbitrary", "arbitrary")),
    )(p, lw, k2, p, na, bb, st, dy, *carry.args)
    return res[:6], list(res[6:])


def _loss_head(h2, g_final, target, tile):
    n, d = h2.shape
    nt = n // tile

    def kern(h_ref, g_ref, t_ref, dh_ref, dg_ref, l_ref, dhb_ref):
        def f(h, g):
            y = h * lax.rsqrt(jnp.mean(h * h, axis=-1, keepdims=True) + NORM_EPS) * g
            e = y - t_ref[...]
            return 0.5 * jnp.sum(jnp.mean(e * e, axis=-1, keepdims=True))

        loss, (dh, dg) = jax.value_and_grad(f, argnums=(0, 1))(h_ref[...], g_ref[...])
        dh_ref[...] = dh
        dhb_ref[...] = dh.astype(dhb_ref.dtype)
        first = pl.program_id(0) == 0

        @pl.when(first)
        def _():
            dg_ref[...] = dg
            l_ref[...] = jnp.zeros_like(l_ref) + loss

        @pl.when(jnp.logical_not(first))
        def _():
            dg_ref[...] += dg
            l_ref[...] += loss

    row = pl.BlockSpec((tile, d), lambda i: (i, 0))
    vec = pl.BlockSpec((1, d), lambda i: (0, 0))
    return pl.pallas_call(
        kern, name="loss_head", grid=(nt,), in_specs=[row, vec, row],
        out_specs=[row, vec, pl.BlockSpec((1, LANE), lambda i: (0, 0)), row],
        out_shape=[jax.ShapeDtypeStruct((n, d), F32), jax.ShapeDtypeStruct((1, d), F32), jax.ShapeDtypeStruct((1, LANE), F32),
                   jax.ShapeDtypeStruct((n, d), BF16)],
        compiler_params=_cparams(("arbitrary",)),
    )(h2, g_final, target)


def _adamw(parts, w, m, v, name, carry=None):
    n_parts, Rp, Cp = parts.shape
    R, Cc = w.shape
    assert Rp >= R and Cp >= Cc
    tr = _pick(R, tuple(t for t in (1024, 512, 256, 128, 64, 32, 16) if t * Cp <= 128 * 1024) + (8,))
    part = (lambda ref, s: ref[s]) if Cp == Cc else (lambda ref, s: ref[s, :, pl.ds(0, Cc)])
    c1, c2 = 1.0 - ADAM_B1, 1.0 - ADAM_B2
    bc1, bc2 = 1.0 - ADAM_B1 ** ADAM_STEP, 1.0 - ADAM_B2 ** ADAM_STEP

    nt = len(carry.tensors) if carry else 0

    def kern(*refs):
        p_ref, w_ref, m_ref, v_ref = refs[:4]
        g_ref, d_ref, nm_ref, nv_ref = refs[4 + 2 * nt:8 + 2 * nt]
        if carry:
            carry.hook(pl.program_id(0), R // tr - 1, refs[4:4 + nt], refs[8 + 2 * nt:8 + 3 * nt], refs[8 + 3 * nt:])
        g = part(p_ref, 0).astype(F32)
        for s in range(1, n_parts):
            g = g + part(p_ref, s).astype(F32)
        m2 = ADAM_B1 * m_ref[...] + c1 * g
        v2 = ADAM_B2 * v_ref[...] + c2 * (g * g)
        g_ref[...] = g
        nm_ref[...] = m2
        nv_ref[...] = v2
        d_ref[...] = -ADAM_LR * ((m2 / bc1) / (jnp.sqrt(v2 / bc2) + ADAM_EPS) + ADAM_WD * w_ref[...])

    blk = pl.BlockSpec((tr, Cc), lambda i: (i, 0))
    in_specs = [pl.BlockSpec((n_parts, tr, Cp), lambda i: (0, i, 0)), blk, blk, blk]
    out_shape = [jax.ShapeDtypeStruct((R, Cc), F32)] * 4
    if not carry:
        return pl.pallas_call(
            kern, name=name, grid=(R // tr,), in_specs=in_specs, out_specs=[blk] * 4, out_shape=out_shape,
            compiler_params=_cparams(("parallel",)),
        )(parts, w, m, v)
    res = pl.pallas_call(
        kern, name=name, grid=(R // tr,), in_specs=in_specs + carry.in_specs, out_specs=[blk] * 4 + carry.out_specs,
        out_shape=out_shape + carry.out_shape, scratch_shapes=carry.scratch, input_output_aliases=carry.aliases(4, 4),
        compiler_params=_cparams(("arbitrary",)),
    )(parts, w, m, v, *carry.args)
    return res[:4], list(res[4:])


def _exchange_now(carry, name):
    nt = len(carry.tensors)

    def body(*refs):
        carry.hook(0, 0, refs[:nt], refs[2 * nt:3 * nt], refs[3 * nt:])

    return list(pl.pallas_call(
        body, name=name, in_specs=carry.in_specs, out_specs=carry.out_specs, out_shape=carry.out_shape,
        scratch_shapes=carry.scratch, input_output_aliases=carry.aliases(0, 0),
    )(*carry.args))


def _carried(queue, capacity_us, *args, fn, **kw):
    carry = queue.take(capacity_us * CARRY_FILL)
    if carry is None:
        return fn(*args, **kw)
    res = fn(*args, carry=carry, **kw)
    queue.done(carry, res[-1])
    return res[0] if len(res) == 2 else res[:-1]


class _Queue:
    def __init__(self, gather, label):
        self.gather, self.label = gather, label
        self.tensors, self.fifo, self.n_flush = {}, [], 0

    def push(self, name, src, n_pieces, cost_us, cols=False):
        cw = None
        if cols:
            cw = src.shape[1] if self.gather else src.shape[1] // N_DEV
            assert cw % LANE == 0
            dst_shape = (src.shape[0], N_DEV * cw) if self.gather else (N_DEV, src.shape[0], cw)
        else:
            dst_shape = ((N_DEV,) + src.shape) if self.gather else src.shape
        n_rows = src.shape[0] if (self.gather or cols) else src.shape[1]
        rows = n_rows // n_pieces
        assert rows * n_pieces == n_rows and rows % 16 == 0, (name, src.shape)
        self.tensors[name] = [src, lax.empty(dst_shape, src.dtype), cw]
        self.fifo += [(name, p * rows, rows, cost_us / n_pieces) for p in range(n_pieces)]

    def take(self, capacity_us, count=None):
        picked = []
        while self.fifo and (len(picked) < count if count is not None else capacity_us >= 0.85 * self.fifo[0][3]):
            picked.append(self.fifo.pop(0))
            capacity_us -= picked[-1][3]
        if not picked:
            return None
        names = list(dict.fromkeys(n for n, _, _, _ in picked))
        cls = _GatherCarry if self.gather else _Carry
        carry = cls([tuple(self.tensors[n]) for n in names], [(names.index(n), r0, rows) for n, r0, rows, _ in picked])
        carry.names = names
        return carry

    def done(self, carry, dsts):
        for n, d in zip(carry.names, dsts):
            self.tensors[n][1] = d

    def flush(self, count=None):
        carry = self.take(float("inf"), count)
        if carry:
            self.done(carry, _exchange_now(carry, "%s_now_%d" % (self.label, self.n_flush)))
            self.n_flush += 1

    def result(self, name, r0=0, r1=None):
        late = [i for i, (n, p0, rows, _) in enumerate(self.fifo) if n == name and p0 < (r1 or p0 + rows) and p0 + rows > r0]
        if late:
            self.flush(late[-1] + 1)
        return self.tensors[name][1]


def _cols_from_shards(g):
    return jnp.transpose(g, (1, 0, 2)).reshape(g.shape[1], N_DEV * g.shape[2])


def _shards_from_cols(w):
    r, n = w.shape
    return jnp.transpose(w.reshape(r, N_DEV, n // N_DEV), (1, 0, 2))


def _pad_cols(w, to):
    return jnp.pad(w, ((0, 0), (0, to - w.shape[1])))


def _pack(arrs):
    flat = jnp.concatenate([a.reshape(-1) for a in arrs])
    n = _rup(flat.shape[0], 256 * LANE)
    return jnp.pad(flat, (0, n - flat.shape[0])).reshape(n // LANE, LANE)


def _unpack(mat, shapes):
    flat = mat.reshape(-1)
    out, o = [], 0
    for s in shapes:
        n = math.prod(s)
        out.append(flat[o:o + n].reshape(s))
        o += n
    return out


_SMALL = ["norm_mix_g", "mu_shift", "rwkv_w0", "rwkv_a0", "rwkv_k_k", "rwkv_k_a", "rwkv_r_k", "rwkv_ln_g", "rwkv_ln_b", "conv_b",
          "lru_wr", "lru_br", "lru_wi", "lru_bi", "lru_lambda", "lru_norm_g", "norm_ffn_g", "norm_final_g"]
_SMALL_SHARDED = ["rwkv_w2", "rwkv_a2", "rwkv_g2", "conv_w"]
_BIG = ["w_in", "w_out", "ffn_w_gate", "ffn_w_up", "ffn_w_down"]
_WEIGHTS = ['norm_mix_g', 'w_in', 'mu_shift', 'rwkv_w0', 'rwkv_w2', 'rwkv_a0', 'rwkv_a2', 'rwkv_g2', 'rwkv_k_k', 'rwkv_k_a', 'rwkv_r_k',
            'rwkv_ln_g', 'rwkv_ln_b', 'conv_w', 'conv_b', 'lru_wr', 'lru_br', 'lru_wi', 'lru_bi', 'lru_lambda', 'lru_norm_g', 'w_out',
            'norm_ffn_g', 'ffn_w_gate', 'ffn_w_up', 'ffn_w_down', 'norm_final_g']


def _step(W, M, V, x, loss_target):
    bl, seq, d = x.shape
    n = bl * seq
    rw = W["rwkv_w0"].shape[1]
    nh = W["rwkv_r_k"].shape[1]
    assert W["rwkv_r_k"].shape[2] == HEAD and nh * HEAD == rw and rw % LANE == 0
    dl, al, gl = W["rwkv_w2"].shape[1], W["rwkv_a2"].shape[1], W["rwkv_g2"].shape[1]
    dlp, alp, glp = _rup(dl, LANE), _rup(al, LANE), _rup(gl, LANE)
    lorap = dlp + alp + glp
    lw_ = W["conv_b"].shape[1]
    nblk, lbw = W["lru_wr"].shape[1], W["lru_wr"].shape[2]
    assert lbw == LANE and nblk * lbw == lw_
    o_xb, o_gate, o_rw = 0, lw_, 2 * lw_
    o_lora = 3 * rw
    rwp = o_lora + lorap
    inp = o_rw + rwp
    nsh_ff = W["ffn_w_gate"].shape[2]
    nshp = _rup(nsh_ff, LANE)
    dffp = N_DEV * nshp
    x2 = x.reshape(n, d)
    tgt2 = loss_target.reshape(n, d)

    gq = _Queue(True, "gather")
    kp = 2 if d % (2 * LANE) == 0 else 1
    gq.push("w_in", W["w_in"][0].astype(BF16), kp, 490)
    gq.push("small", _pack([W[k][0] for k in _SMALL_SHARDED]), 1, 10)
    gq.push("w_out", W["w_out"][0].astype(BF16), 2, 180)
    pad_ff = nshp - nsh_ff
    gq.push("ffn_w_gate", jnp.pad(W["ffn_w_gate"][0].astype(BF16), ((0, 0), (0, pad_ff))), 4, 490, cols=True)
    gq.push("ffn_w_up", jnp.pad(W["ffn_w_up"][0].astype(BF16), ((0, 0), (0, pad_ff))), 4, 490, cols=True)
    gq.push("ffn_w_down", jnp.pad(W["ffn_w_down"][0].astype(BF16), ((0, pad_ff), (0, 0))), 4, 490)
    gmm = functools.partial(_carried, gq, fn=_mm)
    gstage = functools.partial(_carried, gq, fn=_stage_fwd)

    o1 = 3 * rw

    def my_cols(g):
        w_l = _cols_from_shards(g)
        return jnp.concatenate([w_l[:, o1 + dl + al + gl:], w_l[:, :o1], _pad_cols(w_l[:, o1:o1 + dl], dlp),
                                _pad_cols(w_l[:, o1 + dl:o1 + dl + al], alp), _pad_cols(w_l[:, o1 + dl + al:o1 + dl + al + gl], glp)], axis=1)

    mu_l = W["mu_shift"]
    mu = jnp.concatenate([mu_l[:, :o1], _pad_cols(mu_l[:, o1:o1 + dl], dlp), _pad_cols(mu_l[:, o1 + dl:o1 + dl + al], alp),
                          _pad_cols(mu_l[:, o1 + dl + al:], glp)], axis=1)
    r_k = W["rwkv_r_k"].reshape(1, rw)

    tile = _pick(n, (256, 128, 64))
    tile_s = _pick(n, (128, 64))
    ct_seq = _pick(math.gcd(rwp, lw_), (256, 128))
    assert o_rw % ct_seq == 0 and o_gate % lw_ == 0
    ct_h = _pick(rw, (512, 256, 128))
    gi = lax.broadcasted_iota(jnp.int32, (ct_h, ct_h), 0) // HEAD
    gj = lax.broadcasted_iota(jnp.int32, (ct_h, ct_h), 1) // HEAD
    gsum = ((gi == gj).astype(BF16), (ct_h, ct_h), lambda j: (0, 0))
    full = lambda a: (a, a.shape, lambda j: (0,) * a.ndim)
    rowp = lambda a, ct: (a,) + _row(ct)

    u1, = gstage(210, _f_rmsnorm, "norm_mix_fwd", n, d, tile, d, [(x2, 0)], [full(W["norm_mix_g"])], [], [BF16])
    p, w_rows = None, []
    for i in range(kp):
        rows = slice(i * (d // kp), (i + 1) * (d // kp))
        g_in = gq.result("w_in", rows.start, rows.stop)
        w_rows.append(my_cols(g_in[:, rows, :]))
        p = gmm(190, u1[:, rows], w_rows[-1], name="mm_in_%d" % i, add=p)
    w_in = jnp.concatenate(w_rows, axis=0)
    g_small = gq.result("small")
    sm_shapes = [W[k][0].shape for k in _SMALL_SHARDED]
    sm = [_unpack(g_small[s], sm_shapes) for s in range(N_DEV)]
    w2, a2, g2, conv_w = [jnp.concatenate([sm[s][i] for s in range(N_DEV)], axis=1) for i in range(4)]
    w_lora = jnp.zeros((lorap, 3 * rw), F32)
    w_lora = w_lora.at[:dl, :rw].set(w2).at[dlp:dlp + al, rw:2 * rw].set(a2).at[dlp + alp:dlp + alp + gl, 2 * rw:].set(g2)
    w_lora = w_lora.astype(BF16)
    ps = _lerp_fwd(p, o_rw, mu, bl, seq, rwp, ct_seq)
    f_lora = functools.partial(_f_lora_act, widths=(dlp, alp))
    lact, = _stage_fwd(f_lora, "lora_act_fwd", n, lorap, tile, lorap, [(ps, o_lora)], [], [], [BF16])
    wag = _mm(lact, w_lora, name="mm_lora")
    pre_par = [rowp(W["rwkv_w0"], ct_h), rowp(W["rwkv_a0"], ct_h), rowp(W["rwkv_k_k"], ct_h), rowp(W["rwkv_k_a"], ct_h)]
    pre_acts = [(ps, rw), (wag, 0), (wag, rw)]
    lw, k2, na, bb = gstage(120, _f_rwkv_pre, "rwkv_pre_fwd", n, rw, tile_s, ct_h, pre_acts, pre_par, [gsum], [F32] * 4)
    ysc, st = _carried(gq, 230, None, lw, k2, None, na, bb, ps, bl, seq, rw, fn=_rwkv_scan_fwd)
    post_par = [rowp(W["rwkv_ln_g"], ct_h), rowp(W["rwkv_ln_b"], ct_h), rowp(r_k, ct_h)]
    post_acts = [(ysc, 0), (ps, 0), (k2, 0), (ps, 2 * rw), (wag, 2 * rw)]
    ya, = gstage(120, _f_rwkv_post, "rwkv_post_fwd", n, rw, tile_s, ct_h, post_acts, post_par, [gsum], [BF16])

    xc = _conv_fwd(p, o_xb, conv_w, W["conv_b"], bl, seq, lw_, ct_seq)
    f_gates = functools.partial(_f_lru_gates, seq=seq)
    blk3 = lambda a: (a[0], (1, LANE, LANE), lambda j: (j, 0, 0))
    gate_par = [blk3(W["lru_wr"]), rowp(W["lru_br"], LANE), blk3(W["lru_wi"]), rowp(W["lru_bi"], LANE), rowp(W["lru_lambda"], LANE)]
    tile_g = _pick(n, (1024, 512, 256, 128, 64))
    a_l, bx = gstage(160, f_gates, "lru_gates_fwd", n, lw_, tile_g, LANE, [(xc, 0)], gate_par, [], [F32, F32])
    ct_l = _pick(lw_, (256, 128))
    h_l = _lru_scan_fwd(a_l, bx, bl, seq, lw_, ct_l)
    lpost_par = [full(W["lru_norm_g"])]
    yb, = _stage_fwd(_f_lru_post, "lru_post_fwd", n, lw_, tile_s, lw_, [(h_l, 0), (p, o_gate)], lpost_par, [], [BF16])

    ycat = jnp.concatenate([ya, yb], axis=1)
    g_out = gq.result("w_out")
    w_out = g_out.reshape(N_DEV * g_out.shape[1], d)
    h1 = gmm(130, ycat, w_out, name="mm_out", add=x2)
    u2, = _stage_fwd(_f_rmsnorm, "norm_ffn_fwd", n, d, tile, d, [(h1, 0)], [full(W["norm_ffn_g"])], [], [BF16])
    w_gate = gq.result("ffn_w_gate")
    ff_gate = gmm(340, u2, w_gate, name="mm_gate", out_dtype=BF16)
    w_up = gq.result("ffn_w_up")
    ff_up = gmm(340, u2, w_up, name="mm_up", out_dtype=BF16)
    ct_f = _pick(dffp, (1024, 512, 256, 128))
    ff_acts = [(ff_gate, 0), (ff_up, 0)]
    act, = _stage_fwd(_f_swiglu, "swiglu_fwd", n, dffp, tile, ct_f, ff_acts, [], [], [BF16])
    gq.flush()
    w_down = gq.result("ffn_w_down").reshape(dffp, d)
    h2 = _mm(act, w_down, name="mm_down", add=h1)

    dh2, dg_final, lsum, dh2b = _loss_head(h2, W["norm_final_g"].reshape(1, d), tgt2, tile_s)
    loss = lax.psum(lsum[0, 0], ("x", "y", "c"))
    queue = _Queue(False, "exchange")

    cmm = functools.partial(_carried, queue, fn=_mm)
    cstage = functools.partial(_carried, queue, fn=_stage_bwd)
    dact = _mm(dh2b, w_down, name="mm_dact", tb=True, out_dtype=BF16)
    dw_down = _mm(act, dh2b, name="mm_dw_down", ta=True, out_dtype=BF16)
    queue.push("ffn_w_down", dw_down.reshape(N_DEV, nshp, d), 8, 1000)
    (dgate, dup), _ = _stage_bwd(_f_swiglu, "swiglu_bwd", n, dffp, tile, ct_f, ff_acts, [], [], [(dact, 0)], [BF16, BF16])
    du2 = cmm(290, dgate, w_gate, name="mm_du2_gate", tb=True)
    dw_gate = cmm(420, u2, dgate, name="mm_dw_gate", ta=True, out_dtype=BF16)
    queue.push("ffn_w_gate", dw_gate, 8, 1000, cols=True)
    du2 = cmm(290, dup, w_up, name="mm_du2_up", tb=True, add=du2)
    dw_up = cmm(420, u2, dup, name="mm_dw_up", ta=True, out_dtype=BF16)
    queue.push("ffn_w_up", dw_up, 8, 1000, cols=True)
    (dh1,), (dg_ffn,) = cstage(130, _f_rmsnorm, "norm_ffn_bwd", n, d, tile_s, d, [(h1, 0)], [full(W["norm_ffn_g"])], [], [(du2, 0)], [F32],
                               extra_add=(dh2, 0))
    dh1b = dh1.astype(BF16)
    dycat = cmm(135, dh1b, w_out, name="mm_dycat", tb=True)
    dw_out = cmm(170, ycat, dh1b, name="mm_dw_out", ta=True, out_dtype=BF16)
    queue.push("w_out", dw_out.reshape(N_DEV, -1, d), 4, 370)

    (dysc, dr_p, dk2_p, dv_p, dg_g), (dln_g, dln_b, dr_k) = cstage(
        195, _f_rwkv_post, "rwkv_post_bwd", n, rw, tile_s, ct_h, post_acts, post_par, [gsum], [(dycat, 0)], [F32] * 5)
    dr_s, dlw, dk2_s, dv_s, dna, dbb = _carried(queue, 650, lw, k2, na, bb, ps, st, dysc, bl, seq, rw, fn=_rwkv_scan_bwd)
    dk2 = dk2_p + dk2_s
    (dk, dwlin, dalin), (dw0, da0, dk_k, dk_a) = cstage(
        180, _f_rwkv_pre, "rwkv_pre_bwd", n, rw, tile_s, ct_h, pre_acts, pre_par, [gsum], [(dlw, 0), (dk2, 0), (dna, 0), (dbb, 0)], [F32] * 3)
    dwag = jnp.concatenate([dwlin, dalin, dg_g], axis=1).astype(BF16)
    dlact = cmm(75, dwag, w_lora, name="mm_dlact", tb=True)
    dw_lora = cmm(50, lact, dwag, name="mm_dw_lora", ta=True)
    (dps_lora,), _ = _stage_bwd(f_lora, "lora_act_bwd", n, lorap, tile, lorap, [(ps, o_lora)], [], [], [(dlact, 0)], [F32])
    dp_segs, dmu_segs = [], []
    for nm, o, wdt, parts in (("r", 0, rw, [dr_p, dr_s]), ("k", rw, rw, [dk]), ("v", 2 * rw, rw, [dv_p, dv_s]), ("lora", o_lora, lorap, [dps_lora])):
        dp_s, dmu_s = _lerp_bwd(p, o_rw + o, mu, o, parts, "lerp_bwd_" + nm, bl, seq, wdt, ct_seq, BF16)
        dp_segs.append(dp_s)
        dmu_segs.append(dmu_s)
    dmu = jnp.concatenate(dmu_segs, axis=1)

    (dh_l, dgate_l), (dlru_norm_g,) = cstage(80, _f_lru_post, "lru_post_bwd", n, lw_, tile_s, lw_, [(h_l, 0), (p, o_gate)], lpost_par, [],
                                                 [(dycat, rw)], [F32, BF16])
    da_l, dbx = _lru_scan_bwd(a_l, h_l, dh_l, bl, seq, lw_, ct_l)
    (dxc,), (dwr, dbr, dwi, dbi, dlam) = cstage(240, f_gates, "lru_gates_bwd", n, lw_, tile_g, LANE, [(xc, 0)], gate_par, [],
                                                [(da_l, 0), (dbx, 0)], [F32])
    dxb, dconv_w, dconv_b = _conv_bwd(p, o_xb, conv_w, dxc, bl, seq, lw_, ct_seq, BF16)
    sh_full = [dw_lora[:dl, :rw], dw_lora[dlp:dlp + al, rw:2 * rw], dw_lora[dlp + alp:dlp + alp + gl, 2 * rw:], dconv_w]
    assert rw == lw_
    rows_sh = sum(a.shape[0] for a in sh_full)
    pad_sh = _rup(rows_sh, 16) - rows_sh
    queue.push("small_sharded", jnp.pad(jnp.concatenate([_shards_from_cols(a) for a in sh_full], axis=1), ((0, 0), (0, pad_sh), (0, 0))), 1, 40)
    stack_sh = lambda D: jnp.pad(jnp.concatenate([D[k][0] for k in _SMALL_SHARDED], axis=0), ((0, pad_sh), (0, 0)))

    dp = jnp.concatenate([dxb, dgate_l] + dp_segs, axis=1)
    dw_in = cmm(340, u1, dp, name="mm_dw_in", ta=True, out_dtype=BF16)
    ol = o_rw + o_lora
    dw_in_l = jnp.concatenate([dw_in[:, o_rw:ol], dw_in[:, ol:ol + dl], dw_in[:, ol + dlp:ol + dlp + al],
                               dw_in[:, ol + dlp + alp:ol + dlp + alp + gl], dw_in[:, :o_rw]], axis=1)
    queue.push("w_in", _shards_from_cols(dw_in_l), 8, 970)
    du1 = cmm(380, dp, w_in, name="mm_du1", tb=True)
    (grad_x,), (dg_mix,) = cstage(90, _f_rmsnorm, "norm_mix_bwd", n, d, tile_s, d, [(x2, 0)], [full(W["norm_mix_g"])], [], [(du1, 0)], [F32],
                                  extra_add=(dh1, 0))
    out = {}
    for k in ["ffn_w_down", "ffn_w_gate", "ffn_w_up", "w_out", "w_in"]:
        if k == "w_in":
            queue.flush()
        res = _carried(queue, 105, queue.result(k), W[k][0], M[k][0], V[k][0], "adamw_" + k, fn=_adamw)
        out[k] = [o[None] for o in res]

    dmu_l = jnp.concatenate([dmu[:, :o1], dmu[:, o_lora:o_lora + dl], dmu[:, o_lora + dlp:o_lora + dlp + al],
                             dmu[:, o_lora + dlp + alp:o_lora + dlp + alp + gl]], axis=1)
    small_g = {"norm_mix_g": dg_mix, "mu_shift": dmu_l, "rwkv_w0": dw0, "rwkv_a0": da0, "rwkv_k_k": dk_k, "rwkv_k_a": dk_a,
               "rwkv_r_k": dr_k.reshape(W["rwkv_r_k"].shape), "rwkv_ln_g": dln_g, "rwkv_ln_b": dln_b, "conv_b": dconv_b,
               "lru_wr": dwr[None], "lru_br": dbr, "lru_wi": dwi[None], "lru_bi": dbi, "lru_lambda": dlam, "lru_norm_g": dlru_norm_g,
               "norm_ffn_g": dg_ffn, "norm_final_g": dg_final.reshape(W["norm_final_g"].shape)}
    gq.push("small_grads", _pack([small_g[k] for k in _SMALL]), 1, 50)
    gq.flush()
    pk = lambda D: _pack([D[k] for k in _SMALL])
    res = _adamw(gq.result("small_grads"), pk(W), pk(M), pk(V), "adamw_small")
    shapes = [W[k].shape for k in _SMALL]
    for i, r in enumerate(res):
        for k, a in zip(_SMALL, _unpack(r, shapes)):
            out.setdefault(k, [None] * 4)[i] = a
    res = _adamw(queue.result("small_sharded"), stack_sh(W), stack_sh(M), stack_sh(V), "adamw_small_sharded")
    for i, r in enumerate(res):
        o = 0
        for k in _SMALL_SHARDED:
            rows = W[k].shape[1]
            out.setdefault(k, [None] * 4)[i] = r[o:o + rows][None]
            o += rows
    return loss, grad_x.reshape(x.shape), out


def kernel(x, norm_mix_g, w_in, mu_shift, rwkv_w0, rwkv_w2, rwkv_a0, rwkv_a2, rwkv_g2, rwkv_k_k, rwkv_k_a, rwkv_r_k, rwkv_ln_g, rwkv_ln_b, conv_w, conv_b, lru_wr, lru_br, lru_wi, lru_bi, lru_lambda, lru_norm_g, w_out, norm_ffn_g, ffn_w_gate, ffn_w_up, ffn_w_down, norm_final_g, loss_target, m_norm_mix_g, m_w_in, m_mu_shift, m_rwkv_w0, m_rwkv_w2, m_rwkv_a0, m_rwkv_a2, m_rwkv_g2, m_rwkv_k_k, m_rwkv_k_a, m_rwkv_r_k, m_rwkv_ln_g, m_rwkv_ln_b, m_conv_w, m_conv_b, m_lru_wr, m_lru_br, m_lru_wi, m_lru_bi, m_lru_lambda, m_lru_norm_g, m_w_out, m_norm_ffn_g, m_ffn_w_gate, m_ffn_w_up, m_ffn_w_down, m_norm_final_g, v_norm_mix_g, v_w_in, v_mu_shift, v_rwkv_w0, v_rwkv_w2, v_rwkv_a0, v_rwkv_a2, v_rwkv_g2, v_rwkv_k_k, v_rwkv_k_a, v_rwkv_r_k, v_rwkv_ln_g, v_rwkv_ln_b, v_conv_w, v_conv_b, v_lru_wr, v_lru_br, v_lru_wi, v_lru_bi, v_lru_lambda, v_lru_norm_g, v_w_out, v_norm_ffn_g, v_ffn_w_gate, v_ffn_w_up, v_ffn_w_down, v_norm_final_g):
    a = locals()
    W = {k: a[k] for k in _WEIGHTS}
    M = {k: a["m_" + k] for k in _WEIGHTS}
    V = {k: a["v_" + k] for k in _WEIGHTS}
    loss, grad_x, out = _step(W, M, V, x, loss_target)
    res = [loss, grad_x]
    for i in range(4):
        res += [out[k][i].reshape(W[k].shape) for k in _WEIGHTS]
    return tuple(res)
```

```python
import functools
import math

import jax
import jax.numpy as jnp
from jax import lax
from jax.experimental import pallas as pl
from jax.experimental.pallas import tpu as pltpu

F32 = jnp.float32
BF16 = jnp.bfloat16
HI = lax.Precision.HIGHEST
MESH = pl.DeviceIdType.MESH

N_DEV = 8
LANE = 128
HEAD = 64
MM_MAX_TK = 4096
CARRY_FILL = 1.2
SCAN_CHUNK = 64
SCAN_GROUP = 2
SCAN_PAIRS = 8
VMEM_LIMIT = 56 * 1024 * 1024

NORM_EPS = 1e-6
GN_EPS = 64e-5
LRU_C = 8.0
ADAM_LR, ADAM_B1, ADAM_B2, ADAM_EPS, ADAM_WD, ADAM_STEP = 0.001, 0.9, 0.999, 1e-08, 0.01, 10


def _pick(n, cands):
    for c in cands:
        if n % c == 0:
            return c
    return n


def _rup(n, m):
    return (n + m - 1) // m * m


def _cparams(dims):
    return pltpu.CompilerParams(dimension_semantics=dims, vmem_limit_bytes=VMEM_LIMIT)


def _sigmoid(x):
    return 1.0 / (1.0 + jnp.exp(-x))


def _softplus(z):
    return jnp.maximum(z, 0.0) + jnp.log(1.0 + jnp.exp(-jnp.abs(z)))


def _neg_expm1(x):
    series = -(x * (1.0 + 0.5 * x * (1.0 + (x / 3.0) * (1.0 + 0.25 * x))))
    return jnp.where(jnp.abs(x) < 0.03, series, 1.0 - jnp.exp(x))


def _gelu(x):
    return 0.5 * x * (1.0 + jnp.tanh(0.7978845608028654 * (x + 0.044715 * (x * x * x))))


def _dot(a, b, dims, precision=None):
    return lax.dot_general(a, b, (dims, ((), ())), precision=precision, preferred_element_type=F32)


def _nn(a, b, precision=None):
    return _dot(a, b, ((1,), (0,)), precision)


def _coords():
    return lax.axis_index("x"), lax.axis_index("y"), lax.axis_index("c")


class _Carry:
    def __init__(self, tensors, items):
        self.tensors, self.items = tensors, items
        nt, ni = len(tensors), len(items)
        any_spec = pl.BlockSpec(memory_space=pl.ANY)
        self.args = [t[0] for t in tensors] + [t[1] for t in tensors]
        self.in_specs = [any_spec] * (2 * nt)
        self.out_specs = [any_spec] * nt
        self.out_shape = [jax.ShapeDtypeStruct(t[1].shape, t[1].dtype) for t in tensors]
        self.scratch = [pltpu.SemaphoreType.DMA((ni, N_DEV - 1)), pltpu.SemaphoreType.DMA((ni, N_DEV - 1)), pltpu.SemaphoreType.DMA((ni,))]

    def aliases(self, first_in, first_out):
        nt = len(self.tensors)
        return {first_in + nt + t: first_out + t for t in range(nt)}

    def _slot(self, ref, t, idx, win):
        cw = self.tensors[t][2]
        return ref.at[idx, win] if cw is None else ref.at[win, pl.ds(pl.multiple_of(idx * cw, LANE), cw)]

    def _copies(self, src_refs, dst_refs, sems):
        send_sems, recv_sems, local_sems = sems
        x, y, c = _coords()
        my = 4 * x + 2 * y + c
        out = []
        for n, (t, r0, rows) in enumerate(self.items):
            win = pl.ds(r0, rows)
            out.append(pltpu.make_async_copy(self._slot(src_refs[t], t, my, win), dst_refs[t].at[my, win], local_sems.at[n]))
            for k in range(1, N_DEV):
                px, py, pc = x ^ ((k >> 2) & 1), y ^ ((k >> 1) & 1), c ^ (k & 1)
                out.append(pltpu.make_async_remote_copy(
                    src_ref=self._slot(src_refs[t], t, 4 * px + 2 * py + pc, win), dst_ref=dst_refs[t].at[my, win],
                    send_sem=send_sems.at[n, k - 1], recv_sem=recv_sems.at[n, k - 1],
                    device_id=(px, py, pc), device_id_type=MESH))
        return out

    def hook(self, step, last, src_refs, dst_refs, sems):
        if last == 0:
            for cp in self._copies(src_refs, dst_refs, sems):
                cp.start()
            for cp in self._copies(src_refs, dst_refs, sems):
                cp.wait()
            return

        @pl.when(step == 0)
        def _():
            for cp in self._copies(src_refs, dst_refs, sems):
                cp.start()

        @pl.when(step == last)
        def _():
            for cp in self._copies(src_refs, dst_refs, sems):
                cp.wait()


class _GatherCarry(_Carry):
    def hook(self, step, last, src_refs, dst_refs, sems):
        send_sems, recv_sems, local_sems = sems
        x, y, c = _coords()
        me, sibling = (x, y, c), (x, y, 1 - c)
        chips = [(1 - x, y), (x, 1 - y), (1 - x, 1 - y)]

        def per_item(fn):
            for n, (t, r0, rows) in enumerate(self.items):
                win = pl.ds(r0, rows)

                def copy(k, block, to, own=False, n=n, t=t, win=win):
                    slot = self._slot(dst_refs[t], t, 4 * block[0] + 2 * block[1] + block[2], win)
                    return pltpu.make_async_remote_copy(
                        src_ref=src_refs[t].at[win] if own else slot, dst_ref=slot,
                        send_sem=send_sems.at[n, k], recv_sem=recv_sems.at[n, k], device_id=to, device_id_type=MESH)

                mine = pltpu.make_async_copy(src_refs[t].at[win], self._slot(dst_refs[t], t, 4 * x + 2 * y + c, win), local_sems.at[n])
                first = [copy(0, me, sibling, own=True)] + [copy(1 + j, me, (*chip, c), own=True) for j, chip in enumerate(chips)]
                fn(copy, mine, first)

        def begin(copy, mine, first):
            mine.start()
            for cp in first:
                cp.start()

        def pass_on(copy, mine, first):
            for j, chip in enumerate(chips):
                copy(1 + j, (*chip, c), me).wait_recv()
                copy(4 + j, (*chip, c), sibling).start()

        def finish(copy, mine, first):
            copy(0, sibling, me).wait_recv()
            for j, chip in enumerate(chips):
                copy(4 + j, (*chip, 1 - c), me).wait_recv()
            for cp in first + [copy(4 + j, (*chip, c), sibling) for j, chip in enumerate(chips)]:
                cp.wait_send()
            mine.wait()

        if last == 0:
            for fn in (begin, pass_on, finish):
                per_item(fn)
            return
        late = max(1, (7 * last) // 8)
        for at, fn in ((0, begin), (late, pass_on), (last, finish)):
            pl.when(step == at)(functools.partial(per_item, fn))


def _mm(a, b, *, name, ta=False, tb=False, out_dtype=F32, add=None, tiles=None, carry=None):
    M, K = (a.shape[1], a.shape[0]) if ta else a.shape
    N = b.shape[0] if tb else b.shape[1]
    assert (b.shape[1] if tb else b.shape[0]) == K, (a.shape, b.shape, ta, tb)
    tk = max(t for t in range(LANE, min(K, MM_MAX_TK) + 1, LANE) if K % t == 0)
    tm, tn, tk = tiles or (_pick(M, (1024, 512, 256, 128)), _pick(N, (512, 256, 128)), tk)
    nk = K // tk
    dims = ((0 if ta else 1,), (1 if tb else 0,))

    n_in = 2 + (add is not None)
    nt = len(carry.tensors) if carry else 0
    gi, gj = M // tm, N // tn

    def kern(*refs):
        a_ref, b_ref = refs[:2]
        add_ref = refs[2] if add is not None else None
        o_ref = refs[n_in + 2 * nt]
        scr = refs[n_in + 3 * nt + 1:]
        if carry:
            step = (pl.program_id(0) * gj + pl.program_id(1)) * nk + pl.program_id(2)
            carry.hook(step, gi * gj * nk - 1, refs[n_in:n_in + nt], refs[n_in + 2 * nt + 1:n_in + 3 * nt + 1], scr[-3:])

        def finish(r):
            if add is not None:
                r = r + add_ref[...].astype(F32)
            o_ref[...] = r.astype(o_ref.dtype)

        if nk == 1:
            finish(_dot(a_ref[...], b_ref[...], dims))
            return
        acc = scr[0]
        k = pl.program_id(2)

        @pl.when(k == 0)
        def _():
            acc[...] = jnp.zeros_like(acc)

        acc[...] += _dot(a_ref[...], b_ref[...], dims)

        @pl.when(k == nk - 1)
        def _():
            finish(acc[...])

    a_spec = pl.BlockSpec((tk, tm), lambda i, j, k: (k, i)) if ta else pl.BlockSpec((tm, tk), lambda i, j, k: (i, k))
    b_spec = pl.BlockSpec((tn, tk), lambda i, j, k: (j, k)) if tb else pl.BlockSpec((tk, tn), lambda i, j, k: (k, j))
    o_spec = pl.BlockSpec((tm, tn), lambda i, j, k: (i, j))
    in_specs = [a_spec, b_spec] + ([o_spec] if add is not None else [])
    args = (a, b) + ((add,) if add is not None else ())
    scratch = [pltpu.VMEM((tm, tn), F32)] if nk > 1 else []
    o_shape = jax.ShapeDtypeStruct((M, N), out_dtype)
    if not carry:
        return pl.pallas_call(
            kern, name=name, grid=(gi, gj, nk), in_specs=in_specs, out_specs=o_spec, out_shape=o_shape, scratch_shapes=scratch,
            compiler_params=_cparams(("parallel", "parallel", "arbitrary")),
        )(*args)
    res = pl.pallas_call(
        kern, name=name, grid=(gi, gj, nk), in_specs=in_specs + carry.in_specs, out_specs=[o_spec] + carry.out_specs,
        out_shape=[o_shape] + carry.out_shape, scratch_shapes=scratch + carry.scratch,
        input_output_aliases=carry.aliases(n_in, 1), compiler_params=_cparams(("arbitrary", "arbitrary", "arbitrary")),
    )(*args, *carry.args)
    return res[0], list(res[1:])


def _stage_specs(acts, params, consts, tile, ct):
    act_specs = [pl.BlockSpec((tile, ct), functools.partial(lambda j, i, o: (i, o + j), o=off // ct)) for _, off in acts]
    par_specs = [pl.BlockSpec(bs, functools.partial(lambda j, i, im: im(j), im=im)) for _, bs, im in params]
    con_specs = [pl.BlockSpec(bs, functools.partial(lambda j, i, im: im(j), im=im)) for _, bs, im in consts]
    return act_specs, par_specs, con_specs


def _stage_fwd(f, name, n_rows, width, tile, ct, acts, params, consts, out_dtypes, carry=None):
    for _, off in acts:
        assert off % ct == 0
    na, npar, nc, no = len(acts), len(params), len(consts), len(out_dtypes)
    n_in = na + npar + nc
    nt = len(carry.tensors) if carry else 0
    gj, gi = width // ct, n_rows // tile

    def kern(*refs):
        if carry:
            step = pl.program_id(0) * gi + pl.program_id(1)
            carry.hook(step, gj * gi - 1, refs[n_in:n_in + nt], refs[n_in + 2 * nt + no:n_in + 3 * nt + no], refs[n_in + 3 * nt + no:])
        a = [r[...].astype(F32) for r in refs[:na]]
        p = [r[...] for r in refs[na:na + npar]]
        c = [r[...] for r in refs[na + npar:n_in]]
        outs = f(a, p, c, pl.program_id(1) * tile)
        for r, o in zip(refs[n_in + 2 * nt:n_in + 2 * nt + no], outs):
            r[...] = o.astype(r.dtype)

    act_specs, par_specs, con_specs = _stage_specs(acts, params, consts, tile, ct)
    o_spec = pl.BlockSpec((tile, ct), lambda j, i: (i, j))
    in_specs = act_specs + par_specs + con_specs
    out_shape = [jax.ShapeDtypeStruct((n_rows, width), d) for d in out_dtypes]
    args = [a for a, _ in acts] + [p for p, _, _ in params] + [c for c, _, _ in consts]
    if not carry:
        return tuple(pl.pallas_call(
            kern, name=name, grid=(gj, gi), in_specs=in_specs, out_specs=[o_spec] * no, out_shape=out_shape,
            compiler_params=_cparams(("parallel", "parallel")),
        )(*args))
    res = pl.pallas_call(
        kern, name=name, grid=(gj, gi), in_specs=in_specs + carry.in_specs, out_specs=[o_spec] * no + carry.out_specs,
        out_shape=out_shape + carry.out_shape, scratch_shapes=carry.scratch, input_output_aliases=carry.aliases(n_in, no),
        compiler_params=_cparams(("arbitrary", "arbitrary")),
    )(*args, *carry.args)
    return tuple(res[:no]), list(res[no:])


def _stage_bwd(f, name, n_rows, width, tile, ct, acts, params, consts, couts, dact_dtypes, extra_add=None, carry=None):
    na, npar, nc, no = len(acts), len(params), len(consts), len(couts)
    nx = 0 if extra_add is None else 1
    nt = len(carry.tensors) if carry else 0
    n_in = na + npar + nc + no + nx
    gj, gi = width // ct, n_rows // tile

    def kern(*refs):
        if carry:
            step = pl.program_id(0) * gi + pl.program_id(1)
            n_out = n_in + 2 * nt + na + npar
            carry.hook(step, gj * gi - 1, refs[n_in:n_in + nt], refs[n_out:n_out + nt], refs[n_out + nt:])
        a = [r[...].astype(F32) for r in refs[:na]]
        p = [r[...] for r in refs[na:na + npar]]
        c = [r[...] for r in refs[na + npar:na + npar + nc]]
        base = na + npar + nc
        co = [r[...].astype(F32) for r in refs[base:base + no]]
        base += no
        x_refs = refs[base:base + nx]
        base += nx + 2 * nt
        da_refs = refs[base:base + na]
        dp_refs = refs[base + na:base + na + npar]
        row0 = pl.program_id(1) * tile
        _, vjp = jax.vjp(lambda aa, pp: tuple(f(aa, pp, c, row0)), a, p)
        da, dp = vjp(tuple(co))
        for k, (r, d) in enumerate(zip(da_refs, da)):
            if k == 0 and nx:
                d = d + x_refs[0][...].astype(F32)
            r[...] = d.astype(r.dtype)
        first = pl.program_id(1) == 0
        for r, d in zip(dp_refs, dp):
            @pl.when(first)
            def _(r=r, d=d):
                r[...] = d

            @pl.when(jnp.logical_not(first))
            def _(r=r, d=d):
                r[...] += d

    act_specs, par_specs, con_specs = _stage_specs(acts, params, consts, tile, ct)
    t_spec = pl.BlockSpec((tile, ct), lambda j, i: (i, j))
    co_specs = [pl.BlockSpec((tile, ct), functools.partial(lambda j, i, o: (i, o + j), o=off // ct)) for _, off in couts]
    x_specs = [] if extra_add is None else [pl.BlockSpec((tile, ct), functools.partial(lambda j, i, o: (i, o + j), o=extra_add[1] // ct))]
    x_args = [] if extra_add is None else [extra_add[0]]
    in_specs = act_specs + par_specs + con_specs + co_specs + x_specs
    out_specs = [t_spec] * na + par_specs
    out_shape = [jax.ShapeDtypeStruct((n_rows, width), d) for d in dact_dtypes] + [jax.ShapeDtypeStruct(p.shape, F32) for p, _, _ in params]
    args = [a for a, _ in acts] + [p for p, _, _ in params] + [c for c, _, _ in consts] + [c for c, _ in couts] + x_args
    if not carry:
        outs = pl.pallas_call(
            kern, name=name, grid=(gj, gi), in_specs=in_specs, out_specs=out_specs, out_shape=out_shape,
            compiler_params=_cparams(("parallel", "arbitrary")),
        )(*args)
        return tuple(outs[:na]), tuple(outs[na:])
    outs = pl.pallas_call(
        kern, name=name, grid=(gj, gi), in_specs=in_specs + carry.in_specs, out_specs=out_specs + carry.out_specs,
        out_shape=out_shape + carry.out_shape, scratch_shapes=carry.scratch, input_output_aliases=carry.aliases(n_in, na + npar),
        compiler_params=_cparams(("arbitrary", "arbitrary")),
    )(*args, *carry.args)
    return tuple(outs[:na]), tuple(outs[na:na + npar]), list(outs[na + npar:])


def _row(ct):
    return (1, ct), (lambda j: (0, j))


def _f_rmsnorm(a, p, c, row0):
    x, = a
    g, = p
    return (x * lax.rsqrt(jnp.mean(x * x, axis=-1, keepdims=True) + NORM_EPS) * g,)


def _f_lora_act(a, p, c, row0, widths):
    x, = a
    dl, al = widths
    col = lax.broadcasted_iota(jnp.int32, x.shape, 1)
    return (jnp.where(col < dl, jnp.tanh(x), jnp.where(col < dl + al, x, _sigmoid(x))),)


def _head_sums_raw(x, ones):
    hi = x.astype(BF16)
    lo = (x - hi.astype(F32)).astype(BF16)
    return _nn(hi, ones) + _nn(lo, ones)


@jax.custom_vjp
def _head_sums(x, ones):
    return _head_sums_raw(x, ones)


_head_sums.defvjp(lambda x, ones: (_head_sums_raw(x, ones), ones),
                  lambda ones, ct: (_head_sums_raw(ct, ones), jnp.zeros_like(ones)))


def _f_rwkv_pre(a, p, c, row0):
    k, wlin, alin = a
    w0, a0, k_k, k_a = p
    gsum, = c
    w = -_softplus(-(w0 + wlin)) - 0.5
    lw = -jnp.exp(w)
    alpha = _sigmoid(a0 + alin)
    kk = k * k_k
    ss = _head_sums(kk * kk, gsum)
    kk = kk * lax.rsqrt(jnp.maximum(ss, 1e-24))
    k2 = k * (1.0 + (alpha - 1.0) * k_a)
    return lw, k2, -kk, kk * alpha


def _f_rwkv_post(a, p, c, row0):
    y, r, k2, v, g = a
    ln_g, ln_b, r_k = p
    gsum, = c
    inv = 1.0 / HEAD
    mean = _head_sums(y, gsum) * inv
    yc = y - mean
    var = _head_sums(yc * yc, gsum) * inv
    yn = yc * lax.rsqrt(var + GN_EPS) * ln_g + ln_b
    bonus = _head_sums(r * k2 * r_k, gsum)
    return ((yn + bonus * v) * g,)


def _f_lru_gates(a, p, c, row0, seq):
    xc, = a
    wr, br, wi, bi, lam = p
    xb = xc.astype(BF16)
    rg = _sigmoid(_nn(xb, wr[0].astype(BF16)) + br)
    ig = _sigmoid(_nn(xb, wi[0].astype(BF16)) + bi)
    log_a = -LRU_C * rg * _softplus(-lam)
    a_t = jnp.exp(log_a)
    mult = jnp.sqrt(_neg_expm1(2.0 * log_a))
    row = row0 + lax.broadcasted_iota(jnp.int32, xc.shape, 0)
    mult = jnp.where(row % seq == 0, 1.0, mult)
    return a_t, mult * ig * xc


def _f_lru_post(a, p, c, row0):
    h, gate = a
    g, = p
    y = h * _gelu(gate)
    return (y * lax.rsqrt(jnp.mean(y * y, axis=-1, keepdims=True) + NORM_EPS) * g,)


def _f_swiglu(a, p, c, row0):
    gate, up = a
    return (gate * _sigmoid(gate) * up,)


def _shift_down(x, s, row):
    return jnp.where(row >= s, pltpu.roll(x, s, 0), 0.0)


def _shift_up(x, s, row):
    n = x.shape[0]
    return jnp.where(row < n - s, pltpu.roll(x, n - s, 0), 0.0)


def _seq_call(kern, name, bl, seq, width, ct, ins, outs, acc_outs=()):
    def spec(off, rows):
        if rows is None:
            return pl.BlockSpec((seq, ct), functools.partial(lambda j, b, o: (b, o + j), o=off // ct))
        return pl.BlockSpec((rows, ct), lambda j, b: (0, j))

    in_specs = [spec(off, rows) for _, off, rows in ins]
    out_specs = [spec(0, None) for _ in outs] + [spec(0, rows) for _, rows in acc_outs]
    out_shape = [jax.ShapeDtypeStruct((bl * seq, width), d) for d in outs] + [jax.ShapeDtypeStruct((rows, width), F32) for _, rows in acc_outs]
    res = pl.pallas_call(
        kern, name=name, grid=(width // ct, bl), in_specs=in_specs, out_specs=out_specs, out_shape=out_shape,
        compiler_params=_cparams(("parallel", "arbitrary")),
    )(*[a for a, _, _ in ins])
    return tuple(res)


def _acc(ref, val):
    first = pl.program_id(1) == 0

    @pl.when(first)
    def _():
        ref[...] = val

    @pl.when(jnp.logical_not(first))
    def _():
        ref[...] += val


def _lerp_fwd(p, off, mu, bl, seq, width, ct):
    def kern(p_ref, mu_ref, o_ref):
        x = p_ref[...]
        row = lax.broadcasted_iota(jnp.int32, x.shape, 0)
        o_ref[...] = x + (_shift_down(x, 1, row) - x) * mu_ref[...]

    return _seq_call(kern, "lerp_fwd", bl, seq, width, ct, [(p, off, None), (mu, 0, 1)], [F32])[0]


def _lerp_bwd(p, off, mu, mu_off, dps_parts, name, bl, seq, width, ct, out_dtype):
    nd = len(dps_parts)

    def kern(*refs):
        p_ref, mu_ref = refs[:2]
        dp_ref, dmu_ref = refs[2 + nd:]
        x = p_ref[...]
        d = refs[2][...].astype(F32)
        for r in refs[3:2 + nd]:
            d = d + r[...].astype(F32)
        m = mu_ref[...]
        row = lax.broadcasted_iota(jnp.int32, x.shape, 0)
        dp_ref[...] = (d * (1.0 - m) + _shift_up(d * m, 1, row)).astype(dp_ref.dtype)
        _acc(dmu_ref, jnp.sum(d * (_shift_down(x, 1, row) - x), axis=0, keepdims=True))

    ins = [(p, off, None), (mu[:, mu_off:mu_off + width], 0, 1)] + [(a, 0, None) for a in dps_parts]
    return _seq_call(kern, name, bl, seq, width, ct, ins, [out_dtype], [(None, 1)])


def _conv_fwd(p, off, cw, cb, bl, seq, width, ct):
    nw = cw.shape[0]

    def kern(x_ref, w_ref, b_ref, o_ref):
        x = x_ref[...]
        row = lax.broadcasted_iota(jnp.int32, x.shape, 0)
        acc = b_ref[...] + x * w_ref[pl.ds(nw - 1, 1), :]
        for s in range(1, nw):
            acc = acc + _shift_down(x, s, row) * w_ref[pl.ds(nw - 1 - s, 1), :]
        o_ref[...] = acc

    return _seq_call(kern, "conv_fwd", bl, seq, width, ct, [(p, off, None), (cw, 0, nw), (cb, 0, 1)], [F32])[0]


def _conv_bwd(p, off, cw, dxc, bl, seq, width, ct, out_dtype):
    nw = cw.shape[0]

    def kern(x_ref, w_ref, d_ref, dx_ref, dw_ref, db_ref):
        x = x_ref[...]
        d = d_ref[...]
        row = lax.broadcasted_iota(jnp.int32, x.shape, 0)
        wrow = lax.broadcasted_iota(jnp.int32, dw_ref.shape, 0)
        dx = d * w_ref[pl.ds(nw - 1, 1), :]
        dw = jnp.where(wrow == nw - 1, jnp.sum(d * x, axis=0, keepdims=True), 0.0)
        for s in range(1, nw):
            dx = dx + _shift_up(d, s, row) * w_ref[pl.ds(nw - 1 - s, 1), :]
            dw = jnp.where(wrow == nw - 1 - s, jnp.sum(d * _shift_down(x, s, row), axis=0, keepdims=True), dw)
        dx_ref[...] = dx.astype(dx_ref.dtype)
        _acc(dw_ref, dw)
        _acc(db_ref, jnp.sum(d, axis=0, keepdims=True))

    return _seq_call(kern, "conv_bwd", bl, seq, width, ct, [(p, off, None), (cw, 0, nw), (dxc, 0, None)], [out_dtype], [(None, nw), (None, 1)])


def _lru_scan_fwd(a, bx, bl, seq, width, ct):
    def kern(a_ref, b_ref, h_ref):
        av = a_ref[...]
        bv = b_ref[...]
        row = lax.broadcasted_iota(jnp.int32, av.shape, 0)
        d = 1
        while d < seq:
            a_sh = jnp.where(row >= d, pltpu.roll(av, d, 0), 1.0)
            b_sh = jnp.where(row >= d, pltpu.roll(bv, d, 0), 0.0)
            bv = av * b_sh + bv
            av = av * a_sh
            d *= 2
        h_ref[...] = bv

    return _seq_call(kern, "lru_scan_fwd", bl, seq, width, ct, [(a, 0, None), (bx, 0, None)], [F32])[0]


def _lru_scan_bwd(a, h, dh, bl, seq, width, ct):
    def kern(a_ref, h_ref, d_ref, da_ref, db_ref):
        row = lax.broadcasted_iota(jnp.int32, a_ref.shape, 0)
        al = _shift_up(a_ref[...], 1, row)
        g = d_ref[...]
        d = 1
        while d < seq:
            keep = row < seq - d
            al_sh = jnp.where(keep, pltpu.roll(al, seq - d, 0), 1.0)
            g_sh = jnp.where(keep, pltpu.roll(g, seq - d, 0), 0.0)
            g = al * g_sh + g
            al = al * al_sh
            d *= 2
        db_ref[...] = g
        da_ref[...] = g * _shift_down(h_ref[...], 1, row)

    return _seq_call(kern, "lru_scan_bwd", bl, seq, width, ct, [(a, 0, None), (h, 0, None), (dh, 0, None)], [F32, F32])


_FORMS = {"nn": ((1,), (0,)), "nt": ((1,), (1,)), "tn": ((0,), (0,))}
_FORM_GRADS = {"nn": (("nt", "g", "b"), ("tn", "a", "g")),
               "nt": (("nn", "g", "b"), ("tn", "g", "a")),
               "tn": (("nt", "b", "g"), ("nn", "a", "g"))}


def _split_bf16(x):
    hi = x.astype(BF16)
    return hi, (x - hi.astype(F32)).astype(BF16)


def _pdot_raw(a, b, form, passes):
    dims = _FORMS[form]
    if passes == 1:
        return _dot(a.astype(BF16), b.astype(BF16), dims)
    ah, al = _split_bf16(a)
    bh, bl = _split_bf16(b)
    return _dot(ah, bh, dims) + (_dot(ah, bl, dims) + _dot(al, bh, dims))


@functools.partial(jax.custom_vjp, nondiff_argnums=(2, 3))
def _pdot(a, b, form, passes):
    return _pdot_raw(a, b, form, passes)


def _pdot_fwd(a, b, form, passes):
    return _pdot_raw(a, b, form, passes), (a, b)


def _pdot_bwd(form, passes, res, g):
    vals = {"a": res[0], "b": res[1], "g": g}
    (fa, xa, ya), (fb, xb, yb) = _FORM_GRADS[form]
    return _pdot_raw(vals[xa], vals[ya], fa, passes), _pdot_raw(vals[xb], vals[yb], fb, passes)


_pdot.defvjp(_pdot_fwd, _pdot_bwd)


def _neumann_raw(a_list, n_levels, passes):
    eye = (lax.broadcasted_iota(jnp.int32, a_list[0].shape, 0) == lax.broadcasted_iota(jnp.int32, a_list[0].shape, 1)).astype(F32)
    pw = list(a_list)
    x = [eye + a for a in a_list]
    for _ in range(n_levels):
        pw = [_pdot_raw(p, p, "nn", passes) for p in pw]
        x = [xi + _pdot_raw(xi, p, "nn", passes) for xi, p in zip(x, pw)]
    return x


@functools.partial(jax.custom_vjp, nondiff_argnums=(1, 2))
def _neumann_inverse(a_list, n_levels, passes):
    return _neumann_raw(a_list, n_levels, passes)


def _neumann_fwd(a_list, n_levels, passes):
    x = _neumann_raw(a_list, n_levels, passes)
    return x, x


def _neumann_bwd(n_levels, passes, x, ct):
    return ([_pdot_raw(xi, _pdot_raw(c, xi, "nt", passes), "tn", passes) for xi, c in zip(x, ct)],)


_neumann_inverse.defvjp(_neumann_fwd, _neumann_bwd)


def _scan_block(states, units, p_main=1, p_inv=1):
    C = units[0][0][0].shape[0]
    C2 = 2 * C
    ri = lax.broadcasted_iota(jnp.int32, (C, C), 0)
    ci = lax.broadcasted_iota(jnp.int32, (C, C), 1)
    tri = (ri >= ci).astype(F32)
    i2 = lax.broadcasted_iota(jnp.int32, (C2, C2), 0)
    j2 = lax.broadcasted_iota(jnp.int32, (C2, C2), 1)
    same = (i2 // C) == (j2 // C)
    strict = jnp.logical_and(same, (i2 % C) > (j2 % C))
    incl = jnp.logical_and(same, (i2 % C) >= (j2 % C))
    lane = lax.broadcasted_iota(jnp.int32, (1, LANE), 1)
    m0, m1 = (lane < HEAD).astype(F32), (lane >= HEAD).astype(F32)
    stack = lambda z: jnp.concatenate([z * m0, z * m1], axis=0)
    ids = [(i, g) for g in range(len(units[0])) for i in range(len(units))]

    pre = {}
    for i, g in ids:
        r, lw, k, v, a, b = units[i][g]
        cs = _nn(tri, lw, HI)
        p_incl = jnp.exp(cs)
        p_rec = jnp.exp(-cs)
        xr = jnp.concatenate([stack(a * jnp.exp(cs - lw)), stack(r * p_incl)], axis=0)
        bk = jnp.concatenate([stack(b * p_rec), stack(k * p_rec)], axis=0)
        pre[i, g] = (xr, bk, stack(v), jnp.exp(jnp.sum(lw, axis=0, keepdims=True)))
    gm = {u: _pdot(pre[u][0], pre[u][1], "nt", p_main) for u in ids}
    a_ak = {u: jnp.where(strict, gm[u][:C2, C2:], 0.0) for u in ids}
    r_bk = {u: jnp.concatenate([jnp.where(incl, gm[u][C2:, :C2], 0.0), jnp.where(incl, gm[u][C2:, C2:], 0.0)], axis=1) for u in ids}
    a_ab = [jnp.where(strict, gm[u][:C2, :C2], 0.0) for u in ids]
    x = dict(zip(ids, _neumann_inverse(a_ab, int(math.log2(C)) - 1, p_inv)))
    akv = {u: _pdot(a_ak[u], pre[u][2], "nn", p_main) for u in ids}

    states = list(states)
    pairs = range(len(units))
    ys = [[None] * len(units[0]) for _ in units]
    for g in range(len(units[0])):
        xs = [_pdot(pre[i, g][0], states[i], "nt", p_main) for i in pairs]
        us = [_pdot(x[i, g], xs[i][:C2] + akv[i, g], "nn", p_inv) for i in pairs]
        uv = [jnp.concatenate([us[i], pre[i, g][2]], axis=0) for i in pairs]
        y2 = [xs[i][C2:] + _pdot(r_bk[i, g], uv[i], "nn", p_main) for i in pairs]
        for i in pairs:
            ys[i][g] = y2[i][:C] + y2[i][C:]
        states = [(states[i] + _pdot(uv[i], pre[i, g][1], "tn", p_main)) * pre[i, g][3] for i in pairs]
    return ys, states


def _scan_dims(seq, rw):
    G = _pick(seq // SCAN_CHUNK, (SCAN_GROUP, 2, 1))
    NP = _pick(rw // LANE, (SCAN_PAIRS, 4, 2, 1))
    C = SCAN_CHUNK * G
    return SCAN_CHUNK, G, NP, C, seq // C, rw // (NP * LANE)


def _rwkv_scan_fwd(r, lw, k2, v, na, bb, p, bl, seq, rw, carry=None):
    cs, G, NP, C, nc, nhg = _scan_dims(seq, rw)
    nt = len(carry.tensors) if carry else 0

    def kern(*refs):
        in_refs = refs[:6]
        y_ref, st_ref = refs[6 + 2 * nt:8 + 2 * nt]
        s_scr = refs[8 + 3 * nt]
        if carry:
            step = (pl.program_id(0) * nhg + pl.program_id(1)) * nc + pl.program_id(2)
            carry.hook(step, bl * nhg * nc - 1, refs[6:6 + nt], refs[8 + 2 * nt:8 + 3 * nt], refs[9 + 3 * nt:])

        @pl.when(pl.program_id(2) == 0)
        def _():
            s_scr[...] = jnp.zeros_like(s_scr)

        st_ref[...] = s_scr[...]
        units = [[tuple(ref[pl.ds(g * cs, cs), pl.ds(i * LANE, LANE)] for ref in in_refs) for g in range(G)] for i in range(NP)]
        ys, s_new = _scan_block([s_scr[i] for i in range(NP)], units)
        for i in range(NP):
            s_scr[i] = s_new[i]
            for g in range(G):
                y_ref[pl.ds(g * cs, cs), pl.ds(i * LANE, LANE)] = ys[i][g]

    def tok(off):
        return pl.BlockSpec((C, NP * LANE), functools.partial(lambda b, h, c, o: (b * nc + c, o + h), o=off // (NP * LANE)))

    in_specs = [tok(0), tok(0), tok(0), tok(2 * rw), tok(0), tok(0)]
    out_specs = [tok(0), pl.BlockSpec((NP, LANE, LANE), lambda b, h, c: ((b * nhg + h) * nc + c, 0, 0))]
    out_shape = [jax.ShapeDtypeStruct((bl * seq, rw), F32), jax.ShapeDtypeStruct((bl * nhg * nc * NP, LANE, LANE), F32)]
    scratch = [pltpu.VMEM((NP, LANE, LANE), F32)]
    if not carry:
        y, st = pl.pallas_call(
            kern, name="rwkv_scan_fwd", grid=(bl, nhg, nc), in_specs=in_specs, out_specs=out_specs, out_shape=out_shape,
            scratch_shapes=scratch, compiler_params=_cparams(("parallel", "parallel", "arbitrary")),
        )(p, lw, k2, p, na, bb)
        return y, st
    res = pl.pallas_call(
        kern, name="rwkv_scan_fwd", grid=(bl, nhg, nc), in_specs=in_specs + carry.in_specs, out_specs=out_specs + carry.out_specs,
        out_shape=out_shape + carry.out_shape, scratch_shapes=scratch + carry.scratch, input_output_aliases=carry.aliases(6, 2),
        compiler_params=_cparams(("arbitrary", "arbitrary", "arbitrary")),
    )(p, lw, k2, p, na, bb, *carry.args)
    return res[0], res[1], list(res[2:])


def _rwkv_scan_bwd(lw, k2, na, bb, p, st, dy, bl, seq, rw, carry=None):
    cs, G, NP, C, nc, nhg = _scan_dims(seq, rw)
    nt = len(carry.tensors) if carry else 0

    def kern(*refs):
        in_refs = refs[:6]
        st_ref, dy_ref = refs[6:8]
        out_refs = refs[8 + 2 * nt:14 + 2 * nt]
        ds_scr = refs[14 + 3 * nt]
        if carry:
            step = (pl.program_id(0) * nhg + pl.program_id(1)) * nc + pl.program_id(2)
            carry.hook(step, bl * nhg * nc - 1, refs[8:8 + nt], refs[14 + 2 * nt:14 + 3 * nt], refs[15 + 3 * nt:])

        @pl.when(pl.program_id(2) == 0)
        def _():
            ds_scr[...] = jnp.zeros_like(ds_scr)

        win = lambda ref, i, g: ref[pl.ds(g * cs, cs), pl.ds(i * LANE, LANE)]
        units = [[tuple(win(ref, i, g) for ref in in_refs) for g in range(G)] for i in range(NP)]
        _, vjp = jax.vjp(_scan_block, [st_ref[i] for i in range(NP)], units)
        dys = [[win(dy_ref, i, g) for g in range(G)] for i in range(NP)]
        ds, dunits = vjp((dys, [ds_scr[i] for i in range(NP)]))
        for i in range(NP):
            ds_scr[i] = ds[i]
            for g in range(G):
                for ref, d in zip(out_refs, dunits[i][g]):
                    ref[pl.ds(g * cs, cs), pl.ds(i * LANE, LANE)] = d

    def tok(off):
        return pl.BlockSpec((C, NP * LANE), functools.partial(lambda b, h, c, o: (b * nc + (nc - 1 - c), o + h), o=off // (NP * LANE)))

    st_spec = pl.BlockSpec((NP, LANE, LANE), lambda b, h, c: ((b * nhg + h) * nc + (nc - 1 - c), 0, 0))
    in_specs = [tok(0), tok(0), tok(0), tok(2 * rw), tok(0), tok(0), st_spec, tok(0)]
    out_shape = [jax.ShapeDtypeStruct((bl * seq, rw), F32)] * 6
    scratch = [pltpu.VMEM((NP, LANE, LANE), F32)]
    if not carry:
        return pl.pallas_call(
            kern, name="rwkv_scan_bwd", grid=(bl, nhg, nc), in_specs=in_specs, out_specs=[tok(0)] * 6, out_shape=out_shape,
            scratch_shapes=scratch, compiler_params=_cparams(("parallel", "parallel", "arbitrary")),
        )(p, lw, k2, p, na, bb, st, dy)
    res = pl.pallas_call(
        kern, name="rwkv_scan_bwd", grid=(bl, nhg, nc), in_specs=in_specs + carry.in_specs, out_specs=[tok(0)] * 6 + carry.out_specs,
        out_shape=out_shape + carry.out_shape, scratch_shapes=scratch + carry.scratch, input_output_aliases=carry.aliases(8, 6),
        compiler_params=_cparams(("arbitrary", "arbitrary", "arbitrary")),
    )(p, lw, k2, p, na, bb, st, dy, *carry.args)
    return res[:6], list(res[6:])


def _loss_head(h2, g_final, target, tile):
    n, d = h2.shape
    nt = n // tile

    def kern(h_ref, g_ref, t_ref, dh_ref, dg_ref, l_ref, dhb_ref):
        def f(h, g):
            y = h * lax.rsqrt(jnp.mean(h * h, axis=-1, keepdims=True) + NORM_EPS) * g
            e = y - t_ref[...]
            return 0.5 * jnp.sum(jnp.mean(e * e, axis=-1, keepdims=True))

        loss, (dh, dg) = jax.value_and_grad(f, argnums=(0, 1))(h_ref[...], g_ref[...])
        dh_ref[...] = dh
        dhb_ref[...] = dh.astype(dhb_ref.dtype)
        first = pl.program_id(0) == 0

        @pl.when(first)
        def _():
            dg_ref[...] = dg
            l_ref[...] = jnp.zeros_like(l_ref) + loss

        @pl.when(jnp.logical_not(first))
        def _():
            dg_ref[...] += dg
            l_ref[...] += loss

    row = pl.BlockSpec((tile, d), lambda i: (i, 0))
    vec = pl.BlockSpec((1, d), lambda i: (0, 0))
    return pl.pallas_call(
        kern, name="loss_head", grid=(nt,), in_specs=[row, vec, row],
        out_specs=[row, vec, pl.BlockSpec((1, LANE), lambda i: (0, 0)), row],
        out_shape=[jax.ShapeDtypeStruct((n, d), F32), jax.ShapeDtypeStruct((1, d), F32), jax.ShapeDtypeStruct((1, LANE), F32),
                   jax.ShapeDtypeStruct((n, d), BF16)],
        compiler_params=_cparams(("arbitrary",)),
    )(h2, g_final, target)


def _adamw(parts, w, m, v, name, carry=None):
    n_parts, Rp, Cp = parts.shape
    R, Cc = w.shape
    assert Rp >= R and Cp >= Cc
    tr = _pick(R, tuple(t for t in (1024, 512, 256, 128, 64, 32, 16) if t * Cp <= 128 * 1024) + (8,))
    part = (lambda ref, s: ref[s]) if Cp == Cc else (lambda ref, s: ref[s, :, pl.ds(0, Cc)])
    c1, c2 = 1.0 - ADAM_B1, 1.0 - ADAM_B2
    bc1, bc2 = 1.0 - ADAM_B1 ** ADAM_STEP, 1.0 - ADAM_B2 ** ADAM_STEP

    nt = len(carry.tensors) if carry else 0

    def kern(*refs):
        p_ref, w_ref, m_ref, v_ref = refs[:4]
        g_ref, d_ref, nm_ref, nv_ref = refs[4 + 2 * nt:8 + 2 * nt]
        if carry:
            carry.hook(pl.program_id(0), R // tr - 1, refs[4:4 + nt], refs[8 + 2 * nt:8 + 3 * nt], refs[8 + 3 * nt:])
        g = part(p_ref, 0).astype(F32)
        for s in range(1, n_parts):
            g = g + part(p_ref, s).astype(F32)
        m2 = ADAM_B1 * m_ref[...] + c1 * g
        v2 = ADAM_B2 * v_ref[...] + c2 * (g * g)
        g_ref[...] = g
        nm_ref[...] = m2
        nv_ref[...] = v2
        d_ref[...] = -ADAM_LR * ((m2 / bc1) / (jnp.sqrt(v2 / bc2) + ADAM_EPS) + ADAM_WD * w_ref[...])

    blk = pl.BlockSpec((tr, Cc), lambda i: (i, 0))
    in_specs = [pl.BlockSpec((n_parts, tr, Cp), lambda i: (0, i, 0)), blk, blk, blk]
    out_shape = [jax.ShapeDtypeStruct((R, Cc), F32)] * 4
    if not carry:
        return pl.pallas_call(
            kern, name=name, grid=(R // tr,), in_specs=in_specs, out_specs=[blk] * 4, out_shape=out_shape,
            compiler_params=_cparams(("parallel",)),
        )(parts, w, m, v)
    res = pl.pallas_call(
        kern, name=name, grid=(R // tr,), in_specs=in_specs + carry.in_specs, out_specs=[blk] * 4 + carry.out_specs,
        out_shape=out_shape + carry.out_shape, scratch_shapes=carry.scratch, input_output_aliases=carry.aliases(4, 4),
        compiler_params=_cparams(("arbitrary",)),
    )(parts, w, m, v, *carry.args)
    return res[:4], list(res[4:])


def _exchange_now(carry, name):
    nt = len(carry.tensors)

    def body(*refs):
        carry.hook(0, 0, refs[:nt], refs[2 * nt:3 * nt], refs[3 * nt:])

    return list(pl.pallas_call(
        body, name=name, in_specs=carry.in_specs, out_specs=carry.out_specs, out_shape=carry.out_shape,
        scratch_shapes=carry.scratch, input_output_aliases=carry.aliases(0, 0),
    )(*carry.args))


def _carried(queue, capacity_us, *args, fn, **kw):
    carry = queue.take(capacity_us * CARRY_FILL)
    if carry is None:
        return fn(*args, **kw)
    res = fn(*args, carry=carry, **kw)
    queue.done(carry, res[-1])
    return res[0] if len(res) == 2 else res[:-1]


class _Queue:
    def __init__(self, gather, label):
        self.gather, self.label = gather, label
        self.tensors, self.fifo, self.n_flush = {}, [], 0

    def push(self, name, src, n_pieces, cost_us, cols=False):
        cw = None
        if cols:
            cw = src.shape[1] if self.gather else src.shape[1] // N_DEV
            assert cw % LANE == 0
            dst_shape = (src.shape[0], N_DEV * cw) if self.gather else (N_DEV, src.shape[0], cw)
        else:
            dst_shape = ((N_DEV,) + src.shape) if self.gather else src.shape
        n_rows = src.shape[0] if (self.gather or cols) else src.shape[1]
        rows = n_rows // n_pieces
        assert rows * n_pieces == n_rows and rows % 16 == 0, (name, src.shape)
        self.tensors[name] = [src, lax.empty(dst_shape, src.dtype), cw]
        self.fifo += [(name, p * rows, rows, cost_us / n_pieces) for p in range(n_pieces)]

    def take(self, capacity_us, count=None):
        picked = []
        while self.fifo and (len(picked) < count if count is not None else capacity_us >= 0.85 * self.fifo[0][3]):
            picked.append(self.fifo.pop(0))
            capacity_us -= picked[-1][3]
        if not picked:
            return None
        names = list(dict.fromkeys(n for n, _, _, _ in picked))
        cls = _GatherCarry if self.gather else _Carry
        carry = cls([tuple(self.tensors[n]) for n in names], [(names.index(n), r0, rows) for n, r0, rows, _ in picked])
        carry.names = names
        return carry

    def done(self, carry, dsts):
        for n, d in zip(carry.names, dsts):
            self.tensors[n][1] = d

    def flush(self, count=None):
        carry = self.take(float("inf"), count)
        if carry:
            self.done(carry, _exchange_now(carry, "%s_now_%d" % (self.label, self.n_flush)))
            self.n_flush += 1

    def result(self, name, r0=0, r1=None):
        late = [i for i, (n, p0, rows, _) in enumerate(self.fifo) if n == name and p0 < (r1 or p0 + rows) and p0 + rows > r0]
        if late:
            self.flush(late[-1] + 1)
        return self.tensors[name][1]


def _cols_from_shards(g):
    return jnp.transpose(g, (1, 0, 2)).reshape(g.shape[1], N_DEV * g.shape[2])


def _shards_from_cols(w):
    r, n = w.shape
    return jnp.transpose(w.reshape(r, N_DEV, n // N_DEV), (1, 0, 2))


def _pad_cols(w, to):
    return jnp.pad(w, ((0, 0), (0, to - w.shape[1])))


def _pack(arrs):
    flat = jnp.concatenate([a.reshape(-1) for a in arrs])
    n = _rup(flat.shape[0], 256 * LANE)
    return jnp.pad(flat, (0, n - flat.shape[0])).reshape(n // LANE, LANE)


def _unpack(mat, shapes):
    flat = mat.reshape(-1)
    out, o = [], 0
    for s in shapes:
        n = math.prod(s)
        out.append(flat[o:o + n].reshape(s))
        o += n
    return out


_SMALL = ["norm_mix_g", "mu_shift", "rwkv_w0", "rwkv_a0", "rwkv_k_k", "rwkv_k_a", "rwkv_r_k", "rwkv_ln_g", "rwkv_ln_b", "conv_b",
          "lru_wr", "lru_br", "lru_wi", "lru_bi", "lru_lambda", "lru_norm_g", "norm_ffn_g", "norm_final_g"]
_SMALL_SHARDED = ["rwkv_w2", "rwkv_a2", "rwkv_g2", "conv_w"]
_BIG = ["w_in", "w_out", "ffn_w_gate", "ffn_w_up", "ffn_w_down"]
_WEIGHTS = ['norm_mix_g', 'w_in', 'mu_shift', 'rwkv_w0', 'rwkv_w2', 'rwkv_a0', 'rwkv_a2', 'rwkv_g2', 'rwkv_k_k', 'rwkv_k_a', 'rwkv_r_k',
            'rwkv_ln_g', 'rwkv_ln_b', 'conv_w', 'conv_b', 'lru_wr', 'lru_br', 'lru_wi', 'lru_bi', 'lru_lambda', 'lru_norm_g', 'w_out',
            'norm_ffn_g', 'ffn_w_gate', 'ffn_w_up', 'ffn_w_down', 'norm_final_g']


def _step(W, M, V, x, loss_target):
    bl, seq, d = x.shape
    n = bl * seq
    rw = W["rwkv_w0"].shape[1]
    nh = W["rwkv_r_k"].shape[1]
    assert W["rwkv_r_k"].shape[2] == HEAD and nh * HEAD == rw and rw % LANE == 0
    dl, al, gl = W["rwkv_w2"].shape[1], W["rwkv_a2"].shape[1], W["rwkv_g2"].shape[1]
    dlp, alp, glp = _rup(dl, LANE), _rup(al, LANE), _rup(gl, LANE)
    lorap = dlp + alp + glp
    lw_ = W["conv_b"].shape[1]
    nblk, lbw = W["lru_wr"].shape[1], W["lru_wr"].shape[2]
    assert lbw == LANE and nblk * lbw == lw_
    o_xb, o_gate, o_rw = 0, lw_, 2 * lw_
    o_lora = 3 * rw
    rwp = o_lora + lorap
    inp = o_rw + rwp
    nsh_ff = W["ffn_w_gate"].shape[2]
    nshp = _rup(nsh_ff, LANE)
    dffp = N_DEV * nshp
    x2 = x.reshape(n, d)
    tgt2 = loss_target.reshape(n, d)

    gq = _Queue(True, "gather")
    kp = 2 if d % (2 * LANE) == 0 else 1
    gq.push("w_in", W["w_in"][0].astype(BF16), kp, 490)
    gq.push("small", _pack([W[k][0] for k in _SMALL_SHARDED]), 1, 10)
    gq.push("w_out", W["w_out"][0].astype(BF16), 2, 180)
    pad_ff = nshp - nsh_ff
    gq.push("ffn_w_gate", jnp.pad(W["ffn_w_gate"][0].astype(BF16), ((0, 0), (0, pad_ff))), 4, 490, cols=True)
    gq.push("ffn_w_up", jnp.pad(W["ffn_w_up"][0].astype(BF16), ((0, 0), (0, pad_ff))), 4, 490, cols=True)
    gq.push("ffn_w_down", jnp.pad(W["ffn_w_down"][0].astype(BF16), ((0, pad_ff), (0, 0))), 4, 490)
    gmm = functools.partial(_carried, gq, fn=_mm)
    gstage = functools.partial(_carried, gq, fn=_stage_fwd)

    o1 = 3 * rw

    def my_cols(g):
        w_l = _cols_from_shards(g)
        return jnp.concatenate([w_l[:, o1 + dl + al + gl:], w_l[:, :o1], _pad_cols(w_l[:, o1:o1 + dl], dlp),
                                _pad_cols(w_l[:, o1 + dl:o1 + dl + al], alp), _pad_cols(w_l[:, o1 + dl + al:o1 + dl + al + gl], glp)], axis=1)

    mu_l = W["mu_shift"]
    mu = jnp.concatenate([mu_l[:, :o1], _pad_cols(mu_l[:, o1:o1 + dl], dlp), _pad_cols(mu_l[:, o1 + dl:o1 + dl + al], alp),
                          _pad_cols(mu_l[:, o1 + dl + al:], glp)], axis=1)
    r_k = W["rwkv_r_k"].reshape(1, rw)

    tile = _pick(n, (256, 128, 64))
    tile_s = _pick(n, (128, 64))
    ct_seq = _pick(math.gcd(rwp, lw_), (256, 128))
    assert o_rw % ct_seq == 0 and o_gate % lw_ == 0
    ct_h = _pick(rw, (512, 256, 128))
    gi = lax.broadcasted_iota(jnp.int32, (ct_h, ct_h), 0) // HEAD
    gj = lax.broadcasted_iota(jnp.int32, (ct_h, ct_h), 1) // HEAD
    gsum = ((gi == gj).astype(BF16), (ct_h, ct_h), lambda j: (0, 0))
    full = lambda a: (a, a.shape, lambda j: (0,) * a.ndim)
    rowp = lambda a, ct: (a,) + _row(ct)

    u1, = gstage(210, _f_rmsnorm, "norm_mix_fwd", n, d, tile, d, [(x2, 0)], [full(W["norm_mix_g"])], [], [BF16])
    p, w_rows = None, []
    for i in range(kp):
        rows = slice(i * (d // kp), (i + 1) * (d // kp))
        g_in = gq.result("w_in", rows.start, rows.stop)
        w_rows.append(my_cols(g_in[:, rows, :]))
        p = gmm(190, u1[:, rows], w_rows[-1], name="mm_in_%d" % i, add=p)
    w_in = jnp.concatenate(w_rows, axis=0)
    g_small = gq.result("small")
    sm_shapes = [W[k][0].shape for k in _SMALL_SHARDED]
    sm = [_unpack(g_small[s], sm_shapes) for s in range(N_DEV)]
    w2, a2, g2, conv_w = [jnp.concatenate([sm[s][i] for s in range(N_DEV)], axis=1) for i in range(4)]
    w_lora = jnp.zeros((lorap, 3 * rw), F32)
    w_lora = w_lora.at[:dl, :rw].set(w2).at[dlp:dlp + al, rw:2 * rw].set(a2).at[dlp + alp:dlp + alp + gl, 2 * rw:].set(g2)
    w_lora = w_lora.astype(BF16)
    ps = _lerp_fwd(p, o_rw, mu, bl, seq, rwp, ct_seq)
    f_lora = functools.partial(_f_lora_act, widths=(dlp, alp))
    lact, = _stage_fwd(f_lora, "lora_act_fwd", n, lorap, tile, lorap, [(ps, o_lora)], [], [], [BF16])
    wag = _mm(lact, w_lora, name="mm_lora")
    pre_par = [rowp(W["rwkv_w0"], ct_h), rowp(W["rwkv_a0"], ct_h), rowp(W["rwkv_k_k"], ct_h), rowp(W["rwkv_k_a"], ct_h)]
    pre_acts = [(ps, rw), (wag, 0), (wag, rw)]
    lw, k2, na, bb = gstage(120, _f_rwkv_pre, "rwkv_pre_fwd", n, rw, tile_s, ct_h, pre_acts, pre_par, [gsum], [F32] * 4)
    ysc, st = _carried(gq, 230, None, lw, k2, None, na, bb, ps, bl, seq, rw, fn=_rwkv_scan_fwd)
    post_par = [rowp(W["rwkv_ln_g"], ct_h), rowp(W["rwkv_ln_b"], ct_h), rowp(r_k, ct_h)]
    post_acts = [(ysc, 0), (ps, 0), (k2, 0), (ps, 2 * rw), (wag, 2 * rw)]
    ya, = gstage(120, _f_rwkv_post, "rwkv_post_fwd", n, rw, tile_s, ct_h, post_acts, post_par, [gsum], [BF16])

    xc = _conv_fwd(p, o_xb, conv_w, W["conv_b"], bl, seq, lw_, ct_seq)
    f_gates = functools.partial(_f_lru_gates, seq=seq)
    blk3 = lambda a: (a[0], (1, LANE, LANE), lambda j: (j, 0, 0))
    gate_par = [blk3(W["lru_wr"]), rowp(W["lru_br"], LANE), blk3(W["lru_wi"]), rowp(W["lru_bi"], LANE), rowp(W["lru_lambda"], LANE)]
    tile_g = _pick(n, (1024, 512, 256, 128, 64))
    a_l, bx = gstage(160, f_gates, "lru_gates_fwd", n, lw_, tile_g, LANE, [(xc, 0)], gate_par, [], [F32, F32])
    ct_l = _pick(lw_, (256, 128))
    h_l = _lru_scan_fwd(a_l, bx, bl, seq, lw_, ct_l)
    lpost_par = [full(W["lru_norm_g"])]
    yb, = _stage_fwd(_f_lru_post, "lru_post_fwd", n, lw_, tile_s, lw_, [(h_l, 0), (p, o_gate)], lpost_par, [], [BF16])

    ycat = jnp.concatenate([ya, yb], axis=1)
    g_out = gq.result("w_out")
    w_out = g_out.reshape(N_DEV * g_out.shape[1], d)
    h1 = gmm(130, ycat, w_out, name="mm_out", add=x2)
    u2, = _stage_fwd(_f_rmsnorm, "norm_ffn_fwd", n, d, tile, d, [(h1, 0)], [full(W["norm_ffn_g"])], [], [BF16])
    w_gate = gq.result("ffn_w_gate")
    ff_gate = gmm(340, u2, w_gate, name="mm_gate", out_dtype=BF16)
    w_up = gq.result("ffn_w_up")
    ff_up = gmm(340, u2, w_up, name="mm_up", out_dtype=BF16)
    ct_f = _pick(dffp, (1024, 512, 256, 128))
    ff_acts = [(ff_gate, 0), (ff_up, 0)]
    act, = _stage_fwd(_f_swiglu, "swiglu_fwd", n, dffp, tile, ct_f, ff_acts, [], [], [BF16])
    gq.flush()
    w_down = gq.result("ffn_w_down").reshape(dffp, d)
    h2 = _mm(act, w_down, name="mm_down", add=h1)

    dh2, dg_final, lsum, dh2b = _loss_head(h2, W["norm_final_g"].reshape(1, d), tgt2, tile_s)
    loss = lax.psum(lsum[0, 0], ("x", "y", "c"))
    queue = _Queue(False, "exchange")

    cmm = functools.partial(_carried, queue, fn=_mm)
    cstage = functools.partial(_carried, queue, fn=_stage_bwd)
    dact = _mm(dh2b, w_down, name="mm_dact", tb=True, out_dtype=BF16)
    dw_down = _mm(act, dh2b, name="mm_dw_down", ta=True, out_dtype=BF16)
    queue.push("ffn_w_down", dw_down.reshape(N_DEV, nshp, d), 8, 1000)
    (dgate, dup), _ = _stage_bwd(_f_swiglu, "swiglu_bwd", n, dffp, tile, ct_f, ff_acts, [], [], [(dact, 0)], [BF16, BF16])
    du2 = cmm(400, dgate, w_gate, name="mm_du2_gate", tb=True)
    dw_gate = cmm(350, u2, dgate, name="mm_dw_gate", ta=True, out_dtype=BF16)
    queue.push("ffn_w_gate", dw_gate, 8, 1000, cols=True)
    du2 = cmm(400, dup, w_up, name="mm_du2_up", tb=True, add=du2)
    dw_up = cmm(350, u2, dup, name="mm_dw_up", ta=True, out_dtype=BF16)
    queue.push("ffn_w_up", dw_up, 8, 1000, cols=True)
    (dh1,), (dg_ffn,) = cstage(130, _f_rmsnorm, "norm_ffn_bwd", n, d, tile_s, d, [(h1, 0)], [full(W["norm_ffn_g"])], [], [(du2, 0)], [F32],
                               extra_add=(dh2, 0))
    dh1b = dh1.astype(BF16)
    dycat = cmm(135, dh1b, w_out, name="mm_dycat", tb=True)
    dw_out = cmm(170, ycat, dh1b, name="mm_dw_out", ta=True, out_dtype=BF16)
    queue.push("w_out", dw_out.reshape(N_DEV, -1, d), 4, 370)

    (dysc, dr_p, dk2_p, dv_p, dg_g), (dln_g, dln_b, dr_k) = cstage(
        195, _f_rwkv_post, "rwkv_post_bwd", n, rw, tile_s, ct_h, post_acts, post_par, [gsum], [(dycat, 0)], [F32] * 5)
    dr_s, dlw, dk2_s, dv_s, dna, dbb = _carried(queue, 650, lw, k2, na, bb, ps, st, dysc, bl, seq, rw, fn=_rwkv_scan_bwd)
    dk2 = dk2_p + dk2_s
    (dk, dwlin, dalin), (dw0, da0, dk_k, dk_a) = cstage(
        180, _f_rwkv_pre, "rwkv_pre_bwd", n, rw, tile_s, ct_h, pre_acts, pre_par, [gsum], [(dlw, 0), (dk2, 0), (dna, 0), (dbb, 0)], [F32] * 3)
    dwag = jnp.concatenate([dwlin, dalin, dg_g], axis=1).astype(BF16)
    dlact = cmm(75, dwag, w_lora, name="mm_dlact", tb=True)
    dw_lora = cmm(50, lact, dwag, name="mm_dw_lora", ta=True)
    (dps_lora,), _ = _stage_bwd(f_lora, "lora_act_bwd", n, lorap, tile, lorap, [(ps, o_lora)], [], [], [(dlact, 0)], [F32])
    dp_segs, dmu_segs = [], []
    for nm, o, wdt, parts in (("r", 0, rw, [dr_p, dr_s]), ("k", rw, rw, [dk]), ("v", 2 * rw, rw, [dv_p, dv_s]), ("lora", o_lora, lorap, [dps_lora])):
        dp_s, dmu_s = _lerp_bwd(p, o_rw + o, mu, o, parts, "lerp_bwd_" + nm, bl, seq, wdt, ct_seq, BF16)
        dp_segs.append(dp_s)
        dmu_segs.append(dmu_s)
    dmu = jnp.concatenate(dmu_segs, axis=1)

    (dh_l, dgate_l), (dlru_norm_g,) = cstage(80, _f_lru_post, "lru_post_bwd", n, lw_, tile_s, lw_, [(h_l, 0), (p, o_gate)], lpost_par, [],
                                                 [(dycat, rw)], [F32, BF16])
    da_l, dbx = _lru_scan_bwd(a_l, h_l, dh_l, bl, seq, lw_, ct_l)
    (dxc,), (dwr, dbr, dwi, dbi, dlam) = cstage(240, f_gates, "lru_gates_bwd", n, lw_, tile_g, LANE, [(xc, 0)], gate_par, [],
                                                [(da_l, 0), (dbx, 0)], [F32])
    dxb, dconv_w, dconv_b = _conv_bwd(p, o_xb, conv_w, dxc, bl, seq, lw_, ct_seq, BF16)
    sh_full = [dw_lora[:dl, :rw], dw_lora[dlp:dlp + al, rw:2 * rw], dw_lora[dlp + alp:dlp + alp + gl, 2 * rw:], dconv_w]
    assert rw == lw_
    rows_sh = sum(a.shape[0] for a in sh_full)
    pad_sh = _rup(rows_sh, 16) - rows_sh
    queue.push("small_sharded", jnp.pad(jnp.concatenate([_shards_from_cols(a) for a in sh_full], axis=1), ((0, 0), (0, pad_sh), (0, 0))), 1, 40)
    stack_sh = lambda D: jnp.pad(jnp.concatenate([D[k][0] for k in _SMALL_SHARDED], axis=0), ((0, pad_sh), (0, 0)))

    dp = jnp.concatenate([dxb, dgate_l] + dp_segs, axis=1)
    dw_in = cmm(340, u1, dp, name="mm_dw_in", ta=True, out_dtype=BF16)
    ol = o_rw + o_lora
    dw_in_l = jnp.concatenate([dw_in[:, o_rw:ol], dw_in[:, ol:ol + dl], dw_in[:, ol + dlp:ol + dlp + al],
                               dw_in[:, ol + dlp + alp:ol + dlp + alp + gl], dw_in[:, :o_rw]], axis=1)
    queue.push("w_in", _shards_from_cols(dw_in_l), 8, 970)
    du1 = cmm(380, dp, w_in, name="mm_du1", tb=True)
    (grad_x,), (dg_mix,) = cstage(90, _f_rmsnorm, "norm_mix_bwd", n, d, tile_s, d, [(x2, 0)], [full(W["norm_mix_g"])], [], [(du1, 0)], [F32],
                                  extra_add=(dh1, 0))
    dmu_l = jnp.concatenate([dmu[:, :o1], dmu[:, o_lora:o_lora + dl], dmu[:, o_lora + dlp:o_lora + dlp + al],
                             dmu[:, o_lora + dlp + alp:o_lora + dlp + alp + gl]], axis=1)
    small_g = {"norm_mix_g": dg_mix, "mu_shift": dmu_l, "rwkv_w0": dw0, "rwkv_a0": da0, "rwkv_k_k": dk_k, "rwkv_k_a": dk_a,
               "rwkv_r_k": dr_k.reshape(W["rwkv_r_k"].shape), "rwkv_ln_g": dln_g, "rwkv_ln_b": dln_b, "conv_b": dconv_b,
               "lru_wr": dwr[None], "lru_br": dbr, "lru_wi": dwi[None], "lru_bi": dbi, "lru_lambda": dlam, "lru_norm_g": dlru_norm_g,
               "norm_ffn_g": dg_ffn, "norm_final_g": dg_final.reshape(W["norm_final_g"].shape)}
    gq.push("small_grads", _pack([small_g[k] for k in _SMALL]), 1, 50)
    out = {}
    for k in ["ffn_w_down", "ffn_w_gate", "ffn_w_up", "w_out", "w_in"]:
        if k == "w_in":
            queue.flush()
        res = _carried(gq if k == "w_in" else queue, 105, queue.result(k), W[k][0], M[k][0], V[k][0], "adamw_" + k, fn=_adamw)
        out[k] = [o[None] for o in res]
    pk = lambda D: _pack([D[k] for k in _SMALL])
    res = _adamw(gq.result("small_grads"), pk(W), pk(M), pk(V), "adamw_small")
    shapes = [W[k].shape for k in _SMALL]
    for i, r in enumerate(res):
        for k, a in zip(_SMALL, _unpack(r, shapes)):
            out.setdefault(k, [None] * 4)[i] = a
    res = _adamw(queue.result("small_sharded"), stack_sh(W), stack_sh(M), stack_sh(V), "adamw_small_sharded")
    for i, r in enumerate(res):
        o = 0
        for k in _SMALL_SHARDED:
            rows = W[k].shape[1]
            out.setdefault(k, [None] * 4)[i] = r[o:o + rows][None]
            o += rows
    return loss, grad_x.reshape(x.shape), out


def kernel(x, norm_mix_g, w_in, mu_shift, rwkv_w0, rwkv_w2, rwkv_a0, rwkv_a2, rwkv_g2, rwkv_k_k, rwkv_k_a, rwkv_r_k, rwkv_ln_g, rwkv_ln_b, conv_w, conv_b, lru_wr, lru_br, lru_wi, lru_bi, lru_lambda, lru_norm_g, w_out, norm_ffn_g, ffn_w_gate, ffn_w_up, ffn_w_down, norm_final_g, loss_target, m_norm_mix_g, m_w_in, m_mu_shift, m_rwkv_w0, m_rwkv_w2, m_rwkv_a0, m_rwkv_a2, m_rwkv_g2, m_rwkv_k_k, m_rwkv_k_a, m_rwkv_r_k, m_rwkv_ln_g, m_rwkv_ln_b, m_conv_w, m_conv_b, m_lru_wr, m_lru_br, m_lru_wi, m_lru_bi, m_lru_lambda, m_lru_norm_g, m_w_out, m_norm_ffn_g, m_ffn_w_gate, m_ffn_w_up, m_ffn_w_down, m_norm_final_g, v_norm_mix_g, v_w_in, v_mu_shift, v_rwkv_w0, v_rwkv_w2, v_rwkv_a0, v_rwkv_a2, v_rwkv_g2, v_rwkv_k_k, v_rwkv_k_a, v_rwkv_r_k, v_rwkv_ln_g, v_rwkv_ln_b, v_conv_w, v_conv_b, v_lru_wr, v_lru_br, v_lru_wi, v_lru_bi, v_lru_lambda, v_lru_norm_g, v_w_out, v_norm_ffn_g, v_ffn_w_gate, v_ffn_w_up, v_ffn_w_down, v_norm_final_g):
    a = locals()
    W = {k: a[k] for k in _WEIGHTS}
    M = {k: a["m_" + k] for k in _WEIGHTS}
    V = {k: a["v_" + k] for k in _WEIGHTS}
    loss, grad_x, out = _step(W, M, V, x, loss_target)
    res = [loss, grad_x]
    for i in range(4):
        res += [out[k][i].reshape(W[k].shape) for k in _WEIGHTS]
    return tuple(res)
```

```python
import functools
import math

import jax
import jax.numpy as jnp
from jax import lax
from jax.experimental import pallas as pl
from jax.experimental.pallas import tpu as pltpu

F32 = jnp.float32
BF16 = jnp.bfloat16
HI = lax.Precision.HIGHEST
MESH = pl.DeviceIdType.MESH

N_DEV = 8
LANE = 128
HEAD = 64
MM_MAX_TK = 4096
CARRY_FILL = 1.2
SCAN_CHUNK = 64
SCAN_GROUP = 2
SCAN_PAIRS = 8
VMEM_LIMIT = 56 * 1024 * 1024

NORM_EPS = 1e-6
GN_EPS = 64e-5
LRU_C = 8.0
ADAM_LR, ADAM_B1, ADAM_B2, ADAM_EPS, ADAM_WD, ADAM_STEP = 0.001, 0.9, 0.999, 1e-08, 0.01, 10


def _pick(n, cands):
    for c in cands:
        if n % c == 0:
            return c
    return n


def _rup(n, m):
    return (n + m - 1) // m * m


def _cparams(dims):
    return pltpu.CompilerParams(dimension_semantics=dims, vmem_limit_bytes=VMEM_LIMIT)


def _sigmoid(x):
    return 1.0 / (1.0 + jnp.exp(-x))


def _softplus(z):
    return jnp.maximum(z, 0.0) + jnp.log(1.0 + jnp.exp(-jnp.abs(z)))


def _neg_expm1(x):
    series = -(x * (1.0 + 0.5 * x * (1.0 + (x / 3.0) * (1.0 + 0.25 * x))))
    return jnp.where(jnp.abs(x) < 0.03, series, 1.0 - jnp.exp(x))


def _gelu(x):
    return 0.5 * x * (1.0 + jnp.tanh(0.7978845608028654 * (x + 0.044715 * (x * x * x))))


def _dot(a, b, dims, precision=None):
    return lax.dot_general(a, b, (dims, ((), ())), precision=precision, preferred_element_type=F32)


def _nn(a, b, precision=None):
    return _dot(a, b, ((1,), (0,)), precision)


def _coords():
    return lax.axis_index("x"), lax.axis_index("y"), lax.axis_index("c")


class _Carry:
    def __init__(self, tensors, items):
        self.tensors, self.items = tensors, items
        nt, ni = len(tensors), len(items)
        any_spec = pl.BlockSpec(memory_space=pl.ANY)
        self.args = [t[0] for t in tensors] + [t[1] for t in tensors]
        self.in_specs = [any_spec] * (2 * nt)
        self.out_specs = [any_spec] * nt
        self.out_shape = [jax.ShapeDtypeStruct(t[1].shape, t[1].dtype) for t in tensors]
        self.scratch = [pltpu.SemaphoreType.DMA((ni, N_DEV - 1)), pltpu.SemaphoreType.DMA((ni, N_DEV - 1)), pltpu.SemaphoreType.DMA((ni,))]

    def aliases(self, first_in, first_out):
        nt = len(self.tensors)
        return {first_in + nt + t: first_out + t for t in range(nt)}

    def _slot(self, ref, t, idx, win):
        cw = self.tensors[t][2]
        return ref.at[idx, win] if cw is None else ref.at[win, pl.ds(pl.multiple_of(idx * cw, LANE), cw)]

    def _copies(self, src_refs, dst_refs, sems):
        send_sems, recv_sems, local_sems = sems
        x, y, c = _coords()
        my = 4 * x + 2 * y + c
        out = []
        for n, (t, r0, rows) in enumerate(self.items):
            win = pl.ds(r0, rows)
            out.append(pltpu.make_async_copy(self._slot(src_refs[t], t, my, win), dst_refs[t].at[my, win], local_sems.at[n]))
            for k in range(1, N_DEV):
                px, py, pc = x ^ ((k >> 2) & 1), y ^ ((k >> 1) & 1), c ^ (k & 1)
                out.append(pltpu.make_async_remote_copy(
                    src_ref=self._slot(src_refs[t], t, 4 * px + 2 * py + pc, win), dst_ref=dst_refs[t].at[my, win],
                    send_sem=send_sems.at[n, k - 1], recv_sem=recv_sems.at[n, k - 1],
                    device_id=(px, py, pc), device_id_type=MESH))
        return out

    def hook(self, step, last, src_refs, dst_refs, sems):
        if last == 0:
            for cp in self._copies(src_refs, dst_refs, sems):
                cp.start()
            for cp in self._copies(src_refs, dst_refs, sems):
                cp.wait()
            return

        @pl.when(step == 0)
        def _():
            for cp in self._copies(src_refs, dst_refs, sems):
                cp.start()

        @pl.when(step == last)
        def _():
            for cp in self._copies(src_refs, dst_refs, sems):
                cp.wait()


class _GatherCarry(_Carry):
    def hook(self, step, last, src_refs, dst_refs, sems):
        send_sems, recv_sems, local_sems = sems
        x, y, c = _coords()
        me, sibling = (x, y, c), (x, y, 1 - c)
        chips = [(1 - x, y), (x, 1 - y), (1 - x, 1 - y)]

        def per_item(fn):
            for n, (t, r0, rows) in enumerate(self.items):
                win = pl.ds(r0, rows)

                def copy(k, block, to, own=False, n=n, t=t, win=win):
                    slot = self._slot(dst_refs[t], t, 4 * block[0] + 2 * block[1] + block[2], win)
                    return pltpu.make_async_remote_copy(
                        src_ref=src_refs[t].at[win] if own else slot, dst_ref=slot,
                        send_sem=send_sems.at[n, k], recv_sem=recv_sems.at[n, k], device_id=to, device_id_type=MESH)

                mine = pltpu.make_async_copy(src_refs[t].at[win], self._slot(dst_refs[t], t, 4 * x + 2 * y + c, win), local_sems.at[n])
                first = [copy(0, me, sibling, own=True)] + [copy(1 + j, me, (*chip, c), own=True) for j, chip in enumerate(chips)]
                fn(copy, mine, first)

        def begin(copy, mine, first):
            mine.start()
            for cp in first:
                cp.start()

        def pass_on(copy, mine, first):
            for j, chip in enumerate(chips):
                copy(1 + j, (*chip, c), me).wait_recv()
                copy(4 + j, (*chip, c), sibling).start()

        def finish(copy, mine, first):
            copy(0, sibling, me).wait_recv()
            for j, chip in enumerate(chips):
                copy(4 + j, (*chip, 1 - c), me).wait_recv()
            for cp in first + [copy(4 + j, (*chip, c), sibling) for j, chip in enumerate(chips)]:
                cp.wait_send()
            mine.wait()

        if last == 0:
            for fn in (begin, pass_on, finish):
                per_item(fn)
            return
        late = max(1, (7 * last) // 8)
        for at, fn in ((0, begin), (late, pass_on), (last, finish)):
            pl.when(step == at)(functools.partial(per_item, fn))


def _mm(a, b, *, name, ta=False, tb=False, out_dtype=F32, add=None, tiles=None, carry=None):
    M, K = (a.shape[1], a.shape[0]) if ta else a.shape
    N = b.shape[0] if tb else b.shape[1]
    assert (b.shape[1] if tb else b.shape[0]) == K, (a.shape, b.shape, ta, tb)
    tk = max(t for t in range(LANE, min(K, MM_MAX_TK) + 1, LANE) if K % t == 0)
    tm, tn, tk = tiles or (_pick(M, (1024, 512, 256, 128)), _pick(N, (512, 256, 128)), tk)
    nk = K // tk
    dims = ((0 if ta else 1,), (1 if tb else 0,))

    n_in = 2 + (add is not None)
    nt = len(carry.tensors) if carry else 0
    gi, gj = M // tm, N // tn

    def kern(*refs):
        a_ref, b_ref = refs[:2]
        add_ref = refs[2] if add is not None else None
        o_ref = refs[n_in + 2 * nt]
        scr = refs[n_in + 3 * nt + 1:]
        if carry:
            step = (pl.program_id(0) * gj + pl.program_id(1)) * nk + pl.program_id(2)
            carry.hook(step, gi * gj * nk - 1, refs[n_in:n_in + nt], refs[n_in + 2 * nt + 1:n_in + 3 * nt + 1], scr[-3:])

        def finish(r):
            if add is not None:
                r = r + add_ref[...].astype(F32)
            o_ref[...] = r.astype(o_ref.dtype)

        if nk == 1:
            finish(_dot(a_ref[...], b_ref[...], dims))
            return
        acc = scr[0]
        k = pl.program_id(2)

        @pl.when(k == 0)
        def _():
            acc[...] = jnp.zeros_like(acc)

        acc[...] += _dot(a_ref[...], b_ref[...], dims)

        @pl.when(k == nk - 1)
        def _():
            finish(acc[...])

    a_spec = pl.BlockSpec((tk, tm), lambda i, j, k: (k, i)) if ta else pl.BlockSpec((tm, tk), lambda i, j, k: (i, k))
    b_spec = pl.BlockSpec((tn, tk), lambda i, j, k: (j, k)) if tb else pl.BlockSpec((tk, tn), lambda i, j, k: (k, j))
    o_spec = pl.BlockSpec((tm, tn), lambda i, j, k: (i, j))
    in_specs = [a_spec, b_spec] + ([o_spec] if add is not None else [])
    args = (a, b) + ((add,) if add is not None else ())
    scratch = [pltpu.VMEM((tm, tn), F32)] if nk > 1 else []
    o_shape = jax.ShapeDtypeStruct((M, N), out_dtype)
    if not carry:
        return pl.pallas_call(
            kern, name=name, grid=(gi, gj, nk), in_specs=in_specs, out_specs=o_spec, out_shape=o_shape, scratch_shapes=scratch,
            compiler_params=_cparams(("parallel", "parallel", "arbitrary")),
        )(*args)
    res = pl.pallas_call(
        kern, name=name, grid=(gi, gj, nk), in_specs=in_specs + carry.in_specs, out_specs=[o_spec] + carry.out_specs,
        out_shape=[o_shape] + carry.out_shape, scratch_shapes=scratch + carry.scratch,
        input_output_aliases=carry.aliases(n_in, 1), compiler_params=_cparams(("arbitrary", "arbitrary", "arbitrary")),
    )(*args, *carry.args)
    return res[0], list(res[1:])


def _stage_specs(acts, params, consts, tile, ct):
    act_specs = [pl.BlockSpec((tile, ct), functools.partial(lambda j, i, o: (i, o + j), o=off // ct)) for _, off in acts]
    par_specs = [pl.BlockSpec(bs, functools.partial(lambda j, i, im: im(j), im=im)) for _, bs, im in params]
    con_specs = [pl.BlockSpec(bs, functools.partial(lambda j, i, im: im(j), im=im)) for _, bs, im in consts]
    return act_specs, par_specs, con_specs


def _stage_fwd(f, name, n_rows, width, tile, ct, acts, params, consts, out_dtypes, carry=None):
    for _, off in acts:
        assert off % ct == 0
    na, npar, nc, no = len(acts), len(params), len(consts), len(out_dtypes)
    n_in = na + npar + nc
    nt = len(carry.tensors) if carry else 0
    gj, gi = width // ct, n_rows // tile

    def kern(*refs):
        if carry:
            step = pl.program_id(0) * gi + pl.program_id(1)
            carry.hook(step, gj * gi - 1, refs[n_in:n_in + nt], refs[n_in + 2 * nt + no:n_in + 3 * nt + no], refs[n_in + 3 * nt + no:])
        a = [r[...].astype(F32) for r in refs[:na]]
        p = [r[...] for r in refs[na:na + npar]]
        c = [r[...] for r in refs[na + npar:n_in]]
        outs = f(a, p, c, pl.program_id(1) * tile)
        for r, o in zip(refs[n_in + 2 * nt:n_in + 2 * nt + no], outs):
            r[...] = o.astype(r.dtype)

    act_specs, par_specs, con_specs = _stage_specs(acts, params, consts, tile, ct)
    o_spec = pl.BlockSpec((tile, ct), lambda j, i: (i, j))
    in_specs = act_specs + par_specs + con_specs
    out_shape = [jax.ShapeDtypeStruct((n_rows, width), d) for d in out_dtypes]
    args = [a for a, _ in acts] + [p for p, _, _ in params] + [c for c, _, _ in consts]
    if not carry:
        return tuple(pl.pallas_call(
            kern, name=name, grid=(gj, gi), in_specs=in_specs, out_specs=[o_spec] * no, out_shape=out_shape,
            compiler_params=_cparams(("parallel", "parallel")),
        )(*args))
    res = pl.pallas_call(
        kern, name=name, grid=(gj, gi), in_specs=in_specs + carry.in_specs, out_specs=[o_spec] * no + carry.out_specs,
        out_shape=out_shape + carry.out_shape, scratch_shapes=carry.scratch, input_output_aliases=carry.aliases(n_in, no),
        compiler_params=_cparams(("arbitrary", "arbitrary")),
    )(*args, *carry.args)
    return tuple(res[:no]), list(res[no:])


def _stage_bwd(f, name, n_rows, width, tile, ct, acts, params, consts, couts, dact_dtypes, extra_add=None, carry=None):
    na, npar, nc, no = len(acts), len(params), len(consts), len(couts)
    nx = 0 if extra_add is None else 1
    nt = len(carry.tensors) if carry else 0
    n_in = na + npar + nc + no + nx
    gj, gi = width // ct, n_rows // tile

    def kern(*refs):
        if carry:
            step = pl.program_id(0) * gi + pl.program_id(1)
            n_out = n_in + 2 * nt + na + npar
            carry.hook(step, gj * gi - 1, refs[n_in:n_in + nt], refs[n_out:n_out + nt], refs[n_out + nt:])
        a = [r[...].astype(F32) for r in refs[:na]]
        p = [r[...] for r in refs[na:na + npar]]
        c = [r[...] for r in refs[na + npar:na + npar + nc]]
        base = na + npar + nc
        co = [r[...].astype(F32) for r in refs[base:base + no]]
        base += no
        x_refs = refs[base:base + nx]
        base += nx + 2 * nt
        da_refs = refs[base:base + na]
        dp_refs = refs[base + na:base + na + npar]
        row0 = pl.program_id(1) * tile
        _, vjp = jax.vjp(lambda aa, pp: tuple(f(aa, pp, c, row0)), a, p)
        da, dp = vjp(tuple(co))
        for k, (r, d) in enumerate(zip(da_refs, da)):
            if k == 0 and nx:
                d = d + x_refs[0][...].astype(F32)
            r[...] = d.astype(r.dtype)
        first = pl.program_id(1) == 0
        for r, d in zip(dp_refs, dp):
            @pl.when(first)
            def _(r=r, d=d):
                r[...] = d

            @pl.when(jnp.logical_not(first))
            def _(r=r, d=d):
                r[...] += d

    act_specs, par_specs, con_specs = _stage_specs(acts, params, consts, tile, ct)
    t_spec = pl.BlockSpec((tile, ct), lambda j, i: (i, j))
    co_specs = [pl.BlockSpec((tile, ct), functools.partial(lambda j, i, o: (i, o + j), o=off // ct)) for _, off in couts]
    x_specs = [] if extra_add is None else [pl.BlockSpec((tile, ct), functools.partial(lambda j, i, o: (i, o + j), o=extra_add[1] // ct))]
    x_args = [] if extra_add is None else [extra_add[0]]
    in_specs = act_specs + par_specs + con_specs + co_specs + x_specs
    out_specs = [t_spec] * na + par_specs
    out_shape = [jax.ShapeDtypeStruct((n_rows, width), d) for d in dact_dtypes] + [jax.ShapeDtypeStruct(p.shape, F32) for p, _, _ in params]
    args = [a for a, _ in acts] + [p for p, _, _ in params] + [c for c, _, _ in consts] + [c for c, _ in couts] + x_args
    if not carry:
        outs = pl.pallas_call(
            kern, name=name, grid=(gj, gi), in_specs=in_specs, out_specs=out_specs, out_shape=out_shape,
            compiler_params=_cparams(("parallel", "arbitrary")),
        )(*args)
        return tuple(outs[:na]), tuple(outs[na:])
    outs = pl.pallas_call(
        kern, name=name, grid=(gj, gi), in_specs=in_specs + carry.in_specs, out_specs=out_specs + carry.out_specs,
        out_shape=out_shape + carry.out_shape, scratch_shapes=carry.scratch, input_output_aliases=carry.aliases(n_in, na + npar),
        compiler_params=_cparams(("arbitrary", "arbitrary")),
    )(*args, *carry.args)
    return tuple(outs[:na]), tuple(outs[na:na + npar]), list(outs[na + npar:])


def _row(ct):
    return (1, ct), (lambda j: (0, j))


def _f_rmsnorm(a, p, c, row0):
    x, = a
    g, = p
    return (x * lax.rsqrt(jnp.mean(x * x, axis=-1, keepdims=True) + NORM_EPS) * g,)


def _f_lora_act(a, p, c, row0, widths):
    x, = a
    dl, al = widths
    col = lax.broadcasted_iota(jnp.int32, x.shape, 1)
    return (jnp.where(col < dl, jnp.tanh(x), jnp.where(col < dl + al, x, _sigmoid(x))),)


def _head_sums_raw(x, ones):
    hi = x.astype(BF16)
    lo = (x - hi.astype(F32)).astype(BF16)
    return _nn(hi, ones) + _nn(lo, ones)


@jax.custom_vjp
def _head_sums(x, ones):
    return _head_sums_raw(x, ones)


_head_sums.defvjp(lambda x, ones: (_head_sums_raw(x, ones), ones),
                  lambda ones, ct: (_head_sums_raw(ct, ones), jnp.zeros_like(ones)))


def _f_rwkv_pre(a, p, c, row0):
    k, wlin, alin = a
    w0, a0, k_k, k_a = p
    gsum, = c
    w = -_softplus(-(w0 + wlin)) - 0.5
    lw = -jnp.exp(w)
    alpha = _sigmoid(a0 + alin)
    kk = k * k_k
    ss = _head_sums(kk * kk, gsum)
    kk = kk * lax.rsqrt(jnp.maximum(ss, 1e-24))
    k2 = k * (1.0 + (alpha - 1.0) * k_a)
    return lw, k2, -kk, kk * alpha


def _f_rwkv_post(a, p, c, row0):
    y, r, k2, v, g = a
    ln_g, ln_b, r_k = p
    gsum, = c
    inv = 1.0 / HEAD
    mean = _head_sums(y, gsum) * inv
    yc = y - mean
    var = _head_sums(yc * yc, gsum) * inv
    yn = yc * lax.rsqrt(var + GN_EPS) * ln_g + ln_b
    bonus = _head_sums(r * k2 * r_k, gsum)
    return ((yn + bonus * v) * g,)


def _f_lru_gates(a, p, c, row0, seq):
    xc, = a
    wr, br, wi, bi, lam = p
    xb = xc.astype(BF16)
    rg = _sigmoid(_nn(xb, wr[0].astype(BF16)) + br)
    ig = _sigmoid(_nn(xb, wi[0].astype(BF16)) + bi)
    log_a = -LRU_C * rg * _softplus(-lam)
    a_t = jnp.exp(log_a)
    mult = jnp.sqrt(_neg_expm1(2.0 * log_a))
    row = row0 + lax.broadcasted_iota(jnp.int32, xc.shape, 0)
    mult = jnp.where(row % seq == 0, 1.0, mult)
    return a_t, mult * ig * xc


def _f_lru_post(a, p, c, row0):
    h, gate = a
    g, = p
    y = h * _gelu(gate)
    return (y * lax.rsqrt(jnp.mean(y * y, axis=-1, keepdims=True) + NORM_EPS) * g,)


def _f_swiglu(a, p, c, row0):
    gate, up = a
    return (gate * _sigmoid(gate) * up,)


def _shift_down(x, s, row):
    return jnp.where(row >= s, pltpu.roll(x, s, 0), 0.0)


def _shift_up(x, s, row):
    n = x.shape[0]
    return jnp.where(row < n - s, pltpu.roll(x, n - s, 0), 0.0)


def _seq_call(kern, name, bl, seq, width, ct, ins, outs, acc_outs=()):
    def spec(off, rows):
        if rows is None:
            return pl.BlockSpec((seq, ct), functools.partial(lambda j, b, o: (b, o + j), o=off // ct))
        return pl.BlockSpec((rows, ct), lambda j, b: (0, j))

    in_specs = [spec(off, rows) for _, off, rows in ins]
    out_specs = [spec(0, None) for _ in outs] + [spec(0, rows) for _, rows in acc_outs]
    out_shape = [jax.ShapeDtypeStruct((bl * seq, width), d) for d in outs] + [jax.ShapeDtypeStruct((rows, width), F32) for _, rows in acc_outs]
    res = pl.pallas_call(
        kern, name=name, grid=(width // ct, bl), in_specs=in_specs, out_specs=out_specs, out_shape=out_shape,
        compiler_params=_cparams(("parallel", "arbitrary")),
    )(*[a for a, _, _ in ins])
    return tuple(res)


def _acc(ref, val):
    first = pl.program_id(1) == 0

    @pl.when(first)
    def _():
        ref[...] = val

    @pl.when(jnp.logical_not(first))
    def _():
        ref[...] += val


def _lerp_fwd(p, off, mu, bl, seq, width, ct):
    def kern(p_ref, mu_ref, o_ref):
        x = p_ref[...]
        row = lax.broadcasted_iota(jnp.int32, x.shape, 0)
        o_ref[...] = x + (_shift_down(x, 1, row) - x) * mu_ref[...]

    return _seq_call(kern, "lerp_fwd", bl, seq, width, ct, [(p, off, None), (mu, 0, 1)], [F32])[0]


def _lerp_bwd(p, off, mu, mu_off, dps_parts, name, bl, seq, width, ct, out_dtype):
    nd = len(dps_parts)

    def kern(*refs):
        p_ref, mu_ref = refs[:2]
        dp_ref, dmu_ref = refs[2 + nd:]
        x = p_ref[...]
        d = refs[2][...].astype(F32)
        for r in refs[3:2 + nd]:
            d = d + r[...].astype(F32)
        m = mu_ref[...]
        row = lax.broadcasted_iota(jnp.int32, x.shape, 0)
        dp_ref[...] = (d * (1.0 - m) + _shift_up(d * m, 1, row)).astype(dp_ref.dtype)
        _acc(dmu_ref, jnp.sum(d * (_shift_down(x, 1, row) - x), axis=0, keepdims=True))

    ins = [(p, off, None), (mu[:, mu_off:mu_off + width], 0, 1)] + [(a, 0, None) for a in dps_parts]
    return _seq_call(kern, name, bl, seq, width, ct, ins, [out_dtype], [(None, 1)])


def _conv_fwd(p, off, cw, cb, bl, seq, width, ct):
    nw = cw.shape[0]

    def kern(x_ref, w_ref, b_ref, o_ref):
        x = x_ref[...]
        row = lax.broadcasted_iota(jnp.int32, x.shape, 0)
        acc = b_ref[...] + x * w_ref[pl.ds(nw - 1, 1), :]
        for s in range(1, nw):
            acc = acc + _shift_down(x, s, row) * w_ref[pl.ds(nw - 1 - s, 1), :]
        o_ref[...] = acc

    return _seq_call(kern, "conv_fwd", bl, seq, width, ct, [(p, off, None), (cw, 0, nw), (cb, 0, 1)], [F32])[0]


def _conv_bwd(p, off, cw, dxc, bl, seq, width, ct, out_dtype):
    nw = cw.shape[0]

    def kern(x_ref, w_ref, d_ref, dx_ref, dw_ref, db_ref):
        x = x_ref[...]
        d = d_ref[...]
        row = lax.broadcasted_iota(jnp.int32, x.shape, 0)
        wrow = lax.broadcasted_iota(jnp.int32, dw_ref.shape, 0)
        dx = d * w_ref[pl.ds(nw - 1, 1), :]
        dw = jnp.where(wrow == nw - 1, jnp.sum(d * x, axis=0, keepdims=True), 0.0)
        for s in range(1, nw):
            dx = dx + _shift_up(d, s, row) * w_ref[pl.ds(nw - 1 - s, 1), :]
            dw = jnp.where(wrow == nw - 1 - s, jnp.sum(d * _shift_down(x, s, row), axis=0, keepdims=True), dw)
        dx_ref[...] = dx.astype(dx_ref.dtype)
        _acc(dw_ref, dw)
        _acc(db_ref, jnp.sum(d, axis=0, keepdims=True))

    return _seq_call(kern, "conv_bwd", bl, seq, width, ct, [(p, off, None), (cw, 0, nw), (dxc, 0, None)], [out_dtype], [(None, nw), (None, 1)])


def _lru_scan_fwd(a, bx, bl, seq, width, ct):
    def kern(a_ref, b_ref, h_ref):
        av = a_ref[...]
        bv = b_ref[...]
        row = lax.broadcasted_iota(jnp.int32, av.shape, 0)
        d = 1
        while d < seq:
            a_sh = jnp.where(row >= d, pltpu.roll(av, d, 0), 1.0)
            b_sh = jnp.where(row >= d, pltpu.roll(bv, d, 0), 0.0)
            bv = av * b_sh + bv
            av = av * a_sh
            d *= 2
        h_ref[...] = bv

    return _seq_call(kern, "lru_scan_fwd", bl, seq, width, ct, [(a, 0, None), (bx, 0, None)], [F32])[0]


def _lru_scan_bwd(a, h, dh, bl, seq, width, ct):
    def kern(a_ref, h_ref, d_ref, da_ref, db_ref):
        row = lax.broadcasted_iota(jnp.int32, a_ref.shape, 0)
        al = _shift_up(a_ref[...], 1, row)
        g = d_ref[...]
        d = 1
        while d < seq:
            keep = row < seq - d
            al_sh = jnp.where(keep, pltpu.roll(al, seq - d, 0), 1.0)
            g_sh = jnp.where(keep, pltpu.roll(g, seq - d, 0), 0.0)
            g = al * g_sh + g
            al = al * al_sh
            d *= 2
        db_ref[...] = g
        da_ref[...] = g * _shift_down(h_ref[...], 1, row)

    return _seq_call(kern, "lru_scan_bwd", bl, seq, width, ct, [(a, 0, None), (h, 0, None), (dh, 0, None)], [F32, F32])


_FORMS = {"nn": ((1,), (0,)), "nt": ((1,), (1,)), "tn": ((0,), (0,))}
_FORM_GRADS = {"nn": (("nt", "g", "b"), ("tn", "a", "g")),
               "nt": (("nn", "g", "b"), ("tn", "g", "a")),
               "tn": (("nt", "b", "g"), ("nn", "a", "g"))}


def _split_bf16(x):
    hi = x.astype(BF16)
    return hi, (x - hi.astype(F32)).astype(BF16)


def _pdot_raw(a, b, form, passes):
    dims = _FORMS[form]
    if passes == 1:
        return _dot(a.astype(BF16), b.astype(BF16), dims)
    ah, al = _split_bf16(a)
    bh, bl = _split_bf16(b)
    return _dot(ah, bh, dims) + (_dot(ah, bl, dims) + _dot(al, bh, dims))


@functools.partial(jax.custom_vjp, nondiff_argnums=(2, 3))
def _pdot(a, b, form, passes):
    return _pdot_raw(a, b, form, passes)


def _pdot_fwd(a, b, form, passes):
    return _pdot_raw(a, b, form, passes), (a, b)


def _pdot_bwd(form, passes, res, g):
    vals = {"a": res[0], "b": res[1], "g": g}
    (fa, xa, ya), (fb, xb, yb) = _FORM_GRADS[form]
    return _pdot_raw(vals[xa], vals[ya], fa, passes), _pdot_raw(vals[xb], vals[yb], fb, passes)


_pdot.defvjp(_pdot_fwd, _pdot_bwd)


def _neumann_raw(a_list, n_levels, passes):
    eye = (lax.broadcasted_iota(jnp.int32, a_list[0].shape, 0) == lax.broadcasted_iota(jnp.int32, a_list[0].shape, 1)).astype(F32)
    pw = list(a_list)
    x = [eye + a for a in a_list]
    for _ in range(n_levels):
        pw = [_pdot_raw(p, p, "nn", passes) for p in pw]
        x = [xi + _pdot_raw(xi, p, "nn", passes) for xi, p in zip(x, pw)]
    return x


@functools.partial(jax.custom_vjp, nondiff_argnums=(1, 2))
def _neumann_inverse(a_list, n_levels, passes):
    return _neumann_raw(a_list, n_levels, passes)


def _neumann_fwd(a_list, n_levels, passes):
    x = _neumann_raw(a_list, n_levels, passes)
    return x, x


def _neumann_bwd(n_levels, passes, x, ct):
    return ([_pdot_raw(xi, _pdot_raw(c, xi, "nt", passes), "tn", passes) for xi, c in zip(x, ct)],)


_neumann_inverse.defvjp(_neumann_fwd, _neumann_bwd)


def _scan_block(states, units, p_main=1, p_inv=1):
    C = units[0][0][0].shape[0]
    C2 = 2 * C
    ri = lax.broadcasted_iota(jnp.int32, (C, C), 0)
    ci = lax.broadcasted_iota(jnp.int32, (C, C), 1)
    tri = (ri >= ci).astype(F32)
    i2 = lax.broadcasted_iota(jnp.int32, (C2, C2), 0)
    j2 = lax.broadcasted_iota(jnp.int32, (C2, C2), 1)
    same = (i2 // C) == (j2 // C)
    strict = jnp.logical_and(same, (i2 % C) > (j2 % C))
    incl = jnp.logical_and(same, (i2 % C) >= (j2 % C))
    lane = lax.broadcasted_iota(jnp.int32, (1, LANE), 1)
    m0, m1 = (lane < HEAD).astype(F32), (lane >= HEAD).astype(F32)
    stack = lambda z: jnp.concatenate([z * m0, z * m1], axis=0)
    ids = [(i, g) for g in range(len(units[0])) for i in range(len(units))]

    pre = {}
    for i, g in ids:
        r, lw, k, v, a, b = units[i][g]
        cs = _nn(tri, lw, HI)
        p_incl = jnp.exp(cs)
        p_rec = jnp.exp(-cs)
        xr = jnp.concatenate([stack(a * jnp.exp(cs - lw)), stack(r * p_incl)], axis=0)
        bk = jnp.concatenate([stack(b * p_rec), stack(k * p_rec)], axis=0)
        pre[i, g] = (xr, bk, stack(v), jnp.exp(jnp.sum(lw, axis=0, keepdims=True)))
    gm = {u: _pdot(pre[u][0], pre[u][1], "nt", p_main) for u in ids}
    a_ak = {u: jnp.where(strict, gm[u][:C2, C2:], 0.0) for u in ids}
    r_bk = {u: jnp.concatenate([jnp.where(incl, gm[u][C2:, :C2], 0.0), jnp.where(incl, gm[u][C2:, C2:], 0.0)], axis=1) for u in ids}
    a_ab = [jnp.where(strict, gm[u][:C2, :C2], 0.0) for u in ids]
    x = dict(zip(ids, _neumann_inverse(a_ab, int(math.log2(C)) - 1, p_inv)))
    akv = {u: _pdot(a_ak[u], pre[u][2], "nn", p_main) for u in ids}

    states = list(states)
    pairs = range(len(units))
    ys = [[None] * len(units[0]) for _ in units]
    for g in range(len(units[0])):
        xs = [_pdot(pre[i, g][0], states[i], "nt", p_main) for i in pairs]
        us = [_pdot(x[i, g], xs[i][:C2] + akv[i, g], "nn", p_inv) for i in pairs]
        uv = [jnp.concatenate([us[i], pre[i, g][2]], axis=0) for i in pairs]
        y2 = [xs[i][C2:] + _pdot(r_bk[i, g], uv[i], "nn", p_main) for i in pairs]
        for i in pairs:
            ys[i][g] = y2[i][:C] + y2[i][C:]
        states = [(states[i] + _pdot(uv[i], pre[i, g][1], "tn", p_main)) * pre[i, g][3] for i in pairs]
    return ys, states


def _scan_dims(seq, rw):
    G = _pick(seq // SCAN_CHUNK, (SCAN_GROUP, 2, 1))
    NP = _pick(rw // LANE, (SCAN_PAIRS, 4, 2, 1))
    C = SCAN_CHUNK * G
    return SCAN_CHUNK, G, NP, C, seq // C, rw // (NP * LANE)


def _rwkv_scan_fwd(r, lw, k2, v, na, bb, p, bl, seq, rw, carry=None):
    cs, G, NP, C, nc, nhg = _scan_dims(seq, rw)
    nt = len(carry.tensors) if carry else 0

    def kern(*refs):
        in_refs = refs[:6]
        y_ref, st_ref = refs[6 + 2 * nt:8 + 2 * nt]
        s_scr = refs[8 + 3 * nt]
        if carry:
            step = (pl.program_id(0) * nhg + pl.program_id(1)) * nc + pl.program_id(2)
            carry.hook(step, bl * nhg * nc - 1, refs[6:6 + nt], refs[8 + 2 * nt:8 + 3 * nt], refs[9 + 3 * nt:])

        @pl.when(pl.program_id(2) == 0)
        def _():
            s_scr[...] = jnp.zeros_like(s_scr)

        st_ref[...] = s_scr[...]
        units = [[tuple(ref[pl.ds(g * cs, cs), pl.ds(i * LANE, LANE)] for ref in in_refs) for g in range(G)] for i in range(NP)]
        ys, s_new = _scan_block([s_scr[i] for i in range(NP)], units)
        for i in range(NP):
            s_scr[i] = s_new[i]
            for g in range(G):
                y_ref[pl.ds(g * cs, cs), pl.ds(i * LANE, LANE)] = ys[i][g]

    def tok(off):
        return pl.BlockSpec((C, NP * LANE), functools.partial(lambda b, h, c, o: (b * nc + c, o + h), o=off // (NP * LANE)))

    in_specs = [tok(0), tok(0), tok(0), tok(2 * rw), tok(0), tok(0)]
    out_specs = [tok(0), pl.BlockSpec((NP, LANE, LANE), lambda b, h, c: ((b * nhg + h) * nc + c, 0, 0))]
    out_shape = [jax.ShapeDtypeStruct((bl * seq, rw), F32), jax.ShapeDtypeStruct((bl * nhg * nc * NP, LANE, LANE), F32)]
    scratch = [pltpu.VMEM((NP, LANE, LANE), F32)]
    if not carry:
        y, st = pl.pallas_call(
            kern, name="rwkv_scan_fwd", grid=(bl, nhg, nc), in_specs=in_specs, out_specs=out_specs, out_shape=out_shape,
            scratch_shapes=scratch, compiler_params=_cparams(("parallel", "parallel", "arbitrary")),
        )(p, lw, k2, p, na, bb)
        return y, st
    res = pl.pallas_call(
        kern, name="rwkv_scan_fwd", grid=(bl, nhg, nc), in_specs=in_specs + carry.in_specs, out_specs=out_specs + carry.out_specs,
        out_shape=out_shape + carry.out_shape, scratch_shapes=scratch + carry.scratch, input_output_aliases=carry.aliases(6, 2),
        compiler_params=_cparams(("arbitrary", "arbitrary", "arbitrary")),
    )(p, lw, k2, p, na, bb, *carry.args)
    return res[0], res[1], list(res[2:])


def _rwkv_scan_bwd(lw, k2, na, bb, p, st, dy, bl, seq, rw, carry=None):
    cs, G, NP, C, nc, nhg = _scan_dims(seq, rw)
    nt = len(carry.tensors) if carry else 0

    def kern(*refs):
        in_refs = refs[:6]
        st_ref, dy_ref = refs[6:8]
        out_refs = refs[8 + 2 * nt:14 + 2 * nt]
        ds_scr = refs[14 + 3 * nt]
        if carry:
            step = (pl.program_id(0) * nhg + pl.program_id(1)) * nc + pl.program_id(2)
            carry.hook(step, bl * nhg * nc - 1, refs[8:8 + nt], refs[14 + 2 * nt:14 + 3 * nt], refs[15 + 3 * nt:])

        @pl.when(pl.program_id(2) == 0)
        def _():
            ds_scr[...] = jnp.zeros_like(ds_scr)

        win = lambda ref, i, g: ref[pl.ds(g * cs, cs), pl.ds(i * LANE, LANE)]
        units = [[tuple(win(ref, i, g) for ref in in_refs) for g in range(G)] for i in range(NP)]
        _, vjp = jax.vjp(_scan_block, [st_ref[i] for i in range(NP)], units)
        dys = [[win(dy_ref, i, g) for g in range(G)] for i in range(NP)]
        ds, dunits = vjp((dys, [ds_scr[i] for i in range(NP)]))
        for i in range(NP):
            ds_scr[i] = ds[i]
            for g in range(G):
                for ref, d in zip(out_refs, dunits[i][g]):
                    ref[pl.ds(g * cs, cs), pl.ds(i * LANE, LANE)] = d

    def tok(off):
        return pl.BlockSpec((C, NP * LANE), functools.partial(lambda b, h, c, o: (b * nc + (nc - 1 - c), o + h), o=off // (NP * LANE)))

    st_spec = pl.BlockSpec((NP, LANE, LANE), lambda b, h, c: ((b * nhg + h) * nc + (nc - 1 - c), 0, 0))
    in_specs = [tok(0), tok(0), tok(0), tok(2 * rw), tok(0), tok(0), st_spec, tok(0)]
    out_shape = [jax.ShapeDtypeStruct((bl * seq, rw), F32)] * 6
    scratch = [pltpu.VMEM((NP, LANE, LANE), F32)]
    if not carry:
        return pl.pallas_call(
            kern, name="rwkv_scan_bwd", grid=(bl, nhg, nc), in_specs=in_specs, out_specs=[tok(0)] * 6, out_shape=out_shape,
            scratch_shapes=scratch, compiler_params=_cparams(("parallel", "parallel", "arbitrary")),
        )(p, lw, k2, p, na, bb, st, dy)
    res = pl.pallas_call(
        kern, name="rwkv_scan_bwd", grid=(bl, nhg, nc), in_specs=in_specs + carry.in_specs, out_specs=[tok(0)] * 6 + carry.out_specs,
        out_shape=out_shape + carry.out_shape, scratch_shapes=scratch + carry.scratch, input_output_aliases=carry.aliases(8, 6),
        compiler_params=_cparams(("arbitrary", "arbitrary", "arbitrary")),
    )(p, lw, k2, p, na, bb, st, dy, *carry.args)
    return res[:6], list(res[6:])


def _loss_head(h2, g_final, target, tile):
    n, d = h2.shape
    nt = n // tile

    def kern(h_ref, g_ref, t_ref, dh_ref, dg_ref, l_ref, dhb_ref):
        def f(h, g):
            y = h * lax.rsqrt(jnp.mean(h * h, axis=-1, keepdims=True) + NORM_EPS) * g
            e = y - t_ref[...]
            return 0.5 * jnp.sum(jnp.mean(e * e, axis=-1, keepdims=True))

        loss, (dh, dg) = jax.value_and_grad(f, argnums=(0, 1))(h_ref[...], g_ref[...])
        dh_ref[...] = dh
        dhb_ref[...] = dh.astype(dhb_ref.dtype)
        first = pl.program_id(0) == 0

        @pl.when(first)
        def _():
            dg_ref[...] = dg
            l_ref[...] = jnp.zeros_like(l_ref) + loss

        @pl.when(jnp.logical_not(first))
        def _():
            dg_ref[...] += dg
            l_ref[...] += loss

    row = pl.BlockSpec((tile, d), lambda i: (i, 0))
    vec = pl.BlockSpec((1, d), lambda i: (0, 0))
    return pl.pallas_call(
        kern, name="loss_head", grid=(nt,), in_specs=[row, vec, row],
        out_specs=[row, vec, pl.BlockSpec((1, LANE), lambda i: (0, 0)), row],
        out_shape=[jax.ShapeDtypeStruct((n, d), F32), jax.ShapeDtypeStruct((1, d), F32), jax.ShapeDtypeStruct((1, LANE), F32),
                   jax.ShapeDtypeStruct((n, d), BF16)],
        compiler_params=_cparams(("arbitrary",)),
    )(h2, g_final, target)


def _adamw(parts, w, m, v, name, carry=None):
    n_parts, Rp, Cp = parts.shape
    R, Cc = w.shape
    assert Rp >= R and Cp >= Cc
    tr = _pick(R, tuple(t for t in (1024, 512, 256, 128, 64, 32, 16) if t * Cp <= 128 * 1024) + (8,))
    part = (lambda ref, s: ref[s]) if Cp == Cc else (lambda ref, s: ref[s, :, pl.ds(0, Cc)])
    c1, c2 = 1.0 - ADAM_B1, 1.0 - ADAM_B2
    bc1, bc2 = 1.0 - ADAM_B1 ** ADAM_STEP, 1.0 - ADAM_B2 ** ADAM_STEP

    nt = len(carry.tensors) if carry else 0

    def kern(*refs):
        p_ref, w_ref, m_ref, v_ref = refs[:4]
        g_ref, d_ref, nm_ref, nv_ref = refs[4 + 2 * nt:8 + 2 * nt]
        if carry:
            carry.hook(pl.program_id(0), R // tr - 1, refs[4:4 + nt], refs[8 + 2 * nt:8 + 3 * nt], refs[8 + 3 * nt:])
        g = part(p_ref, 0).astype(F32)
        for s in range(1, n_parts):
            g = g + part(p_ref, s).astype(F32)
        m2 = ADAM_B1 * m_ref[...] + c1 * g
        v2 = ADAM_B2 * v_ref[...] + c2 * (g * g)
        g_ref[...] = g
        nm_ref[...] = m2
        nv_ref[...] = v2
        d_ref[...] = -ADAM_LR * ((m2 / bc1) / (jnp.sqrt(v2 / bc2) + ADAM_EPS) + ADAM_WD * w_ref[...])

    blk = pl.BlockSpec((tr, Cc), lambda i: (i, 0))
    in_specs = [pl.BlockSpec((n_parts, tr, Cp), lambda i: (0, i, 0)), blk, blk, blk]
    out_shape = [jax.ShapeDtypeStruct((R, Cc), F32)] * 4
    if not carry:
        return pl.pallas_call(
            kern, name=name, grid=(R // tr,), in_specs=in_specs, out_specs=[blk] * 4, out_shape=out_shape,
            compiler_params=_cparams(("parallel",)),
        )(parts, w, m, v)
    res = pl.pallas_call(
        kern, name=name, grid=(R // tr,), in_specs=in_specs + carry.in_specs, out_specs=[blk] * 4 + carry.out_specs,
        out_shape=out_shape + carry.out_shape, scratch_shapes=carry.scratch, input_output_aliases=carry.aliases(4, 4),
        compiler_params=_cparams(("arbitrary",)),
    )(parts, w, m, v, *carry.args)
    return res[:4], list(res[4:])


def _exchange_now(carry, name):
    nt = len(carry.tensors)

    def body(*refs):
        carry.hook(0, 0, refs[:nt], refs[2 * nt:3 * nt], refs[3 * nt:])

    return list(pl.pallas_call(
        body, name=name, in_specs=carry.in_specs, out_specs=carry.out_specs, out_shape=carry.out_shape,
        scratch_shapes=carry.scratch, input_output_aliases=carry.aliases(0, 0),
    )(*carry.args))


def _carried(queue, capacity_us, *args, fn, **kw):
    carry = queue.take(capacity_us * CARRY_FILL)
    if carry is None:
        return fn(*args, **kw)
    res = fn(*args, carry=carry, **kw)
    queue.done(carry, res[-1])
    return res[0] if len(res) == 2 else res[:-1]


class _Queue:
    def __init__(self, gather, label):
        self.gather, self.label = gather, label
        self.tensors, self.fifo, self.n_flush = {}, [], 0

    def push(self, name, src, n_pieces, cost_us, cols=False):
        cw = None
        if cols:
            cw = src.shape[1] if self.gather else src.shape[1] // N_DEV
            assert cw % LANE == 0
            dst_shape = (src.shape[0], N_DEV * cw) if self.gather else (N_DEV, src.shape[0], cw)
        else:
            dst_shape = ((N_DEV,) + src.shape) if self.gather else src.shape
        n_rows = src.shape[0] if (self.gather or cols) else src.shape[1]
        rows = n_rows // n_pieces
        assert rows * n_pieces == n_rows and rows % 16 == 0, (name, src.shape)
        self.tensors[name] = [src, lax.empty(dst_shape, src.dtype), cw]
        self.fifo += [(name, p * rows, rows, cost_us / n_pieces) for p in range(n_pieces)]

    def take(self, capacity_us, count=None):
        picked = []
        while self.fifo and (len(picked) < count if count is not None else capacity_us >= 0.85 * self.fifo[0][3]):
            picked.append(self.fifo.pop(0))
            capacity_us -= picked[-1][3]
        if not picked:
            return None
        names = list(dict.fromkeys(n for n, _, _, _ in picked))
        cls = _GatherCarry if self.gather else _Carry
        carry = cls([tuple(self.tensors[n]) for n in names], [(names.index(n), r0, rows) for n, r0, rows, _ in picked])
        carry.names = names
        return carry

    def done(self, carry, dsts):
        for n, d in zip(carry.names, dsts):
            self.tensors[n][1] = d

    def flush(self, count=None):
        carry = self.take(float("inf"), count)
        if carry:
            self.done(carry, _exchange_now(carry, "%s_now_%d" % (self.label, self.n_flush)))
            self.n_flush += 1

    def result(self, name, r0=0, r1=None):
        late = [i for i, (n, p0, rows, _) in enumerate(self.fifo) if n == name and p0 < (r1 or p0 + rows) and p0 + rows > r0]
        if late:
            self.flush(late[-1] + 1)
        return self.tensors[name][1]


def _cols_from_shards(g):
    return jnp.transpose(g, (1, 0, 2)).reshape(g.shape[1], N_DEV * g.shape[2])


def _shards_from_cols(w):
    r, n = w.shape
    return jnp.transpose(w.reshape(r, N_DEV, n // N_DEV), (1, 0, 2))


def _pad_cols(w, to):
    return jnp.pad(w, ((0, 0), (0, to - w.shape[1])))


def _pack(arrs):
    flat = jnp.concatenate([a.reshape(-1) for a in arrs])
    n = _rup(flat.shape[0], 256 * LANE)
    return jnp.pad(flat, (0, n - flat.shape[0])).reshape(n // LANE, LANE)


def _unpack(mat, shapes):
    flat = mat.reshape(-1)
    out, o = [], 0
    for s in shapes:
        n = math.prod(s)
        out.append(flat[o:o + n].reshape(s))
        o += n
    return out


_SMALL = ["norm_mix_g", "mu_shift", "rwkv_w0", "rwkv_a0", "rwkv_k_k", "rwkv_k_a", "rwkv_r_k", "rwkv_ln_g", "rwkv_ln_b", "conv_b",
          "lru_wr", "lru_br", "lru_wi", "lru_bi", "lru_lambda", "lru_norm_g", "norm_ffn_g", "norm_final_g"]
_SMALL_SHARDED = ["rwkv_w2", "rwkv_a2", "rwkv_g2", "conv_w"]
_BIG = ["w_in", "w_out", "ffn_w_gate", "ffn_w_up", "ffn_w_down"]
_WEIGHTS = ['norm_mix_g', 'w_in', 'mu_shift', 'rwkv_w0', 'rwkv_w2', 'rwkv_a0', 'rwkv_a2', 'rwkv_g2', 'rwkv_k_k', 'rwkv_k_a', 'rwkv_r_k',
            'rwkv_ln_g', 'rwkv_ln_b', 'conv_w', 'conv_b', 'lru_wr', 'lru_br', 'lru_wi', 'lru_bi', 'lru_lambda', 'lru_norm_g', 'w_out',
            'norm_ffn_g', 'ffn_w_gate', 'ffn_w_up', 'ffn_w_down', 'norm_final_g']


def _step(W, M, V, x, loss_target):
    bl, seq, d = x.shape
    n = bl * seq
    rw = W["rwkv_w0"].shape[1]
    nh = W["rwkv_r_k"].shape[1]
    assert W["rwkv_r_k"].shape[2] == HEAD and nh * HEAD == rw and rw % LANE == 0
    dl, al, gl = W["rwkv_w2"].shape[1], W["rwkv_a2"].shape[1], W["rwkv_g2"].shape[1]
    dlp, alp, glp = _rup(dl, LANE), _rup(al, LANE), _rup(gl, LANE)
    lorap = dlp + alp + glp
    lw_ = W["conv_b"].shape[1]
    nblk, lbw = W["lru_wr"].shape[1], W["lru_wr"].shape[2]
    assert lbw == LANE and nblk * lbw == lw_
    o_xb, o_gate, o_rw = 0, lw_, 2 * lw_
    o_lora = 3 * rw
    rwp = o_lora + lorap
    inp = o_rw + rwp
    nsh_ff = W["ffn_w_gate"].shape[2]
    nshp = _rup(nsh_ff, LANE)
    dffp = N_DEV * nshp
    x2 = x.reshape(n, d)
    tgt2 = loss_target.reshape(n, d)

    gq = _Queue(True, "gather")
    kp = 2 if d % (2 * LANE) == 0 else 1
    gq.push("w_in", W["w_in"][0].astype(BF16), kp, 490)
    gq.push("small", _pack([W[k][0] for k in _SMALL_SHARDED]), 1, 10)
    gq.push("w_out", W["w_out"][0].astype(BF16), 2, 180)
    pad_ff = nshp - nsh_ff
    gq.push("ffn_w_gate", jnp.pad(W["ffn_w_gate"][0].astype(BF16), ((0, 0), (0, pad_ff))), 4, 490, cols=True)
    gq.push("ffn_w_up", jnp.pad(W["ffn_w_up"][0].astype(BF16), ((0, 0), (0, pad_ff))), 4, 490, cols=True)
    gq.push("ffn_w_down", jnp.pad(W["ffn_w_down"][0].astype(BF16), ((0, pad_ff), (0, 0))), 4, 490)
    gmm = functools.partial(_carried, gq, fn=_mm)
    gstage = functools.partial(_carried, gq, fn=_stage_fwd)

    o1 = 3 * rw

    def my_cols(g):
        w_l = _cols_from_shards(g)
        return jnp.concatenate([w_l[:, o1 + dl + al + gl:], w_l[:, :o1], _pad_cols(w_l[:, o1:o1 + dl], dlp),
                                _pad_cols(w_l[:, o1 + dl:o1 + dl + al], alp), _pad_cols(w_l[:, o1 + dl + al:o1 + dl + al + gl], glp)], axis=1)

    mu_l = W["mu_shift"]
    mu = jnp.concatenate([mu_l[:, :o1], _pad_cols(mu_l[:, o1:o1 + dl], dlp), _pad_cols(mu_l[:, o1 + dl:o1 + dl + al], alp),
                          _pad_cols(mu_l[:, o1 + dl + al:], glp)], axis=1)
    r_k = W["rwkv_r_k"].reshape(1, rw)

    tile = _pick(n, (256, 128, 64))
    tile_s = _pick(n, (128, 64))
    tile_f = _pick(n, (256, 128, 64))
    ct_seq = _pick(math.gcd(rwp, lw_), (256, 128))
    assert o_rw % ct_seq == 0 and o_gate % lw_ == 0
    ct_h = _pick(rw, (512, 256, 128))
    gi = lax.broadcasted_iota(jnp.int32, (ct_h, ct_h), 0) // HEAD
    gj = lax.broadcasted_iota(jnp.int32, (ct_h, ct_h), 1) // HEAD
    gsum = ((gi == gj).astype(BF16), (ct_h, ct_h), lambda j: (0, 0))
    full = lambda a: (a, a.shape, lambda j: (0,) * a.ndim)
    rowp = lambda a, ct: (a,) + _row(ct)

    u1, = gstage(210, _f_rmsnorm, "norm_mix_fwd", n, d, tile, d, [(x2, 0)], [full(W["norm_mix_g"])], [], [BF16])
    p, w_rows = None, []
    for i in range(kp):
        rows = slice(i * (d // kp), (i + 1) * (d // kp))
        g_in = gq.result("w_in", rows.start, rows.stop)
        w_rows.append(my_cols(g_in[:, rows, :]))
        p = gmm(190, u1[:, rows], w_rows[-1], name="mm_in_%d" % i, add=p)
    w_in = jnp.concatenate(w_rows, axis=0)
    g_small = gq.result("small")
    sm_shapes = [W[k][0].shape for k in _SMALL_SHARDED]
    sm = [_unpack(g_small[s], sm_shapes) for s in range(N_DEV)]
    w2, a2, g2, conv_w = [jnp.concatenate([sm[s][i] for s in range(N_DEV)], axis=1) for i in range(4)]
    w_lora = jnp.zeros((lorap, 3 * rw), F32)
    w_lora = w_lora.at[:dl, :rw].set(w2).at[dlp:dlp + al, rw:2 * rw].set(a2).at[dlp + alp:dlp + alp + gl, 2 * rw:].set(g2)
    w_lora = w_lora.astype(BF16)
    ps = _lerp_fwd(p, o_rw, mu, bl, seq, rwp, ct_seq)
    f_lora = functools.partial(_f_lora_act, widths=(dlp, alp))
    lact, = _stage_fwd(f_lora, "lora_act_fwd", n, lorap, tile, lorap, [(ps, o_lora)], [], [], [BF16])
    wag = _mm(lact, w_lora, name="mm_lora")
    pre_par = [rowp(W["rwkv_w0"], ct_h), rowp(W["rwkv_a0"], ct_h), rowp(W["rwkv_k_k"], ct_h), rowp(W["rwkv_k_a"], ct_h)]
    pre_acts = [(ps, rw), (wag, 0), (wag, rw)]
    lw, k2, na, bb = gstage(120, _f_rwkv_pre, "rwkv_pre_fwd", n, rw, tile_f, ct_h, pre_acts, pre_par, [gsum], [F32] * 4)
    ysc, st = _carried(gq, 230, None, lw, k2, None, na, bb, ps, bl, seq, rw, fn=_rwkv_scan_fwd)
    post_par = [rowp(W["rwkv_ln_g"], ct_h), rowp(W["rwkv_ln_b"], ct_h), rowp(r_k, ct_h)]
    post_acts = [(ysc, 0), (ps, 0), (k2, 0), (ps, 2 * rw), (wag, 2 * rw)]
    ya, = gstage(120, _f_rwkv_post, "rwkv_post_fwd", n, rw, tile_f, ct_h, post_acts, post_par, [gsum], [BF16])

    xc = _conv_fwd(p, o_xb, conv_w, W["conv_b"], bl, seq, lw_, ct_seq)
    f_gates = functools.partial(_f_lru_gates, seq=seq)
    blk3 = lambda a: (a[0], (1, LANE, LANE), lambda j: (j, 0, 0))
    gate_par = [blk3(W["lru_wr"]), rowp(W["lru_br"], LANE), blk3(W["lru_wi"]), rowp(W["lru_bi"], LANE), rowp(W["lru_lambda"], LANE)]
    tile_g = _pick(n, (1024, 512, 256, 128, 64))
    a_l, bx = gstage(160, f_gates, "lru_gates_fwd", n, lw_, tile_g, LANE, [(xc, 0)], gate_par, [], [F32, F32])
    ct_l = _pick(lw_, (256, 128))
    h_l = _lru_scan_fwd(a_l, bx, bl, seq, lw_, ct_l)
    lpost_par = [full(W["lru_norm_g"])]
    yb, = _stage_fwd(_f_lru_post, "lru_post_fwd", n, lw_, tile_s, lw_, [(h_l, 0), (p, o_gate)], lpost_par, [], [BF16])

    ycat = jnp.concatenate([ya, yb], axis=1)
    g_out = gq.result("w_out")
    w_out = g_out.reshape(N_DEV * g_out.shape[1], d)
    h1 = gmm(130, ycat, w_out, name="mm_out", add=x2)
    u2, = _stage_fwd(_f_rmsnorm, "norm_ffn_fwd", n, d, tile, d, [(h1, 0)], [full(W["norm_ffn_g"])], [], [BF16])
    w_gate = gq.result("ffn_w_gate")
    ff_gate = gmm(340, u2, w_gate, name="mm_gate", out_dtype=BF16)
    w_up = gq.result("ffn_w_up")
    ff_up = gmm(340, u2, w_up, name="mm_up", out_dtype=BF16)
    ct_f = _pick(dffp, (1024, 512, 256, 128))
    ff_acts = [(ff_gate, 0), (ff_up, 0)]
    act, = _stage_fwd(_f_swiglu, "swiglu_fwd", n, dffp, _pick(n, (512, 256, 128, 64)), ct_f, ff_acts, [], [], [BF16])
    gq.flush()
    w_down = gq.result("ffn_w_down").reshape(dffp, d)
    h2 = _mm(act, w_down, name="mm_down", add=h1)

    dh2, dg_final, lsum, dh2b = _loss_head(h2, W["norm_final_g"].reshape(1, d), tgt2, tile_s)
    loss = lax.psum(lsum[0, 0], ("x", "y", "c"))
    queue = _Queue(False, "exchange")

    cmm = functools.partial(_carried, queue, fn=_mm)
    cstage = functools.partial(_carried, queue, fn=_stage_bwd)
    dact = _mm(dh2b, w_down, name="mm_dact", tb=True, out_dtype=BF16)
    dw_down = _mm(act, dh2b, name="mm_dw_down", ta=True, out_dtype=BF16)
    queue.push("ffn_w_down", dw_down.reshape(N_DEV, nshp, d), 8, 1000)
    (dgate, dup), _ = _stage_bwd(_f_swiglu, "swiglu_bwd", n, dffp, tile, ct_f, ff_acts, [], [], [(dact, 0)], [BF16, BF16])
    du2 = cmm(400, dgate, w_gate, name="mm_du2_gate", tb=True)
    dw_gate = cmm(350, u2, dgate, name="mm_dw_gate", ta=True, out_dtype=BF16)
    queue.push("ffn_w_gate", dw_gate, 8, 1000, cols=True)
    du2 = cmm(400, dup, w_up, name="mm_du2_up", tb=True, add=du2)
    dw_up = cmm(350, u2, dup, name="mm_dw_up", ta=True, out_dtype=BF16)
    queue.push("ffn_w_up", dw_up, 8, 1000, cols=True)
    (dh1,), (dg_ffn,) = cstage(130, _f_rmsnorm, "norm_ffn_bwd", n, d, tile_s, d, [(h1, 0)], [full(W["norm_ffn_g"])], [], [(du2, 0)], [F32],
                               extra_add=(dh2, 0))
    dh1b = dh1.astype(BF16)
    dycat = cmm(135, dh1b, w_out, name="mm_dycat", tb=True)
    dw_out = cmm(170, ycat, dh1b, name="mm_dw_out", ta=True, out_dtype=BF16)
    queue.push("w_out", dw_out.reshape(N_DEV, -1, d), 4, 370)

    (dysc, dr_p, dk2_p, dv_p, dg_g), (dln_g, dln_b, dr_k) = cstage(
        195, _f_rwkv_post, "rwkv_post_bwd", n, rw, tile_s, ct_h, post_acts, post_par, [gsum], [(dycat, 0)], [F32] * 5)
    dr_s, dlw, dk2_s, dv_s, dna, dbb = _carried(queue, 650, lw, k2, na, bb, ps, st, dysc, bl, seq, rw, fn=_rwkv_scan_bwd)
    dk2 = dk2_p + dk2_s
    (dk, dwlin, dalin), (dw0, da0, dk_k, dk_a) = cstage(
        180, _f_rwkv_pre, "rwkv_pre_bwd", n, rw, tile_s, ct_h, pre_acts, pre_par, [gsum], [(dlw, 0), (dk2, 0), (dna, 0), (dbb, 0)], [F32] * 3)
    dwag = jnp.concatenate([dwlin, dalin, dg_g], axis=1).astype(BF16)
    dlact = cmm(75, dwag, w_lora, name="mm_dlact", tb=True)
    dw_lora = cmm(50, lact, dwag, name="mm_dw_lora", ta=True)
    (dps_lora,), _ = _stage_bwd(f_lora, "lora_act_bwd", n, lorap, tile, lorap, [(ps, o_lora)], [], [], [(dlact, 0)], [F32])
    dp_segs, dmu_segs = [], []
    for nm, o, wdt, parts in (("r", 0, rw, [dr_p, dr_s]), ("k", rw, rw, [dk]), ("v", 2 * rw, rw, [dv_p, dv_s]), ("lora", o_lora, lorap, [dps_lora])):
        dp_s, dmu_s = _lerp_bwd(p, o_rw + o, mu, o, parts, "lerp_bwd_" + nm, bl, seq, wdt, ct_seq, BF16)
        dp_segs.append(dp_s)
        dmu_segs.append(dmu_s)
    dmu = jnp.concatenate(dmu_segs, axis=1)

    (dh_l, dgate_l), (dlru_norm_g,) = cstage(80, _f_lru_post, "lru_post_bwd", n, lw_, tile_s, lw_, [(h_l, 0), (p, o_gate)], lpost_par, [],
                                                 [(dycat, rw)], [F32, BF16])
    da_l, dbx = _lru_scan_bwd(a_l, h_l, dh_l, bl, seq, lw_, ct_l)
    (dxc,), (dwr, dbr, dwi, dbi, dlam) = cstage(240, f_gates, "lru_gates_bwd", n, lw_, tile_g, LANE, [(xc, 0)], gate_par, [],
                                                [(da_l, 0), (dbx, 0)], [F32])
    dxb, dconv_w, dconv_b = _conv_bwd(p, o_xb, conv_w, dxc, bl, seq, lw_, ct_seq, BF16)
    sh_full = [dw_lora[:dl, :rw], dw_lora[dlp:dlp + al, rw:2 * rw], dw_lora[dlp + alp:dlp + alp + gl, 2 * rw:], dconv_w]
    assert rw == lw_
    rows_sh = sum(a.shape[0] for a in sh_full)
    pad_sh = _rup(rows_sh, 16) - rows_sh
    queue.push("small_sharded", jnp.pad(jnp.concatenate([_shards_from_cols(a) for a in sh_full], axis=1), ((0, 0), (0, pad_sh), (0, 0))), 1, 40)
    stack_sh = lambda D: jnp.pad(jnp.concatenate([D[k][0] for k in _SMALL_SHARDED], axis=0), ((0, pad_sh), (0, 0)))

    dp = jnp.concatenate([dxb, dgate_l] + dp_segs, axis=1)
    dw_in = cmm(340, u1, dp, name="mm_dw_in", ta=True, out_dtype=BF16)
    ol = o_rw + o_lora
    dw_in_l = jnp.concatenate([dw_in[:, o_rw:ol], dw_in[:, ol:ol + dl], dw_in[:, ol + dlp:ol + dlp + al],
                               dw_in[:, ol + dlp + alp:ol + dlp + alp + gl], dw_in[:, :o_rw]], axis=1)
    queue.push("w_in", _shards_from_cols(dw_in_l), 8, 970)
    du1 = cmm(380, dp, w_in, name="mm_du1", tb=True)
    (grad_x,), (dg_mix,) = cstage(90, _f_rmsnorm, "norm_mix_bwd", n, d, tile_s, d, [(x2, 0)], [full(W["norm_mix_g"])], [], [(du1, 0)], [F32],
                                  extra_add=(dh1, 0))
    dmu_l = jnp.concatenate([dmu[:, :o1], dmu[:, o_lora:o_lora + dl], dmu[:, o_lora + dlp:o_lora + dlp + al],
                             dmu[:, o_lora + dlp + alp:o_lora + dlp + alp + gl]], axis=1)
    small_g = {"norm_mix_g": dg_mix, "mu_shift": dmu_l, "rwkv_w0": dw0, "rwkv_a0": da0, "rwkv_k_k": dk_k, "rwkv_k_a": dk_a,
               "rwkv_r_k": dr_k.reshape(W["rwkv_r_k"].shape), "rwkv_ln_g": dln_g, "rwkv_ln_b": dln_b, "conv_b": dconv_b,
               "lru_wr": dwr[None], "lru_br": dbr, "lru_wi": dwi[None], "lru_bi": dbi, "lru_lambda": dlam, "lru_norm_g": dlru_norm_g,
               "norm_ffn_g": dg_ffn, "norm_final_g": dg_final.reshape(W["norm_final_g"].shape)}
    gq.push("small_grads", _pack([small_g[k] for k in _SMALL]), 1, 50)
    out = {}
    for k in ["ffn_w_down", "ffn_w_gate", "ffn_w_up", "w_out", "w_in"]:
        if k == "w_in":
            queue.flush()
        res = _carried(gq if k == "w_in" else queue, 105, queue.result(k), W[k][0], M[k][0], V[k][0], "adamw_" + k, fn=_adamw)
        out[k] = [o[None] for o in res]
    pk = lambda D: _pack([D[k] for k in _SMALL])
    res = _adamw(gq.result("small_grads"), pk(W), pk(M), pk(V), "adamw_small")
    shapes = [W[k].shape for k in _SMALL]
    for i, r in enumerate(res):
        for k, a in zip(_SMALL, _unpack(r, shapes)):
            out.setdefault(k, [None] * 4)[i] = a
    res = _adamw(queue.result("small_sharded"), stack_sh(W), stack_sh(M), stack_sh(V), "adamw_small_sharded")
    for i, r in enumerate(res):
        o = 0
        for k in _SMALL_SHARDED:
            rows = W[k].shape[1]
            out.setdefault(k, [None] * 4)[i] = r[o:o + rows][None]
            o += rows
    return loss, grad_x.reshape(x.shape), out


def kernel(x, norm_mix_g, w_in, mu_shift, rwkv_w0, rwkv_w2, rwkv_a0, rwkv_a2, rwkv_g2, rwkv_k_k, rwkv_k_a, rwkv_r_k, rwkv_ln_g, rwkv_ln_b, conv_w, conv_b, lru_wr, lru_br, lru_wi, lru_bi, lru_lambda, lru_norm_g, w_out, norm_ffn_g, ffn_w_gate, ffn_w_up, ffn_w_down, norm_final_g, loss_target, m_norm_mix_g, m_w_in, m_mu_shift, m_rwkv_w0, m_rwkv_w2, m_rwkv_a0, m_rwkv_a2, m_rwkv_g2, m_rwkv_k_k, m_rwkv_k_a, m_rwkv_r_k, m_rwkv_ln_g, m_rwkv_ln_b, m_conv_w, m_conv_b, m_lru_wr, m_lru_br, m_lru_wi, m_lru_bi, m_lru_lambda, m_lru_norm_g, m_w_out, m_norm_ffn_g, m_ffn_w_gate, m_ffn_w_up, m_ffn_w_down, m_norm_final_g, v_norm_mix_g, v_w_in, v_mu_shift, v_rwkv_w0, v_rwkv_w2, v_rwkv_a0, v_rwkv_a2, v_rwkv_g2, v_rwkv_k_k, v_rwkv_k_a, v_rwkv_r_k, v_rwkv_ln_g, v_rwkv_ln_b, v_conv_w, v_conv_b, v_lru_wr, v_lru_br, v_lru_wi, v_lru_bi, v_lru_lambda, v_lru_norm_g, v_w_out, v_norm_ffn_g, v_ffn_w_gate, v_ffn_w_up, v_ffn_w_down, v_norm_final_g):
    a = locals()
    W = {k: a[k] for k in _WEIGHTS}
    M = {k: a["m_" + k] for k in _WEIGHTS}
    V = {k: a["v_" + k] for k in _WEIGHTS}
    loss, grad_x, out = _step(W, M, V, x, loss_target)
    res = [loss, grad_x]
    for i in range(4):
        res += [out[k][i].reshape(W[k].shape) for k in _WEIGHTS]
    return tuple(res)
```

```python
import functools
import math

import jax
import jax.numpy as jnp
from jax import lax
from jax.experimental import pallas as pl
from jax.experimental.pallas import tpu as pltpu

F32 = jnp.float32
BF16 = jnp.bfloat16
HI = lax.Precision.HIGHEST
MESH = pl.DeviceIdType.MESH

N_DEV = 8
LANE = 128
HEAD = 64
MM_MAX_TK = 5632
CARRY_FILL = 1.2
SCAN_CHUNK = 64
SCAN_GROUP = 2
SCAN_PAIRS = 8
VMEM_LIMIT = 56 * 1024 * 1024

NORM_EPS = 1e-6
GN_EPS = 64e-5
LRU_C = 8.0
ADAM_LR, ADAM_B1, ADAM_B2, ADAM_EPS, ADAM_WD, ADAM_STEP = 0.001, 0.9, 0.999, 1e-08, 0.01, 10


def _pick(n, cands):
    for c in cands:
        if n % c == 0:
            return c
    return n


def _rup(n, m):
    return (n + m - 1) // m * m


def _cparams(dims):
    return pltpu.CompilerParams(dimension_semantics=dims, vmem_limit_bytes=VMEM_LIMIT)


def _sigmoid(x):
    return 1.0 / (1.0 + jnp.exp(-x))


def _softplus(z):
    return jnp.maximum(z, 0.0) + jnp.log(1.0 + jnp.exp(-jnp.abs(z)))


def _neg_expm1(x):
    series = -(x * (1.0 + 0.5 * x * (1.0 + (x / 3.0) * (1.0 + 0.25 * x))))
    return jnp.where(jnp.abs(x) < 0.03, series, 1.0 - jnp.exp(x))


def _gelu(x):
    return 0.5 * x * (1.0 + jnp.tanh(0.7978845608028654 * (x + 0.044715 * (x * x * x))))


def _dot(a, b, dims, precision=None):
    return lax.dot_general(a, b, (dims, ((), ())), precision=precision, preferred_element_type=F32)


def _nn(a, b, precision=None):
    return _dot(a, b, ((1,), (0,)), precision)


def _coords():
    return lax.axis_index("x"), lax.axis_index("y"), lax.axis_index("c")


class _Carry:
    def __init__(self, tensors, items):
        self.tensors, self.items = tensors, items
        nt, ni = len(tensors), len(items)
        any_spec = pl.BlockSpec(memory_space=pl.ANY)
        self.args = [t[0] for t in tensors] + [t[1] for t in tensors]
        self.in_specs = [any_spec] * (2 * nt)
        self.out_specs = [any_spec] * nt
        self.out_shape = [jax.ShapeDtypeStruct(t[1].shape, t[1].dtype) for t in tensors]
        self.scratch = [pltpu.SemaphoreType.DMA((ni, N_DEV - 1)), pltpu.SemaphoreType.DMA((ni, N_DEV - 1)), pltpu.SemaphoreType.DMA((ni,))]

    def aliases(self, first_in, first_out):
        nt = len(self.tensors)
        return {first_in + nt + t: first_out + t for t in range(nt)}

    def _slot(self, ref, t, idx, win):
        cw = self.tensors[t][2]
        return ref.at[idx, win] if cw is None else ref.at[win, pl.ds(pl.multiple_of(idx * cw, LANE), cw)]

    def _copies(self, src_refs, dst_refs, sems):
        send_sems, recv_sems, local_sems = sems
        x, y, c = _coords()
        my = 4 * x + 2 * y + c
        out = []
        for n, (t, r0, rows) in enumerate(self.items):
            win = pl.ds(r0, rows)
            out.append(pltpu.make_async_copy(self._slot(src_refs[t], t, my, win), dst_refs[t].at[my, win], local_sems.at[n]))
            for k in range(1, N_DEV):
                px, py, pc = x ^ ((k >> 2) & 1), y ^ ((k >> 1) & 1), c ^ (k & 1)
                out.append(pltpu.make_async_remote_copy(
                    src_ref=self._slot(src_refs[t], t, 4 * px + 2 * py + pc, win), dst_ref=dst_refs[t].at[my, win],
                    send_sem=send_sems.at[n, k - 1], recv_sem=recv_sems.at[n, k - 1],
                    device_id=(px, py, pc), device_id_type=MESH))
        return out

    def hook(self, step, last, src_refs, dst_refs, sems):
        if last == 0:
            for cp in self._copies(src_refs, dst_refs, sems):
                cp.start()
            for cp in self._copies(src_refs, dst_refs, sems):
                cp.wait()
            return

        @pl.when(step == 0)
        def _():
            for cp in self._copies(src_refs, dst_refs, sems):
                cp.start()

        @pl.when(step == last)
        def _():
            for cp in self._copies(src_refs, dst_refs, sems):
                cp.wait()


class _GatherCarry(_Carry):
    def hook(self, step, last, src_refs, dst_refs, sems):
        send_sems, recv_sems, local_sems = sems
        x, y, c = _coords()
        me, sibling = (x, y, c), (x, y, 1 - c)
        chips = [(1 - x, y), (x, 1 - y), (1 - x, 1 - y)]

        def per_item(fn):
            for n, (t, r0, rows) in enumerate(self.items):
                win = pl.ds(r0, rows)

                def copy(k, block, to, own=False, n=n, t=t, win=win):
                    slot = self._slot(dst_refs[t], t, 4 * block[0] + 2 * block[1] + block[2], win)
                    return pltpu.make_async_remote_copy(
                        src_ref=src_refs[t].at[win] if own else slot, dst_ref=slot,
                        send_sem=send_sems.at[n, k], recv_sem=recv_sems.at[n, k], device_id=to, device_id_type=MESH)

                mine = pltpu.make_async_copy(src_refs[t].at[win], self._slot(dst_refs[t], t, 4 * x + 2 * y + c, win), local_sems.at[n])
                first = [copy(0, me, sibling, own=True)] + [copy(1 + j, me, (*chip, c), own=True) for j, chip in enumerate(chips)]
                fn(copy, mine, first)

        def begin(copy, mine, first):
            mine.start()
            for cp in first:
                cp.start()

        def pass_on(copy, mine, first):
            for j, chip in enumerate(chips):
                copy(1 + j, (*chip, c), me).wait_recv()
                copy(4 + j, (*chip, c), sibling).start()

        def finish(copy, mine, first):
            copy(0, sibling, me).wait_recv()
            for j, chip in enumerate(chips):
                copy(4 + j, (*chip, 1 - c), me).wait_recv()
            for cp in first + [copy(4 + j, (*chip, c), sibling) for j, chip in enumerate(chips)]:
                cp.wait_send()
            mine.wait()

        if last == 0:
            for fn in (begin, pass_on, finish):
                per_item(fn)
            return
        late = max(1, (7 * last) // 8)
        for at, fn in ((0, begin), (late, pass_on), (last, finish)):
            pl.when(step == at)(functools.partial(per_item, fn))


def _mm(a, b, *, name, ta=False, tb=False, out_dtype=F32, add=None, tiles=None, carry=None):
    M, K = (a.shape[1], a.shape[0]) if ta else a.shape
    N = b.shape[0] if tb else b.shape[1]
    assert (b.shape[1] if tb else b.shape[0]) == K, (a.shape, b.shape, ta, tb)
    tk = max(t for t in range(LANE, min(K, MM_MAX_TK) + 1, LANE) if K % t == 0)
    tm, tn, tk = tiles or (_pick(M, (1024, 512, 256, 128)), _pick(N, (512, 256, 128)), tk)
    nk = K // tk
    dims = ((0 if ta else 1,), (1 if tb else 0,))

    n_in = 2 + (add is not None)
    nt = len(carry.tensors) if carry else 0
    gi, gj = M // tm, N // tn

    def kern(*refs):
        a_ref, b_ref = refs[:2]
        add_ref = refs[2] if add is not None else None
        o_ref = refs[n_in + 2 * nt]
        scr = refs[n_in + 3 * nt + 1:]
        if carry:
            step = (pl.program_id(0) * gj + pl.program_id(1)) * nk + pl.program_id(2)
            carry.hook(step, gi * gj * nk - 1, refs[n_in:n_in + nt], refs[n_in + 2 * nt + 1:n_in + 3 * nt + 1], scr[-3:])

        def finish(r):
            if add is not None:
                r = r + add_ref[...].astype(F32)
            o_ref[...] = r.astype(o_ref.dtype)

        if nk == 1:
            finish(_dot(a_ref[...], b_ref[...], dims))
            return
        acc = scr[0]
        k = pl.program_id(2)

        @pl.when(k == 0)
        def _():
            acc[...] = jnp.zeros_like(acc)

        acc[...] += _dot(a_ref[...], b_ref[...], dims)

        @pl.when(k == nk - 1)
        def _():
            finish(acc[...])

    a_spec = pl.BlockSpec((tk, tm), lambda i, j, k: (k, i)) if ta else pl.BlockSpec((tm, tk), lambda i, j, k: (i, k))
    b_spec = pl.BlockSpec((tn, tk), lambda i, j, k: (j, k)) if tb else pl.BlockSpec((tk, tn), lambda i, j, k: (k, j))
    o_spec = pl.BlockSpec((tm, tn), lambda i, j, k: (i, j))
    in_specs = [a_spec, b_spec] + ([o_spec] if add is not None else [])
    args = (a, b) + ((add,) if add is not None else ())
    scratch = [pltpu.VMEM((tm, tn), F32)] if nk > 1 else []
    o_shape = jax.ShapeDtypeStruct((M, N), out_dtype)
    if not carry:
        return pl.pallas_call(
            kern, name=name, grid=(gi, gj, nk), in_specs=in_specs, out_specs=o_spec, out_shape=o_shape, scratch_shapes=scratch,
            compiler_params=_cparams(("parallel", "parallel", "arbitrary")),
        )(*args)
    res = pl.pallas_call(
        kern, name=name, grid=(gi, gj, nk), in_specs=in_specs + carry.in_specs, out_specs=[o_spec] + carry.out_specs,
        out_shape=[o_shape] + carry.out_shape, scratch_shapes=scratch + carry.scratch,
        input_output_aliases=carry.aliases(n_in, 1), compiler_params=_cparams(("arbitrary", "arbitrary", "arbitrary")),
    )(*args, *carry.args)
    return res[0], list(res[1:])


def _stage_specs(acts, params, consts, tile, ct):
    act_specs = [pl.BlockSpec((tile, ct), functools.partial(lambda j, i, o: (i, o + j), o=off // ct)) for _, off in acts]
    par_specs = [pl.BlockSpec(bs, functools.partial(lambda j, i, im: im(j), im=im)) for _, bs, im in params]
    con_specs = [pl.BlockSpec(bs, functools.partial(lambda j, i, im: im(j), im=im)) for _, bs, im in consts]
    return act_specs, par_specs, con_specs


def _stage_fwd(f, name, n_rows, width, tile, ct, acts, params, consts, out_dtypes, carry=None):
    for _, off in acts:
        assert off % ct == 0
    na, npar, nc, no = len(acts), len(params), len(consts), len(out_dtypes)
    n_in = na + npar + nc
    nt = len(carry.tensors) if carry else 0
    gj, gi = width // ct, n_rows // tile

    def kern(*refs):
        if carry:
            step = pl.program_id(0) * gi + pl.program_id(1)
            carry.hook(step, gj * gi - 1, refs[n_in:n_in + nt], refs[n_in + 2 * nt + no:n_in + 3 * nt + no], refs[n_in + 3 * nt + no:])
        a = [r[...].astype(F32) for r in refs[:na]]
        p = [r[...] for r in refs[na:na + npar]]
        c = [r[...] for r in refs[na + npar:n_in]]
        outs = f(a, p, c, pl.program_id(1) * tile)
        for r, o in zip(refs[n_in + 2 * nt:n_in + 2 * nt + no], outs):
            r[...] = o.astype(r.dtype)

    act_specs, par_specs, con_specs = _stage_specs(acts, params, consts, tile, ct)
    o_spec = pl.BlockSpec((tile, ct), lambda j, i: (i, j))
    in_specs = act_specs + par_specs + con_specs
    out_shape = [jax.ShapeDtypeStruct((n_rows, width), d) for d in out_dtypes]
    args = [a for a, _ in acts] + [p for p, _, _ in params] + [c for c, _, _ in consts]
    if not carry:
        return tuple(pl.pallas_call(
            kern, name=name, grid=(gj, gi), in_specs=in_specs, out_specs=[o_spec] * no, out_shape=out_shape,
            compiler_params=_cparams(("parallel", "parallel")),
        )(*args))
    res = pl.pallas_call(
        kern, name=name, grid=(gj, gi), in_specs=in_specs + carry.in_specs, out_specs=[o_spec] * no + carry.out_specs,
        out_shape=out_shape + carry.out_shape, scratch_shapes=carry.scratch, input_output_aliases=carry.aliases(n_in, no),
        compiler_params=_cparams(("arbitrary", "arbitrary")),
    )(*args, *carry.args)
    return tuple(res[:no]), list(res[no:])


def _stage_bwd(f, name, n_rows, width, tile, ct, acts, params, consts, couts, dact_dtypes, extra_add=None, carry=None):
    na, npar, nc, no = len(acts), len(params), len(consts), len(couts)
    nx = 0 if extra_add is None else 1
    nt = len(carry.tensors) if carry else 0
    n_in = na + npar + nc + no + nx
    gj, gi = width // ct, n_rows // tile

    def kern(*refs):
        if carry:
            step = pl.program_id(0) * gi + pl.program_id(1)
            n_out = n_in + 2 * nt + na + npar
            carry.hook(step, gj * gi - 1, refs[n_in:n_in + nt], refs[n_out:n_out + nt], refs[n_out + nt:])
        a = [r[...].astype(F32) for r in refs[:na]]
        p = [r[...] for r in refs[na:na + npar]]
        c = [r[...] for r in refs[na + npar:na + npar + nc]]
        base = na + npar + nc
        co = [r[...].astype(F32) for r in refs[base:base + no]]
        base += no
        x_refs = refs[base:base + nx]
        base += nx + 2 * nt
        da_refs = refs[base:base + na]
        dp_refs = refs[base + na:base + na + npar]
        row0 = pl.program_id(1) * tile
        _, vjp = jax.vjp(lambda aa, pp: tuple(f(aa, pp, c, row0)), a, p)
        da, dp = vjp(tuple(co))
        for k, (r, d) in enumerate(zip(da_refs, da)):
            if k == 0 and nx:
                d = d + x_refs[0][...].astype(F32)
            r[...] = d.astype(r.dtype)
        first = pl.program_id(1) == 0
        for r, d in zip(dp_refs, dp):
            @pl.when(first)
            def _(r=r, d=d):
                r[...] = d

            @pl.when(jnp.logical_not(first))
            def _(r=r, d=d):
                r[...] += d

    act_specs, par_specs, con_specs = _stage_specs(acts, params, consts, tile, ct)
    t_spec = pl.BlockSpec((tile, ct), lambda j, i: (i, j))
    co_specs = [pl.BlockSpec((tile, ct), functools.partial(lambda j, i, o: (i, o + j), o=off // ct)) for _, off in couts]
    x_specs = [] if extra_add is None else [pl.BlockSpec((tile, ct), functools.partial(lambda j, i, o: (i, o + j), o=extra_add[1] // ct))]
    x_args = [] if extra_add is None else [extra_add[0]]
    in_specs = act_specs + par_specs + con_specs + co_specs + x_specs
    out_specs = [t_spec] * na + par_specs
    out_shape = [jax.ShapeDtypeStruct((n_rows, width), d) for d in dact_dtypes] + [jax.ShapeDtypeStruct(p.shape, F32) for p, _, _ in params]
    args = [a for a, _ in acts] + [p for p, _, _ in params] + [c for c, _, _ in consts] + [c for c, _ in couts] + x_args
    if not carry:
        outs = pl.pallas_call(
            kern, name=name, grid=(gj, gi), in_specs=in_specs, out_specs=out_specs, out_shape=out_shape,
            compiler_params=_cparams(("parallel", "arbitrary")),
        )(*args)
        return tuple(outs[:na]), tuple(outs[na:])
    outs = pl.pallas_call(
        kern, name=name, grid=(gj, gi), in_specs=in_specs + carry.in_specs, out_specs=out_specs + carry.out_specs,
        out_shape=out_shape + carry.out_shape, scratch_shapes=carry.scratch, input_output_aliases=carry.aliases(n_in, na + npar),
        compiler_params=_cparams(("arbitrary", "arbitrary")),
    )(*args, *carry.args)
    return tuple(outs[:na]), tuple(outs[na:na + npar]), list(outs[na + npar:])


def _row(ct):
    return (1, ct), (lambda j: (0, j))


def _f_rmsnorm(a, p, c, row0):
    x, = a
    g, = p
    return (x * lax.rsqrt(jnp.mean(x * x, axis=-1, keepdims=True) + NORM_EPS) * g,)


def _f_lora_act(a, p, c, row0, widths):
    x, = a
    dl, al = widths
    col = lax.broadcasted_iota(jnp.int32, x.shape, 1)
    return (jnp.where(col < dl, jnp.tanh(x), jnp.where(col < dl + al, x, _sigmoid(x))),)


def _head_sums_raw(x, ones):
    hi = x.astype(BF16)
    lo = (x - hi.astype(F32)).astype(BF16)
    return _nn(hi, ones) + _nn(lo, ones)


@jax.custom_vjp
def _head_sums(x, ones):
    return _head_sums_raw(x, ones)


_head_sums.defvjp(lambda x, ones: (_head_sums_raw(x, ones), ones),
                  lambda ones, ct: (_head_sums_raw(ct, ones), jnp.zeros_like(ones)))


def _f_rwkv_pre(a, p, c, row0):
    k, wlin, alin = a
    w0, a0, k_k, k_a = p
    gsum, = c
    w = -_softplus(-(w0 + wlin)) - 0.5
    lw = -jnp.exp(w)
    alpha = _sigmoid(a0 + alin)
    kk = k * k_k
    ss = _head_sums(kk * kk, gsum)
    kk = kk * lax.rsqrt(jnp.maximum(ss, 1e-24))
    k2 = k * (1.0 + (alpha - 1.0) * k_a)
    return lw, k2, -kk, kk * alpha


def _f_rwkv_post(a, p, c, row0):
    y, r, k2, v, g = a
    ln_g, ln_b, r_k = p
    gsum, = c
    inv = 1.0 / HEAD
    mean = _head_sums(y, gsum) * inv
    yc = y - mean
    var = _head_sums(yc * yc, gsum) * inv
    yn = yc * lax.rsqrt(var + GN_EPS) * ln_g + ln_b
    bonus = _head_sums(r * k2 * r_k, gsum)
    return ((yn + bonus * v) * g,)


def _f_lru_gates(a, p, c, row0, seq):
    xc, = a
    wr, br, wi, bi, lam = p
    xb = xc.astype(BF16)
    rg = _sigmoid(_nn(xb, wr[0].astype(BF16)) + br)
    ig = _sigmoid(_nn(xb, wi[0].astype(BF16)) + bi)
    log_a = -LRU_C * rg * _softplus(-lam)
    a_t = jnp.exp(log_a)
    mult = jnp.sqrt(_neg_expm1(2.0 * log_a))
    row = row0 + lax.broadcasted_iota(jnp.int32, xc.shape, 0)
    mult = jnp.where(row % seq == 0, 1.0, mult)
    return a_t, mult * ig * xc


def _f_lru_post(a, p, c, row0):
    h, gate = a
    g, = p
    y = h * _gelu(gate)
    return (y * lax.rsqrt(jnp.mean(y * y, axis=-1, keepdims=True) + NORM_EPS) * g,)


def _f_swiglu(a, p, c, row0):
    gate, up = a
    return (gate * _sigmoid(gate) * up,)


def _shift_down(x, s, row):
    return jnp.where(row >= s, pltpu.roll(x, s, 0), 0.0)


def _shift_up(x, s, row):
    n = x.shape[0]
    return jnp.where(row < n - s, pltpu.roll(x, n - s, 0), 0.0)


def _seq_call(kern, name, bl, seq, width, ct, ins, outs, acc_outs=()):
    def spec(off, rows):
        if rows is None:
            return pl.BlockSpec((seq, ct), functools.partial(lambda j, b, o: (b, o + j), o=off // ct))
        return pl.BlockSpec((rows, ct), lambda j, b: (0, j))

    in_specs = [spec(off, rows) for _, off, rows in ins]
    out_specs = [spec(0, None) for _ in outs] + [spec(0, rows) for _, rows in acc_outs]
    out_shape = [jax.ShapeDtypeStruct((bl * seq, width), d) for d in outs] + [jax.ShapeDtypeStruct((rows, width), F32) for _, rows in acc_outs]
    res = pl.pallas_call(
        kern, name=name, grid=(width // ct, bl), in_specs=in_specs, out_specs=out_specs, out_shape=out_shape,
        compiler_params=_cparams(("parallel", "arbitrary")),
    )(*[a for a, _, _ in ins])
    return tuple(res)


def _acc(ref, val):
    first = pl.program_id(1) == 0

    @pl.when(first)
    def _():
        ref[...] = val

    @pl.when(jnp.logical_not(first))
    def _():
        ref[...] += val


def _lerp_fwd(p, off, mu, bl, seq, width, ct):
    def kern(p_ref, mu_ref, o_ref):
        x = p_ref[...]
        row = lax.broadcasted_iota(jnp.int32, x.shape, 0)
        o_ref[...] = x + (_shift_down(x, 1, row) - x) * mu_ref[...]

    return _seq_call(kern, "lerp_fwd", bl, seq, width, ct, [(p, off, None), (mu, 0, 1)], [F32])[0]


def _lerp_bwd(p, off, mu, mu_off, dps_parts, name, bl, seq, width, ct, out_dtype):
    nd = len(dps_parts)

    def kern(*refs):
        p_ref, mu_ref = refs[:2]
        dp_ref, dmu_ref = refs[2 + nd:]
        x = p_ref[...]
        d = refs[2][...].astype(F32)
        for r in refs[3:2 + nd]:
            d = d + r[...].astype(F32)
        m = mu_ref[...]
        row = lax.broadcasted_iota(jnp.int32, x.shape, 0)
        dp_ref[...] = (d * (1.0 - m) + _shift_up(d * m, 1, row)).astype(dp_ref.dtype)
        _acc(dmu_ref, jnp.sum(d * (_shift_down(x, 1, row) - x), axis=0, keepdims=True))

    ins = [(p, off, None), (mu[:, mu_off:mu_off + width], 0, 1)] + [(a, 0, None) for a in dps_parts]
    return _seq_call(kern, name, bl, seq, width, ct, ins, [out_dtype], [(None, 1)])


def _conv_fwd(p, off, cw, cb, bl, seq, width, ct):
    nw = cw.shape[0]

    def kern(x_ref, w_ref, b_ref, o_ref):
        x = x_ref[...]
        row = lax.broadcasted_iota(jnp.int32, x.shape, 0)
        acc = b_ref[...] + x * w_ref[pl.ds(nw - 1, 1), :]
        for s in range(1, nw):
            acc = acc + _shift_down(x, s, row) * w_ref[pl.ds(nw - 1 - s, 1), :]
        o_ref[...] = acc

    return _seq_call(kern, "conv_fwd", bl, seq, width, ct, [(p, off, None), (cw, 0, nw), (cb, 0, 1)], [F32])[0]


def _conv_bwd(p, off, cw, dxc, bl, seq, width, ct, out_dtype):
    nw = cw.shape[0]

    def kern(x_ref, w_ref, d_ref, dx_ref, dw_ref, db_ref):
        x = x_ref[...]
        d = d_ref[...]
        row = lax.broadcasted_iota(jnp.int32, x.shape, 0)
        wrow = lax.broadcasted_iota(jnp.int32, dw_ref.shape, 0)
        dx = d * w_ref[pl.ds(nw - 1, 1), :]
        dw = jnp.where(wrow == nw - 1, jnp.sum(d * x, axis=0, keepdims=True), 0.0)
        for s in range(1, nw):
            dx = dx + _shift_up(d, s, row) * w_ref[pl.ds(nw - 1 - s, 1), :]
            dw = jnp.where(wrow == nw - 1 - s, jnp.sum(d * _shift_down(x, s, row), axis=0, keepdims=True), dw)
        dx_ref[...] = dx.astype(dx_ref.dtype)
        _acc(dw_ref, dw)
        _acc(db_ref, jnp.sum(d, axis=0, keepdims=True))

    return _seq_call(kern, "conv_bwd", bl, seq, width, ct, [(p, off, None), (cw, 0, nw), (dxc, 0, None)], [out_dtype], [(None, nw), (None, 1)])


def _lru_scan_fwd(a, bx, bl, seq, width, ct):
    def kern(a_ref, b_ref, h_ref):
        av = a_ref[...]
        bv = b_ref[...]
        row = lax.broadcasted_iota(jnp.int32, av.shape, 0)
        d = 1
        while d < seq:
            a_sh = jnp.where(row >= d, pltpu.roll(av, d, 0), 1.0)
            b_sh = jnp.where(row >= d, pltpu.roll(bv, d, 0), 0.0)
            bv = av * b_sh + bv
            av = av * a_sh
            d *= 2
        h_ref[...] = bv

    return _seq_call(kern, "lru_scan_fwd", bl, seq, width, ct, [(a, 0, None), (bx, 0, None)], [F32])[0]


def _lru_scan_bwd(a, h, dh, bl, seq, width, ct):
    def kern(a_ref, h_ref, d_ref, da_ref, db_ref):
        row = lax.broadcasted_iota(jnp.int32, a_ref.shape, 0)
        al = _shift_up(a_ref[...], 1, row)
        g = d_ref[...]
        d = 1
        while d < seq:
            keep = row < seq - d
            al_sh = jnp.where(keep, pltpu.roll(al, seq - d, 0), 1.0)
            g_sh = jnp.where(keep, pltpu.roll(g, seq - d, 0), 0.0)
            g = al * g_sh + g
            al = al * al_sh
            d *= 2
        db_ref[...] = g
        da_ref[...] = g * _shift_down(h_ref[...], 1, row)

    return _seq_call(kern, "lru_scan_bwd", bl, seq, width, ct, [(a, 0, None), (h, 0, None), (dh, 0, None)], [F32, F32])


_FORMS = {"nn": ((1,), (0,)), "nt": ((1,), (1,)), "tn": ((0,), (0,))}
_FORM_GRADS = {"nn": (("nt", "g", "b"), ("tn", "a", "g")),
               "nt": (("nn", "g", "b"), ("tn", "g", "a")),
               "tn": (("nt", "b", "g"), ("nn", "a", "g"))}


def _split_bf16(x):
    hi = x.astype(BF16)
    return hi, (x - hi.astype(F32)).astype(BF16)


def _pdot_raw(a, b, form, passes):
    dims = _FORMS[form]
    if passes == 1:
        return _dot(a.astype(BF16), b.astype(BF16), dims)
    ah, al = _split_bf16(a)
    bh, bl = _split_bf16(b)
    return _dot(ah, bh, dims) + (_dot(ah, bl, dims) + _dot(al, bh, dims))


@functools.partial(jax.custom_vjp, nondiff_argnums=(2, 3))
def _pdot(a, b, form, passes):
    return _pdot_raw(a, b, form, passes)


def _pdot_fwd(a, b, form, passes):
    return _pdot_raw(a, b, form, passes), (a, b)


def _pdot_bwd(form, passes, res, g):
    vals = {"a": res[0], "b": res[1], "g": g}
    (fa, xa, ya), (fb, xb, yb) = _FORM_GRADS[form]
    return _pdot_raw(vals[xa], vals[ya], fa, passes), _pdot_raw(vals[xb], vals[yb], fb, passes)


_pdot.defvjp(_pdot_fwd, _pdot_bwd)


def _neumann_raw(a_list, n_levels, passes):
    eye = (lax.broadcasted_iota(jnp.int32, a_list[0].shape, 0) == lax.broadcasted_iota(jnp.int32, a_list[0].shape, 1)).astype(F32)
    pw = list(a_list)
    x = [eye + a for a in a_list]
    for _ in range(n_levels):
        pw = [_pdot_raw(p, p, "nn", passes) for p in pw]
        x = [xi + _pdot_raw(xi, p, "nn", passes) for xi, p in zip(x, pw)]
    return x


@functools.partial(jax.custom_vjp, nondiff_argnums=(1, 2))
def _neumann_inverse(a_list, n_levels, passes):
    return _neumann_raw(a_list, n_levels, passes)


def _neumann_fwd(a_list, n_levels, passes):
    x = _neumann_raw(a_list, n_levels, passes)
    return x, x


def _neumann_bwd(n_levels, passes, x, ct):
    return ([_pdot_raw(xi, _pdot_raw(c, xi, "nt", passes), "tn", passes) for xi, c in zip(x, ct)],)


_neumann_inverse.defvjp(_neumann_fwd, _neumann_bwd)


def _scan_block(states, units, p_main=1, p_inv=1):
    C = units[0][0][0].shape[0]
    C2 = 2 * C
    ri = lax.broadcasted_iota(jnp.int32, (C, C), 0)
    ci = lax.broadcasted_iota(jnp.int32, (C, C), 1)
    tri = (ri >= ci).astype(F32)
    i2 = lax.broadcasted_iota(jnp.int32, (C2, C2), 0)
    j2 = lax.broadcasted_iota(jnp.int32, (C2, C2), 1)
    same = (i2 // C) == (j2 // C)
    strict = jnp.logical_and(same, (i2 % C) > (j2 % C))
    incl = jnp.logical_and(same, (i2 % C) >= (j2 % C))
    lane = lax.broadcasted_iota(jnp.int32, (1, LANE), 1)
    m0, m1 = (lane < HEAD).astype(F32), (lane >= HEAD).astype(F32)
    stack = lambda z: jnp.concatenate([z * m0, z * m1], axis=0)
    ids = [(i, g) for g in range(len(units[0])) for i in range(len(units))]

    pre = {}
    for i, g in ids:
        r, lw, k, v, a, b = units[i][g]
        cs = _nn(tri, lw, HI)
        p_incl = jnp.exp(cs)
        p_rec = jnp.exp(-cs)
        xr = jnp.concatenate([stack(a * jnp.exp(cs - lw)), stack(r * p_incl)], axis=0)
        bk = jnp.concatenate([stack(b * p_rec), stack(k * p_rec)], axis=0)
        pre[i, g] = (xr, bk, stack(v), jnp.exp(jnp.sum(lw, axis=0, keepdims=True)))
    gm = {u: _pdot(pre[u][0], pre[u][1], "nt", p_main) for u in ids}
    a_ak = {u: jnp.where(strict, gm[u][:C2, C2:], 0.0) for u in ids}
    r_bk = {u: jnp.concatenate([jnp.where(incl, gm[u][C2:, :C2], 0.0), jnp.where(incl, gm[u][C2:, C2:], 0.0)], axis=1) for u in ids}
    a_ab = [jnp.where(strict, gm[u][:C2, :C2], 0.0) for u in ids]
    x = dict(zip(ids, _neumann_inverse(a_ab, int(math.log2(C)) - 1, p_inv)))
    akv = {u: _pdot(a_ak[u], pre[u][2], "nn", p_main) for u in ids}

    states = list(states)
    pairs = range(len(units))
    ys = [[None] * len(units[0]) for _ in units]
    for g in range(len(units[0])):
        xs = [_pdot(pre[i, g][0], states[i], "nt", p_main) for i in pairs]
        us = [_pdot(x[i, g], xs[i][:C2] + akv[i, g], "nn", p_inv) for i in pairs]
        uv = [jnp.concatenate([us[i], pre[i, g][2]], axis=0) for i in pairs]
        y2 = [xs[i][C2:] + _pdot(r_bk[i, g], uv[i], "nn", p_main) for i in pairs]
        for i in pairs:
            ys[i][g] = y2[i][:C] + y2[i][C:]
        states = [(states[i] + _pdot(uv[i], pre[i, g][1], "tn", p_main)) * pre[i, g][3] for i in pairs]
    return ys, states


def _scan_dims(seq, rw):
    G = _pick(seq // SCAN_CHUNK, (SCAN_GROUP, 2, 1))
    NP = _pick(rw // LANE, (SCAN_PAIRS, 4, 2, 1))
    C = SCAN_CHUNK * G
    return SCAN_CHUNK, G, NP, C, seq // C, rw // (NP * LANE)


def _rwkv_scan_fwd(r, lw, k2, v, na, bb, p, bl, seq, rw, carry=None):
    cs, G, NP, C, nc, nhg = _scan_dims(seq, rw)
    nt = len(carry.tensors) if carry else 0

    def kern(*refs):
        in_refs = refs[:6]
        y_ref, st_ref = refs[6 + 2 * nt:8 + 2 * nt]
        s_scr = refs[8 + 3 * nt]
        if carry:
            step = (pl.program_id(0) * nhg + pl.program_id(1)) * nc + pl.program_id(2)
            carry.hook(step, bl * nhg * nc - 1, refs[6:6 + nt], refs[8 + 2 * nt:8 + 3 * nt], refs[9 + 3 * nt:])

        @pl.when(pl.program_id(2) == 0)
        def _():
            s_scr[...] = jnp.zeros_like(s_scr)

        st_ref[...] = s_scr[...]
        units = [[tuple(ref[pl.ds(g * cs, cs), pl.ds(i * LANE, LANE)] for ref in in_refs) for g in range(G)] for i in range(NP)]
        ys, s_new = _scan_block([s_scr[i] for i in range(NP)], units)
        for i in range(NP):
            s_scr[i] = s_new[i]
            for g in range(G):
                y_ref[pl.ds(g * cs, cs), pl.ds(i * LANE, LANE)] = ys[i][g]

    def tok(off):
        return pl.BlockSpec((C, NP * LANE), functools.partial(lambda b, h, c, o: (b * nc + c, o + h), o=off // (NP * LANE)))

    in_specs = [tok(0), tok(0), tok(0), tok(2 * rw), tok(0), tok(0)]
    out_specs = [tok(0), pl.BlockSpec((NP, LANE, LANE), lambda b, h, c: ((b * nhg + h) * nc + c, 0, 0))]
    out_shape = [jax.ShapeDtypeStruct((bl * seq, rw), F32), jax.ShapeDtypeStruct((bl * nhg * nc * NP, LANE, LANE), F32)]
    scratch = [pltpu.VMEM((NP, LANE, LANE), F32)]
    if not carry:
        y, st = pl.pallas_call(
            kern, name="rwkv_scan_fwd", grid=(bl, nhg, nc), in_specs=in_specs, out_specs=out_specs, out_shape=out_shape,
            scratch_shapes=scratch, compiler_params=_cparams(("parallel", "parallel", "arbitrary")),
        )(p, lw, k2, p, na, bb)
        return y, st
    res = pl.pallas_call(
        kern, name="rwkv_scan_fwd", grid=(bl, nhg, nc), in_specs=in_specs + carry.in_specs, out_specs=out_specs + carry.out_specs,
        out_shape=out_shape + carry.out_shape, scratch_shapes=scratch + carry.scratch, input_output_aliases=carry.aliases(6, 2),
        compiler_params=_cparams(("arbitrary", "arbitrary", "arbitrary")),
    )(p, lw, k2, p, na, bb, *carry.args)
    return res[0], res[1], list(res[2:])


def _rwkv_scan_bwd(lw, k2, na, bb, p, st, dy, bl, seq, rw, carry=None):
    cs, G, NP, C, nc, nhg = _scan_dims(seq, rw)
    nt = len(carry.tensors) if carry else 0

    def kern(*refs):
        in_refs = refs[:6]
        st_ref, dy_ref = refs[6:8]
        out_refs = refs[8 + 2 * nt:14 + 2 * nt]
        ds_scr = refs[14 + 3 * nt]
        if carry:
            step = (pl.program_id(0) * nhg + pl.program_id(1)) * nc + pl.program_id(2)
            carry.hook(step, bl * nhg * nc - 1, refs[8:8 + nt], refs[14 + 2 * nt:14 + 3 * nt], refs[15 + 3 * nt:])

        @pl.when(pl.program_id(2) == 0)
        def _():
            ds_scr[...] = jnp.zeros_like(ds_scr)

        win = lambda ref, i, g: ref[pl.ds(g * cs, cs), pl.ds(i * LANE, LANE)]
        units = [[tuple(win(ref, i, g) for ref in in_refs) for g in range(G)] for i in range(NP)]
        _, vjp = jax.vjp(_scan_block, [st_ref[i] for i in range(NP)], units)
        dys = [[win(dy_ref, i, g) for g in range(G)] for i in range(NP)]
        ds, dunits = vjp((dys, [ds_scr[i] for i in range(NP)]))
        for i in range(NP):
            ds_scr[i] = ds[i]
            for g in range(G):
                for ref, d in zip(out_refs, dunits[i][g]):
                    ref[pl.ds(g * cs, cs), pl.ds(i * LANE, LANE)] = d

    def tok(off):
        return pl.BlockSpec((C, NP * LANE), functools.partial(lambda b, h, c, o: (b * nc + (nc - 1 - c), o + h), o=off // (NP * LANE)))

    st_spec = pl.BlockSpec((NP, LANE, LANE), lambda b, h, c: ((b * nhg + h) * nc + (nc - 1 - c), 0, 0))
    in_specs = [tok(0), tok(0), tok(0), tok(2 * rw), tok(0), tok(0), st_spec, tok(0)]
    out_shape = [jax.ShapeDtypeStruct((bl * seq, rw), F32)] * 6
    scratch = [pltpu.VMEM((NP, LANE, LANE), F32)]
    if not carry:
        return pl.pallas_call(
            kern, name="rwkv_scan_bwd", grid=(bl, nhg, nc), in_specs=in_specs, out_specs=[tok(0)] * 6, out_shape=out_shape,
            scratch_shapes=scratch, compiler_params=_cparams(("parallel", "parallel", "arbitrary")),
        )(p, lw, k2, p, na, bb, st, dy)
    res = pl.pallas_call(
        kern, name="rwkv_scan_bwd", grid=(bl, nhg, nc), in_specs=in_specs + carry.in_specs, out_specs=[tok(0)] * 6 + carry.out_specs,
        out_shape=out_shape + carry.out_shape, scratch_shapes=scratch + carry.scratch, input_output_aliases=carry.aliases(8, 6),
        compiler_params=_cparams(("arbitrary", "arbitrary", "arbitrary")),
    )(p, lw, k2, p, na, bb, st, dy, *carry.args)
    return res[:6], list(res[6:])


def _loss_head(h2, g_final, target, tile):
    n, d = h2.shape
    nt = n // tile

    def kern(h_ref, g_ref, t_ref, dh_ref, dg_ref, l_ref, dhb_ref):
        def f(h, g):
            y = h * lax.rsqrt(jnp.mean(h * h, axis=-1, keepdims=True) + NORM_EPS) * g
            e = y - t_ref[...]
            return 0.5 * jnp.sum(jnp.mean(e * e, axis=-1, keepdims=True))

        loss, (dh, dg) = jax.value_and_grad(f, argnums=(0, 1))(h_ref[...], g_ref[...])
        dh_ref[...] = dh
        dhb_ref[...] = dh.astype(dhb_ref.dtype)
        first = pl.program_id(0) == 0

        @pl.when(first)
        def _():
            dg_ref[...] = dg
            l_ref[...] = jnp.zeros_like(l_ref) + loss

        @pl.when(jnp.logical_not(first))
        def _():
            dg_ref[...] += dg
            l_ref[...] += loss

    row = pl.BlockSpec((tile, d), lambda i: (i, 0))
    vec = pl.BlockSpec((1, d), lambda i: (0, 0))
    return pl.pallas_call(
        kern, name="loss_head", grid=(nt,), in_specs=[row, vec, row],
        out_specs=[row, vec, pl.BlockSpec((1, LANE), lambda i: (0, 0)), row],
        out_shape=[jax.ShapeDtypeStruct((n, d), F32), jax.ShapeDtypeStruct((1, d), F32), jax.ShapeDtypeStruct((1, LANE), F32),
                   jax.ShapeDtypeStruct((n, d), BF16)],
        compiler_params=_cparams(("arbitrary",)),
    )(h2, g_final, target)


def _adamw(parts, w, m, v, name, carry=None):
    n_parts, Rp, Cp = parts.shape
    R, Cc = w.shape
    assert Rp >= R and Cp >= Cc
    tr = _pick(R, tuple(t for t in (1024, 512, 256, 128, 64, 32, 16) if t * Cp <= 128 * 1024) + (8,))
    part = (lambda ref, s: ref[s]) if Cp == Cc else (lambda ref, s: ref[s, :, pl.ds(0, Cc)])
    c1, c2 = 1.0 - ADAM_B1, 1.0 - ADAM_B2
    bc1, bc2 = 1.0 - ADAM_B1 ** ADAM_STEP, 1.0 - ADAM_B2 ** ADAM_STEP

    nt = len(carry.tensors) if carry else 0

    def kern(*refs):
        p_ref, w_ref, m_ref, v_ref = refs[:4]
        g_ref, d_ref, nm_ref, nv_ref = refs[4 + 2 * nt:8 + 2 * nt]
        if carry:
            carry.hook(pl.program_id(0), R // tr - 1, refs[4:4 + nt], refs[8 + 2 * nt:8 + 3 * nt], refs[8 + 3 * nt:])
        g = part(p_ref, 0).astype(F32)
        for s in range(1, n_parts):
            g = g + part(p_ref, s).astype(F32)
        m2 = ADAM_B1 * m_ref[...] + c1 * g
        v2 = ADAM_B2 * v_ref[...] + c2 * (g * g)
        g_ref[...] = g
        nm_ref[...] = m2
        nv_ref[...] = v2
        d_ref[...] = -ADAM_LR * ((m2 / bc1) / (jnp.sqrt(v2 / bc2) + ADAM_EPS) + ADAM_WD * w_ref[...])

    blk = pl.BlockSpec((tr, Cc), lambda i: (i, 0))
    in_specs = [pl.BlockSpec((n_parts, tr, Cp), lambda i: (0, i, 0)), blk, blk, blk]
    out_shape = [jax.ShapeDtypeStruct((R, Cc), F32)] * 4
    if not carry:
        return pl.pallas_call(
            kern, name=name, grid=(R // tr,), in_specs=in_specs, out_specs=[blk] * 4, out_shape=out_shape,
            compiler_params=_cparams(("parallel",)),
        )(parts, w, m, v)
    res = pl.pallas_call(
        kern, name=name, grid=(R // tr,), in_specs=in_specs + carry.in_specs, out_specs=[blk] * 4 + carry.out_specs,
        out_shape=out_shape + carry.out_shape, scratch_shapes=carry.scratch, input_output_aliases=carry.aliases(4, 4),
        compiler_params=_cparams(("arbitrary",)),
    )(parts, w, m, v, *carry.args)
    return res[:4], list(res[4:])


def _exchange_now(carry, name):
    nt = len(carry.tensors)

    def body(*refs):
        carry.hook(0, 0, refs[:nt], refs[2 * nt:3 * nt], refs[3 * nt:])

    return list(pl.pallas_call(
        body, name=name, in_specs=carry.in_specs, out_specs=carry.out_specs, out_shape=carry.out_shape,
        scratch_shapes=carry.scratch, input_output_aliases=carry.aliases(0, 0),
    )(*carry.args))


def _carried(queue, capacity_us, *args, fn, **kw):
    carry = queue.take(capacity_us * CARRY_FILL)
    if carry is None:
        return fn(*args, **kw)
    res = fn(*args, carry=carry, **kw)
    queue.done(carry, res[-1])
    return res[0] if len(res) == 2 else res[:-1]


class _Queue:
    def __init__(self, gather, label):
        self.gather, self.label = gather, label
        self.tensors, self.fifo, self.n_flush = {}, [], 0

    def push(self, name, src, n_pieces, cost_us, cols=False):
        cw = None
        if cols:
            cw = src.shape[1] if self.gather else src.shape[1] // N_DEV
            assert cw % LANE == 0
            dst_shape = (src.shape[0], N_DEV * cw) if self.gather else (N_DEV, src.shape[0], cw)
        else:
            dst_shape = ((N_DEV,) + src.shape) if self.gather else src.shape
        n_rows = src.shape[0] if (self.gather or cols) else src.shape[1]
        rows = n_rows // n_pieces
        assert rows * n_pieces == n_rows and rows % 16 == 0, (name, src.shape)
        self.tensors[name] = [src, lax.empty(dst_shape, src.dtype), cw]
        self.fifo += [(name, p * rows, rows, cost_us / n_pieces) for p in range(n_pieces)]

    def take(self, capacity_us, count=None):
        picked = []
        while self.fifo and (len(picked) < count if count is not None else capacity_us >= 0.85 * self.fifo[0][3]):
            picked.append(self.fifo.pop(0))
            capacity_us -= picked[-1][3]
        if not picked:
            return None
        names = list(dict.fromkeys(n for n, _, _, _ in picked))
        cls = _GatherCarry if self.gather else _Carry
        carry = cls([tuple(self.tensors[n]) for n in names], [(names.index(n), r0, rows) for n, r0, rows, _ in picked])
        carry.names = names
        return carry

    def done(self, carry, dsts):
        for n, d in zip(carry.names, dsts):
            self.tensors[n][1] = d

    def flush(self, count=None):
        carry = self.take(float("inf"), count)
        if carry:
            self.done(carry, _exchange_now(carry, "%s_now_%d" % (self.label, self.n_flush)))
            self.n_flush += 1

    def result(self, name, r0=0, r1=None):
        late = [i for i, (n, p0, rows, _) in enumerate(self.fifo) if n == name and p0 < (r1 or p0 + rows) and p0 + rows > r0]
        if late:
            self.flush(late[-1] + 1)
        return self.tensors[name][1]


def _cols_from_shards(g):
    return jnp.transpose(g, (1, 0, 2)).reshape(g.shape[1], N_DEV * g.shape[2])


def _shards_from_cols(w):
    r, n = w.shape
    return jnp.transpose(w.reshape(r, N_DEV, n // N_DEV), (1, 0, 2))


def _pad_cols(w, to):
    return jnp.pad(w, ((0, 0), (0, to - w.shape[1])))


def _pack(arrs):
    flat = jnp.concatenate([a.reshape(-1) for a in arrs])
    n = _rup(flat.shape[0], 256 * LANE)
    return jnp.pad(flat, (0, n - flat.shape[0])).reshape(n // LANE, LANE)


def _unpack(mat, shapes):
    flat = mat.reshape(-1)
    out, o = [], 0
    for s in shapes:
        n = math.prod(s)
        out.append(flat[o:o + n].reshape(s))
        o += n
    return out


_SMALL = ["norm_mix_g", "mu_shift", "rwkv_w0", "rwkv_a0", "rwkv_k_k", "rwkv_k_a", "rwkv_r_k", "rwkv_ln_g", "rwkv_ln_b", "conv_b",
          "lru_wr", "lru_br", "lru_wi", "lru_bi", "lru_lambda", "lru_norm_g", "norm_ffn_g", "norm_final_g"]
_SMALL_SHARDED = ["rwkv_w2", "rwkv_a2", "rwkv_g2", "conv_w"]
_BIG = ["w_in", "w_out", "ffn_w_gate", "ffn_w_up", "ffn_w_down"]
_WEIGHTS = ['norm_mix_g', 'w_in', 'mu_shift', 'rwkv_w0', 'rwkv_w2', 'rwkv_a0', 'rwkv_a2', 'rwkv_g2', 'rwkv_k_k', 'rwkv_k_a', 'rwkv_r_k',
            'rwkv_ln_g', 'rwkv_ln_b', 'conv_w', 'conv_b', 'lru_wr', 'lru_br', 'lru_wi', 'lru_bi', 'lru_lambda', 'lru_norm_g', 'w_out',
            'norm_ffn_g', 'ffn_w_gate', 'ffn_w_up', 'ffn_w_down', 'norm_final_g']


def _step(W, M, V, x, loss_target):
    bl, seq, d = x.shape
    n = bl * seq
    rw = W["rwkv_w0"].shape[1]
    nh = W["rwkv_r_k"].shape[1]
    assert W["rwkv_r_k"].shape[2] == HEAD and nh * HEAD == rw and rw % LANE == 0
    dl, al, gl = W["rwkv_w2"].shape[1], W["rwkv_a2"].shape[1], W["rwkv_g2"].shape[1]
    dlp, alp, glp = _rup(dl, LANE), _rup(al, LANE), _rup(gl, LANE)
    lorap = dlp + alp + glp
    lw_ = W["conv_b"].shape[1]
    nblk, lbw = W["lru_wr"].shape[1], W["lru_wr"].shape[2]
    assert lbw == LANE and nblk * lbw == lw_
    o_xb, o_gate, o_rw = 0, lw_, 2 * lw_
    o_lora = 3 * rw
    rwp = o_lora + lorap
    inp = o_rw + rwp
    nsh_ff = W["ffn_w_gate"].shape[2]
    nshp = _rup(nsh_ff, LANE)
    dffp = N_DEV * nshp
    x2 = x.reshape(n, d)
    tgt2 = loss_target.reshape(n, d)

    gq = _Queue(True, "gather")
    kp = 2 if d % (2 * LANE) == 0 else 1
    gq.push("w_in", W["w_in"][0].astype(BF16), kp, 490)
    gq.push("small", _pack([W[k][0] for k in _SMALL_SHARDED]), 1, 10)
    gq.push("w_out", W["w_out"][0].astype(BF16), 2, 180)
    pad_ff = nshp - nsh_ff
    gq.push("ffn_w_gate", jnp.pad(W["ffn_w_gate"][0].astype(BF16), ((0, 0), (0, pad_ff))), 4, 490, cols=True)
    gq.push("ffn_w_up", jnp.pad(W["ffn_w_up"][0].astype(BF16), ((0, 0), (0, pad_ff))), 4, 490, cols=True)
    gq.push("ffn_w_down", jnp.pad(W["ffn_w_down"][0].astype(BF16), ((0, pad_ff), (0, 0))), 4, 490)
    gmm = functools.partial(_carried, gq, fn=_mm)
    gstage = functools.partial(_carried, gq, fn=_stage_fwd)

    o1 = 3 * rw

    def my_cols(g):
        w_l = _cols_from_shards(g)
        return jnp.concatenate([w_l[:, o1 + dl + al + gl:], w_l[:, :o1], _pad_cols(w_l[:, o1:o1 + dl], dlp),
                                _pad_cols(w_l[:, o1 + dl:o1 + dl + al], alp), _pad_cols(w_l[:, o1 + dl + al:o1 + dl + al + gl], glp)], axis=1)

    mu_l = W["mu_shift"]
    mu = jnp.concatenate([mu_l[:, :o1], _pad_cols(mu_l[:, o1:o1 + dl], dlp), _pad_cols(mu_l[:, o1 + dl:o1 + dl + al], alp),
                          _pad_cols(mu_l[:, o1 + dl + al:], glp)], axis=1)
    r_k = W["rwkv_r_k"].reshape(1, rw)

    tile = _pick(n, (256, 128, 64))
    tile_s = _pick(n, (128, 64))
    tile_f = _pick(n, (256, 128, 64))
    ct_seq = _pick(math.gcd(rwp, lw_), (256, 128))
    assert o_rw % ct_seq == 0 and o_gate % lw_ == 0
    ct_h = _pick(rw, (512, 256, 128))
    gi = lax.broadcasted_iota(jnp.int32, (ct_h, ct_h), 0) // HEAD
    gj = lax.broadcasted_iota(jnp.int32, (ct_h, ct_h), 1) // HEAD
    gsum = ((gi == gj).astype(BF16), (ct_h, ct_h), lambda j: (0, 0))
    full = lambda a: (a, a.shape, lambda j: (0,) * a.ndim)
    rowp = lambda a, ct: (a,) + _row(ct)

    u1, = gstage(210, _f_rmsnorm, "norm_mix_fwd", n, d, tile, d, [(x2, 0)], [full(W["norm_mix_g"])], [], [BF16])
    p, w_rows = None, []
    for i in range(kp):
        rows = slice(i * (d // kp), (i + 1) * (d // kp))
        g_in = gq.result("w_in", rows.start, rows.stop)
        w_rows.append(my_cols(g_in[:, rows, :]))
        p = gmm(190, u1[:, rows], w_rows[-1], name="mm_in_%d" % i, add=p)
    w_in = jnp.concatenate(w_rows, axis=0)
    g_small = gq.result("small")
    sm_shapes = [W[k][0].shape for k in _SMALL_SHARDED]
    sm = [_unpack(g_small[s], sm_shapes) for s in range(N_DEV)]
    w2, a2, g2, conv_w = [jnp.concatenate([sm[s][i] for s in range(N_DEV)], axis=1) for i in range(4)]
    w_lora = jnp.zeros((lorap, 3 * rw), F32)
    w_lora = w_lora.at[:dl, :rw].set(w2).at[dlp:dlp + al, rw:2 * rw].set(a2).at[dlp + alp:dlp + alp + gl, 2 * rw:].set(g2)
    w_lora = w_lora.astype(BF16)
    ps = _lerp_fwd(p, o_rw, mu, bl, seq, rwp, ct_seq)
    f_lora = functools.partial(_f_lora_act, widths=(dlp, alp))
    lact, = _stage_fwd(f_lora, "lora_act_fwd", n, lorap, tile, lorap, [(ps, o_lora)], [], [], [BF16])
    wag = _mm(lact, w_lora, name="mm_lora")
    pre_par = [rowp(W["rwkv_w0"], ct_h), rowp(W["rwkv_a0"], ct_h), rowp(W["rwkv_k_k"], ct_h), rowp(W["rwkv_k_a"], ct_h)]
    pre_acts = [(ps, rw), (wag, 0), (wag, rw)]
    lw, k2, na, bb = gstage(120, _f_rwkv_pre, "rwkv_pre_fwd", n, rw, tile_f, ct_h, pre_acts, pre_par, [gsum], [F32] * 4)
    ysc, st = _carried(gq, 230, None, lw, k2, None, na, bb, ps, bl, seq, rw, fn=_rwkv_scan_fwd)
    post_par = [rowp(W["rwkv_ln_g"], ct_h), rowp(W["rwkv_ln_b"], ct_h), rowp(r_k, ct_h)]
    post_acts = [(ysc, 0), (ps, 0), (k2, 0), (ps, 2 * rw), (wag, 2 * rw)]
    ya, = gstage(120, _f_rwkv_post, "rwkv_post_fwd", n, rw, tile_f, ct_h, post_acts, post_par, [gsum], [BF16])

    xc = _conv_fwd(p, o_xb, conv_w, W["conv_b"], bl, seq, lw_, ct_seq)
    f_gates = functools.partial(_f_lru_gates, seq=seq)
    blk3 = lambda a: (a[0], (1, LANE, LANE), lambda j: (j, 0, 0))
    gate_par = [blk3(W["lru_wr"]), rowp(W["lru_br"], LANE), blk3(W["lru_wi"]), rowp(W["lru_bi"], LANE), rowp(W["lru_lambda"], LANE)]
    tile_g = _pick(n, (1024, 512, 256, 128, 64))
    a_l, bx = gstage(160, f_gates, "lru_gates_fwd", n, lw_, tile_g, LANE, [(xc, 0)], gate_par, [], [F32, F32])
    ct_l = _pick(lw_, (256, 128))
    h_l = _lru_scan_fwd(a_l, bx, bl, seq, lw_, ct_l)
    lpost_par = [full(W["lru_norm_g"])]
    yb, = _stage_fwd(_f_lru_post, "lru_post_fwd", n, lw_, tile_s, lw_, [(h_l, 0), (p, o_gate)], lpost_par, [], [BF16])

    ycat = jnp.concatenate([ya, yb], axis=1)
    g_out = gq.result("w_out")
    w_out = g_out.reshape(N_DEV * g_out.shape[1], d)
    h1 = gmm(130, ycat, w_out, name="mm_out", add=x2)
    u2, = _stage_fwd(_f_rmsnorm, "norm_ffn_fwd", n, d, tile, d, [(h1, 0)], [full(W["norm_ffn_g"])], [], [BF16])
    w_gate = gq.result("ffn_w_gate")
    ff_gate = gmm(340, u2, w_gate, name="mm_gate", out_dtype=BF16)
    w_up = gq.result("ffn_w_up")
    ff_up = gmm(340, u2, w_up, name="mm_up", out_dtype=BF16)
    ct_f = _pick(dffp, (1024, 512, 256, 128))
    ff_acts = [(ff_gate, 0), (ff_up, 0)]
    act, = _stage_fwd(_f_swiglu, "swiglu_fwd", n, dffp, _pick(n, (512, 256, 128, 64)), ct_f, ff_acts, [], [], [BF16])
    gq.flush()
    w_down = gq.result("ffn_w_down").reshape(dffp, d)
    h2 = _mm(act, w_down, name="mm_down", add=h1)

    dh2, dg_final, lsum, dh2b = _loss_head(h2, W["norm_final_g"].reshape(1, d), tgt2, tile_s)
    loss = lax.psum(lsum[0, 0], ("x", "y", "c"))
    queue = _Queue(False, "exchange")

    cmm = functools.partial(_carried, queue, fn=_mm)
    cstage = functools.partial(_carried, queue, fn=_stage_bwd)
    dact = _mm(dh2b, w_down, name="mm_dact", tb=True, out_dtype=BF16)
    dw_down = _mm(act, dh2b, name="mm_dw_down", ta=True, out_dtype=BF16)
    queue.push("ffn_w_down", dw_down.reshape(N_DEV, nshp, d), 8, 1000)
    (dgate, dup), _ = _stage_bwd(_f_swiglu, "swiglu_bwd", n, dffp, _pick(n, (512, 256, 128, 64)), ct_f, ff_acts, [], [], [(dact, 0)],
                                 [BF16, BF16])
    du2 = cmm(400, dgate, w_gate, name="mm_du2_gate", tb=True)
    dw_gate = cmm(350, u2, dgate, name="mm_dw_gate", ta=True, out_dtype=BF16)
    queue.push("ffn_w_gate", dw_gate, 8, 1000, cols=True)
    du2 = cmm(400, dup, w_up, name="mm_du2_up", tb=True, add=du2)
    dw_up = cmm(350, u2, dup, name="mm_dw_up", ta=True, out_dtype=BF16)
    queue.push("ffn_w_up", dw_up, 8, 1000, cols=True)
    (dh1,), (dg_ffn,) = cstage(130, _f_rmsnorm, "norm_ffn_bwd", n, d, tile_s, d, [(h1, 0)], [full(W["norm_ffn_g"])], [], [(du2, 0)], [F32],
                               extra_add=(dh2, 0))
    dh1b = dh1.astype(BF16)
    dycat = cmm(135, dh1b, w_out, name="mm_dycat", tb=True)
    dw_out = cmm(170, ycat, dh1b, name="mm_dw_out", ta=True, out_dtype=BF16)
    queue.push("w_out", dw_out.reshape(N_DEV, -1, d), 4, 370)

    (dysc, dr_p, dk2_p, dv_p, dg_g), (dln_g, dln_b, dr_k) = cstage(
        195, _f_rwkv_post, "rwkv_post_bwd", n, rw, tile_f, ct_h, post_acts, post_par, [gsum], [(dycat, 0)], [F32] * 5)
    dr_s, dlw, dk2_s, dv_s, dna, dbb = _carried(queue, 650, lw, k2, na, bb, ps, st, dysc, bl, seq, rw, fn=_rwkv_scan_bwd)
    dk2 = dk2_p + dk2_s
    (dk, dwlin, dalin), (dw0, da0, dk_k, dk_a) = cstage(
        180, _f_rwkv_pre, "rwkv_pre_bwd", n, rw, tile_f, ct_h, pre_acts, pre_par, [gsum], [(dlw, 0), (dk2, 0), (dna, 0), (dbb, 0)], [F32] * 3)
    dwag = jnp.concatenate([dwlin, dalin, dg_g], axis=1).astype(BF16)
    dlact = cmm(75, dwag, w_lora, name="mm_dlact", tb=True)
    dw_lora = cmm(50, lact, dwag, name="mm_dw_lora", ta=True)
    (dps_lora,), _ = _stage_bwd(f_lora, "lora_act_bwd", n, lorap, tile, lorap, [(ps, o_lora)], [], [], [(dlact, 0)], [F32])
    dp_segs, dmu_segs = [], []
    for nm, o, wdt, parts in (("r", 0, rw, [dr_p, dr_s]), ("k", rw, rw, [dk]), ("v", 2 * rw, rw, [dv_p, dv_s]), ("lora", o_lora, lorap, [dps_lora])):
        dp_s, dmu_s = _lerp_bwd(p, o_rw + o, mu, o, parts, "lerp_bwd_" + nm, bl, seq, wdt, ct_seq, BF16)
        dp_segs.append(dp_s)
        dmu_segs.append(dmu_s)
    dmu = jnp.concatenate(dmu_segs, axis=1)

    (dh_l, dgate_l), (dlru_norm_g,) = cstage(80, _f_lru_post, "lru_post_bwd", n, lw_, tile_s, lw_, [(h_l, 0), (p, o_gate)], lpost_par, [],
                                                 [(dycat, rw)], [F32, BF16])
    da_l, dbx = _lru_scan_bwd(a_l, h_l, dh_l, bl, seq, lw_, ct_l)
    (dxc,), (dwr, dbr, dwi, dbi, dlam) = cstage(240, f_gates, "lru_gates_bwd", n, lw_, tile_g, LANE, [(xc, 0)], gate_par, [],
                                                [(da_l, 0), (dbx, 0)], [F32])
    dxb, dconv_w, dconv_b = _conv_bwd(p, o_xb, conv_w, dxc, bl, seq, lw_, ct_seq, BF16)
    sh_full = [dw_lora[:dl, :rw], dw_lora[dlp:dlp + al, rw:2 * rw], dw_lora[dlp + alp:dlp + alp + gl, 2 * rw:], dconv_w]
    assert rw == lw_
    rows_sh = sum(a.shape[0] for a in sh_full)
    pad_sh = _rup(rows_sh, 16) - rows_sh
    queue.push("small_sharded", jnp.pad(jnp.concatenate([_shards_from_cols(a) for a in sh_full], axis=1), ((0, 0), (0, pad_sh), (0, 0))), 1, 40)
    stack_sh = lambda D: jnp.pad(jnp.concatenate([D[k][0] for k in _SMALL_SHARDED], axis=0), ((0, pad_sh), (0, 0)))

    dp = jnp.concatenate([dxb, dgate_l] + dp_segs, axis=1)
    dw_in = cmm(340, u1, dp, name="mm_dw_in", ta=True, out_dtype=BF16)
    ol = o_rw + o_lora
    dw_in_l = jnp.concatenate([dw_in[:, o_rw:ol], dw_in[:, ol:ol + dl], dw_in[:, ol + dlp:ol + dlp + al],
                               dw_in[:, ol + dlp + alp:ol + dlp + alp + gl], dw_in[:, :o_rw]], axis=1)
    queue.push("w_in", _shards_from_cols(dw_in_l), 8, 970)
    du1 = cmm(380, dp, w_in, name="mm_du1", tb=True)
    (grad_x,), (dg_mix,) = cstage(90, _f_rmsnorm, "norm_mix_bwd", n, d, tile_s, d, [(x2, 0)], [full(W["norm_mix_g"])], [], [(du1, 0)], [F32],
                                  extra_add=(dh1, 0))
    dmu_l = jnp.concatenate([dmu[:, :o1], dmu[:, o_lora:o_lora + dl], dmu[:, o_lora + dlp:o_lora + dlp + al],
                             dmu[:, o_lora + dlp + alp:o_lora + dlp + alp + gl]], axis=1)
    small_g = {"norm_mix_g": dg_mix, "mu_shift": dmu_l, "rwkv_w0": dw0, "rwkv_a0": da0, "rwkv_k_k": dk_k, "rwkv_k_a": dk_a,
               "rwkv_r_k": dr_k.reshape(W["rwkv_r_k"].shape), "rwkv_ln_g": dln_g, "rwkv_ln_b": dln_b, "conv_b": dconv_b,
               "lru_wr": dwr[None], "lru_br": dbr, "lru_wi": dwi[None], "lru_bi": dbi, "lru_lambda": dlam, "lru_norm_g": dlru_norm_g,
               "norm_ffn_g": dg_ffn, "norm_final_g": dg_final.reshape(W["norm_final_g"].shape)}
    gq.push("small_grads", _pack([small_g[k] for k in _SMALL]), 1, 50)
    out = {}
    for k in ["ffn_w_down", "ffn_w_gate", "ffn_w_up", "w_out", "w_in"]:
        if k == "w_in":
            queue.flush()
        res = _carried(gq if k == "w_in" else queue, 105, queue.result(k), W[k][0], M[k][0], V[k][0], "adamw_" + k, fn=_adamw)
        out[k] = [o[None] for o in res]
    pk = lambda D: _pack([D[k] for k in _SMALL])
    res = _adamw(gq.result("small_grads"), pk(W), pk(M), pk(V), "adamw_small")
    shapes = [W[k].shape for k in _SMALL]
    for i, r in enumerate(res):
        for k, a in zip(_SMALL, _unpack(r, shapes)):
            out.setdefault(k, [None] * 4)[i] = a
    res = _adamw(queue.result("small_sharded"), stack_sh(W), stack_sh(M), stack_sh(V), "adamw_small_sharded")
    for i, r in enumerate(res):
        o = 0
        for k in _SMALL_SHARDED:
            rows = W[k].shape[1]
            out.setdefault(k, [None] * 4)[i] = r[o:o + rows][None]
            o += rows
    return loss, grad_x.reshape(x.shape), out


def kernel(x, norm_mix_g, w_in, mu_shift, rwkv_w0, rwkv_w2, rwkv_a0, rwkv_a2, rwkv_g2, rwkv_k_k, rwkv_k_a, rwkv_r_k, rwkv_ln_g, rwkv_ln_b, conv_w, conv_b, lru_wr, lru_br, lru_wi, lru_bi, lru_lambda, lru_norm_g, w_out, norm_ffn_g, ffn_w_gate, ffn_w_up, ffn_w_down, norm_final_g, loss_target, m_norm_mix_g, m_w_in, m_mu_shift, m_rwkv_w0, m_rwkv_w2, m_rwkv_a0, m_rwkv_a2, m_rwkv_g2, m_rwkv_k_k, m_rwkv_k_a, m_rwkv_r_k, m_rwkv_ln_g, m_rwkv_ln_b, m_conv_w, m_conv_b, m_lru_wr, m_lru_br, m_lru_wi, m_lru_bi, m_lru_lambda, m_lru_norm_g, m_w_out, m_norm_ffn_g, m_ffn_w_gate, m_ffn_w_up, m_ffn_w_down, m_norm_final_g, v_norm_mix_g, v_w_in, v_mu_shift, v_rwkv_w0, v_rwkv_w2, v_rwkv_a0, v_rwkv_a2, v_rwkv_g2, v_rwkv_k_k, v_rwkv_k_a, v_rwkv_r_k, v_rwkv_ln_g, v_rwkv_ln_b, v_conv_w, v_conv_b, v_lru_wr, v_lru_br, v_lru_wi, v_lru_bi, v_lru_lambda, v_lru_norm_g, v_w_out, v_norm_ffn_g, v_ffn_w_gate, v_ffn_w_up, v_ffn_w_down, v_norm_final_g):
    a = locals()
    W = {k: a[k] for k in _WEIGHTS}
    M = {k: a["m_" + k] for k in _WEIGHTS}
    V = {k: a["v_" + k] for k in _WEIGHTS}
    loss, grad_x, out = _step(W, M, V, x, loss_target)
    res = [loss, grad_x]
    for i in range(4):
        res += [out[k][i].reshape(W[k].shape) for k in _WEIGHTS]
    return tuple(res)
```

```python
import functools
import math

import jax
import jax.numpy as jnp
from jax import lax
from jax.experimental import pallas as pl
from jax.experimental.pallas import tpu as pltpu

F32 = jnp.float32
BF16 = jnp.bfloat16
HI = lax.Precision.HIGHEST
MESH = pl.DeviceIdType.MESH

N_DEV = 8
LANE = 128
HEAD = 64
MM_MAX_TK = 5632
CARRY_FILL = 1.2
SCAN_CHUNK = 64
SCAN_GROUP = 2
SCAN_PAIRS = 8
VMEM_LIMIT = 56 * 1024 * 1024

NORM_EPS = 1e-6
GN_EPS = 64e-5
LRU_C = 8.0
ADAM_LR, ADAM_B1, ADAM_B2, ADAM_EPS, ADAM_WD, ADAM_STEP = 0.001, 0.9, 0.999, 1e-08, 0.01, 10


def _pick(n, cands):
    for c in cands:
        if n % c == 0:
            return c
    return n


def _rup(n, m):
    return (n + m - 1) // m * m


def _cparams(dims):
    return pltpu.CompilerParams(dimension_semantics=dims, vmem_limit_bytes=VMEM_LIMIT)


def _sigmoid(x):
    return 1.0 / (1.0 + jnp.exp(-x))


def _softplus(z):
    return jnp.maximum(z, 0.0) + jnp.log(1.0 + jnp.exp(-jnp.abs(z)))


def _neg_expm1(x):
    series = -(x * (1.0 + 0.5 * x * (1.0 + (x / 3.0) * (1.0 + 0.25 * x))))
    return jnp.where(jnp.abs(x) < 0.03, series, 1.0 - jnp.exp(x))


def _gelu(x):
    return 0.5 * x * (1.0 + jnp.tanh(0.7978845608028654 * (x + 0.044715 * (x * x * x))))


def _dot(a, b, dims, precision=None):
    return lax.dot_general(a, b, (dims, ((), ())), precision=precision, preferred_element_type=F32)


def _nn(a, b, precision=None):
    return _dot(a, b, ((1,), (0,)), precision)


def _coords():
    return lax.axis_index("x"), lax.axis_index("y"), lax.axis_index("c")


class _Carry:
    def __init__(self, tensors, items):
        self.tensors, self.items = tensors, items
        nt, ni = len(tensors), len(items)
        any_spec = pl.BlockSpec(memory_space=pl.ANY)
        self.args = [t[0] for t in tensors] + [t[1] for t in tensors]
        self.in_specs = [any_spec] * (2 * nt)
        self.out_specs = [any_spec] * nt
        self.out_shape = [jax.ShapeDtypeStruct(t[1].shape, t[1].dtype) for t in tensors]
        self.scratch = [pltpu.SemaphoreType.DMA((ni, N_DEV - 1)), pltpu.SemaphoreType.DMA((ni, N_DEV - 1)), pltpu.SemaphoreType.DMA((ni,))]

    def aliases(self, first_in, first_out):
        nt = len(self.tensors)
        return {first_in + nt + t: first_out + t for t in range(nt)}

    def _slot(self, ref, t, idx, win):
        cw = self.tensors[t][2]
        return ref.at[idx, win] if cw is None else ref.at[win, pl.ds(pl.multiple_of(idx * cw, LANE), cw)]

    def _copies(self, src_refs, dst_refs, sems):
        send_sems, recv_sems, local_sems = sems
        x, y, c = _coords()
        my = 4 * x + 2 * y + c
        out = []
        for n, (t, r0, rows) in enumerate(self.items):
            win = pl.ds(r0, rows)
            out.append(pltpu.make_async_copy(self._slot(src_refs[t], t, my, win), dst_refs[t].at[my, win], local_sems.at[n]))
            for k in range(1, N_DEV):
                px, py, pc = x ^ ((k >> 2) & 1), y ^ ((k >> 1) & 1), c ^ (k & 1)
                out.append(pltpu.make_async_remote_copy(
                    src_ref=self._slot(src_refs[t], t, 4 * px + 2 * py + pc, win), dst_ref=dst_refs[t].at[my, win],
                    send_sem=send_sems.at[n, k - 1], recv_sem=recv_sems.at[n, k - 1],
                    device_id=(px, py, pc), device_id_type=MESH))
        return out

    def hook(self, step, last, src_refs, dst_refs, sems):
        if last == 0:
            for cp in self._copies(src_refs, dst_refs, sems):
                cp.start()
            for cp in self._copies(src_refs, dst_refs, sems):
                cp.wait()
            return

        @pl.when(step == 0)
        def _():
            for cp in self._copies(src_refs, dst_refs, sems):
                cp.start()

        @pl.when(step == last)
        def _():
            for cp in self._copies(src_refs, dst_refs, sems):
                cp.wait()


class _GatherCarry(_Carry):
    def hook(self, step, last, src_refs, dst_refs, sems):
        send_sems, recv_sems, local_sems = sems
        x, y, c = _coords()
        me, sibling = (x, y, c), (x, y, 1 - c)
        chips = [(1 - x, y), (x, 1 - y), (1 - x, 1 - y)]

        def per_item(fn):
            for n, (t, r0, rows) in enumerate(self.items):
                win = pl.ds(r0, rows)

                def copy(k, block, to, own=False, n=n, t=t, win=win):
                    slot = self._slot(dst_refs[t], t, 4 * block[0] + 2 * block[1] + block[2], win)
                    return pltpu.make_async_remote_copy(
                        src_ref=src_refs[t].at[win] if own else slot, dst_ref=slot,
                        send_sem=send_sems.at[n, k], recv_sem=recv_sems.at[n, k], device_id=to, device_id_type=MESH)

                mine = pltpu.make_async_copy(src_refs[t].at[win], self._slot(dst_refs[t], t, 4 * x + 2 * y + c, win), local_sems.at[n])
                first = [copy(0, me, sibling, own=True)] + [copy(1 + j, me, (*chip, c), own=True) for j, chip in enumerate(chips)]
                fn(copy, mine, first)

        def begin(copy, mine, first):
            mine.start()
            for cp in first:
                cp.start()

        def pass_on(copy, mine, first):
            for j, chip in enumerate(chips):
                copy(1 + j, (*chip, c), me).wait_recv()
                copy(4 + j, (*chip, c), sibling).start()

        def finish(copy, mine, first):
            copy(0, sibling, me).wait_recv()
            for j, chip in enumerate(chips):
                copy(4 + j, (*chip, 1 - c), me).wait_recv()
            for cp in first + [copy(4 + j, (*chip, c), sibling) for j, chip in enumerate(chips)]:
                cp.wait_send()
            mine.wait()

        if last == 0:
            for fn in (begin, pass_on, finish):
                per_item(fn)
            return
        late = max(1, (7 * last) // 8)
        for at, fn in ((0, begin), (late, pass_on), (last, finish)):
            pl.when(step == at)(functools.partial(per_item, fn))


def _mm(a, b, *, name, ta=False, tb=False, out_dtype=F32, add=None, tiles=None, carry=None):
    M, K = (a.shape[1], a.shape[0]) if ta else a.shape
    N = b.shape[0] if tb else b.shape[1]
    assert (b.shape[1] if tb else b.shape[0]) == K, (a.shape, b.shape, ta, tb)
    tk = max(t for t in range(LANE, min(K, MM_MAX_TK) + 1, LANE) if K % t == 0)
    tm, tn, tk = tiles or (_pick(M, (1024, 512, 256, 128)), _pick(N, (512, 256, 128)), tk)
    nk = K // tk
    dims = ((0 if ta else 1,), (1 if tb else 0,))

    n_in = 2 + (add is not None)
    nt = len(carry.tensors) if carry else 0
    gi, gj = M // tm, N // tn

    def kern(*refs):
        a_ref, b_ref = refs[:2]
        add_ref = refs[2] if add is not None else None
        o_ref = refs[n_in + 2 * nt]
        scr = refs[n_in + 3 * nt + 1:]
        if carry:
            step = (pl.program_id(0) * gj + pl.program_id(1)) * nk + pl.program_id(2)
            carry.hook(step, gi * gj * nk - 1, refs[n_in:n_in + nt], refs[n_in + 2 * nt + 1:n_in + 3 * nt + 1], scr[-3:])

        def finish(r):
            if add is not None:
                r = r + add_ref[...].astype(F32)
            o_ref[...] = r.astype(o_ref.dtype)

        if nk == 1:
            finish(_dot(a_ref[...], b_ref[...], dims))
            return
        acc = scr[0]
        k = pl.program_id(2)

        @pl.when(k == 0)
        def _():
            acc[...] = jnp.zeros_like(acc)

        acc[...] += _dot(a_ref[...], b_ref[...], dims)

        @pl.when(k == nk - 1)
        def _():
            finish(acc[...])

    a_spec = pl.BlockSpec((tk, tm), lambda i, j, k: (k, i)) if ta else pl.BlockSpec((tm, tk), lambda i, j, k: (i, k))
    b_spec = pl.BlockSpec((tn, tk), lambda i, j, k: (j, k)) if tb else pl.BlockSpec((tk, tn), lambda i, j, k: (k, j))
    o_spec = pl.BlockSpec((tm, tn), lambda i, j, k: (i, j))
    in_specs = [a_spec, b_spec] + ([o_spec] if add is not None else [])
    args = (a, b) + ((add,) if add is not None else ())
    scratch = [pltpu.VMEM((tm, tn), F32)] if nk > 1 else []
    o_shape = jax.ShapeDtypeStruct((M, N), out_dtype)
    if not carry:
        return pl.pallas_call(
            kern, name=name, grid=(gi, gj, nk), in_specs=in_specs, out_specs=o_spec, out_shape=o_shape, scratch_shapes=scratch,
            compiler_params=_cparams(("parallel", "parallel", "arbitrary")),
        )(*args)
    res = pl.pallas_call(
        kern, name=name, grid=(gi, gj, nk), in_specs=in_specs + carry.in_specs, out_specs=[o_spec] + carry.out_specs,
        out_shape=[o_shape] + carry.out_shape, scratch_shapes=scratch + carry.scratch,
        input_output_aliases=carry.aliases(n_in, 1), compiler_params=_cparams(("arbitrary", "arbitrary", "arbitrary")),
    )(*args, *carry.args)
    return res[0], list(res[1:])


def _stage_specs(acts, params, consts, tile, ct):
    act_specs = [pl.BlockSpec((tile, ct), functools.partial(lambda j, i, o: (i, o + j), o=off // ct)) for _, off in acts]
    par_specs = [pl.BlockSpec(bs, functools.partial(lambda j, i, im: im(j), im=im)) for _, bs, im in params]
    con_specs = [pl.BlockSpec(bs, functools.partial(lambda j, i, im: im(j), im=im)) for _, bs, im in consts]
    return act_specs, par_specs, con_specs


def _stage_fwd(f, name, n_rows, width, tile, ct, acts, params, consts, out_dtypes, carry=None):
    for _, off in acts:
        assert off % ct == 0
    na, npar, nc, no = len(acts), len(params), len(consts), len(out_dtypes)
    n_in = na + npar + nc
    nt = len(carry.tensors) if carry else 0
    gj, gi = width // ct, n_rows // tile

    def kern(*refs):
        if carry:
            step = pl.program_id(0) * gi + pl.program_id(1)
            carry.hook(step, gj * gi - 1, refs[n_in:n_in + nt], refs[n_in + 2 * nt + no:n_in + 3 * nt + no], refs[n_in + 3 * nt + no:])
        a = [r[...].astype(F32) for r in refs[:na]]
        p = [r[...] for r in refs[na:na + npar]]
        c = [r[...] for r in refs[na + npar:n_in]]
        outs = f(a, p, c, pl.program_id(1) * tile)
        for r, o in zip(refs[n_in + 2 * nt:n_in + 2 * nt + no], outs):
            r[...] = o.astype(r.dtype)

    act_specs, par_specs, con_specs = _stage_specs(acts, params, consts, tile, ct)
    o_spec = pl.BlockSpec((tile, ct), lambda j, i: (i, j))
    in_specs = act_specs + par_specs + con_specs
    out_shape = [jax.ShapeDtypeStruct((n_rows, width), d) for d in out_dtypes]
    args = [a for a, _ in acts] + [p for p, _, _ in params] + [c for c, _, _ in consts]
    if not carry:
        return tuple(pl.pallas_call(
            kern, name=name, grid=(gj, gi), in_specs=in_specs, out_specs=[o_spec] * no, out_shape=out_shape,
            compiler_params=_cparams(("parallel", "parallel")),
        )(*args))
    res = pl.pallas_call(
        kern, name=name, grid=(gj, gi), in_specs=in_specs + carry.in_specs, out_specs=[o_spec] * no + carry.out_specs,
        out_shape=out_shape + carry.out_shape, scratch_shapes=carry.scratch, input_output_aliases=carry.aliases(n_in, no),
        compiler_params=_cparams(("arbitrary", "arbitrary")),
    )(*args, *carry.args)
    return tuple(res[:no]), list(res[no:])


def _stage_bwd(f, name, n_rows, width, tile, ct, acts, params, consts, couts, dact_dtypes, extra_add=None, carry=None):
    na, npar, nc, no = len(acts), len(params), len(consts), len(couts)
    nx = 0 if extra_add is None else 1
    nt = len(carry.tensors) if carry else 0
    n_in = na + npar + nc + no + nx
    gj, gi = width // ct, n_rows // tile

    def kern(*refs):
        if carry:
            step = pl.program_id(0) * gi + pl.program_id(1)
            n_out = n_in + 2 * nt + na + npar
            carry.hook(step, gj * gi - 1, refs[n_in:n_in + nt], refs[n_out:n_out + nt], refs[n_out + nt:])
        a = [r[...].astype(F32) for r in refs[:na]]
        p = [r[...] for r in refs[na:na + npar]]
        c = [r[...] for r in refs[na + npar:na + npar + nc]]
        base = na + npar + nc
        co = [r[...].astype(F32) for r in refs[base:base + no]]
        base += no
        x_refs = refs[base:base + nx]
        base += nx + 2 * nt
        da_refs = refs[base:base + na]
        dp_refs = refs[base + na:base + na + npar]
        row0 = pl.program_id(1) * tile
        _, vjp = jax.vjp(lambda aa, pp: tuple(f(aa, pp, c, row0)), a, p)
        da, dp = vjp(tuple(co))
        for k, (r, d) in enumerate(zip(da_refs, da)):
            if k == 0 and nx:
                d = d + x_refs[0][...].astype(F32)
            r[...] = d.astype(r.dtype)
        first = pl.program_id(1) == 0
        for r, d in zip(dp_refs, dp):
            @pl.when(first)
            def _(r=r, d=d):
                r[...] = d

            @pl.when(jnp.logical_not(first))
            def _(r=r, d=d):
                r[...] += d

    act_specs, par_specs, con_specs = _stage_specs(acts, params, consts, tile, ct)
    t_spec = pl.BlockSpec((tile, ct), lambda j, i: (i, j))
    co_specs = [pl.BlockSpec((tile, ct), functools.partial(lambda j, i, o: (i, o + j), o=off // ct)) for _, off in couts]
    x_specs = [] if extra_add is None else [pl.BlockSpec((tile, ct), functools.partial(lambda j, i, o: (i, o + j), o=extra_add[1] // ct))]
    x_args = [] if extra_add is None else [extra_add[0]]
    in_specs = act_specs + par_specs + con_specs + co_specs + x_specs
    out_specs = [t_spec] * na + par_specs
    out_shape = [jax.ShapeDtypeStruct((n_rows, width), d) for d in dact_dtypes] + [jax.ShapeDtypeStruct(p.shape, F32) for p, _, _ in params]
    args = [a for a, _ in acts] + [p for p, _, _ in params] + [c for c, _, _ in consts] + [c for c, _ in couts] + x_args
    if not carry:
        outs = pl.pallas_call(
            kern, name=name, grid=(gj, gi), in_specs=in_specs, out_specs=out_specs, out_shape=out_shape,
            compiler_params=_cparams(("parallel", "arbitrary")),
        )(*args)
        return tuple(outs[:na]), tuple(outs[na:])
    outs = pl.pallas_call(
        kern, name=name, grid=(gj, gi), in_specs=in_specs + carry.in_specs, out_specs=out_specs + carry.out_specs,
        out_shape=out_shape + carry.out_shape, scratch_shapes=carry.scratch, input_output_aliases=carry.aliases(n_in, na + npar),
        compiler_params=_cparams(("arbitrary", "arbitrary")),
    )(*args, *carry.args)
    return tuple(outs[:na]), tuple(outs[na:na + npar]), list(outs[na + npar:])


def _row(ct):
    return (1, ct), (lambda j: (0, j))


def _f_rmsnorm(a, p, c, row0):
    x, = a
    g, = p
    return (x * lax.rsqrt(jnp.mean(x * x, axis=-1, keepdims=True) + NORM_EPS) * g,)


def _f_lora_act(a, p, c, row0, widths):
    x, = a
    dl, al = widths
    col = lax.broadcasted_iota(jnp.int32, x.shape, 1)
    return (jnp.where(col < dl, jnp.tanh(x), jnp.where(col < dl + al, x, _sigmoid(x))),)


def _head_sums_raw(x, ones):
    hi = x.astype(BF16)
    lo = (x - hi.astype(F32)).astype(BF16)
    return _nn(hi, ones) + _nn(lo, ones)


@jax.custom_vjp
def _head_sums(x, ones):
    return _head_sums_raw(x, ones)


_head_sums.defvjp(lambda x, ones: (_head_sums_raw(x, ones), ones),
                  lambda ones, ct: (_head_sums_raw(ct, ones), jnp.zeros_like(ones)))


def _f_rwkv_pre(a, p, c, row0):
    k, wlin, alin = a
    w0, a0, k_k, k_a = p
    gsum, = c
    w = -_softplus(-(w0 + wlin)) - 0.5
    lw = -jnp.exp(w)
    alpha = _sigmoid(a0 + alin)
    kk = k * k_k
    ss = _head_sums(kk * kk, gsum)
    kk = kk * lax.rsqrt(jnp.maximum(ss, 1e-24))
    k2 = k * (1.0 + (alpha - 1.0) * k_a)
    return lw, k2, -kk, kk * alpha


def _f_rwkv_post(a, p, c, row0):
    y, r, k2, v, g = a
    ln_g, ln_b, r_k = p
    gsum, = c
    inv = 1.0 / HEAD
    mean = _head_sums(y, gsum) * inv
    yc = y - mean
    var = _head_sums(yc * yc, gsum) * inv
    yn = yc * lax.rsqrt(var + GN_EPS) * ln_g + ln_b
    bonus = _head_sums(r * k2 * r_k, gsum)
    return ((yn + bonus * v) * g,)


def _f_lru_gates(a, p, c, row0, seq):
    xc, = a
    wr, br, wi, bi, lam = p
    xb = xc.astype(BF16)
    rg = _sigmoid(_nn(xb, wr[0].astype(BF16)) + br)
    ig = _sigmoid(_nn(xb, wi[0].astype(BF16)) + bi)
    log_a = -LRU_C * rg * _softplus(-lam)
    a_t = jnp.exp(log_a)
    mult = jnp.sqrt(_neg_expm1(2.0 * log_a))
    row = row0 + lax.broadcasted_iota(jnp.int32, xc.shape, 0)
    mult = jnp.where(row % seq == 0, 1.0, mult)
    return a_t, mult * ig * xc


def _f_lru_post(a, p, c, row0):
    h, gate = a
    g, = p
    y = h * _gelu(gate)
    return (y * lax.rsqrt(jnp.mean(y * y, axis=-1, keepdims=True) + NORM_EPS) * g,)


def _f_swiglu(a, p, c, row0):
    gate, up = a
    return (gate * _sigmoid(gate) * up,)


def _shift_down(x, s, row):
    return jnp.where(row >= s, pltpu.roll(x, s, 0), 0.0)


def _shift_up(x, s, row):
    n = x.shape[0]
    return jnp.where(row < n - s, pltpu.roll(x, n - s, 0), 0.0)


def _seq_call(kern, name, bl, seq, width, ct, ins, outs, acc_outs=()):
    def spec(off, rows):
        if rows is None:
            return pl.BlockSpec((seq, ct), functools.partial(lambda j, b, o: (b, o + j), o=off // ct))
        return pl.BlockSpec((rows, ct), lambda j, b: (0, j))

    in_specs = [spec(off, rows) for _, off, rows in ins]
    out_specs = [spec(0, None) for _ in outs] + [spec(0, rows) for _, rows in acc_outs]
    out_shape = [jax.ShapeDtypeStruct((bl * seq, width), d) for d in outs] + [jax.ShapeDtypeStruct((rows, width), F32) for _, rows in acc_outs]
    res = pl.pallas_call(
        kern, name=name, grid=(width // ct, bl), in_specs=in_specs, out_specs=out_specs, out_shape=out_shape,
        compiler_params=_cparams(("parallel", "arbitrary")),
    )(*[a for a, _, _ in ins])
    return tuple(res)


def _acc(ref, val):
    first = pl.program_id(1) == 0

    @pl.when(first)
    def _():
        ref[...] = val

    @pl.when(jnp.logical_not(first))
    def _():
        ref[...] += val


def _lerp_fwd(p, off, mu, bl, seq, width, ct):
    def kern(p_ref, mu_ref, o_ref):
        x = p_ref[...]
        row = lax.broadcasted_iota(jnp.int32, x.shape, 0)
        o_ref[...] = x + (_shift_down(x, 1, row) - x) * mu_ref[...]

    return _seq_call(kern, "lerp_fwd", bl, seq, width, ct, [(p, off, None), (mu, 0, 1)], [F32])[0]


def _lerp_bwd(p, off, mu, mu_off, dps_parts, name, bl, seq, width, ct, out_dtype):
    nd = len(dps_parts)

    def kern(*refs):
        p_ref, mu_ref = refs[:2]
        dp_ref, dmu_ref = refs[2 + nd:]
        x = p_ref[...]
        d = refs[2][...].astype(F32)
        for r in refs[3:2 + nd]:
            d = d + r[...].astype(F32)
        m = mu_ref[...]
        row = lax.broadcasted_iota(jnp.int32, x.shape, 0)
        dp_ref[...] = (d * (1.0 - m) + _shift_up(d * m, 1, row)).astype(dp_ref.dtype)
        _acc(dmu_ref, jnp.sum(d * (_shift_down(x, 1, row) - x), axis=0, keepdims=True))

    ins = [(p, off, None), (mu[:, mu_off:mu_off + width], 0, 1)] + [(a, 0, None) for a in dps_parts]
    return _seq_call(kern, name, bl, seq, width, ct, ins, [out_dtype], [(None, 1)])


def _conv_fwd(p, off, cw, cb, bl, seq, width, ct):
    nw = cw.shape[0]

    def kern(x_ref, w_ref, b_ref, o_ref):
        x = x_ref[...]
        row = lax.broadcasted_iota(jnp.int32, x.shape, 0)
        acc = b_ref[...] + x * w_ref[pl.ds(nw - 1, 1), :]
        for s in range(1, nw):
            acc = acc + _shift_down(x, s, row) * w_ref[pl.ds(nw - 1 - s, 1), :]
        o_ref[...] = acc

    return _seq_call(kern, "conv_fwd", bl, seq, width, ct, [(p, off, None), (cw, 0, nw), (cb, 0, 1)], [F32])[0]


def _conv_bwd(p, off, cw, dxc, bl, seq, width, ct, out_dtype):
    nw = cw.shape[0]

    def kern(x_ref, w_ref, d_ref, dx_ref, dw_ref, db_ref):
        x = x_ref[...]
        d = d_ref[...]
        row = lax.broadcasted_iota(jnp.int32, x.shape, 0)
        wrow = lax.broadcasted_iota(jnp.int32, dw_ref.shape, 0)
        dx = d * w_ref[pl.ds(nw - 1, 1), :]
        dw = jnp.where(wrow == nw - 1, jnp.sum(d * x, axis=0, keepdims=True), 0.0)
        for s in range(1, nw):
            dx = dx + _shift_up(d, s, row) * w_ref[pl.ds(nw - 1 - s, 1), :]
            dw = jnp.where(wrow == nw - 1 - s, jnp.sum(d * _shift_down(x, s, row), axis=0, keepdims=True), dw)
        dx_ref[...] = dx.astype(dx_ref.dtype)
        _acc(dw_ref, dw)
        _acc(db_ref, jnp.sum(d, axis=0, keepdims=True))

    return _seq_call(kern, "conv_bwd", bl, seq, width, ct, [(p, off, None), (cw, 0, nw), (dxc, 0, None)], [out_dtype], [(None, nw), (None, 1)])


def _lru_scan_fwd(a, bx, bl, seq, width, ct):
    def kern(a_ref, b_ref, h_ref):
        av = a_ref[...]
        bv = b_ref[...]
        row = lax.broadcasted_iota(jnp.int32, av.shape, 0)
        d = 1
        while d < seq:
            a_sh = jnp.where(row >= d, pltpu.roll(av, d, 0), 1.0)
            b_sh = jnp.where(row >= d, pltpu.roll(bv, d, 0), 0.0)
            bv = av * b_sh + bv
            av = av * a_sh
            d *= 2
        h_ref[...] = bv

    return _seq_call(kern, "lru_scan_fwd", bl, seq, width, ct, [(a, 0, None), (bx, 0, None)], [F32])[0]


def _lru_scan_bwd(a, h, dh, bl, seq, width, ct):
    def kern(a_ref, h_ref, d_ref, da_ref, db_ref):
        row = lax.broadcasted_iota(jnp.int32, a_ref.shape, 0)
        al = _shift_up(a_ref[...], 1, row)
        g = d_ref[...]
        d = 1
        while d < seq:
            keep = row < seq - d
            al_sh = jnp.where(keep, pltpu.roll(al, seq - d, 0), 1.0)
            g_sh = jnp.where(keep, pltpu.roll(g, seq - d, 0), 0.0)
            g = al * g_sh + g
            al = al * al_sh
            d *= 2
        db_ref[...] = g
        da_ref[...] = g * _shift_down(h_ref[...], 1, row)

    return _seq_call(kern, "lru_scan_bwd", bl, seq, width, ct, [(a, 0, None), (h, 0, None), (dh, 0, None)], [F32, F32])


_FORMS = {"nn": ((1,), (0,)), "nt": ((1,), (1,)), "tn": ((0,), (0,))}
_FORM_GRADS = {"nn": (("nt", "g", "b"), ("tn", "a", "g")),
               "nt": (("nn", "g", "b"), ("tn", "g", "a")),
               "tn": (("nt", "b", "g"), ("nn", "a", "g"))}


def _split_bf16(x):
    hi = x.astype(BF16)
    return hi, (x - hi.astype(F32)).astype(BF16)


def _pdot_raw(a, b, form, passes):
    dims = _FORMS[form]
    if passes == 1:
        return _dot(a.astype(BF16), b.astype(BF16), dims)
    ah, al = _split_bf16(a)
    bh, bl = _split_bf16(b)
    return _dot(ah, bh, dims) + (_dot(ah, bl, dims) + _dot(al, bh, dims))


@functools.partial(jax.custom_vjp, nondiff_argnums=(2, 3))
def _pdot(a, b, form, passes):
    return _pdot_raw(a, b, form, passes)


def _pdot_fwd(a, b, form, passes):
    return _pdot_raw(a, b, form, passes), (a, b)


def _pdot_bwd(form, passes, res, g):
    vals = {"a": res[0], "b": res[1], "g": g}
    (fa, xa, ya), (fb, xb, yb) = _FORM_GRADS[form]
    return _pdot_raw(vals[xa], vals[ya], fa, passes), _pdot_raw(vals[xb], vals[yb], fb, passes)


_pdot.defvjp(_pdot_fwd, _pdot_bwd)


def _neumann_raw(a_list, n_levels, passes):
    eye = (lax.broadcasted_iota(jnp.int32, a_list[0].shape, 0) == lax.broadcasted_iota(jnp.int32, a_list[0].shape, 1)).astype(F32)
    pw = list(a_list)
    x = [eye + a for a in a_list]
    for _ in range(n_levels):
        pw = [_pdot_raw(p, p, "nn", passes) for p in pw]
        x = [xi + _pdot_raw(xi, p, "nn", passes) for xi, p in zip(x, pw)]
    return x


@functools.partial(jax.custom_vjp, nondiff_argnums=(1, 2))
def _neumann_inverse(a_list, n_levels, passes):
    return _neumann_raw(a_list, n_levels, passes)


def _neumann_fwd(a_list, n_levels, passes):
    x = _neumann_raw(a_list, n_levels, passes)
    return x, x


def _neumann_bwd(n_levels, passes, x, ct):
    return ([_pdot_raw(xi, _pdot_raw(c, xi, "nt", passes), "tn", passes) for xi, c in zip(x, ct)],)


_neumann_inverse.defvjp(_neumann_fwd, _neumann_bwd)


def _scan_block(states, units, p_main=1, p_inv=1):
    C = units[0][0][0].shape[0]
    C2 = 2 * C
    ri = lax.broadcasted_iota(jnp.int32, (C, C), 0)
    ci = lax.broadcasted_iota(jnp.int32, (C, C), 1)
    tri = (ri >= ci).astype(F32)
    i2 = lax.broadcasted_iota(jnp.int32, (C2, C2), 0)
    j2 = lax.broadcasted_iota(jnp.int32, (C2, C2), 1)
    same = (i2 // C) == (j2 // C)
    strict = jnp.logical_and(same, (i2 % C) > (j2 % C))
    incl = jnp.logical_and(same, (i2 % C) >= (j2 % C))
    lane = lax.broadcasted_iota(jnp.int32, (1, LANE), 1)
    m0, m1 = (lane < HEAD).astype(F32), (lane >= HEAD).astype(F32)
    stack = lambda z: jnp.concatenate([z * m0, z * m1], axis=0)
    ids = [(i, g) for g in range(len(units[0])) for i in range(len(units))]

    pre = {}
    for i, g in ids:
        r, lw, k, v, a, b = units[i][g]
        cs = _nn(tri, lw, HI)
        p_incl = jnp.exp(cs)
        p_rec = jnp.exp(-cs)
        xr = jnp.concatenate([stack(a * jnp.exp(cs - lw)), stack(r * p_incl)], axis=0)
        bk = jnp.concatenate([stack(b * p_rec), stack(k * p_rec)], axis=0)
        pre[i, g] = (xr, bk, stack(v), jnp.exp(jnp.sum(lw, axis=0, keepdims=True)))
    gm = {u: _pdot(pre[u][0], pre[u][1], "nt", p_main) for u in ids}
    a_ak = {u: jnp.where(strict, gm[u][:C2, C2:], 0.0) for u in ids}
    r_bk = {u: jnp.concatenate([jnp.where(incl, gm[u][C2:, :C2], 0.0), jnp.where(incl, gm[u][C2:, C2:], 0.0)], axis=1) for u in ids}
    a_ab = [jnp.where(strict, gm[u][:C2, :C2], 0.0) for u in ids]
    x = dict(zip(ids, _neumann_inverse(a_ab, int(math.log2(C)) - 1, p_inv)))
    akv = {u: _pdot(a_ak[u], pre[u][2], "nn", p_main) for u in ids}

    states = list(states)
    pairs = range(len(units))
    ys = [[None] * len(units[0]) for _ in units]
    for g in range(len(units[0])):
        xs = [_pdot(pre[i, g][0], states[i], "nt", p_main) for i in pairs]
        us = [_pdot(x[i, g], xs[i][:C2] + akv[i, g], "nn", p_inv) for i in pairs]
        uv = [jnp.concatenate([us[i], pre[i, g][2]], axis=0) for i in pairs]
        y2 = [xs[i][C2:] + _pdot(r_bk[i, g], uv[i], "nn", p_main) for i in pairs]
        for i in pairs:
            ys[i][g] = y2[i][:C] + y2[i][C:]
        states = [(states[i] + _pdot(uv[i], pre[i, g][1], "tn", p_main)) * pre[i, g][3] for i in pairs]
    return ys, states


def _scan_dims(seq, rw):
    G = _pick(seq // SCAN_CHUNK, (SCAN_GROUP, 2, 1))
    NP = _pick(rw // LANE, (SCAN_PAIRS, 4, 2, 1))
    C = SCAN_CHUNK * G
    return SCAN_CHUNK, G, NP, C, seq // C, rw // (NP * LANE)


def _rwkv_scan_fwd(r, lw, k2, v, na, bb, p, bl, seq, rw, carry=None):
    cs, G, NP, C, nc, nhg = _scan_dims(seq, rw)
    nt = len(carry.tensors) if carry else 0

    def kern(*refs):
        in_refs = refs[:6]
        y_ref, st_ref = refs[6 + 2 * nt:8 + 2 * nt]
        s_scr = refs[8 + 3 * nt]
        if carry:
            step = (pl.program_id(0) * nhg + pl.program_id(1)) * nc + pl.program_id(2)
            carry.hook(step, bl * nhg * nc - 1, refs[6:6 + nt], refs[8 + 2 * nt:8 + 3 * nt], refs[9 + 3 * nt:])

        @pl.when(pl.program_id(2) == 0)
        def _():
            s_scr[...] = jnp.zeros_like(s_scr)

        st_ref[...] = s_scr[...]
        units = [[tuple(ref[pl.ds(g * cs, cs), pl.ds(i * LANE, LANE)] for ref in in_refs) for g in range(G)] for i in range(NP)]
        ys, s_new = _scan_block([s_scr[i] for i in range(NP)], units)
        for i in range(NP):
            s_scr[i] = s_new[i]
            for g in range(G):
                y_ref[pl.ds(g * cs, cs), pl.ds(i * LANE, LANE)] = ys[i][g]

    def tok(off):
        return pl.BlockSpec((C, NP * LANE), functools.partial(lambda b, h, c, o: (b * nc + c, o + h), o=off // (NP * LANE)))

    in_specs = [tok(0), tok(0), tok(0), tok(2 * rw), tok(0), tok(0)]
    out_specs = [tok(0), pl.BlockSpec((NP, LANE, LANE), lambda b, h, c: ((b * nhg + h) * nc + c, 0, 0))]
    out_shape = [jax.ShapeDtypeStruct((bl * seq, rw), F32), jax.ShapeDtypeStruct((bl * nhg * nc * NP, LANE, LANE), F32)]
    scratch = [pltpu.VMEM((NP, LANE, LANE), F32)]
    if not carry:
        y, st = pl.pallas_call(
            kern, name="rwkv_scan_fwd", grid=(bl, nhg, nc), in_specs=in_specs, out_specs=out_specs, out_shape=out_shape,
            scratch_shapes=scratch, compiler_params=_cparams(("parallel", "parallel", "arbitrary")),
        )(p, lw, k2, p, na, bb)
        return y, st
    res = pl.pallas_call(
        kern, name="rwkv_scan_fwd", grid=(bl, nhg, nc), in_specs=in_specs + carry.in_specs, out_specs=out_specs + carry.out_specs,
        out_shape=out_shape + carry.out_shape, scratch_shapes=scratch + carry.scratch, input_output_aliases=carry.aliases(6, 2),
        compiler_params=_cparams(("arbitrary", "arbitrary", "arbitrary")),
    )(p, lw, k2, p, na, bb, *carry.args)
    return res[0], res[1], list(res[2:])


def _rwkv_scan_bwd(lw, k2, na, bb, p, st, dy, bl, seq, rw, carry=None):
    cs, G, NP, C, nc, nhg = _scan_dims(seq, rw)
    nt = len(carry.tensors) if carry else 0

    def kern(*refs):
        in_refs = refs[:6]
        st_ref, dy_ref = refs[6:8]
        out_refs = refs[8 + 2 * nt:14 + 2 * nt]
        ds_scr = refs[14 + 3 * nt]
        if carry:
            step = (pl.program_id(0) * nhg + pl.program_id(1)) * nc + pl.program_id(2)
            carry.hook(step, bl * nhg * nc - 1, refs[8:8 + nt], refs[14 + 2 * nt:14 + 3 * nt], refs[15 + 3 * nt:])

        @pl.when(pl.program_id(2) == 0)
        def _():
            ds_scr[...] = jnp.zeros_like(ds_scr)

        win = lambda ref, i, g: ref[pl.ds(g * cs, cs), pl.ds(i * LANE, LANE)]
        units = [[tuple(win(ref, i, g) for ref in in_refs) for g in range(G)] for i in range(NP)]
        _, vjp = jax.vjp(_scan_block, [st_ref[i] for i in range(NP)], units)
        dys = [[win(dy_ref, i, g) for g in range(G)] for i in range(NP)]
        ds, dunits = vjp((dys, [ds_scr[i] for i in range(NP)]))
        for i in range(NP):
            ds_scr[i] = ds[i]
            for g in range(G):
                for ref, d in zip(out_refs, dunits[i][g]):
                    ref[pl.ds(g * cs, cs), pl.ds(i * LANE, LANE)] = d

    def tok(off):
        return pl.BlockSpec((C, NP * LANE), functools.partial(lambda b, h, c, o: (b * nc + (nc - 1 - c), o + h), o=off // (NP * LANE)))

    st_spec = pl.BlockSpec((NP, LANE, LANE), lambda b, h, c: ((b * nhg + h) * nc + (nc - 1 - c), 0, 0))
    in_specs = [tok(0), tok(0), tok(0), tok(2 * rw), tok(0), tok(0), st_spec, tok(0)]
    out_shape = [jax.ShapeDtypeStruct((bl * seq, rw), F32)] * 6
    scratch = [pltpu.VMEM((NP, LANE, LANE), F32)]
    if not carry:
        return pl.pallas_call(
            kern, name="rwkv_scan_bwd", grid=(bl, nhg, nc), in_specs=in_specs, out_specs=[tok(0)] * 6, out_shape=out_shape,
            scratch_shapes=scratch, compiler_params=_cparams(("parallel", "parallel", "arbitrary")),
        )(p, lw, k2, p, na, bb, st, dy)
    res = pl.pallas_call(
        kern, name="rwkv_scan_bwd", grid=(bl, nhg, nc), in_specs=in_specs + carry.in_specs, out_specs=[tok(0)] * 6 + carry.out_specs,
        out_shape=out_shape + carry.out_shape, scratch_shapes=scratch + carry.scratch, input_output_aliases=carry.aliases(8, 6),
        compiler_params=_cparams(("arbitrary", "arbitrary", "arbitrary")),
    )(p, lw, k2, p, na, bb, st, dy, *carry.args)
    return res[:6], list(res[6:])


def _loss_head(h2, g_final, target, tile):
    n, d = h2.shape
    nt = n // tile

    def kern(h_ref, g_ref, t_ref, dh_ref, dg_ref, l_ref, dhb_ref):
        def f(h, g):
            y = h * lax.rsqrt(jnp.mean(h * h, axis=-1, keepdims=True) + NORM_EPS) * g
            e = y - t_ref[...]
            return 0.5 * jnp.sum(jnp.mean(e * e, axis=-1, keepdims=True))

        loss, (dh, dg) = jax.value_and_grad(f, argnums=(0, 1))(h_ref[...], g_ref[...])
        dh_ref[...] = dh
        dhb_ref[...] = dh.astype(dhb_ref.dtype)
        first = pl.program_id(0) == 0

        @pl.when(first)
        def _():
            dg_ref[...] = dg
            l_ref[...] = jnp.zeros_like(l_ref) + loss

        @pl.when(jnp.logical_not(first))
        def _():
            dg_ref[...] += dg
            l_ref[...] += loss

    row = pl.BlockSpec((tile, d), lambda i: (i, 0))
    vec = pl.BlockSpec((1, d), lambda i: (0, 0))
    return pl.pallas_call(
        kern, name="loss_head", grid=(nt,), in_specs=[row, vec, row],
        out_specs=[row, vec, pl.BlockSpec((1, LANE), lambda i: (0, 0)), row],
        out_shape=[jax.ShapeDtypeStruct((n, d), F32), jax.ShapeDtypeStruct((1, d), F32), jax.ShapeDtypeStruct((1, LANE), F32),
                   jax.ShapeDtypeStruct((n, d), BF16)],
        compiler_params=_cparams(("arbitrary",)),
    )(h2, g_final, target)


def _adamw(parts, w, m, v, name, carry=None):
    n_parts, Rp, Cp = parts.shape
    R, Cc = w.shape
    assert Rp >= R and Cp >= Cc
    tr = _pick(R, tuple(t for t in (1024, 512, 256, 128, 64, 32, 16) if t * Cp <= 128 * 1024) + (8,))
    part = (lambda ref, s: ref[s]) if Cp == Cc else (lambda ref, s: ref[s, :, pl.ds(0, Cc)])
    c1, c2 = 1.0 - ADAM_B1, 1.0 - ADAM_B2
    bc1, bc2 = 1.0 - ADAM_B1 ** ADAM_STEP, 1.0 - ADAM_B2 ** ADAM_STEP

    nt = len(carry.tensors) if carry else 0

    def kern(*refs):
        p_ref, w_ref, m_ref, v_ref = refs[:4]
        g_ref, d_ref, nm_ref, nv_ref = refs[4 + 2 * nt:8 + 2 * nt]
        if carry:
            carry.hook(pl.program_id(0), R // tr - 1, refs[4:4 + nt], refs[8 + 2 * nt:8 + 3 * nt], refs[8 + 3 * nt:])
        g = part(p_ref, 0).astype(F32)
        for s in range(1, n_parts):
            g = g + part(p_ref, s).astype(F32)
        m2 = ADAM_B1 * m_ref[...] + c1 * g
        v2 = ADAM_B2 * v_ref[...] + c2 * (g * g)
        g_ref[...] = g
        nm_ref[...] = m2
        nv_ref[...] = v2
        d_ref[...] = -ADAM_LR * ((m2 / bc1) / (jnp.sqrt(v2 / bc2) + ADAM_EPS) + ADAM_WD * w_ref[...])

    blk = pl.BlockSpec((tr, Cc), lambda i: (i, 0))
    in_specs = [pl.BlockSpec((n_parts, tr, Cp), lambda i: (0, i, 0)), blk, blk, blk]
    out_shape = [jax.ShapeDtypeStruct((R, Cc), F32)] * 4
    if not carry:
        return pl.pallas_call(
            kern, name=name, grid=(R // tr,), in_specs=in_specs, out_specs=[blk] * 4, out_shape=out_shape,
            compiler_params=_cparams(("parallel",)),
        )(parts, w, m, v)
    res = pl.pallas_call(
        kern, name=name, grid=(R // tr,), in_specs=in_specs + carry.in_specs, out_specs=[blk] * 4 + carry.out_specs,
        out_shape=out_shape + carry.out_shape, scratch_shapes=carry.scratch, input_output_aliases=carry.aliases(4, 4),
        compiler_params=_cparams(("arbitrary",)),
    )(parts, w, m, v, *carry.args)
    return res[:4], list(res[4:])


def _exchange_now(carry, name):
    nt = len(carry.tensors)

    def body(*refs):
        carry.hook(0, 0, refs[:nt], refs[2 * nt:3 * nt], refs[3 * nt:])

    return list(pl.pallas_call(
        body, name=name, in_specs=carry.in_specs, out_specs=carry.out_specs, out_shape=carry.out_shape,
        scratch_shapes=carry.scratch, input_output_aliases=carry.aliases(0, 0),
    )(*carry.args))


def _carried(queue, capacity_us, *args, fn, **kw):
    carry = queue.take(capacity_us * CARRY_FILL)
    if carry is None:
        return fn(*args, **kw)
    res = fn(*args, carry=carry, **kw)
    queue.done(carry, res[-1])
    return res[0] if len(res) == 2 else res[:-1]


class _Queue:
    def __init__(self, gather, label):
        self.gather, self.label = gather, label
        self.tensors, self.fifo, self.n_flush = {}, [], 0

    def push(self, name, src, n_pieces, cost_us, cols=False):
        cw = None
        if cols:
            cw = src.shape[1] if self.gather else src.shape[1] // N_DEV
            assert cw % LANE == 0
            dst_shape = (src.shape[0], N_DEV * cw) if self.gather else (N_DEV, src.shape[0], cw)
        else:
            dst_shape = ((N_DEV,) + src.shape) if self.gather else src.shape
        n_rows = src.shape[0] if (self.gather or cols) else src.shape[1]
        rows = n_rows // n_pieces
        assert rows * n_pieces == n_rows and rows % 16 == 0, (name, src.shape)
        self.tensors[name] = [src, lax.empty(dst_shape, src.dtype), cw]
        self.fifo += [(name, p * rows, rows, cost_us / n_pieces) for p in range(n_pieces)]

    def take(self, capacity_us, count=None):
        picked = []
        while self.fifo and (len(picked) < count if count is not None else capacity_us >= 0.85 * self.fifo[0][3]):
            picked.append(self.fifo.pop(0))
            capacity_us -= picked[-1][3]
        if not picked:
            return None
        names = list(dict.fromkeys(n for n, _, _, _ in picked))
        cls = _GatherCarry if self.gather else _Carry
        carry = cls([tuple(self.tensors[n]) for n in names], [(names.index(n), r0, rows) for n, r0, rows, _ in picked])
        carry.names = names
        return carry

    def done(self, carry, dsts):
        for n, d in zip(carry.names, dsts):
            self.tensors[n][1] = d

    def flush(self, count=None):
        carry = self.take(float("inf"), count)
        if carry:
            self.done(carry, _exchange_now(carry, "%s_now_%d" % (self.label, self.n_flush)))
            self.n_flush += 1

    def result(self, name, r0=0, r1=None):
        late = [i for i, (n, p0, rows, _) in enumerate(self.fifo) if n == name and p0 < (r1 or p0 + rows) and p0 + rows > r0]
        if late:
            self.flush(late[-1] + 1)
        return self.tensors[name][1]


def _cols_from_shards(g):
    return jnp.transpose(g, (1, 0, 2)).reshape(g.shape[1], N_DEV * g.shape[2])


def _shards_from_cols(w):
    r, n = w.shape
    return jnp.transpose(w.reshape(r, N_DEV, n // N_DEV), (1, 0, 2))


def _pad_cols(w, to):
    return jnp.pad(w, ((0, 0), (0, to - w.shape[1])))


def _pack(arrs):
    flat = jnp.concatenate([a.reshape(-1) for a in arrs])
    n = _rup(flat.shape[0], 256 * LANE)
    return jnp.pad(flat, (0, n - flat.shape[0])).reshape(n // LANE, LANE)


def _unpack(mat, shapes):
    flat = mat.reshape(-1)
    out, o = [], 0
    for s in shapes:
        n = math.prod(s)
        out.append(flat[o:o + n].reshape(s))
        o += n
    return out


_SMALL = ["norm_mix_g", "mu_shift", "rwkv_w0", "rwkv_a0", "rwkv_k_k", "rwkv_k_a", "rwkv_r_k", "rwkv_ln_g", "rwkv_ln_b", "conv_b",
          "lru_wr", "lru_br", "lru_wi", "lru_bi", "lru_lambda", "lru_norm_g", "norm_ffn_g", "norm_final_g"]
_SMALL_SHARDED = ["rwkv_w2", "rwkv_a2", "rwkv_g2", "conv_w"]
_BIG = ["w_in", "w_out", "ffn_w_gate", "ffn_w_up", "ffn_w_down"]
_WEIGHTS = ['norm_mix_g', 'w_in', 'mu_shift', 'rwkv_w0', 'rwkv_w2', 'rwkv_a0', 'rwkv_a2', 'rwkv_g2', 'rwkv_k_k', 'rwkv_k_a', 'rwkv_r_k',
            'rwkv_ln_g', 'rwkv_ln_b', 'conv_w', 'conv_b', 'lru_wr', 'lru_br', 'lru_wi', 'lru_bi', 'lru_lambda', 'lru_norm_g', 'w_out',
            'norm_ffn_g', 'ffn_w_gate', 'ffn_w_up', 'ffn_w_down', 'norm_final_g']


def _step(W, M, V, x, loss_target):
    bl, seq, d = x.shape
    n = bl * seq
    rw = W["rwkv_w0"].shape[1]
    nh = W["rwkv_r_k"].shape[1]
    assert W["rwkv_r_k"].shape[2] == HEAD and nh * HEAD == rw and rw % LANE == 0
    dl, al, gl = W["rwkv_w2"].shape[1], W["rwkv_a2"].shape[1], W["rwkv_g2"].shape[1]
    dlp, alp, glp = _rup(dl, LANE), _rup(al, LANE), _rup(gl, LANE)
    lorap = dlp + alp + glp
    lw_ = W["conv_b"].shape[1]
    nblk, lbw = W["lru_wr"].shape[1], W["lru_wr"].shape[2]
    assert lbw == LANE and nblk * lbw == lw_
    o_xb, o_gate, o_rw = 0, lw_, 2 * lw_
    o_lora = 3 * rw
    rwp = o_lora + lorap
    inp = o_rw + rwp
    nsh_ff = W["ffn_w_gate"].shape[2]
    nshp = _rup(nsh_ff, LANE)
    dffp = N_DEV * nshp
    x2 = x.reshape(n, d)
    tgt2 = loss_target.reshape(n, d)

    gq = _Queue(True, "gather")
    kp = 2 if d % (2 * LANE) == 0 else 1
    gq.push("w_in", W["w_in"][0].astype(BF16), kp, 490)
    gq.push("small", _pack([W[k][0] for k in _SMALL_SHARDED]), 1, 10)
    gq.push("w_out", W["w_out"][0].astype(BF16), 2, 180)
    pad_ff = nshp - nsh_ff
    gq.push("ffn_w_gate", jnp.pad(W["ffn_w_gate"][0].astype(BF16), ((0, 0), (0, pad_ff))), 4, 490, cols=True)
    gq.push("ffn_w_up", jnp.pad(W["ffn_w_up"][0].astype(BF16), ((0, 0), (0, pad_ff))), 4, 490, cols=True)
    gq.push("ffn_w_down", jnp.pad(W["ffn_w_down"][0].astype(BF16), ((0, pad_ff), (0, 0))), 4, 490)
    gmm = functools.partial(_carried, gq, fn=_mm)
    gstage = functools.partial(_carried, gq, fn=_stage_fwd)

    o1 = 3 * rw

    def my_cols(g):
        w_l = _cols_from_shards(g)
        return jnp.concatenate([w_l[:, o1 + dl + al + gl:], w_l[:, :o1], _pad_cols(w_l[:, o1:o1 + dl], dlp),
                                _pad_cols(w_l[:, o1 + dl:o1 + dl + al], alp), _pad_cols(w_l[:, o1 + dl + al:o1 + dl + al + gl], glp)], axis=1)

    mu_l = W["mu_shift"]
    mu = jnp.concatenate([mu_l[:, :o1], _pad_cols(mu_l[:, o1:o1 + dl], dlp), _pad_cols(mu_l[:, o1 + dl:o1 + dl + al], alp),
                          _pad_cols(mu_l[:, o1 + dl + al:], glp)], axis=1)
    r_k = W["rwkv_r_k"].reshape(1, rw)

    tile = _pick(n, (256, 128, 64))
    tile_s = _pick(n, (128, 64))
    tile_f = _pick(n, (512, 256, 128, 64))
    ct_seq = _pick(math.gcd(rwp, lw_), (256, 128))
    assert o_rw % ct_seq == 0 and o_gate % lw_ == 0
    ct_h = _pick(rw, (256, 128))
    gi = lax.broadcasted_iota(jnp.int32, (ct_h, ct_h), 0) // HEAD
    gj = lax.broadcasted_iota(jnp.int32, (ct_h, ct_h), 1) // HEAD
    gsum = ((gi == gj).astype(BF16), (ct_h, ct_h), lambda j: (0, 0))
    full = lambda a: (a, a.shape, lambda j: (0,) * a.ndim)
    rowp = lambda a, ct: (a,) + _row(ct)

    u1, = gstage(210, _f_rmsnorm, "norm_mix_fwd", n, d, tile, d, [(x2, 0)], [full(W["norm_mix_g"])], [], [BF16])
    p, w_rows = None, []
    for i in range(kp):
        rows = slice(i * (d // kp), (i + 1) * (d // kp))
        g_in = gq.result("w_in", rows.start, rows.stop)
        w_rows.append(my_cols(g_in[:, rows, :]))
        p = gmm(190, u1[:, rows], w_rows[-1], name="mm_in_%d" % i, add=p)
    w_in = jnp.concatenate(w_rows, axis=0)
    g_small = gq.result("small")
    sm_shapes = [W[k][0].shape for k in _SMALL_SHARDED]
    sm = [_unpack(g_small[s], sm_shapes) for s in range(N_DEV)]
    w2, a2, g2, conv_w = [jnp.concatenate([sm[s][i] for s in range(N_DEV)], axis=1) for i in range(4)]
    w_lora = jnp.zeros((lorap, 3 * rw), F32)
    w_lora = w_lora.at[:dl, :rw].set(w2).at[dlp:dlp + al, rw:2 * rw].set(a2).at[dlp + alp:dlp + alp + gl, 2 * rw:].set(g2)
    w_lora = w_lora.astype(BF16)
    ps = _lerp_fwd(p, o_rw, mu, bl, seq, rwp, ct_seq)
    f_lora = functools.partial(_f_lora_act, widths=(dlp, alp))
    lact, = _stage_fwd(f_lora, "lora_act_fwd", n, lorap, tile, lorap, [(ps, o_lora)], [], [], [BF16])
    wag = _mm(lact, w_lora, name="mm_lora")
    pre_par = [rowp(W["rwkv_w0"], ct_h), rowp(W["rwkv_a0"], ct_h), rowp(W["rwkv_k_k"], ct_h), rowp(W["rwkv_k_a"], ct_h)]
    pre_acts = [(ps, rw), (wag, 0), (wag, rw)]
    lw, k2, na, bb = gstage(120, _f_rwkv_pre, "rwkv_pre_fwd", n, rw, tile_f, ct_h, pre_acts, pre_par, [gsum], [F32] * 4)
    ysc, st = _carried(gq, 230, None, lw, k2, None, na, bb, ps, bl, seq, rw, fn=_rwkv_scan_fwd)
    post_par = [rowp(W["rwkv_ln_g"], ct_h), rowp(W["rwkv_ln_b"], ct_h), rowp(r_k, ct_h)]
    post_acts = [(ysc, 0), (ps, 0), (k2, 0), (ps, 2 * rw), (wag, 2 * rw)]
    ya, = gstage(120, _f_rwkv_post, "rwkv_post_fwd", n, rw, tile_f, ct_h, post_acts, post_par, [gsum], [BF16])

    xc = _conv_fwd(p, o_xb, conv_w, W["conv_b"], bl, seq, lw_, ct_seq)
    f_gates = functools.partial(_f_lru_gates, seq=seq)
    blk3 = lambda a: (a[0], (1, LANE, LANE), lambda j: (j, 0, 0))
    gate_par = [blk3(W["lru_wr"]), rowp(W["lru_br"], LANE), blk3(W["lru_wi"]), rowp(W["lru_bi"], LANE), rowp(W["lru_lambda"], LANE)]
    tile_g = _pick(n, (1024, 512, 256, 128, 64))
    a_l, bx = gstage(160, f_gates, "lru_gates_fwd", n, lw_, tile_g, LANE, [(xc, 0)], gate_par, [], [F32, F32])
    ct_l = _pick(lw_, (256, 128))
    h_l = _lru_scan_fwd(a_l, bx, bl, seq, lw_, ct_l)
    lpost_par = [full(W["lru_norm_g"])]
    yb, = _stage_fwd(_f_lru_post, "lru_post_fwd", n, lw_, tile_s, lw_, [(h_l, 0), (p, o_gate)], lpost_par, [], [BF16])

    ycat = jnp.concatenate([ya, yb], axis=1)
    g_out = gq.result("w_out")
    w_out = g_out.reshape(N_DEV * g_out.shape[1], d)
    h1 = gmm(130, ycat, w_out, name="mm_out", add=x2)
    u2, = _stage_fwd(_f_rmsnorm, "norm_ffn_fwd", n, d, tile, d, [(h1, 0)], [full(W["norm_ffn_g"])], [], [BF16])
    w_gate = gq.result("ffn_w_gate")
    ff_gate = gmm(340, u2, w_gate, name="mm_gate", out_dtype=BF16)
    w_up = gq.result("ffn_w_up")
    ff_up = gmm(340, u2, w_up, name="mm_up", out_dtype=BF16)
    ct_f = _pick(dffp, (1024, 512, 256, 128))
    ff_acts = [(ff_gate, 0), (ff_up, 0)]
    act, = _stage_fwd(_f_swiglu, "swiglu_fwd", n, dffp, _pick(n, (512, 256, 128, 64)), ct_f, ff_acts, [], [], [BF16])
    gq.flush()
    w_down = gq.result("ffn_w_down").reshape(dffp, d)
    h2 = _mm(act, w_down, name="mm_down", add=h1)

    dh2, dg_final, lsum, dh2b = _loss_head(h2, W["norm_final_g"].reshape(1, d), tgt2, tile_s)
    loss = lax.psum(lsum[0, 0], ("x", "y", "c"))
    queue = _Queue(False, "exchange")

    cmm = functools.partial(_carried, queue, fn=_mm)
    cstage = functools.partial(_carried, queue, fn=_stage_bwd)
    dact = _mm(dh2b, w_down, name="mm_dact", tb=True, out_dtype=BF16)
    dw_down = _mm(act, dh2b, name="mm_dw_down", ta=True, out_dtype=BF16)
    queue.push("ffn_w_down", dw_down.reshape(N_DEV, nshp, d), 8, 1000)
    (dgate, dup), _ = _stage_bwd(_f_swiglu, "swiglu_bwd", n, dffp, _pick(n, (512, 256, 128, 64)), ct_f, ff_acts, [], [], [(dact, 0)],
                                 [BF16, BF16])
    du2 = cmm(400, dgate, w_gate, name="mm_du2_gate", tb=True)
    dw_gate = cmm(350, u2, dgate, name="mm_dw_gate", ta=True, out_dtype=BF16)
    queue.push("ffn_w_gate", dw_gate, 8, 1000, cols=True)
    du2 = cmm(400, dup, w_up, name="mm_du2_up", tb=True, add=du2)
    dw_up = cmm(350, u2, dup, name="mm_dw_up", ta=True, out_dtype=BF16)
    queue.push("ffn_w_up", dw_up, 8, 1000, cols=True)
    (dh1,), (dg_ffn,) = cstage(130, _f_rmsnorm, "norm_ffn_bwd", n, d, tile_s, d, [(h1, 0)], [full(W["norm_ffn_g"])], [], [(du2, 0)], [F32],
                               extra_add=(dh2, 0))
    dh1b = dh1.astype(BF16)
    dycat = cmm(135, dh1b, w_out, name="mm_dycat", tb=True)
    dw_out = cmm(170, ycat, dh1b, name="mm_dw_out", ta=True, out_dtype=BF16)
    queue.push("w_out", dw_out.reshape(N_DEV, -1, d), 4, 370)

    (dysc, dr_p, dk2_p, dv_p, dg_g), (dln_g, dln_b, dr_k) = cstage(
        195, _f_rwkv_post, "rwkv_post_bwd", n, rw, tile_f, ct_h, post_acts, post_par, [gsum], [(dycat, 0)], [F32] * 5)
    dr_s, dlw, dk2_s, dv_s, dna, dbb = _carried(queue, 650, lw, k2, na, bb, ps, st, dysc, bl, seq, rw, fn=_rwkv_scan_bwd)
    dk2 = dk2_p + dk2_s
    (dk, dwlin, dalin), (dw0, da0, dk_k, dk_a) = cstage(
        180, _f_rwkv_pre, "rwkv_pre_bwd", n, rw, tile_f, ct_h, pre_acts, pre_par, [gsum], [(dlw, 0), (dk2, 0), (dna, 0), (dbb, 0)], [F32] * 3)
    dwag = jnp.concatenate([dwlin, dalin, dg_g], axis=1).astype(BF16)
    dlact = cmm(75, dwag, w_lora, name="mm_dlact", tb=True)
    dw_lora = cmm(50, lact, dwag, name="mm_dw_lora", ta=True)
    (dps_lora,), _ = _stage_bwd(f_lora, "lora_act_bwd", n, lorap, tile, lorap, [(ps, o_lora)], [], [], [(dlact, 0)], [F32])
    dp_segs, dmu_segs = [], []
    for nm, o, wdt, parts in (("r", 0, rw, [dr_p, dr_s]), ("k", rw, rw, [dk]), ("v", 2 * rw, rw, [dv_p, dv_s]), ("lora", o_lora, lorap, [dps_lora])):
        dp_s, dmu_s = _lerp_bwd(p, o_rw + o, mu, o, parts, "lerp_bwd_" + nm, bl, seq, wdt, ct_seq, BF16)
        dp_segs.append(dp_s)
        dmu_segs.append(dmu_s)
    dmu = jnp.concatenate(dmu_segs, axis=1)

    (dh_l, dgate_l), (dlru_norm_g,) = cstage(80, _f_lru_post, "lru_post_bwd", n, lw_, tile_s, lw_, [(h_l, 0), (p, o_gate)], lpost_par, [],
                                                 [(dycat, rw)], [F32, BF16])
    da_l, dbx = _lru_scan_bwd(a_l, h_l, dh_l, bl, seq, lw_, ct_l)
    (dxc,), (dwr, dbr, dwi, dbi, dlam) = cstage(240, f_gates, "lru_gates_bwd", n, lw_, tile_g, LANE, [(xc, 0)], gate_par, [],
                                                [(da_l, 0), (dbx, 0)], [F32])
    dxb, dconv_w, dconv_b = _conv_bwd(p, o_xb, conv_w, dxc, bl, seq, lw_, ct_seq, BF16)
    sh_full = [dw_lora[:dl, :rw], dw_lora[dlp:dlp + al, rw:2 * rw], dw_lora[dlp + alp:dlp + alp + gl, 2 * rw:], dconv_w]
    assert rw == lw_
    rows_sh = sum(a.shape[0] for a in sh_full)
    pad_sh = _rup(rows_sh, 16) - rows_sh
    queue.push("small_sharded", jnp.pad(jnp.concatenate([_shards_from_cols(a) for a in sh_full], axis=1), ((0, 0), (0, pad_sh), (0, 0))), 1, 40)
    stack_sh = lambda D: jnp.pad(jnp.concatenate([D[k][0] for k in _SMALL_SHARDED], axis=0), ((0, pad_sh), (0, 0)))

    dp = jnp.concatenate([dxb, dgate_l] + dp_segs, axis=1)
    dw_in = cmm(340, u1, dp, name="mm_dw_in", ta=True, out_dtype=BF16)
    ol = o_rw + o_lora
    dw_in_l = jnp.concatenate([dw_in[:, o_rw:ol], dw_in[:, ol:ol + dl], dw_in[:, ol + dlp:ol + dlp + al],
                               dw_in[:, ol + dlp + alp:ol + dlp + alp + gl], dw_in[:, :o_rw]], axis=1)
    queue.push("w_in", _shards_from_cols(dw_in_l), 8, 970)
    du1 = cmm(380, dp, w_in, name="mm_du1", tb=True)
    (grad_x,), (dg_mix,) = cstage(90, _f_rmsnorm, "norm_mix_bwd", n, d, tile_s, d, [(x2, 0)], [full(W["norm_mix_g"])], [], [(du1, 0)], [F32],
                                  extra_add=(dh1, 0))
    dmu_l = jnp.concatenate([dmu[:, :o1], dmu[:, o_lora:o_lora + dl], dmu[:, o_lora + dlp:o_lora + dlp + al],
                             dmu[:, o_lora + dlp + alp:o_lora + dlp + alp + gl]], axis=1)
    small_g = {"norm_mix_g": dg_mix, "mu_shift": dmu_l, "rwkv_w0": dw0, "rwkv_a0": da0, "rwkv_k_k": dk_k, "rwkv_k_a": dk_a,
               "rwkv_r_k": dr_k.reshape(W["rwkv_r_k"].shape), "rwkv_ln_g": dln_g, "rwkv_ln_b": dln_b, "conv_b": dconv_b,
               "lru_wr": dwr[None], "lru_br": dbr, "lru_wi": dwi[None], "lru_bi": dbi, "lru_lambda": dlam, "lru_norm_g": dlru_norm_g,
               "norm_ffn_g": dg_ffn, "norm_final_g": dg_final.reshape(W["norm_final_g"].shape)}
    gq.push("small_grads", _pack([small_g[k] for k in _SMALL]), 1, 50)
    out = {}
    for k in ["ffn_w_down", "ffn_w_gate", "ffn_w_up", "w_out", "w_in"]:
        if k == "w_in":
            queue.flush()
        res = _carried(gq if k == "w_in" else queue, 105, queue.result(k), W[k][0], M[k][0], V[k][0], "adamw_" + k, fn=_adamw)
        out[k] = [o[None] for o in res]
    pk = lambda D: _pack([D[k] for k in _SMALL])
    res = _adamw(gq.result("small_grads"), pk(W), pk(M), pk(V), "adamw_small")
    shapes = [W[k].shape for k in _SMALL]
    for i, r in enumerate(res):
        for k, a in zip(_SMALL, _unpack(r, shapes)):
            out.setdefault(k, [None] * 4)[i] = a
    res = _adamw(queue.result("small_sharded"), stack_sh(W), stack_sh(M), stack_sh(V), "adamw_small_sharded")
    for i, r in enumerate(res):
        o = 0
        for k in _SMALL_SHARDED:
            rows = W[k].shape[1]
            out.setdefault(k, [None] * 4)[i] = r[o:o + rows][None]
            o += rows
    return loss, grad_x.reshape(x.shape), out


def kernel(x, norm_mix_g, w_in, mu_shift, rwkv_w0, rwkv_w2, rwkv_a0, rwkv_a2, rwkv_g2, rwkv_k_k, rwkv_k_a, rwkv_r_k, rwkv_ln_g, rwkv_ln_b, conv_w, conv_b, lru_wr, lru_br, lru_wi, lru_bi, lru_lambda, lru_norm_g, w_out, norm_ffn_g, ffn_w_gate, ffn_w_up, ffn_w_down, norm_final_g, loss_target, m_norm_mix_g, m_w_in, m_mu_shift, m_rwkv_w0, m_rwkv_w2, m_rwkv_a0, m_rwkv_a2, m_rwkv_g2, m_rwkv_k_k, m_rwkv_k_a, m_rwkv_r_k, m_rwkv_ln_g, m_rwkv_ln_b, m_conv_w, m_conv_b, m_lru_wr, m_lru_br, m_lru_wi, m_lru_bi, m_lru_lambda, m_lru_norm_g, m_w_out, m_norm_ffn_g, m_ffn_w_gate, m_ffn_w_up, m_ffn_w_down, m_norm_final_g, v_norm_mix_g, v_w_in, v_mu_shift, v_rwkv_w0, v_rwkv_w2, v_rwkv_a0, v_rwkv_a2, v_rwkv_g2, v_rwkv_k_k, v_rwkv_k_a, v_rwkv_r_k, v_rwkv_ln_g, v_rwkv_ln_b, v_conv_w, v_conv_b, v_lru_wr, v_lru_br, v_lru_wi, v_lru_bi, v_lru_lambda, v_lru_norm_g, v_w_out, v_norm_ffn_g, v_ffn_w_gate, v_ffn_w_up, v_ffn_w_down, v_norm_final_g):
    a = locals()
    W = {k: a[k] for k in _WEIGHTS}
    M = {k: a["m_" + k] for k in _WEIGHTS}
    V = {k: a["v_" + k] for k in _WEIGHTS}
    loss, grad_x, out = _step(W, M, V, x, loss_target)
    res = [loss, grad_x]
    for i in range(4):
        res += [out[k][i].reshape(W[k].shape) for k in _WEIGHTS]
    return tuple(res)
```

```python
import functools
import math

import jax
import jax.numpy as jnp
from jax import lax
from jax.experimental import pallas as pl
from jax.experimental.pallas import tpu as pltpu

F32 = jnp.float32
BF16 = jnp.bfloat16
HI = lax.Precision.HIGHEST
MESH = pl.DeviceIdType.MESH

N_DEV = 8
LANE = 128
HEAD = 64
MM_MAX_TK = 5632
CARRY_FILL = 1.2
SCAN_CHUNK = 64
SCAN_GROUP = 2
SCAN_PAIRS = 8
VMEM_LIMIT = 56 * 1024 * 1024

NORM_EPS = 1e-6
GN_EPS = 64e-5
LRU_C = 8.0
ADAM_LR, ADAM_B1, ADAM_B2, ADAM_EPS, ADAM_WD, ADAM_STEP = 0.001, 0.9, 0.999, 1e-08, 0.01, 10


def _pick(n, cands):
    for c in cands:
        if n % c == 0:
            return c
    return n


def _rup(n, m):
    return (n + m - 1) // m * m


def _cparams(dims):
    return pltpu.CompilerParams(dimension_semantics=dims, vmem_limit_bytes=VMEM_LIMIT)


def _sigmoid(x):
    return 1.0 / (1.0 + jnp.exp(-x))


def _softplus(z):
    return jnp.maximum(z, 0.0) + jnp.log(1.0 + jnp.exp(-jnp.abs(z)))


def _neg_expm1(x):
    series = -(x * (1.0 + 0.5 * x * (1.0 + (x / 3.0) * (1.0 + 0.25 * x))))
    return jnp.where(jnp.abs(x) < 0.03, series, 1.0 - jnp.exp(x))


def _gelu(x):
    return 0.5 * x * (1.0 + jnp.tanh(0.7978845608028654 * (x + 0.044715 * (x * x * x))))


def _dot(a, b, dims, precision=None):
    return lax.dot_general(a, b, (dims, ((), ())), precision=precision, preferred_element_type=F32)


def _nn(a, b, precision=None):
    return _dot(a, b, ((1,), (0,)), precision)


def _coords():
    return lax.axis_index("x"), lax.axis_index("y"), lax.axis_index("c")


class _Carry:
    def __init__(self, tensors, items):
        self.tensors, self.items = tensors, items
        nt, ni = len(tensors), len(items)
        any_spec = pl.BlockSpec(memory_space=pl.ANY)
        self.args = [t[0] for t in tensors] + [t[1] for t in tensors]
        self.in_specs = [any_spec] * (2 * nt)
        self.out_specs = [any_spec] * nt
        self.out_shape = [jax.ShapeDtypeStruct(t[1].shape, t[1].dtype) for t in tensors]
        self.scratch = [pltpu.SemaphoreType.DMA((ni, N_DEV - 1)), pltpu.SemaphoreType.DMA((ni, N_DEV - 1)), pltpu.SemaphoreType.DMA((ni,))]

    def aliases(self, first_in, first_out):
        nt = len(self.tensors)
        return {first_in + nt + t: first_out + t for t in range(nt)}

    def _slot(self, ref, t, idx, win):
        cw = self.tensors[t][2]
        return ref.at[idx, win] if cw is None else ref.at[win, pl.ds(pl.multiple_of(idx * cw, LANE), cw)]

    def _copies(self, src_refs, dst_refs, sems):
        send_sems, recv_sems, local_sems = sems
        x, y, c = _coords()
        my = 4 * x + 2 * y + c
        out = []
        for n, (t, r0, rows) in enumerate(self.items):
            win = pl.ds(r0, rows)
            out.append(pltpu.make_async_copy(self._slot(src_refs[t], t, my, win), dst_refs[t].at[my, win], local_sems.at[n]))
            for k in range(1, N_DEV):
                px, py, pc = x ^ ((k >> 2) & 1), y ^ ((k >> 1) & 1), c ^ (k & 1)
                out.append(pltpu.make_async_remote_copy(
                    src_ref=self._slot(src_refs[t], t, 4 * px + 2 * py + pc, win), dst_ref=dst_refs[t].at[my, win],
                    send_sem=send_sems.at[n, k - 1], recv_sem=recv_sems.at[n, k - 1],
                    device_id=(px, py, pc), device_id_type=MESH))
        return out

    def hook(self, step, last, src_refs, dst_refs, sems):
        if last == 0:
            for cp in self._copies(src_refs, dst_refs, sems):
                cp.start()
            for cp in self._copies(src_refs, dst_refs, sems):
                cp.wait()
            return

        @pl.when(step == 0)
        def _():
            for cp in self._copies(src_refs, dst_refs, sems):
                cp.start()

        @pl.when(step == last)
        def _():
            for cp in self._copies(src_refs, dst_refs, sems):
                cp.wait()


class _GatherCarry(_Carry):
    def hook(self, step, last, src_refs, dst_refs, sems):
        send_sems, recv_sems, local_sems = sems
        x, y, c = _coords()
        me, sibling = (x, y, c), (x, y, 1 - c)
        chips = [(1 - x, y), (x, 1 - y), (1 - x, 1 - y)]

        def per_item(fn):
            for n, (t, r0, rows) in enumerate(self.items):
                win = pl.ds(r0, rows)

                def copy(k, block, to, own=False, n=n, t=t, win=win):
                    slot = self._slot(dst_refs[t], t, 4 * block[0] + 2 * block[1] + block[2], win)
                    return pltpu.make_async_remote_copy(
                        src_ref=src_refs[t].at[win] if own else slot, dst_ref=slot,
                        send_sem=send_sems.at[n, k], recv_sem=recv_sems.at[n, k], device_id=to, device_id_type=MESH)

                mine = pltpu.make_async_copy(src_refs[t].at[win], self._slot(dst_refs[t], t, 4 * x + 2 * y + c, win), local_sems.at[n])
                first = [copy(0, me, sibling, own=True)] + [copy(1 + j, me, (*chip, c), own=True) for j, chip in enumerate(chips)]
                fn(copy, mine, first)

        def begin(copy, mine, first):
            mine.start()
            for cp in first:
                cp.start()

        def pass_on(copy, mine, first):
            for j, chip in enumerate(chips):
                copy(1 + j, (*chip, c), me).wait_recv()
                copy(4 + j, (*chip, c), sibling).start()

        def finish(copy, mine, first):
            copy(0, sibling, me).wait_recv()
            for j, chip in enumerate(chips):
                copy(4 + j, (*chip, 1 - c), me).wait_recv()
            for cp in first + [copy(4 + j, (*chip, c), sibling) for j, chip in enumerate(chips)]:
                cp.wait_send()
            mine.wait()

        if last == 0:
            for fn in (begin, pass_on, finish):
                per_item(fn)
            return
        late = max(1, (7 * last) // 8)
        for at, fn in ((0, begin), (late, pass_on), (last, finish)):
            pl.when(step == at)(functools.partial(per_item, fn))


def _mm(a, b, *, name, ta=False, tb=False, out_dtype=F32, add=None, tiles=None, carry=None, silu_gate=None):
    M, K = (a.shape[1], a.shape[0]) if ta else a.shape
    N = b.shape[0] if tb else b.shape[1]
    assert (b.shape[1] if tb else b.shape[0]) == K, (a.shape, b.shape, ta, tb)
    tk = max(t for t in range(LANE, min(K, MM_MAX_TK) + 1, LANE) if K % t == 0)
    tm, tn, tk = tiles or (_pick(M, (1024, 512, 256, 128)), _pick(N, (512, 256, 128)), tk)
    nk = K // tk
    dims = ((0 if ta else 1,), (1 if tb else 0,))

    n_in = 2 + (add is not None) + (silu_gate is not None)
    n_out = 1 + (silu_gate is not None)
    nt = len(carry.tensors) if carry else 0
    gi, gj = M // tm, N // tn

    def kern(*refs):
        a_ref, b_ref = refs[:2]
        add_ref = refs[2] if add is not None else None
        o_ref = refs[n_in + 2 * nt]
        scr = refs[n_in + 3 * nt + n_out:]
        if carry:
            step = (pl.program_id(0) * gj + pl.program_id(1)) * nk + pl.program_id(2)
            carry.hook(step, gi * gj * nk - 1, refs[n_in:n_in + nt], refs[n_in + 2 * nt + n_out:n_in + 3 * nt + n_out], scr[-3:])

        def finish(r):
            if add is not None:
                r = r + add_ref[...].astype(F32)
            o_ref[...] = r.astype(o_ref.dtype)
            if silu_gate is not None:
                g = refs[n_in - 1][...].astype(F32)
                refs[n_in + 2 * nt + 1][...] = (g * _sigmoid(g) * r).astype(BF16)

        if nk == 1:
            finish(_dot(a_ref[...], b_ref[...], dims))
            return
        acc = scr[0]
        k = pl.program_id(2)

        @pl.when(k == 0)
        def _():
            acc[...] = jnp.zeros_like(acc)

        acc[...] += _dot(a_ref[...], b_ref[...], dims)

        @pl.when(k == nk - 1)
        def _():
            finish(acc[...])

    a_spec = pl.BlockSpec((tk, tm), lambda i, j, k: (k, i)) if ta else pl.BlockSpec((tm, tk), lambda i, j, k: (i, k))
    b_spec = pl.BlockSpec((tn, tk), lambda i, j, k: (j, k)) if tb else pl.BlockSpec((tk, tn), lambda i, j, k: (k, j))
    o_spec = pl.BlockSpec((tm, tn), lambda i, j, k: (i, j))
    extra = [x for x in (add, silu_gate) if x is not None]
    in_specs = [a_spec, b_spec] + [o_spec] * len(extra)
    args = (a, b) + tuple(extra)
    scratch = [pltpu.VMEM((tm, tn), F32)] if nk > 1 else []
    o_shape = [jax.ShapeDtypeStruct((M, N), out_dtype)] + ([jax.ShapeDtypeStruct((M, N), BF16)] if silu_gate is not None else [])
    if not carry:
        res = pl.pallas_call(
            kern, name=name, grid=(gi, gj, nk), in_specs=in_specs, out_specs=[o_spec] * n_out, out_shape=o_shape, scratch_shapes=scratch,
            compiler_params=_cparams(("parallel", "parallel", "arbitrary")),
        )(*args)
        return res[0] if n_out == 1 else tuple(res)
    res = pl.pallas_call(
        kern, name=name, grid=(gi, gj, nk), in_specs=in_specs + carry.in_specs, out_specs=[o_spec] * n_out + carry.out_specs,
        out_shape=o_shape + carry.out_shape, scratch_shapes=scratch + carry.scratch,
        input_output_aliases=carry.aliases(n_in, n_out), compiler_params=_cparams(("arbitrary", "arbitrary", "arbitrary")),
    )(*args, *carry.args)
    return tuple(res[:n_out]) + (list(res[n_out:]),)


def _stage_specs(acts, params, consts, tile, ct):
    act_specs = [pl.BlockSpec((tile, ct), functools.partial(lambda j, i, o: (i, o + j), o=off // ct)) for _, off in acts]
    par_specs = [pl.BlockSpec(bs, functools.partial(lambda j, i, im: im(j), im=im)) for _, bs, im in params]
    con_specs = [pl.BlockSpec(bs, functools.partial(lambda j, i, im: im(j), im=im)) for _, bs, im in consts]
    return act_specs, par_specs, con_specs


def _stage_fwd(f, name, n_rows, width, tile, ct, acts, params, consts, out_dtypes, carry=None):
    for _, off in acts:
        assert off % ct == 0
    na, npar, nc, no = len(acts), len(params), len(consts), len(out_dtypes)
    n_in = na + npar + nc
    nt = len(carry.tensors) if carry else 0
    gj, gi = width // ct, n_rows // tile

    def kern(*refs):
        if carry:
            step = pl.program_id(0) * gi + pl.program_id(1)
            carry.hook(step, gj * gi - 1, refs[n_in:n_in + nt], refs[n_in + 2 * nt + no:n_in + 3 * nt + no], refs[n_in + 3 * nt + no:])
        a = [r[...].astype(F32) for r in refs[:na]]
        p = [r[...] for r in refs[na:na + npar]]
        c = [r[...] for r in refs[na + npar:n_in]]
        outs = f(a, p, c, pl.program_id(1) * tile)
        for r, o in zip(refs[n_in + 2 * nt:n_in + 2 * nt + no], outs):
            r[...] = o.astype(r.dtype)

    act_specs, par_specs, con_specs = _stage_specs(acts, params, consts, tile, ct)
    o_spec = pl.BlockSpec((tile, ct), lambda j, i: (i, j))
    in_specs = act_specs + par_specs + con_specs
    out_shape = [jax.ShapeDtypeStruct((n_rows, width), d) for d in out_dtypes]
    args = [a for a, _ in acts] + [p for p, _, _ in params] + [c for c, _, _ in consts]
    if not carry:
        return tuple(pl.pallas_call(
            kern, name=name, grid=(gj, gi), in_specs=in_specs, out_specs=[o_spec] * no, out_shape=out_shape,
            compiler_params=_cparams(("parallel", "parallel")),
        )(*args))
    res = pl.pallas_call(
        kern, name=name, grid=(gj, gi), in_specs=in_specs + carry.in_specs, out_specs=[o_spec] * no + carry.out_specs,
        out_shape=out_shape + carry.out_shape, scratch_shapes=carry.scratch, input_output_aliases=carry.aliases(n_in, no),
        compiler_params=_cparams(("arbitrary", "arbitrary")),
    )(*args, *carry.args)
    return tuple(res[:no]), list(res[no:])


def _stage_bwd(f, name, n_rows, width, tile, ct, acts, params, consts, couts, dact_dtypes, extra_add=None, carry=None):
    na, npar, nc, no = len(acts), len(params), len(consts), len(couts)
    nx = 0 if extra_add is None else 1
    nt = len(carry.tensors) if carry else 0
    n_in = na + npar + nc + no + nx
    gj, gi = width // ct, n_rows // tile

    def kern(*refs):
        if carry:
            step = pl.program_id(0) * gi + pl.program_id(1)
            n_out = n_in + 2 * nt + na + npar
            carry.hook(step, gj * gi - 1, refs[n_in:n_in + nt], refs[n_out:n_out + nt], refs[n_out + nt:])
        a = [r[...].astype(F32) for r in refs[:na]]
        p = [r[...] for r in refs[na:na + npar]]
        c = [r[...] for r in refs[na + npar:na + npar + nc]]
        base = na + npar + nc
        co = [r[...].astype(F32) for r in refs[base:base + no]]
        base += no
        x_refs = refs[base:base + nx]
        base += nx + 2 * nt
        da_refs = refs[base:base + na]
        dp_refs = refs[base + na:base + na + npar]
        row0 = pl.program_id(1) * tile
        _, vjp = jax.vjp(lambda aa, pp: tuple(f(aa, pp, c, row0)), a, p)
        da, dp = vjp(tuple(co))
        for k, (r, d) in enumerate(zip(da_refs, da)):
            if k == 0 and nx:
                d = d + x_refs[0][...].astype(F32)
            r[...] = d.astype(r.dtype)
        first = pl.program_id(1) == 0
        for r, d in zip(dp_refs, dp):
            @pl.when(first)
            def _(r=r, d=d):
                r[...] = d

            @pl.when(jnp.logical_not(first))
            def _(r=r, d=d):
                r[...] += d

    act_specs, par_specs, con_specs = _stage_specs(acts, params, consts, tile, ct)
    t_spec = pl.BlockSpec((tile, ct), lambda j, i: (i, j))
    co_specs = [pl.BlockSpec((tile, ct), functools.partial(lambda j, i, o: (i, o + j), o=off // ct)) for _, off in couts]
    x_specs = [] if extra_add is None else [pl.BlockSpec((tile, ct), functools.partial(lambda j, i, o: (i, o + j), o=extra_add[1] // ct))]
    x_args = [] if extra_add is None else [extra_add[0]]
    in_specs = act_specs + par_specs + con_specs + co_specs + x_specs
    out_specs = [t_spec] * na + par_specs
    out_shape = [jax.ShapeDtypeStruct((n_rows, width), d) for d in dact_dtypes] + [jax.ShapeDtypeStruct(p.shape, F32) for p, _, _ in params]
    args = [a for a, _ in acts] + [p for p, _, _ in params] + [c for c, _, _ in consts] + [c for c, _ in couts] + x_args
    if not carry:
        outs = pl.pallas_call(
            kern, name=name, grid=(gj, gi), in_specs=in_specs, out_specs=out_specs, out_shape=out_shape,
            compiler_params=_cparams(("parallel", "arbitrary")),
        )(*args)
        return tuple(outs[:na]), tuple(outs[na:])
    outs = pl.pallas_call(
        kern, name=name, grid=(gj, gi), in_specs=in_specs + carry.in_specs, out_specs=out_specs + carry.out_specs,
        out_shape=out_shape + carry.out_shape, scratch_shapes=carry.scratch, input_output_aliases=carry.aliases(n_in, na + npar),
        compiler_params=_cparams(("arbitrary", "arbitrary")),
    )(*args, *carry.args)
    return tuple(outs[:na]), tuple(outs[na:na + npar]), list(outs[na + npar:])


def _row(ct):
    return (1, ct), (lambda j: (0, j))


def _f_rmsnorm(a, p, c, row0):
    x, = a
    g, = p
    return (x * lax.rsqrt(jnp.mean(x * x, axis=-1, keepdims=True) + NORM_EPS) * g,)


def _f_lora_act(a, p, c, row0, widths):
    x, = a
    dl, al = widths
    col = lax.broadcasted_iota(jnp.int32, x.shape, 1)
    return (jnp.where(col < dl, jnp.tanh(x), jnp.where(col < dl + al, x, _sigmoid(x))),)


def _head_sums_raw(x, ones):
    hi = x.astype(BF16)
    lo = (x - hi.astype(F32)).astype(BF16)
    return _nn(hi, ones) + _nn(lo, ones)


@jax.custom_vjp
def _head_sums(x, ones):
    return _head_sums_raw(x, ones)


_head_sums.defvjp(lambda x, ones: (_head_sums_raw(x, ones), ones),
                  lambda ones, ct: (_head_sums_raw(ct, ones), jnp.zeros_like(ones)))


def _f_rwkv_pre(a, p, c, row0):
    k, wlin, alin = a
    w0, a0, k_k, k_a = p
    gsum, = c
    w = -_softplus(-(w0 + wlin)) - 0.5
    lw = -jnp.exp(w)
    alpha = _sigmoid(a0 + alin)
    kk = k * k_k
    ss = _head_sums(kk * kk, gsum)
    kk = kk * lax.rsqrt(jnp.maximum(ss, 1e-24))
    k2 = k * (1.0 + (alpha - 1.0) * k_a)
    return lw, k2, -kk, kk * alpha


def _f_rwkv_post(a, p, c, row0):
    y, r, k2, v, g = a
    ln_g, ln_b, r_k = p
    gsum, = c
    inv = 1.0 / HEAD
    mean = _head_sums(y, gsum) * inv
    yc = y - mean
    var = _head_sums(yc * yc, gsum) * inv
    yn = yc * lax.rsqrt(var + GN_EPS) * ln_g + ln_b
    bonus = _head_sums(r * k2 * r_k, gsum)
    return ((yn + bonus * v) * g,)


def _f_lru_gates(a, p, c, row0, seq):
    xc, = a
    wr, br, wi, bi, lam = p
    xb = xc.astype(BF16)
    rg = _sigmoid(_nn(xb, wr[0].astype(BF16)) + br)
    ig = _sigmoid(_nn(xb, wi[0].astype(BF16)) + bi)
    log_a = -LRU_C * rg * _softplus(-lam)
    a_t = jnp.exp(log_a)
    mult = jnp.sqrt(_neg_expm1(2.0 * log_a))
    row = row0 + lax.broadcasted_iota(jnp.int32, xc.shape, 0)
    mult = jnp.where(row % seq == 0, 1.0, mult)
    return a_t, mult * ig * xc


def _f_lru_post(a, p, c, row0):
    h, gate = a
    g, = p
    y = h * _gelu(gate)
    return (y * lax.rsqrt(jnp.mean(y * y, axis=-1, keepdims=True) + NORM_EPS) * g,)


def _f_swiglu(a, p, c, row0):
    gate, up = a
    return (gate * _sigmoid(gate) * up,)


def _shift_down(x, s, row):
    return jnp.where(row >= s, pltpu.roll(x, s, 0), 0.0)


def _shift_up(x, s, row):
    n = x.shape[0]
    return jnp.where(row < n - s, pltpu.roll(x, n - s, 0), 0.0)


def _seq_call(kern, name, bl, seq, width, ct, ins, outs, acc_outs=()):
    def spec(off, rows):
        if rows is None:
            return pl.BlockSpec((seq, ct), functools.partial(lambda j, b, o: (b, o + j), o=off // ct))
        return pl.BlockSpec((rows, ct), lambda j, b: (0, j))

    in_specs = [spec(off, rows) for _, off, rows in ins]
    out_specs = [spec(0, None) for _ in outs] + [spec(0, rows) for _, rows in acc_outs]
    out_shape = [jax.ShapeDtypeStruct((bl * seq, width), d) for d in outs] + [jax.ShapeDtypeStruct((rows, width), F32) for _, rows in acc_outs]
    res = pl.pallas_call(
        kern, name=name, grid=(width // ct, bl), in_specs=in_specs, out_specs=out_specs, out_shape=out_shape,
        compiler_params=_cparams(("parallel", "arbitrary")),
    )(*[a for a, _, _ in ins])
    return tuple(res)


def _acc(ref, val):
    first = pl.program_id(1) == 0

    @pl.when(first)
    def _():
        ref[...] = val

    @pl.when(jnp.logical_not(first))
    def _():
        ref[...] += val


def _lerp_fwd(p, off, mu, bl, seq, width, ct):
    def kern(p_ref, mu_ref, o_ref):
        x = p_ref[...]
        row = lax.broadcasted_iota(jnp.int32, x.shape, 0)
        o_ref[...] = x + (_shift_down(x, 1, row) - x) * mu_ref[...]

    return _seq_call(kern, "lerp_fwd", bl, seq, width, ct, [(p, off, None), (mu, 0, 1)], [F32])[0]


def _lerp_bwd(p, off, mu, mu_off, dps_parts, name, bl, seq, width, ct, out_dtype):
    nd = len(dps_parts)

    def kern(*refs):
        p_ref, mu_ref = refs[:2]
        dp_ref, dmu_ref = refs[2 + nd:]
        x = p_ref[...]
        d = refs[2][...].astype(F32)
        for r in refs[3:2 + nd]:
            d = d + r[...].astype(F32)
        m = mu_ref[...]
        row = lax.broadcasted_iota(jnp.int32, x.shape, 0)
        dp_ref[...] = (d * (1.0 - m) + _shift_up(d * m, 1, row)).astype(dp_ref.dtype)
        _acc(dmu_ref, jnp.sum(d * (_shift_down(x, 1, row) - x), axis=0, keepdims=True))

    ins = [(p, off, None), (mu[:, mu_off:mu_off + width], 0, 1)] + [(a, 0, None) for a in dps_parts]
    return _seq_call(kern, name, bl, seq, width, ct, ins, [out_dtype], [(None, 1)])


def _conv_fwd(p, off, cw, cb, bl, seq, width, ct):
    nw = cw.shape[0]

    def kern(x_ref, w_ref, b_ref, o_ref):
        x = x_ref[...]
        row = lax.broadcasted_iota(jnp.int32, x.shape, 0)
        acc = b_ref[...] + x * w_ref[pl.ds(nw - 1, 1), :]
        for s in range(1, nw):
            acc = acc + _shift_down(x, s, row) * w_ref[pl.ds(nw - 1 - s, 1), :]
        o_ref[...] = acc

    return _seq_call(kern, "conv_fwd", bl, seq, width, ct, [(p, off, None), (cw, 0, nw), (cb, 0, 1)], [F32])[0]


def _conv_bwd(p, off, cw, dxc, bl, seq, width, ct, out_dtype):
    nw = cw.shape[0]

    def kern(x_ref, w_ref, d_ref, dx_ref, dw_ref, db_ref):
        x = x_ref[...]
        d = d_ref[...]
        row = lax.broadcasted_iota(jnp.int32, x.shape, 0)
        wrow = lax.broadcasted_iota(jnp.int32, dw_ref.shape, 0)
        dx = d * w_ref[pl.ds(nw - 1, 1), :]
        dw = jnp.where(wrow == nw - 1, jnp.sum(d * x, axis=0, keepdims=True), 0.0)
        for s in range(1, nw):
            dx = dx + _shift_up(d, s, row) * w_ref[pl.ds(nw - 1 - s, 1), :]
            dw = jnp.where(wrow == nw - 1 - s, jnp.sum(d * _shift_down(x, s, row), axis=0, keepdims=True), dw)
        dx_ref[...] = dx.astype(dx_ref.dtype)
        _acc(dw_ref, dw)
        _acc(db_ref, jnp.sum(d, axis=0, keepdims=True))

    return _seq_call(kern, "conv_bwd", bl, seq, width, ct, [(p, off, None), (cw, 0, nw), (dxc, 0, None)], [out_dtype], [(None, nw), (None, 1)])


def _lru_scan_fwd(a, bx, bl, seq, width, ct):
    def kern(a_ref, b_ref, h_ref):
        av = a_ref[...]
        bv = b_ref[...]
        row = lax.broadcasted_iota(jnp.int32, av.shape, 0)
        d = 1
        while d < seq:
            a_sh = jnp.where(row >= d, pltpu.roll(av, d, 0), 1.0)
            b_sh = jnp.where(row >= d, pltpu.roll(bv, d, 0), 0.0)
            bv = av * b_sh + bv
            av = av * a_sh
            d *= 2
        h_ref[...] = bv

    return _seq_call(kern, "lru_scan_fwd", bl, seq, width, ct, [(a, 0, None), (bx, 0, None)], [F32])[0]


def _lru_scan_bwd(a, h, dh, bl, seq, width, ct):
    def kern(a_ref, h_ref, d_ref, da_ref, db_ref):
        row = lax.broadcasted_iota(jnp.int32, a_ref.shape, 0)
        al = _shift_up(a_ref[...], 1, row)
        g = d_ref[...]
        d = 1
        while d < seq:
            keep = row < seq - d
            al_sh = jnp.where(keep, pltpu.roll(al, seq - d, 0), 1.0)
            g_sh = jnp.where(keep, pltpu.roll(g, seq - d, 0), 0.0)
            g = al * g_sh + g
            al = al * al_sh
            d *= 2
        db_ref[...] = g
        da_ref[...] = g * _shift_down(h_ref[...], 1, row)

    return _seq_call(kern, "lru_scan_bwd", bl, seq, width, ct, [(a, 0, None), (h, 0, None), (dh, 0, None)], [F32, F32])


_FORMS = {"nn": ((1,), (0,)), "nt": ((1,), (1,)), "tn": ((0,), (0,))}
_FORM_GRADS = {"nn": (("nt", "g", "b"), ("tn", "a", "g")),
               "nt": (("nn", "g", "b"), ("tn", "g", "a")),
               "tn": (("nt", "b", "g"), ("nn", "a", "g"))}


def _split_bf16(x):
    hi = x.astype(BF16)
    return hi, (x - hi.astype(F32)).astype(BF16)


def _pdot_raw(a, b, form, passes):
    dims = _FORMS[form]
    if passes == 1:
        return _dot(a.astype(BF16), b.astype(BF16), dims)
    ah, al = _split_bf16(a)
    bh, bl = _split_bf16(b)
    return _dot(ah, bh, dims) + (_dot(ah, bl, dims) + _dot(al, bh, dims))


@functools.partial(jax.custom_vjp, nondiff_argnums=(2, 3))
def _pdot(a, b, form, passes):
    return _pdot_raw(a, b, form, passes)


def _pdot_fwd(a, b, form, passes):
    return _pdot_raw(a, b, form, passes), (a, b)


def _pdot_bwd(form, passes, res, g):
    vals = {"a": res[0], "b": res[1], "g": g}
    (fa, xa, ya), (fb, xb, yb) = _FORM_GRADS[form]
    return _pdot_raw(vals[xa], vals[ya], fa, passes), _pdot_raw(vals[xb], vals[yb], fb, passes)


_pdot.defvjp(_pdot_fwd, _pdot_bwd)


def _neumann_raw(a_list, n_levels, passes):
    eye = (lax.broadcasted_iota(jnp.int32, a_list[0].shape, 0) == lax.broadcasted_iota(jnp.int32, a_list[0].shape, 1)).astype(F32)
    pw = list(a_list)
    x = [eye + a for a in a_list]
    for _ in range(n_levels):
        pw = [_pdot_raw(p, p, "nn", passes) for p in pw]
        x = [xi + _pdot_raw(xi, p, "nn", passes) for xi, p in zip(x, pw)]
    return x


@functools.partial(jax.custom_vjp, nondiff_argnums=(1, 2))
def _neumann_inverse(a_list, n_levels, passes):
    return _neumann_raw(a_list, n_levels, passes)


def _neumann_fwd(a_list, n_levels, passes):
    x = _neumann_raw(a_list, n_levels, passes)
    return x, x


def _neumann_bwd(n_levels, passes, x, ct):
    return ([_pdot_raw(xi, _pdot_raw(c, xi, "nt", passes), "tn", passes) for xi, c in zip(x, ct)],)


_neumann_inverse.defvjp(_neumann_fwd, _neumann_bwd)


def _scan_block(states, units, p_main=1, p_inv=1):
    C = units[0][0][0].shape[0]
    C2 = 2 * C
    ri = lax.broadcasted_iota(jnp.int32, (C, C), 0)
    ci = lax.broadcasted_iota(jnp.int32, (C, C), 1)
    tri = (ri >= ci).astype(F32)
    i2 = lax.broadcasted_iota(jnp.int32, (C2, C2), 0)
    j2 = lax.broadcasted_iota(jnp.int32, (C2, C2), 1)
    same = (i2 // C) == (j2 // C)
    strict = jnp.logical_and(same, (i2 % C) > (j2 % C))
    incl = jnp.logical_and(same, (i2 % C) >= (j2 % C))
    lane = lax.broadcasted_iota(jnp.int32, (1, LANE), 1)
    m0, m1 = (lane < HEAD).astype(F32), (lane >= HEAD).astype(F32)
    stack = lambda z: jnp.concatenate([z * m0, z * m1], axis=0)
    ids = [(i, g) for g in range(len(units[0])) for i in range(len(units))]

    pre = {}
    for i, g in ids:
        r, lw, k, v, a, b = units[i][g]
        cs = _nn(tri, lw, HI)
        p_incl = jnp.exp(cs)
        p_rec = jnp.exp(-cs)
        xr = jnp.concatenate([stack(a * jnp.exp(cs - lw)), stack(r * p_incl)], axis=0)
        bk = jnp.concatenate([stack(b * p_rec), stack(k * p_rec)], axis=0)
        pre[i, g] = (xr, bk, stack(v), jnp.exp(jnp.sum(lw, axis=0, keepdims=True)))
    gm = {u: _pdot(pre[u][0], pre[u][1], "nt", p_main) for u in ids}
    a_ak = {u: jnp.where(strict, gm[u][:C2, C2:], 0.0) for u in ids}
    r_bk = {u: jnp.concatenate([jnp.where(incl, gm[u][C2:, :C2], 0.0), jnp.where(incl, gm[u][C2:, C2:], 0.0)], axis=1) for u in ids}
    a_ab = [jnp.where(strict, gm[u][:C2, :C2], 0.0) for u in ids]
    x = dict(zip(ids, _neumann_inverse(a_ab, int(math.log2(C)) - 1, p_inv)))
    akv = {u: _pdot(a_ak[u], pre[u][2], "nn", p_main) for u in ids}

    states = list(states)
    pairs = range(len(units))
    ys = [[None] * len(units[0]) for _ in units]
    for g in range(len(units[0])):
        xs = [_pdot(pre[i, g][0], states[i], "nt", p_main) for i in pairs]
        us = [_pdot(x[i, g], xs[i][:C2] + akv[i, g], "nn", p_inv) for i in pairs]
        uv = [jnp.concatenate([us[i], pre[i, g][2]], axis=0) for i in pairs]
        y2 = [xs[i][C2:] + _pdot(r_bk[i, g], uv[i], "nn", p_main) for i in pairs]
        for i in pairs:
            ys[i][g] = y2[i][:C] + y2[i][C:]
        states = [(states[i] + _pdot(uv[i], pre[i, g][1], "tn", p_main)) * pre[i, g][3] for i in pairs]
    return ys, states


def _scan_dims(seq, rw):
    G = _pick(seq // SCAN_CHUNK, (SCAN_GROUP, 2, 1))
    NP = _pick(rw // LANE, (SCAN_PAIRS, 4, 2, 1))
    C = SCAN_CHUNK * G
    return SCAN_CHUNK, G, NP, C, seq // C, rw // (NP * LANE)


def _rwkv_scan_fwd(r, lw, k2, v, na, bb, p, bl, seq, rw, carry=None):
    cs, G, NP, C, nc, nhg = _scan_dims(seq, rw)
    nt = len(carry.tensors) if carry else 0

    def kern(*refs):
        in_refs = refs[:6]
        y_ref, st_ref = refs[6 + 2 * nt:8 + 2 * nt]
        s_scr = refs[8 + 3 * nt]
        if carry:
            step = (pl.program_id(0) * nhg + pl.program_id(1)) * nc + pl.program_id(2)
            carry.hook(step, bl * nhg * nc - 1, refs[6:6 + nt], refs[8 + 2 * nt:8 + 3 * nt], refs[9 + 3 * nt:])

        @pl.when(pl.program_id(2) == 0)
        def _():
            s_scr[...] = jnp.zeros_like(s_scr)

        st_ref[...] = s_scr[...]
        units = [[tuple(ref[pl.ds(g * cs, cs), pl.ds(i * LANE, LANE)] for ref in in_refs) for g in range(G)] for i in range(NP)]
        ys, s_new = _scan_block([s_scr[i] for i in range(NP)], units)
        for i in range(NP):
            s_scr[i] = s_new[i]
            for g in range(G):
                y_ref[pl.ds(g * cs, cs), pl.ds(i * LANE, LANE)] = ys[i][g]

    def tok(off):
        return pl.BlockSpec((C, NP * LANE), functools.partial(lambda b, h, c, o: (b * nc + c, o + h), o=off // (NP * LANE)))

    in_specs = [tok(0), tok(0), tok(0), tok(2 * rw), tok(0), tok(0)]
    out_specs = [tok(0), pl.BlockSpec((NP, LANE, LANE), lambda b, h, c: ((b * nhg + h) * nc + c, 0, 0))]
    out_shape = [jax.ShapeDtypeStruct((bl * seq, rw), F32), jax.ShapeDtypeStruct((bl * nhg * nc * NP, LANE, LANE), F32)]
    scratch = [pltpu.VMEM((NP, LANE, LANE), F32)]
    if not carry:
        y, st = pl.pallas_call(
            kern, name="rwkv_scan_fwd", grid=(bl, nhg, nc), in_specs=in_specs, out_specs=out_specs, out_shape=out_shape,
            scratch_shapes=scratch, compiler_params=_cparams(("parallel", "parallel", "arbitrary")),
        )(p, lw, k2, p, na, bb)
        return y, st
    res = pl.pallas_call(
        kern, name="rwkv_scan_fwd", grid=(bl, nhg, nc), in_specs=in_specs + carry.in_specs, out_specs=out_specs + carry.out_specs,
        out_shape=out_shape + carry.out_shape, scratch_shapes=scratch + carry.scratch, input_output_aliases=carry.aliases(6, 2),
        compiler_params=_cparams(("arbitrary", "arbitrary", "arbitrary")),
    )(p, lw, k2, p, na, bb, *carry.args)
    return res[0], res[1], list(res[2:])


def _rwkv_scan_bwd(lw, k2, na, bb, p, st, dy, bl, seq, rw, carry=None):
    cs, G, NP, C, nc, nhg = _scan_dims(seq, rw)
    nt = len(carry.tensors) if carry else 0

    def kern(*refs):
        in_refs = refs[:6]
        st_ref, dy_ref = refs[6:8]
        out_refs = refs[8 + 2 * nt:14 + 2 * nt]
        ds_scr = refs[14 + 3 * nt]
        if carry:
            step = (pl.program_id(0) * nhg + pl.program_id(1)) * nc + pl.program_id(2)
            carry.hook(step, bl * nhg * nc - 1, refs[8:8 + nt], refs[14 + 2 * nt:14 + 3 * nt], refs[15 + 3 * nt:])

        @pl.when(pl.program_id(2) == 0)
        def _():
            ds_scr[...] = jnp.zeros_like(ds_scr)

        win = lambda ref, i, g: ref[pl.ds(g * cs, cs), pl.ds(i * LANE, LANE)]
        units = [[tuple(win(ref, i, g) for ref in in_refs) for g in range(G)] for i in range(NP)]
        _, vjp = jax.vjp(_scan_block, [st_ref[i] for i in range(NP)], units)
        dys = [[win(dy_ref, i, g) for g in range(G)] for i in range(NP)]
        ds, dunits = vjp((dys, [ds_scr[i] for i in range(NP)]))
        for i in range(NP):
            ds_scr[i] = ds[i]
            for g in range(G):
                for ref, d in zip(out_refs, dunits[i][g]):
                    ref[pl.ds(g * cs, cs), pl.ds(i * LANE, LANE)] = d

    def tok(off):
        return pl.BlockSpec((C, NP * LANE), functools.partial(lambda b, h, c, o: (b * nc + (nc - 1 - c), o + h), o=off // (NP * LANE)))

    st_spec = pl.BlockSpec((NP, LANE, LANE), lambda b, h, c: ((b * nhg + h) * nc + (nc - 1 - c), 0, 0))
    in_specs = [tok(0), tok(0), tok(0), tok(2 * rw), tok(0), tok(0), st_spec, tok(0)]
    out_shape = [jax.ShapeDtypeStruct((bl * seq, rw), F32)] * 6
    scratch = [pltpu.VMEM((NP, LANE, LANE), F32)]
    if not carry:
        return pl.pallas_call(
            kern, name="rwkv_scan_bwd", grid=(bl, nhg, nc), in_specs=in_specs, out_specs=[tok(0)] * 6, out_shape=out_shape,
            scratch_shapes=scratch, compiler_params=_cparams(("parallel", "parallel", "arbitrary")),
        )(p, lw, k2, p, na, bb, st, dy)
    res = pl.pallas_call(
        kern, name="rwkv_scan_bwd", grid=(bl, nhg, nc), in_specs=in_specs + carry.in_specs, out_specs=[tok(0)] * 6 + carry.out_specs,
        out_shape=out_shape + carry.out_shape, scratch_shapes=scratch + carry.scratch, input_output_aliases=carry.aliases(8, 6),
        compiler_params=_cparams(("arbitrary", "arbitrary", "arbitrary")),
    )(p, lw, k2, p, na, bb, st, dy, *carry.args)
    return res[:6], list(res[6:])


def _loss_head(h2, g_final, target, tile):
    n, d = h2.shape
    nt = n // tile

    def kern(h_ref, g_ref, t_ref, dh_ref, dg_ref, l_ref, dhb_ref):
        def f(h, g):
            y = h * lax.rsqrt(jnp.mean(h * h, axis=-1, keepdims=True) + NORM_EPS) * g
            e = y - t_ref[...]
            return 0.5 * jnp.sum(jnp.mean(e * e, axis=-1, keepdims=True))

        loss, (dh, dg) = jax.value_and_grad(f, argnums=(0, 1))(h_ref[...], g_ref[...])
        dh_ref[...] = dh
        dhb_ref[...] = dh.astype(dhb_ref.dtype)
        first = pl.program_id(0) == 0

        @pl.when(first)
        def _():
            dg_ref[...] = dg
            l_ref[...] = jnp.zeros_like(l_ref) + loss

        @pl.when(jnp.logical_not(first))
        def _():
            dg_ref[...] += dg
            l_ref[...] += loss

    row = pl.BlockSpec((tile, d), lambda i: (i, 0))
    vec = pl.BlockSpec((1, d), lambda i: (0, 0))
    return pl.pallas_call(
        kern, name="loss_head", grid=(nt,), in_specs=[row, vec, row],
        out_specs=[row, vec, pl.BlockSpec((1, LANE), lambda i: (0, 0)), row],
        out_shape=[jax.ShapeDtypeStruct((n, d), F32), jax.ShapeDtypeStruct((1, d), F32), jax.ShapeDtypeStruct((1, LANE), F32),
                   jax.ShapeDtypeStruct((n, d), BF16)],
        compiler_params=_cparams(("arbitrary",)),
    )(h2, g_final, target)


def _adamw(parts, w, m, v, name, carry=None):
    n_parts, Rp, Cp = parts.shape
    R, Cc = w.shape
    assert Rp >= R and Cp >= Cc
    tr = _pick(R, tuple(t for t in (1024, 512, 256, 128, 64, 32, 16) if t * Cp <= 128 * 1024) + (8,))
    part = (lambda ref, s: ref[s]) if Cp == Cc else (lambda ref, s: ref[s, :, pl.ds(0, Cc)])
    c1, c2 = 1.0 - ADAM_B1, 1.0 - ADAM_B2
    bc1, bc2 = 1.0 - ADAM_B1 ** ADAM_STEP, 1.0 - ADAM_B2 ** ADAM_STEP

    nt = len(carry.tensors) if carry else 0

    def kern(*refs):
        p_ref, w_ref, m_ref, v_ref = refs[:4]
        g_ref, d_ref, nm_ref, nv_ref = refs[4 + 2 * nt:8 + 2 * nt]
        if carry:
            carry.hook(pl.program_id(0), R // tr - 1, refs[4:4 + nt], refs[8 + 2 * nt:8 + 3 * nt], refs[8 + 3 * nt:])
        g = part(p_ref, 0).astype(F32)
        for s in range(1, n_parts):
            g = g + part(p_ref, s).astype(F32)
        m2 = ADAM_B1 * m_ref[...] + c1 * g
        v2 = ADAM_B2 * v_ref[...] + c2 * (g * g)
        g_ref[...] = g
        nm_ref[...] = m2
        nv_ref[...] = v2
        d_ref[...] = -ADAM_LR * ((m2 / bc1) / (jnp.sqrt(v2 / bc2) + ADAM_EPS) + ADAM_WD * w_ref[...])

    blk = pl.BlockSpec((tr, Cc), lambda i: (i, 0))
    in_specs = [pl.BlockSpec((n_parts, tr, Cp), lambda i: (0, i, 0)), blk, blk, blk]
    out_shape = [jax.ShapeDtypeStruct((R, Cc), F32)] * 4
    if not carry:
        return pl.pallas_call(
            kern, name=name, grid=(R // tr,), in_specs=in_specs, out_specs=[blk] * 4, out_shape=out_shape,
            compiler_params=_cparams(("parallel",)),
        )(parts, w, m, v)
    res = pl.pallas_call(
        kern, name=name, grid=(R // tr,), in_specs=in_specs + carry.in_specs, out_specs=[blk] * 4 + carry.out_specs,
        out_shape=out_shape + carry.out_shape, scratch_shapes=carry.scratch, input_output_aliases=carry.aliases(4, 4),
        compiler_params=_cparams(("arbitrary",)),
    )(parts, w, m, v, *carry.args)
    return res[:4], list(res[4:])


def _exchange_now(carry, name):
    nt = len(carry.tensors)

    def body(*refs):
        carry.hook(0, 0, refs[:nt], refs[2 * nt:3 * nt], refs[3 * nt:])

    return list(pl.pallas_call(
        body, name=name, in_specs=carry.in_specs, out_specs=carry.out_specs, out_shape=carry.out_shape,
        scratch_shapes=carry.scratch, input_output_aliases=carry.aliases(0, 0),
    )(*carry.args))


def _carried(queue, capacity_us, *args, fn, **kw):
    carry = queue.take(capacity_us * CARRY_FILL)
    if carry is None:
        return fn(*args, **kw)
    res = fn(*args, carry=carry, **kw)
    queue.done(carry, res[-1])
    return res[0] if len(res) == 2 else res[:-1]


class _Queue:
    def __init__(self, gather, label):
        self.gather, self.label = gather, label
        self.tensors, self.fifo, self.n_flush = {}, [], 0

    def push(self, name, src, n_pieces, cost_us, cols=False):
        cw = None
        if cols:
            cw = src.shape[1] if self.gather else src.shape[1] // N_DEV
            assert cw % LANE == 0
            dst_shape = (src.shape[0], N_DEV * cw) if self.gather else (N_DEV, src.shape[0], cw)
        else:
            dst_shape = ((N_DEV,) + src.shape) if self.gather else src.shape
        n_rows = src.shape[0] if (self.gather or cols) else src.shape[1]
        rows = n_rows // n_pieces
        assert rows * n_pieces == n_rows and rows % 16 == 0, (name, src.shape)
        self.tensors[name] = [src, lax.empty(dst_shape, src.dtype), cw]
        self.fifo += [(name, p * rows, rows, cost_us / n_pieces) for p in range(n_pieces)]

    def take(self, capacity_us, count=None):
        picked = []
        while self.fifo and (len(picked) < count if count is not None else capacity_us >= 0.85 * self.fifo[0][3]):
            picked.append(self.fifo.pop(0))
            capacity_us -= picked[-1][3]
        if not picked:
            return None
        names = list(dict.fromkeys(n for n, _, _, _ in picked))
        cls = _GatherCarry if self.gather else _Carry
        carry = cls([tuple(self.tensors[n]) for n in names], [(names.index(n), r0, rows) for n, r0, rows, _ in picked])
        carry.names = names
        return carry

    def done(self, carry, dsts):
        for n, d in zip(carry.names, dsts):
            self.tensors[n][1] = d

    def flush(self, count=None):
        carry = self.take(float("inf"), count)
        if carry:
            self.done(carry, _exchange_now(carry, "%s_now_%d" % (self.label, self.n_flush)))
            self.n_flush += 1

    def result(self, name, r0=0, r1=None):
        late = [i for i, (n, p0, rows, _) in enumerate(self.fifo) if n == name and p0 < (r1 or p0 + rows) and p0 + rows > r0]
        if late:
            self.flush(late[-1] + 1)
        return self.tensors[name][1]


def _cols_from_shards(g):
    return jnp.transpose(g, (1, 0, 2)).reshape(g.shape[1], N_DEV * g.shape[2])


def _shards_from_cols(w):
    r, n = w.shape
    return jnp.transpose(w.reshape(r, N_DEV, n // N_DEV), (1, 0, 2))


def _pad_cols(w, to):
    return jnp.pad(w, ((0, 0), (0, to - w.shape[1])))


def _pack(arrs):
    flat = jnp.concatenate([a.reshape(-1) for a in arrs])
    n = _rup(flat.shape[0], 256 * LANE)
    return jnp.pad(flat, (0, n - flat.shape[0])).reshape(n // LANE, LANE)


def _unpack(mat, shapes):
    flat = mat.reshape(-1)
    out, o = [], 0
    for s in shapes:
        n = math.prod(s)
        out.append(flat[o:o + n].reshape(s))
        o += n
    return out


_SMALL = ["norm_mix_g", "mu_shift", "rwkv_w0", "rwkv_a0", "rwkv_k_k", "rwkv_k_a", "rwkv_r_k", "rwkv_ln_g", "rwkv_ln_b", "conv_b",
          "lru_wr", "lru_br", "lru_wi", "lru_bi", "lru_lambda", "lru_norm_g", "norm_ffn_g", "norm_final_g"]
_SMALL_SHARDED = ["rwkv_w2", "rwkv_a2", "rwkv_g2", "conv_w"]
_BIG = ["w_in", "w_out", "ffn_w_gate", "ffn_w_up", "ffn_w_down"]
_WEIGHTS = ['norm_mix_g', 'w_in', 'mu_shift', 'rwkv_w0', 'rwkv_w2', 'rwkv_a0', 'rwkv_a2', 'rwkv_g2', 'rwkv_k_k', 'rwkv_k_a', 'rwkv_r_k',
            'rwkv_ln_g', 'rwkv_ln_b', 'conv_w', 'conv_b', 'lru_wr', 'lru_br', 'lru_wi', 'lru_bi', 'lru_lambda', 'lru_norm_g', 'w_out',
            'norm_ffn_g', 'ffn_w_gate', 'ffn_w_up', 'ffn_w_down', 'norm_final_g']


def _step(W, M, V, x, loss_target):
    bl, seq, d = x.shape
    n = bl * seq
    rw = W["rwkv_w0"].shape[1]
    nh = W["rwkv_r_k"].shape[1]
    assert W["rwkv_r_k"].shape[2] == HEAD and nh * HEAD == rw and rw % LANE == 0
    dl, al, gl = W["rwkv_w2"].shape[1], W["rwkv_a2"].shape[1], W["rwkv_g2"].shape[1]
    dlp, alp, glp = _rup(dl, LANE), _rup(al, LANE), _rup(gl, LANE)
    lorap = dlp + alp + glp
    lw_ = W["conv_b"].shape[1]
    nblk, lbw = W["lru_wr"].shape[1], W["lru_wr"].shape[2]
    assert lbw == LANE and nblk * lbw == lw_
    o_xb, o_gate, o_rw = 0, lw_, 2 * lw_
    o_lora = 3 * rw
    rwp = o_lora + lorap
    inp = o_rw + rwp
    nsh_ff = W["ffn_w_gate"].shape[2]
    nshp = _rup(nsh_ff, LANE)
    dffp = N_DEV * nshp
    x2 = x.reshape(n, d)
    tgt2 = loss_target.reshape(n, d)

    gq = _Queue(True, "gather")
    kp = 2 if d % (2 * LANE) == 0 else 1
    gq.push("w_in", W["w_in"][0].astype(BF16), kp, 490)
    gq.push("small", _pack([W[k][0] for k in _SMALL_SHARDED]), 1, 10)
    gq.push("w_out", W["w_out"][0].astype(BF16), 2, 180)
    pad_ff = nshp - nsh_ff
    gq.push("ffn_w_gate", jnp.pad(W["ffn_w_gate"][0].astype(BF16), ((0, 0), (0, pad_ff))), 4, 490, cols=True)
    gq.push("ffn_w_up", jnp.pad(W["ffn_w_up"][0].astype(BF16), ((0, 0), (0, pad_ff))), 4, 490, cols=True)
    gq.push("ffn_w_down", jnp.pad(W["ffn_w_down"][0].astype(BF16), ((0, pad_ff), (0, 0))), 4, 490)
    gmm = functools.partial(_carried, gq, fn=_mm)
    gstage = functools.partial(_carried, gq, fn=_stage_fwd)

    o1 = 3 * rw

    def my_cols(g):
        w_l = _cols_from_shards(g)
        return jnp.concatenate([w_l[:, o1 + dl + al + gl:], w_l[:, :o1], _pad_cols(w_l[:, o1:o1 + dl], dlp),
                                _pad_cols(w_l[:, o1 + dl:o1 + dl + al], alp), _pad_cols(w_l[:, o1 + dl + al:o1 + dl + al + gl], glp)], axis=1)

    mu_l = W["mu_shift"]
    mu = jnp.concatenate([mu_l[:, :o1], _pad_cols(mu_l[:, o1:o1 + dl], dlp), _pad_cols(mu_l[:, o1 + dl:o1 + dl + al], alp),
                          _pad_cols(mu_l[:, o1 + dl + al:], glp)], axis=1)
    r_k = W["rwkv_r_k"].reshape(1, rw)

    tile = _pick(n, (256, 128, 64))
    tile_s = _pick(n, (128, 64))
    tile_f = _pick(n, (512, 256, 128, 64))
    ct_seq = _pick(math.gcd(rwp, lw_), (256, 128))
    assert o_rw % ct_seq == 0 and o_gate % lw_ == 0
    ct_h = _pick(rw, (256, 128))
    gi = lax.broadcasted_iota(jnp.int32, (ct_h, ct_h), 0) // HEAD
    gj = lax.broadcasted_iota(jnp.int32, (ct_h, ct_h), 1) // HEAD
    gsum = ((gi == gj).astype(BF16), (ct_h, ct_h), lambda j: (0, 0))
    full = lambda a: (a, a.shape, lambda j: (0,) * a.ndim)
    rowp = lambda a, ct: (a,) + _row(ct)

    u1, = gstage(210, _f_rmsnorm, "norm_mix_fwd", n, d, tile, d, [(x2, 0)], [full(W["norm_mix_g"])], [], [BF16])
    p, w_rows = None, []
    for i in range(kp):
        rows = slice(i * (d // kp), (i + 1) * (d // kp))
        g_in = gq.result("w_in", rows.start, rows.stop)
        w_rows.append(my_cols(g_in[:, rows, :]))
        p = gmm(190, u1[:, rows], w_rows[-1], name="mm_in_%d" % i, add=p)
    w_in = jnp.concatenate(w_rows, axis=0)
    g_small = gq.result("small")
    sm_shapes = [W[k][0].shape for k in _SMALL_SHARDED]
    sm = [_unpack(g_small[s], sm_shapes) for s in range(N_DEV)]
    w2, a2, g2, conv_w = [jnp.concatenate([sm[s][i] for s in range(N_DEV)], axis=1) for i in range(4)]
    w_lora = jnp.zeros((lorap, 3 * rw), F32)
    w_lora = w_lora.at[:dl, :rw].set(w2).at[dlp:dlp + al, rw:2 * rw].set(a2).at[dlp + alp:dlp + alp + gl, 2 * rw:].set(g2)
    w_lora = w_lora.astype(BF16)
    ps = _lerp_fwd(p, o_rw, mu, bl, seq, rwp, ct_seq)
    f_lora = functools.partial(_f_lora_act, widths=(dlp, alp))
    lact, = _stage_fwd(f_lora, "lora_act_fwd", n, lorap, tile, lorap, [(ps, o_lora)], [], [], [BF16])
    wag = _mm(lact, w_lora, name="mm_lora")
    pre_par = [rowp(W["rwkv_w0"], ct_h), rowp(W["rwkv_a0"], ct_h), rowp(W["rwkv_k_k"], ct_h), rowp(W["rwkv_k_a"], ct_h)]
    pre_acts = [(ps, rw), (wag, 0), (wag, rw)]
    lw, k2, na, bb = gstage(120, _f_rwkv_pre, "rwkv_pre_fwd", n, rw, tile_f, ct_h, pre_acts, pre_par, [gsum], [F32] * 4)
    ysc, st = _carried(gq, 230, None, lw, k2, None, na, bb, ps, bl, seq, rw, fn=_rwkv_scan_fwd)
    post_par = [rowp(W["rwkv_ln_g"], ct_h), rowp(W["rwkv_ln_b"], ct_h), rowp(r_k, ct_h)]
    post_acts = [(ysc, 0), (ps, 0), (k2, 0), (ps, 2 * rw), (wag, 2 * rw)]
    ya, = gstage(120, _f_rwkv_post, "rwkv_post_fwd", n, rw, tile_f, ct_h, post_acts, post_par, [gsum], [BF16])

    xc = _conv_fwd(p, o_xb, conv_w, W["conv_b"], bl, seq, lw_, ct_seq)
    f_gates = functools.partial(_f_lru_gates, seq=seq)
    blk3 = lambda a: (a[0], (1, LANE, LANE), lambda j: (j, 0, 0))
    gate_par = [blk3(W["lru_wr"]), rowp(W["lru_br"], LANE), blk3(W["lru_wi"]), rowp(W["lru_bi"], LANE), rowp(W["lru_lambda"], LANE)]
    tile_g = _pick(n, (1024, 512, 256, 128, 64))
    a_l, bx = gstage(160, f_gates, "lru_gates_fwd", n, lw_, tile_g, LANE, [(xc, 0)], gate_par, [], [F32, F32])
    ct_l = _pick(lw_, (256, 128))
    h_l = _lru_scan_fwd(a_l, bx, bl, seq, lw_, ct_l)
    lpost_par = [full(W["lru_norm_g"])]
    yb, = _stage_fwd(_f_lru_post, "lru_post_fwd", n, lw_, tile_s, lw_, [(h_l, 0), (p, o_gate)], lpost_par, [], [BF16])

    ycat = jnp.concatenate([ya, yb], axis=1)
    g_out = gq.result("w_out")
    w_out = g_out.reshape(N_DEV * g_out.shape[1], d)
    h1 = gmm(130, ycat, w_out, name="mm_out", add=x2)
    u2, = _stage_fwd(_f_rmsnorm, "norm_ffn_fwd", n, d, tile, d, [(h1, 0)], [full(W["norm_ffn_g"])], [], [BF16])
    w_gate = gq.result("ffn_w_gate")
    ff_gate = gmm(340, u2, w_gate, name="mm_gate", out_dtype=BF16)
    w_up = gq.result("ffn_w_up")
    ff_up, act = gmm(340, u2, w_up, name="mm_up", out_dtype=BF16, silu_gate=ff_gate)
    ct_f = _pick(dffp, (1024, 512, 256, 128))
    ff_acts = [(ff_gate, 0), (ff_up, 0)]
    gq.flush()
    w_down = gq.result("ffn_w_down").reshape(dffp, d)
    h2 = _mm(act, w_down, name="mm_down", add=h1)

    dh2, dg_final, lsum, dh2b = _loss_head(h2, W["norm_final_g"].reshape(1, d), tgt2, tile_s)
    loss = lax.psum(lsum[0, 0], ("x", "y", "c"))
    queue = _Queue(False, "exchange")

    cmm = functools.partial(_carried, queue, fn=_mm)
    cstage = functools.partial(_carried, queue, fn=_stage_bwd)
    dact = _mm(dh2b, w_down, name="mm_dact", tb=True, out_dtype=BF16)
    dw_down = _mm(act, dh2b, name="mm_dw_down", ta=True, out_dtype=BF16)
    queue.push("ffn_w_down", dw_down.reshape(N_DEV, nshp, d), 8, 1000)
    (dgate, dup), _ = _stage_bwd(_f_swiglu, "swiglu_bwd", n, dffp, _pick(n, (512, 256, 128, 64)), ct_f, ff_acts, [], [], [(dact, 0)],
                                 [BF16, BF16])
    du2 = cmm(400, dgate, w_gate, name="mm_du2_gate", tb=True)
    dw_gate = cmm(350, u2, dgate, name="mm_dw_gate", ta=True, out_dtype=BF16)
    queue.push("ffn_w_gate", dw_gate, 8, 1000, cols=True)
    du2 = cmm(400, dup, w_up, name="mm_du2_up", tb=True, add=du2)
    dw_up = cmm(350, u2, dup, name="mm_dw_up", ta=True, out_dtype=BF16)
    queue.push("ffn_w_up", dw_up, 8, 1000, cols=True)
    (dh1,), (dg_ffn,) = cstage(130, _f_rmsnorm, "norm_ffn_bwd", n, d, tile_s, d, [(h1, 0)], [full(W["norm_ffn_g"])], [], [(du2, 0)], [F32],
                               extra_add=(dh2, 0))
    dh1b = dh1.astype(BF16)
    dycat = cmm(135, dh1b, w_out, name="mm_dycat", tb=True)
    dw_out = cmm(170, ycat, dh1b, name="mm_dw_out", ta=True, out_dtype=BF16)
    queue.push("w_out", dw_out.reshape(N_DEV, -1, d), 4, 370)

    (dysc, dr_p, dk2_p, dv_p, dg_g), (dln_g, dln_b, dr_k) = cstage(
        195, _f_rwkv_post, "rwkv_post_bwd", n, rw, tile_f, ct_h, post_acts, post_par, [gsum], [(dycat, 0)], [F32] * 5)
    dr_s, dlw, dk2_s, dv_s, dna, dbb = _carried(queue, 650, lw, k2, na, bb, ps, st, dysc, bl, seq, rw, fn=_rwkv_scan_bwd)
    dk2 = dk2_p + dk2_s
    (dk, dwlin, dalin), (dw0, da0, dk_k, dk_a) = cstage(
        180, _f_rwkv_pre, "rwkv_pre_bwd", n, rw, tile_f, ct_h, pre_acts, pre_par, [gsum], [(dlw, 0), (dk2, 0), (dna, 0), (dbb, 0)], [F32] * 3)
    dwag = jnp.concatenate([dwlin, dalin, dg_g], axis=1).astype(BF16)
    dlact = cmm(75, dwag, w_lora, name="mm_dlact", tb=True)
    dw_lora = cmm(50, lact, dwag, name="mm_dw_lora", ta=True)
    (dps_lora,), _ = _stage_bwd(f_lora, "lora_act_bwd", n, lorap, tile, lorap, [(ps, o_lora)], [], [], [(dlact, 0)], [F32])
    dp_segs, dmu_segs = [], []
    for nm, o, wdt, parts in (("r", 0, rw, [dr_p, dr_s]), ("k", rw, rw, [dk]), ("v", 2 * rw, rw, [dv_p, dv_s]), ("lora", o_lora, lorap, [dps_lora])):
        dp_s, dmu_s = _lerp_bwd(p, o_rw + o, mu, o, parts, "lerp_bwd_" + nm, bl, seq, wdt, ct_seq, BF16)
        dp_segs.append(dp_s)
        dmu_segs.append(dmu_s)
    dmu = jnp.concatenate(dmu_segs, axis=1)

    (dh_l, dgate_l), (dlru_norm_g,) = cstage(80, _f_lru_post, "lru_post_bwd", n, lw_, tile_s, lw_, [(h_l, 0), (p, o_gate)], lpost_par, [],
                                                 [(dycat, rw)], [F32, BF16])
    da_l, dbx = _lru_scan_bwd(a_l, h_l, dh_l, bl, seq, lw_, ct_l)
    (dxc,), (dwr, dbr, dwi, dbi, dlam) = cstage(240, f_gates, "lru_gates_bwd", n, lw_, tile_g, LANE, [(xc, 0)], gate_par, [],
                                                [(da_l, 0), (dbx, 0)], [F32])
    dxb, dconv_w, dconv_b = _conv_bwd(p, o_xb, conv_w, dxc, bl, seq, lw_, ct_seq, BF16)
    sh_full = [dw_lora[:dl, :rw], dw_lora[dlp:dlp + al, rw:2 * rw], dw_lora[dlp + alp:dlp + alp + gl, 2 * rw:], dconv_w]
    assert rw == lw_
    rows_sh = sum(a.shape[0] for a in sh_full)
    pad_sh = _rup(rows_sh, 16) - rows_sh
    queue.push("small_sharded", jnp.pad(jnp.concatenate([_shards_from_cols(a) for a in sh_full], axis=1), ((0, 0), (0, pad_sh), (0, 0))), 1, 40)
    stack_sh = lambda D: jnp.pad(jnp.concatenate([D[k][0] for k in _SMALL_SHARDED], axis=0), ((0, pad_sh), (0, 0)))

    dp = jnp.concatenate([dxb, dgate_l] + dp_segs, axis=1)
    dw_in = cmm(340, u1, dp, name="mm_dw_in", ta=True, out_dtype=BF16)
    ol = o_rw + o_lora
    dw_in_l = jnp.concatenate([dw_in[:, o_rw:ol], dw_in[:, ol:ol + dl], dw_in[:, ol + dlp:ol + dlp + al],
                               dw_in[:, ol + dlp + alp:ol + dlp + alp + gl], dw_in[:, :o_rw]], axis=1)
    queue.push("w_in", _shards_from_cols(dw_in_l), 8, 970)
    du1 = cmm(380, dp, w_in, name="mm_du1", tb=True)
    (grad_x,), (dg_mix,) = cstage(90, _f_rmsnorm, "norm_mix_bwd", n, d, tile_s, d, [(x2, 0)], [full(W["norm_mix_g"])], [], [(du1, 0)], [F32],
                                  extra_add=(dh1, 0))
    dmu_l = jnp.concatenate([dmu[:, :o1], dmu[:, o_lora:o_lora + dl], dmu[:, o_lora + dlp:o_lora + dlp + al],
                             dmu[:, o_lora + dlp + alp:o_lora + dlp + alp + gl]], axis=1)
    small_g = {"norm_mix_g": dg_mix, "mu_shift": dmu_l, "rwkv_w0": dw0, "rwkv_a0": da0, "rwkv_k_k": dk_k, "rwkv_k_a": dk_a,
               "rwkv_r_k": dr_k.reshape(W["rwkv_r_k"].shape), "rwkv_ln_g": dln_g, "rwkv_ln_b": dln_b, "conv_b": dconv_b,
               "lru_wr": dwr[None], "lru_br": dbr, "lru_wi": dwi[None], "lru_bi": dbi, "lru_lambda": dlam, "lru_norm_g": dlru_norm_g,
               "norm_ffn_g": dg_ffn, "norm_final_g": dg_final.reshape(W["norm_final_g"].shape)}
    gq.push("small_grads", _pack([small_g[k] for k in _SMALL]), 1, 50)
    out = {}
    for k in ["ffn_w_down", "ffn_w_gate", "ffn_w_up", "w_out", "w_in"]:
        if k == "w_in":
            queue.flush()
        res = _carried(gq if k == "w_in" else queue, 105, queue.result(k), W[k][0], M[k][0], V[k][0], "adamw_" + k, fn=_adamw)
        out[k] = [o[None] for o in res]
    pk = lambda D: _pack([D[k] for k in _SMALL])
    res = _adamw(gq.result("small_grads"), pk(W), pk(M), pk(V), "adamw_small")
    shapes = [W[k].shape for k in _SMALL]
    for i, r in enumerate(res):
        for k, a in zip(_SMALL, _unpack(r, shapes)):
            out.setdefault(k, [None] * 4)[i] = a
    res = _adamw(queue.result("small_sharded"), stack_sh(W), stack_sh(M), stack_sh(V), "adamw_small_sharded")
    for i, r in enumerate(res):
        o = 0
        for k in _SMALL_SHARDED:
            rows = W[k].shape[1]
            out.setdefault(k, [None] * 4)[i] = r[o:o + rows][None]
            o += rows
    return loss, grad_x.reshape(x.shape), out


def kernel(x, norm_mix_g, w_in, mu_shift, rwkv_w0, rwkv_w2, rwkv_a0, rwkv_a2, rwkv_g2, rwkv_k_k, rwkv_k_a, rwkv_r_k, rwkv_ln_g, rwkv_ln_b, conv_w, conv_b, lru_wr, lru_br, lru_wi, lru_bi, lru_lambda, lru_norm_g, w_out, norm_ffn_g, ffn_w_gate, ffn_w_up, ffn_w_down, norm_final_g, loss_target, m_norm_mix_g, m_w_in, m_mu_shift, m_rwkv_w0, m_rwkv_w2, m_rwkv_a0, m_rwkv_a2, m_rwkv_g2, m_rwkv_k_k, m_rwkv_k_a, m_rwkv_r_k, m_rwkv_ln_g, m_rwkv_ln_b, m_conv_w, m_conv_b, m_lru_wr, m_lru_br, m_lru_wi, m_lru_bi, m_lru_lambda, m_lru_norm_g, m_w_out, m_norm_ffn_g, m_ffn_w_gate, m_ffn_w_up, m_ffn_w_down, m_norm_final_g, v_norm_mix_g, v_w_in, v_mu_shift, v_rwkv_w0, v_rwkv_w2, v_rwkv_a0, v_rwkv_a2, v_rwkv_g2, v_rwkv_k_k, v_rwkv_k_a, v_rwkv_r_k, v_rwkv_ln_g, v_rwkv_ln_b, v_conv_w, v_conv_b, v_lru_wr, v_lru_br, v_lru_wi, v_lru_bi, v_lru_lambda, v_lru_norm_g, v_w_out, v_norm_ffn_g, v_ffn_w_gate, v_ffn_w_up, v_ffn_w_down, v_norm_final_g):
    a = locals()
    W = {k: a[k] for k in _WEIGHTS}
    M = {k: a["m_" + k] for k in _WEIGHTS}
    V = {k: a["v_" + k] for k in _WEIGHTS}
    loss, grad_x, out = _step(W, M, V, x, loss_target)
    res = [loss, grad_x]
    for i in range(4):
        res += [out[k][i].reshape(W[k].shape) for k in _WEIGHTS]
    return tuple(res)
```
